```python
import jax
import jax.numpy as jnp
from jax import lax
import numpy as np

D_MODEL = 1024
BATCH = 8
SEQ = 8192
DEPTH = 1

D_RNN = D_MODEL
N_LRU_BLOCKS = 16
LRU_BLOCK = D_RNN // N_LRU_BLOCKS
CONV_WIDTH = 4
LRU_C = 8.0
N_HEADS = 16
HEAD_DIM = 64
N_KV_GROUPS = 4
HEADS_PER_GROUP = N_HEADS // N_KV_GROUPS
D_ATTN = N_HEADS * HEAD_DIM
D_KV = N_KV_GROUPS * HEAD_DIM
CMP_LEN = 32
CMP_STRIDE = 16
CMP_HIDDEN = 256
SEL_LEN = 64
SEL_TOPK = 16
WINDOW = 512
Q_BLOCK = 64
FORCE_SCORE = 1e4
R_SEL = SEL_LEN // CMP_STRIDE
R_CMP = CMP_LEN // CMP_STRIDE
ALIBI_MAX_BIAS = 8.0
D_FF = 2816
D_PLE = 256
NORM_EPS = 1e-6

_IN_SIZES = (D_RNN, D_RNN, D_ATTN, D_KV, D_KV, D_KV, D_KV, D_KV, D_KV, 3 * N_HEADS, D_MODEL, D_MODEL)
D_IN = sum(_IN_SIZES)
IN_SPLITS = tuple(int(c) for c in np.cumsum(_IN_SIZES)[:-1])

kernel_name = "hybrid_rglru_nsa_sandwich_block"


def rmsnorm(x, g):
    x32 = x.astype(jnp.float32)
    y = x32 * lax.rsqrt(jnp.mean(x32 * x32, axis=-1, keepdims=True) + NORM_EPS)
    return (y * g.astype(jnp.float32)).astype(x.dtype)


def alibi_slopes():
    h = jnp.arange(1, N_HEADS + 1, dtype=jnp.float32)
    return jnp.exp2(-ALIBI_MAX_BIAS * h / N_HEADS)


def masked_softmax(s, mask):
    s = jnp.where(mask, s.astype(jnp.float32), -jnp.inf)
    m = jnp.max(s, axis=-1, keepdims=True)
    m = jnp.where(jnp.isfinite(m), m, 0.0)
    e = jnp.where(mask, jnp.exp(s - m), 0.0)
    return e / jnp.maximum(jnp.sum(e, axis=-1, keepdims=True), 1e-30)


def causal_conv(x, w, b):
    S = x.shape[1]
    xp = jnp.pad(x, ((0, 0), (CONV_WIDTH - 1, 0), (0, 0)))
    y = b
    for k in range(CONV_WIDTH):
        y = y + xp[:, k:k + S] * w[k]
    return y


def _lru_combine(c1, c2):
    a1, b1 = c1
    a2, b2 = c2
    return a1 * a2, a2 * b1 + b2


def rg_lru(xc, wa, ba, wx, bx, lam):
    B, S, _ = xc.shape
    xb = xc.reshape(B, S, N_LRU_BLOCKS, LRU_BLOCK)
    r = jax.nn.sigmoid(jnp.einsum('bsnd,nde->bsne', xb, wa).reshape(B, S, D_RNN) + ba)
    i = jax.nn.sigmoid(jnp.einsum('bsnd,nde->bsne', xb, wx).reshape(B, S, D_RNN) + bx)
    log_a = (-LRU_C * r.astype(jnp.float32)) * jax.nn.softplus(-lam.astype(jnp.float32))
    a = jnp.exp(log_a)
    b = jnp.sqrt(-jnp.expm1(2.0 * log_a)) * (i * xc).astype(jnp.float32)
    _, h = lax.associative_scan(_lru_combine, (a, b), axis=1)
    return h.astype(xc.dtype)


def compress(kv, pos, w1, w2):
    B, S, G, dh = kv.shape
    c = kv.reshape(B, S // CMP_STRIDE, CMP_STRIDE, G, dh)
    blk = jnp.concatenate([c[:, :-1], c[:, 1:]], axis=2)
    blk = blk + pos[:, None, :]
    nc = blk.shape[1]
    blk = blk.transpose(0, 1, 3, 2, 4).reshape(B, nc, G, CMP_LEN * dh)
    return jax.nn.gelu(blk @ w1) @ w2


def nsa(q, kc, vc, ks, vs, kw, vw, gates, slopes):
    B, S = q.shape[0], q.shape[1]
    G, R, dh = N_KV_GROUPS, HEADS_PER_GROUP, HEAD_DIM
    NQ = S // Q_BLOCK
    NB = S // SEL_LEN
    NC = kc.shape[1]
    topk = min(SEL_TOPK, NB)
    scale = HEAD_DIM ** -0.5
    slopes_g = slopes.reshape(G, R)[None, :, :, None, None]
    cmp_end = jnp.arange(NC) * CMP_STRIDE + (CMP_LEN - 1)
    kc_t = kc.transpose(0, 2, 1, 3)
    vc_t = vc.transpose(0, 2, 1, 3)
    ks_blk = ks.reshape(B, NB, SEL_LEN, G, dh).transpose(0, 3, 1, 2, 4)
    vs_blk = vs.reshape(B, NB, SEL_LEN, G, dh).transpose(0, 3, 1, 2, 4)
    kw_pad = jnp.pad(kw, ((0, 0), (WINDOW, 0), (0, 0), (0, 0)))
    vw_pad = jnp.pad(vw, ((0, 0), (WINDOW, 0), (0, 0), (0, 0)))
    q_blocks = q.reshape(B, NQ, Q_BLOCK, G, R, dh).transpose(1, 0, 2, 3, 4, 5)
    g_blocks = gates.reshape(B, NQ, Q_BLOCK, G, R, 3).transpose(1, 0, 2, 3, 4, 5)
    gather = jax.vmap(jax.vmap(lambda kb, idx: kb[idx]))
    blk_ids = jnp.arange(NB)

    def one_block(args):
        qb, qi, gi = args
        q0 = qb * Q_BLOCK
        t = q0 + jnp.arange(Q_BLOCK)
        qf = qi * scale
        dist_c = t[:, None] - cmp_end[None, :]
        s = jnp.einsum('bqgrd,bgcd->bgrqc', qf, kc_t) - slopes_g * dist_c
        p_cmp = masked_softmax(s, dist_c >= 0)
        o_cmp = jnp.einsum('bgrqc,bgcd->bqgrd', p_cmp.astype(vc_t.dtype), vc_t)
        imp = jnp.sum(p_cmp, axis=2)
        left = R_CMP - 1
        imp = jnp.pad(imp, ((0, 0), (0, 0), (0, 0), (left, R_SEL * NB - NC)))
        imp_slc = jnp.zeros(imp.shape[:3] + (NB,), jnp.float32)
        for m in range(R_SEL):
            for n in range(R_CMP):
                st = m - n + left
                imp_slc = imp_slc + imp[..., st:st + R_SEL * NB:R_SEL]
        cur = t // SEL_LEN
        valid = blk_ids[None, :] * SEL_LEN <= t[:, None]
        forced = (blk_ids[None, :] == 0) | (blk_ids[None, :] == cur[:, None]) | (blk_ids[None, :] == cur[:, None] - 1)
        score = jnp.where(valid, imp_slc, -1.0)
        score = jnp.where(forced & valid, FORCE_SCORE, score)
        _, idx = lax.top_k(score, topk)
        k_sel = gather(ks_blk, idx).reshape(B, G, Q_BLOCK, topk * SEL_LEN, dh)
        v_sel = gather(vs_blk, idx).reshape(B, G, Q_BLOCK, topk * SEL_LEN, dh)
        pos = (idx[..., None] * SEL_LEN + jnp.arange(SEL_LEN)).reshape(B, G, Q_BLOCK, topk * SEL_LEN)
        dist_s = t[None, None, :, None] - pos
        s = jnp.einsum('bqgrd,bgqkd->bgrqk', qf, k_sel) - slopes_g * dist_s[:, :, None]
        p_sel = masked_softmax(s, (dist_s >= 0)[:, :, None])
        o_slc = jnp.einsum('bgrqk,bgqkd->bqgrd', p_sel.astype(v_sel.dtype), v_sel)
        kwin = lax.dynamic_slice_in_dim(kw_pad, q0, WINDOW + Q_BLOCK, axis=1)
        vwin = lax.dynamic_slice_in_dim(vw_pad, q0, WINDOW + Q_BLOCK, axis=1)
        pos_w = q0 - WINDOW + jnp.arange(WINDOW + Q_BLOCK)
        dist_w = t[:, None] - pos_w[None, :]
        mask_w = (dist_w >= 0) & (dist_w < WINDOW) & (pos_w[None, :] >= 0)
        s = jnp.einsum('bqgrd,bkgd->bgrqk', qf, kwin) - slopes_g * dist_w
        p_win = masked_softmax(s, mask_w)
        o_win = jnp.einsum('bgrqk,bkgd->bqgrd', p_win.astype(vwin.dtype), vwin)
        g = jax.nn.sigmoid(gi)
        return g[..., 0:1] * o_cmp + g[..., 1:2] * o_slc + g[..., 2:3] * o_win

    o = lax.map(one_block, (jnp.arange(NQ), q_blocks, g_blocks))
    return o.transpose(1, 0, 2, 3, 4, 5).reshape(B, S, D_ATTN)


def setup_inputs(seed: int = 0) -> dict:
    key = jax.random.key(seed)
    ks = jax.random.split(key, 32)
    f32 = jnp.float32

    def nrm(k, shape, fan_in):
        return jax.random.normal(k, shape, f32) * (fan_in ** -0.5)

    def gain(k):
        return 1.0 + 0.05 * jax.random.normal(k, (DEPTH, D_MODEL), f32)

    u = jax.random.uniform(ks[10], (DEPTH, D_RNN), f32, minval=0.9, maxval=0.999)
    s = u ** (1.0 / LRU_C)
    lam = jnp.log(s) - jnp.log1p(-s)
    return {
        'x': jax.random.normal(ks[0], (BATCH, SEQ, D_MODEL), f32),
        'p': jax.random.normal(ks[1], (DEPTH, BATCH, SEQ, D_PLE), f32),
        'norm_mix_pre': gain(ks[2]),
        'norm_mix_post': gain(ks[3]),
        'w_in': nrm(ks[4], (DEPTH, D_MODEL, D_IN), D_MODEL),
        'conv_w': nrm(ks[5], (DEPTH, CONV_WIDTH, D_RNN), CONV_WIDTH),
        'conv_b': 0.02 * jax.random.normal(ks[6], (DEPTH, D_RNN), f32),
        'lru_wa': nrm(ks[7], (DEPTH, N_LRU_BLOCKS, LRU_BLOCK, LRU_BLOCK), LRU_BLOCK),
        'lru_ba': 0.02 * jax.random.normal(ks[8], (DEPTH, D_RNN), f32),
        'lru_wx': nrm(ks[9], (DEPTH, N_LRU_BLOCKS, LRU_BLOCK, LRU_BLOCK), LRU_BLOCK),
        'lru_bx': 0.02 * jax.random.normal(ks[11], (DEPTH, D_RNN), f32),
        'lru_lambda': lam,
        'cmp_pos_k': 0.02 * jax.random.normal(ks[12], (DEPTH, CMP_LEN, HEAD_DIM), f32),
        'cmp_pos_v': 0.02 * jax.random.normal(ks[13], (DEPTH, CMP_LEN, HEAD_DIM), f32),
        'cmp_k_w1': nrm(ks[14], (DEPTH, CMP_LEN * HEAD_DIM, CMP_HIDDEN), CMP_LEN * HEAD_DIM),
        'cmp_k_w2': nrm(ks[15], (DEPTH, CMP_HIDDEN, HEAD_DIM), CMP_HIDDEN),
        'cmp_v_w1': nrm(ks[16], (DEPTH, CMP_LEN * HEAD_DIM, CMP_HIDDEN), CMP_LEN * HEAD_DIM),
        'cmp_v_w2': nrm(ks[17], (DEPTH, CMP_HIDDEN, HEAD_DIM), CMP_HIDDEN),
        'w_out': nrm(ks[18], (DEPTH, D_MODEL, D_MODEL), D_MODEL),
        'norm_ffn_pre': gain(ks[19]),
        'norm_ffn_post': gain(ks[20]),
        'ffn_w_gate_up': nrm(ks[21], (DEPTH, D_MODEL, 2 * D_FF), D_MODEL),
        'ffn_w_down': nrm(ks[22], (DEPTH, D_FF, D_MODEL), D_FF),
        'ple_w_proj': nrm(ks[23], (DEPTH, D_PLE, D_MODEL), D_PLE),
        'ple_w_gate': nrm(ks[24], (DEPTH, D_MODEL, D_MODEL), D_MODEL),
        'ple_b_gate': 0.02 * jax.random.normal(ks[25], (DEPTH, D_MODEL), f32),
    }


def reference(x, p, norm_mix_pre, norm_mix_post, w_in, conv_w, conv_b, lru_wa, lru_ba, lru_wx, lru_bx,
              lru_lambda, cmp_pos_k, cmp_pos_v, cmp_k_w1, cmp_k_w2, cmp_v_w1, cmp_v_w2, w_out,
              norm_ffn_pre, norm_ffn_post, ffn_w_gate_up, ffn_w_down, ple_w_proj, ple_w_gate, ple_b_gate):
    B, S, _ = x.shape
    slopes = alibi_slopes()
    for i in range(DEPTH):
        h = rmsnorm(x, norm_mix_pre[i])
        z = h @ w_in[i]
        xr, gr, q, kc, vc, ksl, vsl, kw, vw, g_nsa, gm_rnn, gm_attn = jnp.split(z, IN_SPLITS, axis=-1)
        xr = causal_conv(xr, conv_w[i], conv_b[i])
        y_rnn = rg_lru(xr, lru_wa[i], lru_ba[i], lru_wx[i], lru_bx[i], lru_lambda[i]) * jax.nn.gelu(gr)
        kv_shape = (B, S, N_KV_GROUPS, HEAD_DIM)
        kc_c = compress(kc.reshape(kv_shape), cmp_pos_k[i], cmp_k_w1[i], cmp_k_w2[i])
        vc_c = compress(vc.reshape(kv_shape), cmp_pos_v[i], cmp_v_w1[i], cmp_v_w2[i])
        y_attn = nsa(q, kc_c, vc_c, ksl.reshape(kv_shape), vsl.reshape(kv_shape),
                     kw.reshape(kv_shape), vw.reshape(kv_shape), g_nsa, slopes)
        y = jax.nn.sigmoid(gm_rnn) * y_rnn + jax.nn.sigmoid(gm_attn) * y_attn
        x = x + rmsnorm(y @ w_out[i], norm_mix_post[i])
        h = rmsnorm(x, norm_ffn_pre[i])
        gate, up = jnp.split(h @ ffn_w_gate_up[i], 2, axis=-1)
        x = x + rmsnorm((jax.nn.silu(gate) * up) @ ffn_w_down[i], norm_ffn_post[i])
        x = x + jax.nn.sigmoid(x @ ple_w_gate[i] + ple_b_gate[i]) * (p[i] @ ple_w_proj[i])
    return x
```

```python
import functools

import numpy as np
import jax
import jax.numpy as jnp
from jax import lax
from jax.experimental import pallas as pl
from jax.experimental.pallas import tpu as pltpu

N_LRU_BLOCKS = 16
CONV_WIDTH = 4
LRU_C = 8.0
N_HEADS = 16
HEAD_DIM = 64
N_KV_GROUPS = 4
HEADS_PER_GROUP = N_HEADS // N_KV_GROUPS
CMP_LEN = 32
CMP_STRIDE = 16
SEL_LEN = 64
SEL_TOPK = 16
WINDOW = 512
FORCE_SCORE = 1e4
ALIBI_MAX_BIAS = 8.0
NORM_EPS = 1e-6

LANES = 128
SUBLANES = 8
VMEM_LIMIT_BYTES = 56 * 1024 * 1024

MASK_NEG = -1e30
BF16 = jnp.bfloat16
F32 = jnp.float32


def _dot(a, b):
    return jnp.dot(a, b, preferred_element_type=F32)


def _dot_nt(a, b):
    return lax.dot_general(a, b, (((1,), (1,)), ((), ())), preferred_element_type=F32)


def _sigmoid(x):
    return 1.0 / (1.0 + jnp.exp(-x))


def _gelu_tanh(x):
    c = np.float32(np.sqrt(2.0 / np.pi))
    return 0.5 * x * (1.0 + jnp.tanh(c * (x + 0.044715 * (x * x * x))))


def _rmsnorm(x, g):
    ms = jnp.mean(x * x, axis=-1, keepdims=True)
    return x * lax.rsqrt(ms + NORM_EPS) * g


def _params(sem):
    return pltpu.CompilerParams(dimension_semantics=sem, vmem_limit_bytes=VMEM_LIMIT_BYTES)


def _norm_matmul_kernel(x_ref, g_ref, w_ref, o_ref, h_ref):
    @pl.when(pl.program_id(1) == 0)
    def _():
        h_ref[...] = _rmsnorm(x_ref[...], g_ref[...]).astype(BF16)

    o_ref[...] = _dot(h_ref[...], w_ref[...]).astype(o_ref.dtype)


def _norm_matmul(x, g, w, out_dtype, tm, tn):
    t, k = x.shape
    n = w.shape[1]
    return pl.pallas_call(
        _norm_matmul_kernel,
        grid=(t // tm, n // tn),
        in_specs=[
            pl.BlockSpec((tm, k), lambda i, j: (i, 0)),
            pl.BlockSpec((1, k), lambda i, j: (0, 0)),
            pl.BlockSpec((k, tn), lambda i, j: (0, j)),
        ],
        out_specs=pl.BlockSpec((tm, tn), lambda i, j: (i, j)),
        out_shape=jax.ShapeDtypeStruct((t, n), out_dtype),
        scratch_shapes=[pltpu.VMEM((tm, k), BF16)],
        compiler_params=_params(("parallel", "arbitrary")),
        name="norm_matmul",
    )(x, g, w)


def _norm_kv_kernel(x_ref, g_ref, w_ref, o_ref):
    h = _rmsnorm(x_ref[...], g_ref[...]).astype(BF16)
    z = _dot(h, w_ref[...])
    for c in range(o_ref.shape[0]):
        o_ref[c] = z[:, c * HEAD_DIM:(c + 1) * HEAD_DIM].astype(o_ref.dtype)


def _norm_kv(x, g, w, tm):
    t, k = x.shape
    n = w.shape[1]
    n_slabs = n // HEAD_DIM
    return pl.pallas_call(
        _norm_kv_kernel,
        grid=(t // tm,),
        in_specs=[
            pl.BlockSpec((tm, k), lambda i: (i, 0)),
            pl.BlockSpec((1, k), lambda i: (0, 0)),
            pl.BlockSpec((k, n), lambda i: (0, 0)),
        ],
        out_specs=pl.BlockSpec((n_slabs, tm, HEAD_DIM), lambda i: (0, i, 0)),
        out_shape=jax.ShapeDtypeStruct((n_slabs, t, HEAD_DIM), BF16),
        compiler_params=_params(("parallel",)),
        name="norm_kv",
    )(x, g, w)


def _rnn_kernel(xr_ref, gr_ref, gm_ref, cw_ref, cb_ref, wa_ref, wx_ref, ba_ref, bx_ref, lam_ref,
                o_ref, xbuf, a_scr, b_scr, h_scr, *, ts):
    s_idx = pl.program_id(1)
    d = xr_ref.shape[1]
    halo = SUBLANES

    @pl.when(s_idx == 0)
    def _():
        xbuf[0:halo, :] = jnp.zeros((halo, d), F32)
        h_scr[...] = jnp.zeros(h_scr.shape, F32)

    xbuf[halo:halo + ts, :] = xr_ref[...]
    xc = cb_ref[...] + xbuf[halo:halo + ts, :] * cw_ref[CONV_WIDTH - 1:CONV_WIDTH, :]
    for k in range(1, CONV_WIDTH):
        xc = xc + xbuf[halo - k:halo - k + ts, :] * cw_ref[CONV_WIDTH - 1 - k:CONV_WIDTH - k, :]
    xbuf[0:halo, :] = xbuf[ts:ts + halo, :]

    xcb = xc.astype(BF16)
    n_chunks = d // LANES
    neg_sp = -LRU_C * (jnp.maximum(-lam_ref[...], 0.0) + jnp.log(1.0 + jnp.exp(-jnp.abs(lam_ref[...]))))
    for c in range(n_chunks):
        sl = slice(c * LANES, (c + 1) * LANES)
        xk = xcb[:, sl]
        r = _sigmoid(_dot(xk, wa_ref[c]) + ba_ref[:, sl])
        i = _sigmoid(_dot(xk, wx_ref[c]) + bx_ref[:, sl])
        log_a = r * neg_sp[:, sl]
        a_scr[:, sl] = jnp.exp(log_a)
        b_scr[:, sl] = jnp.sqrt(1.0 - jnp.exp(2.0 * log_a)) * (i * xc[:, sl])

    def group(gi, h):
        base = pl.multiple_of(gi * SUBLANES, SUBLANES)
        a8 = a_scr[pl.ds(base, SUBLANES), :]
        b8 = b_scr[pl.ds(base, SUBLANES), :]
        rows = []
        for j in range(SUBLANES):
            h = a8[j:j + 1, :] * h + b8[j:j + 1, :]
            rows.append(h)
        a_scr[pl.ds(base, SUBLANES), :] = jnp.concatenate(rows, axis=0)
        return h

    h_last = lax.fori_loop(0, ts // SUBLANES, group, h_scr[0:1, :])
    h_scr[0:1, :] = h_last
    o_ref[...] = (_sigmoid(gm_ref[...]) * a_scr[...] * _gelu_tanh(gr_ref[...])).astype(o_ref.dtype)


def _rnn(f, conv_w, conv_b, wa, wx, ba, bx, lam, batch, seq, ts):
    t = f.shape[0]
    d = conv_w.shape[1]
    ns = seq // ts
    nd = d // d
    del nd
    row = lambda b, s: b * ns + s
    vec = lambda r: pl.BlockSpec((r, d), lambda b, s: (0, 0))
    return pl.pallas_call(
        functools.partial(_rnn_kernel, ts=ts),
        grid=(batch, ns),
        in_specs=[
            pl.BlockSpec((ts, d), lambda b, s: (row(b, s), 0)),
            pl.BlockSpec((ts, d), lambda b, s: (row(b, s), 1)),
            pl.BlockSpec((ts, d), lambda b, s: (row(b, s), 2)),
            vec(CONV_WIDTH), vec(1),
            pl.BlockSpec(wa.shape, lambda b, s: (0, 0, 0)),
            pl.BlockSpec(wx.shape, lambda b, s: (0, 0, 0)),
            vec(1), vec(1), vec(1),
        ],
        out_specs=pl.BlockSpec((ts, d), lambda b, s: (row(b, s), 0)),
        out_shape=jax.ShapeDtypeStruct((t, d), F32),
        scratch_shapes=[
            pltpu.VMEM((ts + SUBLANES, d), F32),
            pltpu.VMEM((ts, d), F32),
            pltpu.VMEM((ts, d), F32),
            pltpu.VMEM((SUBLANES, d), F32),
        ],
        compiler_params=_params(("parallel", "arbitrary")),
        name="rnn_mixer",
    )(f, f, f, conv_w, conv_b, wa, wx, ba, bx, lam)


def _compress_kernel(ak_ref, av_ref, w1k_ref, w2k_ref, pk_ref, w1v_ref, w2v_ref, pv_ref, ok_ref, ov_ref):
    def one(a_ref, w1_ref, w2_ref, p_ref, o_ref):
        a = a_ref[0]
        half = a.shape[1]
        nchunk = a.shape[0]
        lo = _dot(a, w1_ref[0:half, :])
        hi = _dot(a, w1_ref[half:2 * half, :])
        pb = _dot(jnp.broadcast_to(p_ref[...], (SUBLANES, 2 * half)).astype(BF16), w1_ref[...])[0:1, :]
        h = lo + pltpu.roll(hi, nchunk - 1, 0) + pb
        o_ref[0] = _dot(_gelu_tanh(h).astype(BF16), w2_ref[...]).astype(o_ref.dtype)

    one(ak_ref, w1k_ref, w2k_ref, pk_ref, ok_ref)
    one(av_ref, w1v_ref, w2v_ref, pv_ref, ov_ref)


def _compress(kv_chunks, w1k, w2k, pk, w1v, w2v, pv, batch, nchunk):
    g = N_KV_GROUPS
    width = kv_chunks.shape[2]
    full = lambda a: pl.BlockSpec(a.shape, lambda b, gg: (0,) * a.ndim)
    out = jax.ShapeDtypeStruct((batch * g, nchunk, HEAD_DIM), BF16)
    return pl.pallas_call(
        _compress_kernel,
        grid=(batch, g),
        in_specs=[
            pl.BlockSpec((1, nchunk, width), lambda b, gg: (gg, b, 0)),
            pl.BlockSpec((1, nchunk, width), lambda b, gg: (g + gg, b, 0)),
            full(w1k), full(w2k), full(pk), full(w1v), full(w2v), full(pv),
        ],
        out_specs=[
            pl.BlockSpec((1, nchunk, HEAD_DIM), lambda b, gg: (b * g + gg, 0, 0)),
            pl.BlockSpec((1, nchunk, HEAD_DIM), lambda b, gg: (b * g + gg, 0, 0)),
        ],
        out_shape=[out, out],
        compiler_params=_params(("parallel", "parallel")),
        name="compress",
    )(kv_chunks, kv_chunks, w1k, w2k, pk, w1v, w2v, pv)


def _softmax_step(q, k, v, bias, state):
    m_old, l_old, acc_old = state
    s = _dot_nt(q, k)
    if bias is not None:
        s = s + bias
    m_new = jnp.maximum(m_old, jnp.max(s, axis=-1, keepdims=True))
    alpha = jnp.exp(m_old - m_new)
    e = jnp.exp(s - m_new)
    l_new = alpha * l_old + jnp.sum(e, axis=-1, keepdims=True)
    acc_new = alpha * acc_old + _dot(e.astype(BF16), v)
    return m_new, l_new, acc_new


def _nsa_kernel(q_ref, kc_ref, vc_ref, ks_ref, vs_ref, kw_ref, vw_ref, gn_ref, gm_ref,
                qal_ref, kpos_ref, cpos_ref, wsel_ref,
                o_ref, qaug, ksaug, kwaug, kcaug, *, tq, nk, nkw, seq):
    b = pl.program_id(0)
    g = pl.program_id(1)
    qt = pl.program_id(2)
    r_heads = HEADS_PER_GROUP
    dh = HEAD_DIM
    m_rows = r_heads * tq
    nb = seq // SEL_LEN
    nc = kc_ref.shape[1]
    a_w = qal_ref.shape[2]

    @pl.when((b == 0) & (g == 0) & (qt == 0))
    def _():
        ksaug[...] = kpos_ref[...]
        kwaug[...] = kpos_ref[:, 0:a_w]
        kcaug[...] = cpos_ref[...]

    @pl.when(qt == 0)
    def _():
        ksaug[:, 0:dh] = ks_ref[0]
        kwaug[:, 0:dh] = kw_ref[0]
        kcaug[:, 0:dh] = kc_ref[0]

    t0 = qt * tq
    row = lax.broadcasted_iota(jnp.int32, (m_rows, 1), 0)
    tcol = t0 + (row & (tq - 1))

    q = q_ref[...]
    scale = np.float32(HEAD_DIM ** -0.5)
    for r in range(r_heads):
        rs = slice(r * tq, (r + 1) * tq)
        qaug[rs, 0:a_w] = jnp.broadcast_to(qal_ref[0, r:r + 1, :], (tq, a_w))
        qaug[rs, 0:dh] = (q[:, r * dh:(r + 1) * dh].astype(F32) * scale).astype(BF16)

    cend = lax.broadcasted_iota(jnp.int32, (1, nc), 1) * CMP_STRIDE + (CMP_LEN - 1)
    bias_c = jnp.where(cend <= tcol, 0.0, MASK_NEG).astype(F32)
    s = _dot_nt(qaug[:, 0:a_w], kcaug[...]) + bias_c
    m = jnp.max(s, axis=-1, keepdims=True)
    e = jnp.exp(s - m)
    has_key = (tcol >= CMP_LEN - 1).astype(F32)
    p = e * (has_key / jnp.sum(e, axis=-1, keepdims=True))
    o_cmp = _dot(p.astype(BF16), vc_ref[0])
    imp = p[0:tq]
    for r in range(1, r_heads):
        imp = imp + p[r * tq:(r + 1) * tq]

    hi = imp.astype(BF16)
    r1 = imp - hi.astype(F32)
    mid = r1.astype(BF16)
    lo = (r1 - mid.astype(F32)).astype(BF16)
    wsel = wsel_ref[...]
    imp_t = _dot_nt(wsel, hi) + _dot_nt(wsel, mid) + _dot_nt(wsel, lo)

    blk = lax.broadcasted_iota(jnp.int32, (nb, tq), 0).astype(F32)
    tq_l = t0 + lax.broadcasted_iota(jnp.int32, (nb, tq), 1)
    cur = (tq_l // SEL_LEN).astype(F32)
    valid = blk <= cur
    forced = (blk == 0.0) | (blk == cur) | (blk == cur - 1.0)
    score = jnp.where(valid, imp_t, -1.0)
    score = jnp.where(forced & valid, FORCE_SCORE, score)
    sel = jnp.zeros((nb, tq), F32)
    for _ in range(min(SEL_TOPK, nb)):
        mx = jnp.max(score, axis=0, keepdims=True)
        idx = jnp.min(jnp.where(score == mx, blk, float(nb)), axis=0, keepdims=True)
        hit = blk == idx
        sel = jnp.where(hit, 1.0, sel)
        score = jnp.where(hit, -2.0, score)
    selneg_t = jnp.where((sel > 0.0) & valid, 0.0, MASK_NEG).astype(F32)
    selneg = jnp.transpose(selneg_t).astype(BF16)
    for r in range(r_heads):
        qaug[r * tq:(r + 1) * tq, a_w:] = selneg

    init = (jnp.full((m_rows, 1), MASK_NEG, F32), jnp.zeros((m_rows, 1), F32), jnp.zeros((m_rows, dh), F32))
    kd = t0 // nk

    def slc_body(kt, state):
        off = pl.multiple_of(kt * nk, nk)
        return _softmax_step(qaug[...], ksaug[pl.ds(off, nk), :], vs_ref[0, pl.ds(off, nk), :], None, state)

    state = lax.fori_loop(0, kd, slc_body, init)
    off_d = pl.multiple_of(kd * nk, nk)
    kpos_d = off_d + lax.broadcasted_iota(jnp.int32, (1, nk), 1)
    bias_d = jnp.where(kpos_d <= tcol, 0.0, MASK_NEG).astype(F32)
    m_s, l_s, acc_s = _softmax_step(qaug[...], ksaug[pl.ds(off_d, nk), :], vs_ref[0, pl.ds(off_d, nk), :],
                                    bias_d, state)
    o_slc = acc_s / l_s

    n_wt = (WINDOW + tq) // nkw
    state = init
    for i in range(n_wt):
        start = t0 + tq - (i + 1) * nkw
        start_c = jnp.maximum(start, 0)
        off_w = pl.multiple_of(start_c, nkw)
        kpos_w = start + lax.broadcasted_iota(jnp.int32, (1, nkw), 1)
        ok = (kpos_w <= tcol) & (kpos_w > tcol - WINDOW) & (start >= 0)
        bias_w = jnp.where(ok, 0.0, MASK_NEG).astype(F32)
        state = _softmax_step(qaug[:, 0:a_w], kwaug[pl.ds(off_w, nkw), :], vw_ref[0, pl.ds(off_w, nkw), :],
                              bias_w, state)
    m_w, l_w, acc_w = state
    o_win = acc_w / l_w

    gates = _sigmoid(gn_ref[...])
    gm = _sigmoid(gm_ref[...])
    for r in range(r_heads):
        rs = slice(r * tq, (r + 1) * tq)
        o = (gates[:, 3 * r:3 * r + 1] * o_cmp[rs] + gates[:, 3 * r + 1:3 * r + 2] * o_slc[rs]
             + gates[:, 3 * r + 2:3 * r + 3] * o_win[rs])
        o_ref[:, r * dh:(r + 1) * dh] = (gm[:, r * dh:(r + 1) * dh] * o).astype(o_ref.dtype)


def _alibi_tables(seq, nc):
    import ml_dtypes
    bf = ml_dtypes.bfloat16
    h = np.arange(1, N_HEADS + 1, dtype=np.float32)
    slopes = np.exp2(-ALIBI_MAX_BIAS * h / N_HEADS).astype(np.float32)
    s1 = slopes.astype(bf).astype(np.float32)
    s2 = (slopes - s1).astype(bf).astype(np.float32)
    s3 = (slopes - s1 - s2).astype(bf).astype(np.float32)
    dh = HEAD_DIM
    qal = np.zeros((N_HEADS, 2 * dh), np.float32)
    for rep in range(3):
        qal[:, dh + 3 * rep + 0] = s1
        qal[:, dh + 3 * rep + 1] = s2
        qal[:, dh + 3 * rep + 2] = s3
    qal = qal.reshape(N_KV_GROUPS, HEADS_PER_GROUP, 2 * dh)
    qal_p = np.zeros((N_KV_GROUPS, SUBLANES, 2 * dh), np.float32)
    qal_p[:, :HEADS_PER_GROUP] = qal

    nb = seq // SEL_LEN
    pos = np.arange(seq)
    kpos = np.zeros((seq, 2 * dh + nb), np.float32)
    kpos[:, dh:dh + 3] = ((pos // SEL_LEN) * SEL_LEN)[:, None]
    kpos[:, dh + 3:dh + 6] = (pos % SEL_LEN)[:, None]
    kpos[pos, 2 * dh + pos // SEL_LEN] = 1.0

    c = np.arange(nc)
    cpos = np.zeros((nc, 2 * dh), np.float32)
    cpos[:, dh:dh + 3] = ((c // 16) * 16 * CMP_STRIDE)[:, None]
    cpos[:, dh + 3:dh + 6] = ((c % 16) * CMP_STRIDE)[:, None]
    cpos[:, dh + 6:dh + 9] = CMP_LEN - 1

    r_sel = SEL_LEN // CMP_STRIDE
    r_cmp = CMP_LEN // CMP_STRIDE
    wsel = np.zeros((nb, nc), np.float32)
    for j in range(nb):
        for mm in range(r_sel):
            for nn in range(r_cmp):
                ci = r_sel * j + mm - nn
                if 0 <= ci < nc - 1:
                    wsel[j, ci] += 1.0
    as_bf = lambda a: jnp.asarray(a.astype(bf))
    return as_bf(qal_p), as_bf(kpos), as_bf(cpos), as_bf(wsel)


def _nsa(qm, kvm, kcm, vcm, gnm, fm, batch, seq, tq, nk, nkw, gm_col0):
    t = qm.shape[0]
    g = N_KV_GROUPS
    dh = HEAD_DIM
    nqt = seq // tq
    nc = kcm.shape[1]
    nb = seq // SEL_LEN
    gw = HEADS_PER_GROUP * dh
    qal, kpos, cpos, wsel = _alibi_tables(seq, nc)
    row = lambda b, gg, i: b * nqt + i
    slab = lambda base: pl.BlockSpec((1, seq, dh), lambda b, gg, i: (base + gg, b, 0))
    cmp_slab = pl.BlockSpec((1, nc, dh), lambda b, gg, i: (b * g + gg, 0, 0))
    const = lambda a: pl.BlockSpec(a.shape, lambda b, gg, i: (0,) * a.ndim)
    return pl.pallas_call(
        functools.partial(_nsa_kernel, tq=tq, nk=nk, nkw=nkw, seq=seq),
        grid=(batch, g, nqt),
        in_specs=[
            pl.BlockSpec((tq, gw), lambda b, gg, i: (row(b, gg, i), gg)),
            cmp_slab, cmp_slab,
            slab(2 * g), slab(3 * g), slab(4 * g), slab(5 * g),
            pl.BlockSpec((tq, LANES), lambda b, gg, i: (row(b, gg, i), gg)),
            pl.BlockSpec((tq, gw), lambda b, gg, i: (row(b, gg, i), gm_col0 + gg)),
            pl.BlockSpec((1, SUBLANES, 2 * dh), lambda b, gg, i: (gg, 0, 0)),
            const(kpos), const(cpos), const(wsel),
        ],
        out_specs=pl.BlockSpec((tq, gw), lambda b, gg, i: (row(b, gg, i), gg)),
        out_shape=jax.ShapeDtypeStruct((t, g * gw), F32),
        scratch_shapes=[
            pltpu.VMEM((HEADS_PER_GROUP * tq, 2 * dh + nb), BF16),
            pltpu.VMEM((seq, 2 * dh + nb), BF16),
            pltpu.VMEM((seq, 2 * dh), BF16),
            pltpu.VMEM((nc, 2 * dh), BF16),
        ],
        compiler_params=_params(("arbitrary", "arbitrary", "arbitrary")),
        name="nsa",
    )(qm, kcm, vcm, kvm, kvm, kvm, kvm, gnm, fm, qal, kpos, cpos, wsel)


def _out_proj_kernel(x_ref, yr_ref, ya_ref, w_ref, g_ref, o_ref):
    y = (yr_ref[...] + ya_ref[...]).astype(BF16)
    o_ref[...] = x_ref[...] + _rmsnorm(_dot(y, w_ref[...]), g_ref[...])


def _out_proj(x, yr, ya, w, g, tm):
    t, d = x.shape
    rows = pl.BlockSpec((tm, d), lambda i: (i, 0))
    return pl.pallas_call(
        _out_proj_kernel,
        grid=(t // tm,),
        in_specs=[rows, rows, rows, pl.BlockSpec((d, d), lambda i: (0, 0)), pl.BlockSpec((1, d), lambda i: (0, 0))],
        out_specs=rows,
        out_shape=jax.ShapeDtypeStruct((t, d), F32),
        compiler_params=_params(("parallel",)),
        name="out_proj",
    )(x, yr, ya, w, g)


def _ffn_kernel(x_ref, gpre_ref, wg_ref, wu_ref, wd_ref, gpost_ref, p_ref, wpg_ref, bpg_ref, wpp_ref,
                o_ref, h_ref, acc_ref):
    j = pl.program_id(1)

    @pl.when(j == 0)
    def _():
        h_ref[...] = _rmsnorm(x_ref[...], gpre_ref[...]).astype(BF16)
        acc_ref[...] = jnp.zeros(acc_ref.shape, F32)

    h = h_ref[...]
    gate = _dot(h, wg_ref[...])
    up = _dot(h, wu_ref[...])
    act = (gate * _sigmoid(gate) * up).astype(BF16)
    acc_ref[...] += _dot(act, wd_ref[...])

    @pl.when(j == pl.num_programs(1) - 1)
    def _():
        x2 = x_ref[...] + _rmsnorm(acc_ref[...], gpost_ref[...])
        gate_p = _sigmoid(_dot(x2.astype(BF16), wpg_ref[...]) + bpg_ref[...])
        o_ref[...] = x2 + gate_p * _dot(p_ref[...].astype(BF16), wpp_ref[...])


def _ffn(x, gpre, wgu, wd, gpost, p, wpg, bpg, wpp, tm, tf):
    t, d = x.shape
    dff = wd.shape[0]
    nf = dff // tf
    dp = p.shape[1]
    rows = pl.BlockSpec((tm, d), lambda i, j: (i, 0))
    vec = pl.BlockSpec((1, d), lambda i, j: (0, 0))
    return pl.pallas_call(
        _ffn_kernel,
        grid=(t // tm, nf),
        in_specs=[
            rows, vec,
            pl.BlockSpec((d, tf), lambda i, j: (0, j)),
            pl.BlockSpec((d, tf), lambda i, j: (0, nf + j)),
            pl.BlockSpec((tf, d), lambda i, j: (j, 0)),
            vec,
            pl.BlockSpec((tm, dp), lambda i, j: (i, 0)),
            pl.BlockSpec((d, d), lambda i, j: (0, 0)),
            vec,
            pl.BlockSpec((dp, d), lambda i, j: (0, 0)),
        ],
        out_specs=rows,
        out_shape=jax.ShapeDtypeStruct((t, d), F32),
        scratch_shapes=[pltpu.VMEM((tm, d), BF16), pltpu.VMEM((tm, d), F32)],
        compiler_params=_params(("parallel", "arbitrary")),
        name="ffn_ple",
    )(x, gpre, wgu, wgu, wd, gpost, p, wpg, bpg, wpp)


def _block_diag_chunks(w):
    n, bs, _ = w.shape
    per = LANES // bs
    w = w.reshape(n // per, per, bs, bs)
    eye = jnp.eye(per, dtype=w.dtype)
    return jnp.einsum('cpij,pq->cpiqj', w, eye).reshape(n // per, LANES, LANES)


def _layer(x, p, norm_mix_pre, norm_mix_post, w_in, conv_w, conv_b, lru_wa, lru_ba, lru_wx, lru_bx,
           lru_lambda, cmp_pos_k, cmp_pos_v, cmp_k_w1, cmp_k_w2, cmp_v_w1, cmp_v_w2, w_out,
           norm_ffn_pre, norm_ffn_post, ffn_w_gate_up, ffn_w_down, ple_w_proj, ple_w_gate, ple_b_gate,
           batch, seq):
    t, d = x.shape
    d_attn = N_HEADS * HEAD_DIM
    d_kv = N_KV_GROUPS * HEAD_DIM
    row2 = lambda v: v.reshape(1, -1)

    o_q = 2 * d
    o_kv = o_q + d_attn
    o_gn = o_kv + 6 * d_kv
    o_gm = o_gn + 3 * N_HEADS
    w_f = jnp.concatenate([w_in[:, 0:o_q], w_in[:, o_gm:o_gm + 2 * d]], axis=1).astype(BF16)
    w_q = w_in[:, o_q:o_kv].astype(BF16)
    w_kv = w_in[:, o_kv:o_gn].astype(BF16)
    per_g = 3 * HEADS_PER_GROUP
    w_gn = w_in[:, o_gn:o_gm].reshape(d, N_KV_GROUPS, per_g)
    w_gn = jnp.pad(w_gn, ((0, 0), (0, 0), (0, LANES - per_g))).reshape(d, N_KV_GROUPS * LANES).astype(BF16)

    g_pre = row2(norm_mix_pre)
    tm = min(512, t)
    fm = _norm_matmul(x, g_pre, w_f, F32, tm, 1024)
    qm = _norm_matmul(x, g_pre, w_q, BF16, tm, d_attn)
    gnm = _norm_matmul(x, g_pre, w_gn, F32, tm, N_KV_GROUPS * LANES)
    kvm = _norm_kv(x, g_pre, w_kv, tm)

    yr = _rnn(fm, conv_w, row2(conv_b), _block_diag_chunks(lru_wa).astype(BF16),
              _block_diag_chunks(lru_wx).astype(BF16), row2(lru_ba), row2(lru_bx), row2(lru_lambda),
              batch, seq, min(512, seq))

    nchunk = seq // CMP_STRIDE
    kv_chunks = kvm[0:2 * N_KV_GROUPS].reshape(2 * N_KV_GROUPS, t // CMP_STRIDE, CMP_STRIDE * HEAD_DIM)
    kcm, vcm = _compress(kv_chunks, cmp_k_w1.astype(BF16), cmp_k_w2.astype(BF16), cmp_pos_k.reshape(1, -1),
                         cmp_v_w1.astype(BF16), cmp_v_w2.astype(BF16), cmp_pos_v.reshape(1, -1), batch, nchunk)

    tq = min(256, seq)
    ya = _nsa(qm, kvm, kcm, vcm, gnm, fm, batch, seq, tq, min(512, seq), min(256, seq),
              gm_col0=(3 * d) // (HEADS_PER_GROUP * HEAD_DIM))

    x1 = _out_proj(x, yr, ya, w_out.astype(BF16), row2(norm_mix_post), tm)
    dff = ffn_w_down.shape[0]
    tf = dff // 2 if (dff // 2) % LANES == 0 else dff
    return _ffn(x1, row2(norm_ffn_pre), ffn_w_gate_up.astype(BF16), ffn_w_down.astype(BF16), row2(norm_ffn_post),
                p, ple_w_gate.astype(BF16), row2(ple_b_gate), ple_w_proj.astype(BF16), tm, tf)


def kernel(x, p, norm_mix_pre, norm_mix_post, w_in, conv_w, conv_b, lru_wa, lru_ba, lru_wx, lru_bx, lru_lambda, cmp_pos_k, cmp_pos_v, cmp_k_w1, cmp_k_w2, cmp_v_w1, cmp_v_w2, w_out, norm_ffn_pre, norm_ffn_post, ffn_w_gate_up, ffn_w_down, ple_w_proj, ple_w_gate, ple_b_gate):
    batch, seq, d = x.shape
    depth = w_in.shape[0]
    xf = x.reshape(batch * seq, d)
    for i in range(depth):
        xf = _layer(xf, p[i].reshape(batch * seq, -1), norm_mix_pre[i], norm_mix_post[i], w_in[i], conv_w[i],
                    conv_b[i], lru_wa[i], lru_ba[i], lru_wx[i], lru_bx[i], lru_lambda[i], cmp_pos_k[i],
                    cmp_pos_v[i], cmp_k_w1[i], cmp_k_w2[i], cmp_v_w1[i], cmp_v_w2[i], w_out[i],
                    norm_ffn_pre[i], norm_ffn_post[i], ffn_w_gate_up[i], ffn_w_down[i], ple_w_proj[i],
                    ple_w_gate[i], ple_b_gate[i], batch, seq)
    return xf.reshape(batch, seq, d)
```

```python
import functools

import numpy as np
import jax
import jax.numpy as jnp
from jax import lax
from jax.experimental import pallas as pl
from jax.experimental.pallas import tpu as pltpu

N_LRU_BLOCKS = 16
CONV_WIDTH = 4
LRU_C = 8.0
N_HEADS = 16
HEAD_DIM = 64
N_KV_GROUPS = 4
HEADS_PER_GROUP = N_HEADS // N_KV_GROUPS
CMP_LEN = 32
CMP_STRIDE = 16
SEL_LEN = 64
SEL_TOPK = 16
WINDOW = 512
FORCE_SCORE = 1e4
ALIBI_MAX_BIAS = 8.0
NORM_EPS = 1e-6

LANES = 128
SUBLANES = 8
VMEM_LIMIT_BYTES = 56 * 1024 * 1024

MASK_NEG = -1e30
N_ALIBI_COLS = 9
PAD_COL = HEAD_DIM + N_ALIBI_COLS
WORD_BITS = 16
BF16 = jnp.bfloat16
F32 = jnp.float32


def _dot(a, b):
    return jnp.dot(a, b, preferred_element_type=F32)


def _dot_nt(a, b):
    return lax.dot_general(a, b, (((1,), (1,)), ((), ())), preferred_element_type=F32)


def _sigmoid(x):
    return 1.0 / (1.0 + jnp.exp(-x))


def _gelu_tanh(x):
    c = np.float32(np.sqrt(2.0 / np.pi))
    return 0.5 * x * (1.0 + jnp.tanh(c * (x + 0.044715 * (x * x * x))))


def _rmsnorm(x, g):
    ms = jnp.mean(x * x, axis=-1, keepdims=True)
    return x * lax.rsqrt(ms + NORM_EPS) * g


def _params(sem):
    return pltpu.CompilerParams(dimension_semantics=sem, vmem_limit_bytes=VMEM_LIMIT_BYTES)


def _norm_matmul_kernel(x_ref, g_ref, w_ref, o_ref, h_ref):
    @pl.when(pl.program_id(1) == 0)
    def _():
        h_ref[...] = _rmsnorm(x_ref[...], g_ref[...]).astype(BF16)

    o_ref[...] = _dot(h_ref[...], w_ref[...]).astype(o_ref.dtype)


def _norm_matmul(x, g, w, out_dtype, tm, tn):
    t, k = x.shape
    n = w.shape[1]
    return pl.pallas_call(
        _norm_matmul_kernel,
        grid=(t // tm, n // tn),
        in_specs=[
            pl.BlockSpec((tm, k), lambda i, j: (i, 0)),
            pl.BlockSpec((1, k), lambda i, j: (0, 0)),
            pl.BlockSpec((k, tn), lambda i, j: (0, j)),
        ],
        out_specs=pl.BlockSpec((tm, tn), lambda i, j: (i, j)),
        out_shape=jax.ShapeDtypeStruct((t, n), out_dtype),
        scratch_shapes=[pltpu.VMEM((tm, k), BF16)],
        compiler_params=_params(("parallel", "arbitrary")),
        name="norm_matmul",
    )(x, g, w)


def _norm_kv_kernel(x_ref, g_ref, w_ref, o_ref):
    h = _rmsnorm(x_ref[...], g_ref[...]).astype(BF16)
    z = _dot(h, w_ref[...])
    for c in range(o_ref.shape[0]):
        o_ref[c] = z[:, c * HEAD_DIM:(c + 1) * HEAD_DIM].astype(o_ref.dtype)


def _norm_kv(x, g, w, tm):
    t, k = x.shape
    n = w.shape[1]
    n_slabs = n // HEAD_DIM
    return pl.pallas_call(
        _norm_kv_kernel,
        grid=(t // tm,),
        in_specs=[
            pl.BlockSpec((tm, k), lambda i: (i, 0)),
            pl.BlockSpec((1, k), lambda i: (0, 0)),
            pl.BlockSpec((k, n), lambda i: (0, 0)),
        ],
        out_specs=pl.BlockSpec((n_slabs, tm, HEAD_DIM), lambda i: (0, i, 0)),
        out_shape=jax.ShapeDtypeStruct((n_slabs, t, HEAD_DIM), BF16),
        compiler_params=_params(("parallel",)),
        name="norm_kv",
    )(x, g, w)


def _rnn_kernel(xr_ref, gr_ref, gm_ref, cw_ref, cb_ref, wa_ref, wx_ref, ba_ref, bx_ref, lam_ref,
                o_ref, xbuf, a_scr, b_scr, h_scr, *, ts):
    s_idx = pl.program_id(1)
    d = xr_ref.shape[1]
    halo = SUBLANES

    @pl.when(s_idx == 0)
    def _():
        xbuf[0:halo, :] = jnp.zeros((halo, d), F32)
        h_scr[...] = jnp.zeros(h_scr.shape, F32)

    xbuf[halo:halo + ts, :] = xr_ref[...]
    xc = cb_ref[...] + xbuf[halo:halo + ts, :] * cw_ref[CONV_WIDTH - 1:CONV_WIDTH, :]
    for k in range(1, CONV_WIDTH):
        xc = xc + xbuf[halo - k:halo - k + ts, :] * cw_ref[CONV_WIDTH - 1 - k:CONV_WIDTH - k, :]
    xbuf[0:halo, :] = xbuf[ts:ts + halo, :]

    xcb = xc.astype(BF16)
    n_chunks = d // LANES
    neg_sp = -LRU_C * (jnp.maximum(-lam_ref[...], 0.0) + jnp.log(1.0 + jnp.exp(-jnp.abs(lam_ref[...]))))
    for c in range(n_chunks):
        sl = slice(c * LANES, (c + 1) * LANES)
        xk = xcb[:, sl]
        r = _sigmoid(_dot(xk, wa_ref[c]) + ba_ref[:, sl])
        i = _sigmoid(_dot(xk, wx_ref[c]) + bx_ref[:, sl])
        log_a = r * neg_sp[:, sl]
        a_scr[:, sl] = jnp.exp(log_a)
        b_scr[:, sl] = jnp.sqrt(1.0 - jnp.exp(2.0 * log_a)) * (i * xc[:, sl])

    def group(gi, h):
        base = pl.multiple_of(gi * SUBLANES, SUBLANES)
        a8 = a_scr[pl.ds(base, SUBLANES), :]
        b8 = b_scr[pl.ds(base, SUBLANES), :]
        rows = []
        for j in range(SUBLANES):
            h = a8[j:j + 1, :] * h + b8[j:j + 1, :]
            rows.append(h)
        a_scr[pl.ds(base, SUBLANES), :] = jnp.concatenate(rows, axis=0)
        return h

    h_last = lax.fori_loop(0, ts // SUBLANES, group, h_scr[0:1, :])
    h_scr[0:1, :] = h_last
    o_ref[...] = (_sigmoid(gm_ref[...]) * a_scr[...] * _gelu_tanh(gr_ref[...])).astype(o_ref.dtype)


def _rnn(f, conv_w, conv_b, wa, wx, ba, bx, lam, batch, seq, ts):
    t = f.shape[0]
    d = conv_w.shape[1]
    ns = seq // ts
    nd = d // d
    del nd
    row = lambda b, s: b * ns + s
    vec = lambda r: pl.BlockSpec((r, d), lambda b, s: (0, 0))
    return pl.pallas_call(
        functools.partial(_rnn_kernel, ts=ts),
        grid=(batch, ns),
        in_specs=[
            pl.BlockSpec((ts, d), lambda b, s: (row(b, s), 0)),
            pl.BlockSpec((ts, d), lambda b, s: (row(b, s), 1)),
            pl.BlockSpec((ts, d), lambda b, s: (row(b, s), 2)),
            vec(CONV_WIDTH), vec(1),
            pl.BlockSpec(wa.shape, lambda b, s: (0, 0, 0)),
            pl.BlockSpec(wx.shape, lambda b, s: (0, 0, 0)),
            vec(1), vec(1), vec(1),
        ],
        out_specs=pl.BlockSpec((ts, d), lambda b, s: (row(b, s), 0)),
        out_shape=jax.ShapeDtypeStruct((t, d), F32),
        scratch_shapes=[
            pltpu.VMEM((ts + SUBLANES, d), F32),
            pltpu.VMEM((ts, d), F32),
            pltpu.VMEM((ts, d), F32),
            pltpu.VMEM((SUBLANES, d), F32),
        ],
        compiler_params=_params(("parallel", "arbitrary")),
        name="rnn_mixer",
    )(f, f, f, conv_w, conv_b, wa, wx, ba, bx, lam)


def _compress_kernel(ak_ref, av_ref, w1k_ref, w2k_ref, pk_ref, w1v_ref, w2v_ref, pv_ref, ok_ref, ov_ref):
    def one(a_ref, w1_ref, w2_ref, p_ref, o_ref):
        a = a_ref[0]
        half = a.shape[1]
        nchunk = a.shape[0]
        lo = _dot(a, w1_ref[0:half, :])
        hi = _dot(a, w1_ref[half:2 * half, :])
        pb = _dot(jnp.broadcast_to(p_ref[...], (SUBLANES, 2 * half)).astype(BF16), w1_ref[...])[0:1, :]
        h = lo + pltpu.roll(hi, nchunk - 1, 0) + pb
        o_ref[0] = _dot(_gelu_tanh(h).astype(BF16), w2_ref[...]).astype(o_ref.dtype)

    one(ak_ref, w1k_ref, w2k_ref, pk_ref, ok_ref)
    one(av_ref, w1v_ref, w2v_ref, pv_ref, ov_ref)


def _compress(kv_chunks, w1k, w2k, pk, w1v, w2v, pv, batch, nchunk):
    g = N_KV_GROUPS
    width = kv_chunks.shape[2]
    full = lambda a: pl.BlockSpec(a.shape, lambda b, gg: (0,) * a.ndim)
    out = jax.ShapeDtypeStruct((batch * g, nchunk, HEAD_DIM), BF16)
    return pl.pallas_call(
        _compress_kernel,
        grid=(batch, g),
        in_specs=[
            pl.BlockSpec((1, nchunk, width), lambda b, gg: (gg, b, 0)),
            pl.BlockSpec((1, nchunk, width), lambda b, gg: (g + gg, b, 0)),
            full(w1k), full(w2k), full(pk), full(w1v), full(w2v), full(pv),
        ],
        out_specs=[
            pl.BlockSpec((1, nchunk, HEAD_DIM), lambda b, gg: (b * g + gg, 0, 0)),
            pl.BlockSpec((1, nchunk, HEAD_DIM), lambda b, gg: (b * g + gg, 0, 0)),
        ],
        out_shape=[out, out],
        compiler_params=_params(("parallel", "parallel")),
        name="compress",
    )(kv_chunks, kv_chunks, w1k, w2k, pk, w1v, w2v, pv)


def _softmax_step(k, q_t, v_t, bias, state):
    m_old, l_old, acc_old = state
    s = _dot(k, q_t)
    if bias is not None:
        s = s + bias
    m_new = jnp.maximum(m_old, jnp.max(s, axis=0, keepdims=True))
    alpha = jnp.exp(m_old - m_new)
    e = jnp.exp(s - m_new)
    l_new = alpha * l_old + jnp.sum(e, axis=0, keepdims=True)
    acc_new = alpha * acc_old + _dot(v_t, e.astype(BF16))
    return m_new, l_new, acc_new


def _nsa_kernel(q_ref, kc_ref, vc_ref, ks_ref, vs_ref, kw_ref, vw_ref, gn_ref, gm_ref,
                qal_ref, kpos_ref, cpos_ref, wsel_ref,
                o_ref, q_t, ksaug, kwaug, kcaug, kstage, vstage, flags, *, tq, nk, nkw, seq):
    b = pl.program_id(0)
    g = pl.program_id(1)
    qt = pl.program_id(2)
    r_heads = HEADS_PER_GROUP
    dh = HEAD_DIM
    m_cols = r_heads * tq
    nb = seq // SEL_LEN
    nc = kc_ref.shape[1]
    a_w = 2 * dh

    @pl.when((b == 0) & (g == 0) & (qt == 0))
    def _():
        ksaug[...] = kpos_ref[...]
        kwaug[...] = kpos_ref[:, 0:a_w]
        kcaug[...] = cpos_ref[...]
        vstage[...] = jnp.zeros(vstage.shape, BF16)

    @pl.when(qt == 0)
    def _():
        ksaug[:, 0:dh] = ks_ref[0]
        kwaug[:, 0:dh] = kw_ref[0]
        kcaug[:, 0:dh] = kc_ref[0]
        q_t[dh:a_w, :] = qal_ref[0]

    eye = (lax.broadcasted_iota(jnp.int32, (dh, dh), 0)
           == lax.broadcasted_iota(jnp.int32, (dh, dh), 1)).astype(F32).astype(BF16)
    transposed = lambda v: _dot_nt(eye, v).astype(BF16)

    t0 = qt * tq
    col = lax.broadcasted_iota(jnp.int32, (1, m_cols), 1)
    trow = t0 + (col & (tq - 1))

    scale = np.float32(HEAD_DIM ** -0.5)
    qf = jnp.transpose(q_ref[...].astype(F32) * scale)
    for r in range(r_heads):
        q_t[0:dh, r * tq:(r + 1) * tq] = qf[r * dh:(r + 1) * dh, :].astype(BF16)

    cend = lax.broadcasted_iota(jnp.int32, (nc, 1), 0) * CMP_STRIDE + (CMP_LEN - 1)
    bias_c = jnp.where(cend <= trow, 0.0, MASK_NEG).astype(F32)
    s = _dot(kcaug[...], q_t[0:a_w, :]) + bias_c
    m = jnp.max(s, axis=0, keepdims=True)
    e = jnp.exp(s - m)
    has_key = (trow >= CMP_LEN - 1).astype(F32)
    p = e * (has_key / jnp.sum(e, axis=0, keepdims=True))
    o_cmp = _dot(transposed(vc_ref[0]), p.astype(BF16))
    imp = p[:, 0:tq]
    for r in range(1, r_heads):
        imp = imp + p[:, r * tq:(r + 1) * tq]

    hi = imp.astype(BF16)
    r1 = imp - hi.astype(F32)
    mid = r1.astype(BF16)
    lo = (r1 - mid.astype(F32)).astype(BF16)
    wsel = wsel_ref[...]
    imp_t = _dot(wsel, hi) + _dot(wsel, mid) + _dot(wsel, lo)

    blk = lax.broadcasted_iota(jnp.int32, (nb, tq), 0).astype(F32)
    tq_l = t0 + lax.broadcasted_iota(jnp.int32, (nb, tq), 1)
    cur = (tq_l // SEL_LEN).astype(F32)
    valid = blk <= cur
    forced = (blk == 0.0) | (blk == cur) | (blk == cur - 1.0)
    score = jnp.where(valid, imp_t, -1.0)
    score = jnp.where(forced & valid, FORCE_SCORE, score)
    sel = jnp.zeros((nb, tq), F32)
    for _ in range(min(SEL_TOPK, nb)):
        mx = jnp.max(score, axis=0, keepdims=True)
        idx = jnp.min(jnp.where(score == mx, blk, float(nb)), axis=0, keepdims=True)
        hit = blk == idx
        sel = jnp.where(hit, 1.0, sel)
        score = jnp.where(hit, -2.0, score)
    selv = (sel > 0.0) & valid
    selneg_t = jnp.where(selv, 0.0, MASK_NEG).astype(BF16)
    for r in range(r_heads):
        q_t[a_w:, r * tq:(r + 1) * tq] = selneg_t

    used = jnp.max(jnp.where(selv, 1.0, 0.0), axis=1, keepdims=True)
    bit_id = lax.broadcasted_iota(jnp.int32, (nb, 1), 0) & (WORD_BITS - 1)
    weighted = used * jnp.left_shift(1, bit_id).astype(F32)
    for i in range(nb // WORD_BITS):
        flags[i] = jnp.sum(weighted[i * WORD_BITS:(i + 1) * WORD_BITS, :]).astype(jnp.int32)

    init = (jnp.full((1, m_cols), MASK_NEG, F32), jnp.zeros((1, m_cols), F32), jnp.zeros((dh, m_cols), F32))
    off_q = pl.multiple_of(t0, tq)
    kpos_d = t0 + lax.broadcasted_iota(jnp.int32, (tq, 1), 0)
    bias_d = jnp.where(kpos_d <= trow, 0.0, MASK_NEG).astype(F32)
    state = _softmax_step(ksaug[pl.ds(off_q, tq), :], q_t[...], transposed(vs_ref[0, pl.ds(off_q, tq), :]),
                          bias_d, init)

    def gather(j, cnt):
        bit = (flags[j // WORD_BITS] >> (j % WORD_BITS)) & 1

        @pl.when(bit == 1)
        def _():
            src = pl.multiple_of(j * SEL_LEN, SEL_LEN)
            dst = pl.multiple_of(cnt * SEL_LEN, SEL_LEN)
            kstage[pl.ds(dst, SEL_LEN), :] = ksaug[pl.ds(src, SEL_LEN), :]
            vstage[pl.ds(dst, SEL_LEN), :] = vs_ref[0, pl.ds(src, SEL_LEN), :]

        return cnt + bit

    n_used = lax.fori_loop(0, t0 // SEL_LEN, gather, 0)
    per_tile = nk // SEL_LEN
    n_tiles = (n_used + per_tile - 1) // per_tile
    pad_block = jnp.where(lax.broadcasted_iota(jnp.int32, (SEL_LEN, kstage.shape[1]), 1) == PAD_COL,
                          1.0, 0.0).astype(BF16)

    def pad(j, carry):
        dst = pl.multiple_of(j * SEL_LEN, SEL_LEN)
        kstage[pl.ds(dst, SEL_LEN), :] = pad_block
        return carry

    lax.fori_loop(n_used, n_tiles * per_tile, pad, 0)

    def slc_body(kt, state):
        off = pl.multiple_of(kt * nk, nk)
        return _softmax_step(kstage[pl.ds(off, nk), :], q_t[...], transposed(vstage[pl.ds(off, nk), :]), None, state)

    m_s, l_s, acc_s = lax.fori_loop(0, n_tiles, slc_body, state)
    o_slc = acc_s / l_s

    n_wt = (WINDOW + tq) // nkw
    state = init
    for i in range(n_wt):
        start = t0 + tq - (i + 1) * nkw
        start_c = jnp.maximum(start, 0)
        off_w = pl.multiple_of(start_c, nkw)
        kpos_w = start + lax.broadcasted_iota(jnp.int32, (nkw, 1), 0)
        ok = (kpos_w <= trow) & (kpos_w > trow - WINDOW) & (start >= 0)
        bias_w = jnp.where(ok, 0.0, MASK_NEG).astype(F32)
        state = _softmax_step(kwaug[pl.ds(off_w, nkw), :], q_t[0:a_w, :],
                              transposed(vw_ref[0, pl.ds(off_w, nkw), :]), bias_w, state)
    m_w, l_w, acc_w = state
    o_win = acc_w / l_w

    gates = jnp.transpose(_sigmoid(gn_ref[...]))
    heads = []
    for r in range(r_heads):
        cs = slice(r * tq, (r + 1) * tq)
        heads.append(gates[3 * r:3 * r + 1, :] * o_cmp[:, cs] + gates[3 * r + 1:3 * r + 2, :] * o_slc[:, cs]
                     + gates[3 * r + 2:3 * r + 3, :] * o_win[:, cs])
    o = jnp.transpose(jnp.concatenate(heads, axis=0))
    o_ref[...] = (_sigmoid(gm_ref[...]) * o).astype(o_ref.dtype)


def _alibi_tables(seq, nc, tq):
    import ml_dtypes
    bf = ml_dtypes.bfloat16
    h = np.arange(1, N_HEADS + 1, dtype=np.float32)
    slopes = np.exp2(-ALIBI_MAX_BIAS * h / N_HEADS).astype(np.float32)
    s1 = slopes.astype(bf).astype(np.float32)
    s2 = (slopes - s1).astype(bf).astype(np.float32)
    s3 = (slopes - s1 - s2).astype(bf).astype(np.float32)
    dh = HEAD_DIM
    qal = np.zeros((N_HEADS, dh), np.float32)
    for rep in range(3):
        qal[:, 3 * rep + 0] = s1
        qal[:, 3 * rep + 1] = s2
        qal[:, 3 * rep + 2] = s3
    qal[:, PAD_COL - dh] = MASK_NEG
    qal = qal.reshape(N_KV_GROUPS, HEADS_PER_GROUP, dh).transpose(0, 2, 1)
    qal_p = np.repeat(qal, tq, axis=2)

    nb = seq // SEL_LEN
    pos = np.arange(seq)
    kpos = np.zeros((seq, 2 * dh + nb), np.float32)
    kpos[:, dh:dh + 3] = ((pos // SEL_LEN) * SEL_LEN)[:, None]
    kpos[:, dh + 3:dh + 6] = (pos % SEL_LEN)[:, None]
    kpos[pos, 2 * dh + pos // SEL_LEN] = 1.0

    c = np.arange(nc)
    cpos = np.zeros((nc, 2 * dh), np.float32)
    cpos[:, dh:dh + 3] = ((c // 16) * 16 * CMP_STRIDE)[:, None]
    cpos[:, dh + 3:dh + 6] = ((c % 16) * CMP_STRIDE)[:, None]
    cpos[:, dh + 6:dh + 9] = CMP_LEN - 1

    r_sel = SEL_LEN // CMP_STRIDE
    r_cmp = CMP_LEN // CMP_STRIDE
    wsel = np.zeros((nb, nc), np.float32)
    for j in range(nb):
        for mm in range(r_sel):
            for nn in range(r_cmp):
                ci = r_sel * j + mm - nn
                if 0 <= ci < nc - 1:
                    wsel[j, ci] += 1.0
    as_bf = lambda a: jnp.asarray(a.astype(bf))
    return as_bf(qal_p), as_bf(kpos), as_bf(cpos), as_bf(wsel)


def _nsa(qm, kvm, kcm, vcm, gnm, fm, batch, seq, tq, nk, nkw, gm_col0):
    t = qm.shape[0]
    g = N_KV_GROUPS
    dh = HEAD_DIM
    nqt = seq // tq
    nc = kcm.shape[1]
    nb = seq // SEL_LEN
    gw = HEADS_PER_GROUP * dh
    m_cols = HEADS_PER_GROUP * tq
    qal, kpos, cpos, wsel = _alibi_tables(seq, nc, tq)
    row = lambda b, gg, i: b * nqt + i
    slab = lambda base: pl.BlockSpec((1, seq, dh), lambda b, gg, i: (base + gg, b, 0))
    cmp_slab = pl.BlockSpec((1, nc, dh), lambda b, gg, i: (b * g + gg, 0, 0))
    const = lambda a: pl.BlockSpec(a.shape, lambda b, gg, i: (0,) * a.ndim)
    return pl.pallas_call(
        functools.partial(_nsa_kernel, tq=tq, nk=nk, nkw=nkw, seq=seq),
        grid=(batch, g, nqt),
        in_specs=[
            pl.BlockSpec((tq, gw), lambda b, gg, i: (row(b, gg, i), gg)),
            cmp_slab, cmp_slab,
            slab(2 * g), slab(3 * g), slab(4 * g), slab(5 * g),
            pl.BlockSpec((tq, LANES), lambda b, gg, i: (row(b, gg, i), gg)),
            pl.BlockSpec((tq, gw), lambda b, gg, i: (row(b, gg, i), gm_col0 + gg)),
            pl.BlockSpec((1, dh, m_cols), lambda b, gg, i: (gg, 0, 0)),
            const(kpos), const(cpos), const(wsel),
        ],
        out_specs=pl.BlockSpec((tq, gw), lambda b, gg, i: (row(b, gg, i), gg)),
        out_shape=jax.ShapeDtypeStruct((t, g * gw), F32),
        scratch_shapes=[
            pltpu.VMEM((2 * dh + nb, m_cols), BF16),
            pltpu.VMEM((seq, 2 * dh + nb), BF16),
            pltpu.VMEM((seq, 2 * dh), BF16),
            pltpu.VMEM((nc, 2 * dh), BF16),
            pltpu.VMEM((seq, 2 * dh + nb), BF16),
            pltpu.VMEM((seq, dh), BF16),
            pltpu.SMEM((nb // WORD_BITS,), jnp.int32),
        ],
        compiler_params=_params(("arbitrary", "arbitrary", "arbitrary")),
        name="nsa",
    )(qm, kcm, vcm, kvm, kvm, kvm, kvm, gnm, fm, qal, kpos, cpos, wsel)


def _out_proj_kernel(x_ref, yr_ref, ya_ref, w_ref, g_ref, o_ref):
    y = (yr_ref[...] + ya_ref[...]).astype(BF16)
    o_ref[...] = x_ref[...] + _rmsnorm(_dot(y, w_ref[...]), g_ref[...])


def _out_proj(x, yr, ya, w, g, tm):
    t, d = x.shape
    rows = pl.BlockSpec((tm, d), lambda i: (i, 0))
    return pl.pallas_call(
        _out_proj_kernel,
        grid=(t // tm,),
        in_specs=[rows, rows, rows, pl.BlockSpec((d, d), lambda i: (0, 0)), pl.BlockSpec((1, d), lambda i: (0, 0))],
        out_specs=rows,
        out_shape=jax.ShapeDtypeStruct((t, d), F32),
        compiler_params=_params(("parallel",)),
        name="out_proj",
    )(x, yr, ya, w, g)


def _ffn_kernel(x_ref, gpre_ref, wg_ref, wu_ref, wd_ref, gpost_ref, p_ref, wpg_ref, bpg_ref, wpp_ref,
                o_ref, h_ref, acc_ref):
    j = pl.program_id(1)

    @pl.when(j == 0)
    def _():
        h_ref[...] = _rmsnorm(x_ref[...], gpre_ref[...]).astype(BF16)
        acc_ref[...] = jnp.zeros(acc_ref.shape, F32)

    h = h_ref[...]
    gate = _dot(h, wg_ref[...])
    up = _dot(h, wu_ref[...])
    act = (gate * _sigmoid(gate) * up).astype(BF16)
    acc_ref[...] += _dot(act, wd_ref[...])

    @pl.when(j == pl.num_programs(1) - 1)
    def _():
        x2 = x_ref[...] + _rmsnorm(acc_ref[...], gpost_ref[...])
        gate_p = _sigmoid(_dot(x2.astype(BF16), wpg_ref[...]) + bpg_ref[...])
        o_ref[...] = x2 + gate_p * _dot(p_ref[...].astype(BF16), wpp_ref[...])


def _ffn(x, gpre, wgu, wd, gpost, p, wpg, bpg, wpp, tm, tf):
    t, d = x.shape
    dff = wd.shape[0]
    nf = dff // tf
    dp = p.shape[1]
    rows = pl.BlockSpec((tm, d), lambda i, j: (i, 0))
    vec = pl.BlockSpec((1, d), lambda i, j: (0, 0))
    return pl.pallas_call(
        _ffn_kernel,
        grid=(t // tm, nf),
        in_specs=[
            rows, vec,
            pl.BlockSpec((d, tf), lambda i, j: (0, j)),
            pl.BlockSpec((d, tf), lambda i, j: (0, nf + j)),
            pl.BlockSpec((tf, d), lambda i, j: (j, 0)),
            vec,
            pl.BlockSpec((tm, dp), lambda i, j: (i, 0)),
            pl.BlockSpec((d, d), lambda i, j: (0, 0)),
            vec,
            pl.BlockSpec((dp, d), lambda i, j: (0, 0)),
        ],
        out_specs=rows,
        out_shape=jax.ShapeDtypeStruct((t, d), F32),
        scratch_shapes=[pltpu.VMEM((tm, d), BF16), pltpu.VMEM((tm, d), F32)],
        compiler_params=_params(("parallel", "arbitrary")),
        name="ffn_ple",
    )(x, gpre, wgu, wgu, wd, gpost, p, wpg, bpg, wpp)


def _block_diag_chunks(w):
    n, bs, _ = w.shape
    per = LANES // bs
    w = w.reshape(n // per, per, bs, bs)
    eye = jnp.eye(per, dtype=w.dtype)
    return jnp.einsum('cpij,pq->cpiqj', w, eye).reshape(n // per, LANES, LANES)


def _layer(x, p, norm_mix_pre, norm_mix_post, w_in, conv_w, conv_b, lru_wa, lru_ba, lru_wx, lru_bx,
           lru_lambda, cmp_pos_k, cmp_pos_v, cmp_k_w1, cmp_k_w2, cmp_v_w1, cmp_v_w2, w_out,
           norm_ffn_pre, norm_ffn_post, ffn_w_gate_up, ffn_w_down, ple_w_proj, ple_w_gate, ple_b_gate,
           batch, seq):
    t, d = x.shape
    d_attn = N_HEADS * HEAD_DIM
    d_kv = N_KV_GROUPS * HEAD_DIM
    row2 = lambda v: v.reshape(1, -1)

    o_q = 2 * d
    o_kv = o_q + d_attn
    o_gn = o_kv + 6 * d_kv
    o_gm = o_gn + 3 * N_HEADS
    w_f = jnp.concatenate([w_in[:, 0:o_q], w_in[:, o_gm:o_gm + 2 * d]], axis=1).astype(BF16)
    w_q = w_in[:, o_q:o_kv].astype(BF16)
    w_kv = w_in[:, o_kv:o_gn].astype(BF16)
    per_g = 3 * HEADS_PER_GROUP
    w_gn = w_in[:, o_gn:o_gm].reshape(d, N_KV_GROUPS, per_g)
    w_gn = jnp.pad(w_gn, ((0, 0), (0, 0), (0, LANES - per_g))).reshape(d, N_KV_GROUPS * LANES).astype(BF16)

    g_pre = row2(norm_mix_pre)
    tm = min(512, t)
    fm = _norm_matmul(x, g_pre, w_f, F32, tm, 1024)
    qm = _norm_matmul(x, g_pre, w_q, BF16, tm, d_attn)
    gnm = _norm_matmul(x, g_pre, w_gn, F32, tm, N_KV_GROUPS * LANES)
    kvm = _norm_kv(x, g_pre, w_kv, tm)

    yr = _rnn(fm, conv_w, row2(conv_b), _block_diag_chunks(lru_wa).astype(BF16),
              _block_diag_chunks(lru_wx).astype(BF16), row2(lru_ba), row2(lru_bx), row2(lru_lambda),
              batch, seq, min(512, seq))

    nchunk = seq // CMP_STRIDE
    kv_chunks = kvm[0:2 * N_KV_GROUPS].reshape(2 * N_KV_GROUPS, t // CMP_STRIDE, CMP_STRIDE * HEAD_DIM)
    kcm, vcm = _compress(kv_chunks, cmp_k_w1.astype(BF16), cmp_k_w2.astype(BF16), cmp_pos_k.reshape(1, -1),
                         cmp_v_w1.astype(BF16), cmp_v_w2.astype(BF16), cmp_pos_v.reshape(1, -1), batch, nchunk)

    tq = min(256, seq)
    ya = _nsa(qm, kvm, kcm, vcm, gnm, fm, batch, seq, tq, min(512, seq), min(256, seq),
              gm_col0=(3 * d) // (HEADS_PER_GROUP * HEAD_DIM))

    x1 = _out_proj(x, yr, ya, w_out.astype(BF16), row2(norm_mix_post), tm)
    dff = ffn_w_down.shape[0]
    tf = dff // 2 if (dff // 2) % LANES == 0 else dff
    return _ffn(x1, row2(norm_ffn_pre), ffn_w_gate_up.astype(BF16), ffn_w_down.astype(BF16), row2(norm_ffn_post),
                p, ple_w_gate.astype(BF16), row2(ple_b_gate), ple_w_proj.astype(BF16), tm, tf)


def kernel(x, p, norm_mix_pre, norm_mix_post, w_in, conv_w, conv_b, lru_wa, lru_ba, lru_wx, lru_bx, lru_lambda, cmp_pos_k, cmp_pos_v, cmp_k_w1, cmp_k_w2, cmp_v_w1, cmp_v_w2, w_out, norm_ffn_pre, norm_ffn_post, ffn_w_gate_up, ffn_w_down, ple_w_proj, ple_w_gate, ple_b_gate):
    batch, seq, d = x.shape
    depth = w_in.shape[0]
    xf = x.reshape(batch * seq, d)
    for i in range(depth):
        xf = _layer(xf, p[i].reshape(batch * seq, -1), norm_mix_pre[i], norm_mix_post[i], w_in[i], conv_w[i],
                    conv_b[i], lru_wa[i], lru_ba[i], lru_wx[i], lru_bx[i], lru_lambda[i], cmp_pos_k[i],
                    cmp_pos_v[i], cmp_k_w1[i], cmp_k_w2[i], cmp_v_w1[i], cmp_v_w2[i], w_out[i],
                    norm_ffn_pre[i], norm_ffn_post[i], ffn_w_gate_up[i], ffn_w_down[i], ple_w_proj[i],
                    ple_w_gate[i], ple_b_gate[i], batch, seq)
    return xf.reshape(batch, seq, d)
```

```python
import functools

import numpy as np
import jax
import jax.numpy as jnp
from jax import lax
from jax.experimental import pallas as pl
from jax.experimental.pallas import tpu as pltpu

N_LRU_BLOCKS = 16
CONV_WIDTH = 4
LRU_C = 8.0
N_HEADS = 16
HEAD_DIM = 64
N_KV_GROUPS = 4
HEADS_PER_GROUP = N_HEADS // N_KV_GROUPS
CMP_LEN = 32
CMP_STRIDE = 16
SEL_LEN = 64
SEL_TOPK = 16
WINDOW = 512
FORCE_SCORE = 1e4
ALIBI_MAX_BIAS = 8.0
NORM_EPS = 1e-6

LANES = 128
SUBLANES = 8
VMEM_LIMIT_BYTES = 56 * 1024 * 1024

MASK_NEG = -1e30
N_ALIBI_COLS = 9
PAD_COL = HEAD_DIM + N_ALIBI_COLS
WORD_BITS = 16
WORD_SHIFT = 4
SEL_SHIFT = 6
N_FORCED = 3
BF16 = jnp.bfloat16
F32 = jnp.float32


def _dot(a, b):
    return jnp.dot(a, b, preferred_element_type=F32)


def _dot_nt(a, b):
    return lax.dot_general(a, b, (((1,), (1,)), ((), ())), preferred_element_type=F32)


def _sigmoid(x):
    return 1.0 / (1.0 + jnp.exp(-x))


def _gelu_tanh(x):
    c = np.float32(np.sqrt(2.0 / np.pi))
    return 0.5 * x * (1.0 + jnp.tanh(c * (x + 0.044715 * (x * x * x))))


def _rmsnorm(x, g):
    ms = jnp.mean(x * x, axis=-1, keepdims=True)
    return x * lax.rsqrt(ms + NORM_EPS) * g


def _params(sem):
    return pltpu.CompilerParams(dimension_semantics=sem, vmem_limit_bytes=VMEM_LIMIT_BYTES)


def _norm_matmul_kernel(x_ref, g_ref, w_ref, o_ref, h_ref):
    @pl.when(pl.program_id(1) == 0)
    def _():
        h_ref[...] = _rmsnorm(x_ref[...], g_ref[...]).astype(BF16)

    o_ref[...] = _dot(h_ref[...], w_ref[...]).astype(o_ref.dtype)


def _norm_matmul(x, g, w, out_dtype, tm, tn):
    t, k = x.shape
    n = w.shape[1]
    return pl.pallas_call(
        _norm_matmul_kernel,
        grid=(t // tm, n // tn),
        in_specs=[
            pl.BlockSpec((tm, k), lambda i, j: (i, 0)),
            pl.BlockSpec((1, k), lambda i, j: (0, 0)),
            pl.BlockSpec((k, tn), lambda i, j: (0, j)),
        ],
        out_specs=pl.BlockSpec((tm, tn), lambda i, j: (i, j)),
        out_shape=jax.ShapeDtypeStruct((t, n), out_dtype),
        scratch_shapes=[pltpu.VMEM((tm, k), BF16)],
        compiler_params=_params(("parallel", "arbitrary")),
        name="norm_matmul",
    )(x, g, w)


def _norm_kv_kernel(x_ref, g_ref, w_ref, o_ref):
    h = _rmsnorm(x_ref[...], g_ref[...]).astype(BF16)
    z = _dot(h, w_ref[...])
    for c in range(o_ref.shape[0]):
        o_ref[c] = z[:, c * HEAD_DIM:(c + 1) * HEAD_DIM].astype(o_ref.dtype)


def _norm_kv(x, g, w, tm):
    t, k = x.shape
    n = w.shape[1]
    n_slabs = n // HEAD_DIM
    return pl.pallas_call(
        _norm_kv_kernel,
        grid=(t // tm,),
        in_specs=[
            pl.BlockSpec((tm, k), lambda i: (i, 0)),
            pl.BlockSpec((1, k), lambda i: (0, 0)),
            pl.BlockSpec((k, n), lambda i: (0, 0)),
        ],
        out_specs=pl.BlockSpec((n_slabs, tm, HEAD_DIM), lambda i: (0, i, 0)),
        out_shape=jax.ShapeDtypeStruct((n_slabs, t, HEAD_DIM), BF16),
        compiler_params=_params(("parallel",)),
        name="norm_kv",
    )(x, g, w)


def _rnn_kernel(xr_ref, gr_ref, gm_ref, cw_ref, cb_ref, wa_ref, wx_ref, ba_ref, bx_ref, lam_ref,
                o_ref, xbuf, a_scr, b_scr, h_scr, *, ts):
    s_idx = pl.program_id(1)
    d = xr_ref.shape[1]
    halo = SUBLANES

    @pl.when(s_idx == 0)
    def _():
        xbuf[0:halo, :] = jnp.zeros((halo, d), F32)
        h_scr[...] = jnp.zeros(h_scr.shape, F32)

    xbuf[halo:halo + ts, :] = xr_ref[...]
    xc = cb_ref[...] + xbuf[halo:halo + ts, :] * cw_ref[CONV_WIDTH - 1:CONV_WIDTH, :]
    for k in range(1, CONV_WIDTH):
        xc = xc + xbuf[halo - k:halo - k + ts, :] * cw_ref[CONV_WIDTH - 1 - k:CONV_WIDTH - k, :]
    xbuf[0:halo, :] = xbuf[ts:ts + halo, :]

    xcb = xc.astype(BF16)
    n_chunks = d // LANES
    neg_sp = -LRU_C * (jnp.maximum(-lam_ref[...], 0.0) + jnp.log(1.0 + jnp.exp(-jnp.abs(lam_ref[...]))))
    for c in range(n_chunks):
        sl = slice(c * LANES, (c + 1) * LANES)
        xk = xcb[:, sl]
        r = _sigmoid(_dot(xk, wa_ref[c]) + ba_ref[:, sl])
        i = _sigmoid(_dot(xk, wx_ref[c]) + bx_ref[:, sl])
        log_a = r * neg_sp[:, sl]
        a_scr[:, sl] = jnp.exp(log_a)
        b_scr[:, sl] = jnp.sqrt(1.0 - jnp.exp(2.0 * log_a)) * (i * xc[:, sl])

    def group(gi, h):
        base = pl.multiple_of(gi * SUBLANES, SUBLANES)
        a8 = a_scr[pl.ds(base, SUBLANES), :]
        b8 = b_scr[pl.ds(base, SUBLANES), :]
        rows = []
        for j in range(SUBLANES):
            h = a8[j:j + 1, :] * h + b8[j:j + 1, :]
            rows.append(h)
        a_scr[pl.ds(base, SUBLANES), :] = jnp.concatenate(rows, axis=0)
        return h

    h_last = lax.fori_loop(0, ts // SUBLANES, group, h_scr[0:1, :])
    h_scr[0:1, :] = h_last
    o_ref[...] = (_sigmoid(gm_ref[...]) * a_scr[...] * _gelu_tanh(gr_ref[...])).astype(o_ref.dtype)


def _rnn(f, conv_w, conv_b, wa, wx, ba, bx, lam, batch, seq, ts):
    t = f.shape[0]
    d = conv_w.shape[1]
    ns = seq // ts
    nd = d // d
    del nd
    row = lambda b, s: b * ns + s
    vec = lambda r: pl.BlockSpec((r, d), lambda b, s: (0, 0))
    return pl.pallas_call(
        functools.partial(_rnn_kernel, ts=ts),
        grid=(batch, ns),
        in_specs=[
            pl.BlockSpec((ts, d), lambda b, s: (row(b, s), 0)),
            pl.BlockSpec((ts, d), lambda b, s: (row(b, s), 1)),
            pl.BlockSpec((ts, d), lambda b, s: (row(b, s), 2)),
            vec(CONV_WIDTH), vec(1),
            pl.BlockSpec(wa.shape, lambda b, s: (0, 0, 0)),
            pl.BlockSpec(wx.shape, lambda b, s: (0, 0, 0)),
            vec(1), vec(1), vec(1),
        ],
        out_specs=pl.BlockSpec((ts, d), lambda b, s: (row(b, s), 0)),
        out_shape=jax.ShapeDtypeStruct((t, d), F32),
        scratch_shapes=[
            pltpu.VMEM((ts + SUBLANES, d), F32),
            pltpu.VMEM((ts, d), F32),
            pltpu.VMEM((ts, d), F32),
            pltpu.VMEM((SUBLANES, d), F32),
        ],
        compiler_params=_params(("parallel", "arbitrary")),
        name="rnn_mixer",
    )(f, f, f, conv_w, conv_b, wa, wx, ba, bx, lam)


def _compress_kernel(ak_ref, av_ref, w1k_ref, w2k_ref, pk_ref, w1v_ref, w2v_ref, pv_ref, ok_ref, ov_ref):
    def one(a_ref, w1_ref, w2_ref, p_ref, o_ref):
        a = a_ref[0]
        half = a.shape[1]
        nchunk = a.shape[0]
        lo = _dot(a, w1_ref[0:half, :])
        hi = _dot(a, w1_ref[half:2 * half, :])
        pb = _dot(jnp.broadcast_to(p_ref[...], (SUBLANES, 2 * half)).astype(BF16), w1_ref[...])[0:1, :]
        h = lo + pltpu.roll(hi, nchunk - 1, 0) + pb
        o_ref[0] = _dot(_gelu_tanh(h).astype(BF16), w2_ref[...]).astype(o_ref.dtype)

    one(ak_ref, w1k_ref, w2k_ref, pk_ref, ok_ref)
    one(av_ref, w1v_ref, w2v_ref, pv_ref, ov_ref)


def _compress(kv_chunks, w1k, w2k, pk, w1v, w2v, pv, batch, nchunk):
    g = N_KV_GROUPS
    width = kv_chunks.shape[2]
    full = lambda a: pl.BlockSpec(a.shape, lambda b, gg: (0,) * a.ndim)
    out = jax.ShapeDtypeStruct((batch * g, nchunk, HEAD_DIM), BF16)
    return pl.pallas_call(
        _compress_kernel,
        grid=(batch, g),
        in_specs=[
            pl.BlockSpec((1, nchunk, width), lambda b, gg: (gg, b, 0)),
            pl.BlockSpec((1, nchunk, width), lambda b, gg: (g + gg, b, 0)),
            full(w1k), full(w2k), full(pk), full(w1v), full(w2v), full(pv),
        ],
        out_specs=[
            pl.BlockSpec((1, nchunk, HEAD_DIM), lambda b, gg: (b * g + gg, 0, 0)),
            pl.BlockSpec((1, nchunk, HEAD_DIM), lambda b, gg: (b * g + gg, 0, 0)),
        ],
        out_shape=[out, out],
        compiler_params=_params(("parallel", "parallel")),
        name="compress",
    )(kv_chunks, kv_chunks, w1k, w2k, pk, w1v, w2v, pv)


ONES_ROWS = 16


def _scores(k, q_rows, bias, s_ref):
    s = _dot(k, q_rows)
    if bias is not None:
        s = s + bias
    s_ref[...] = s
    return jnp.max(s, axis=0, keepdims=True)


def _accumulate(s_ref, col_max, v_aug_t, state):
    m_old, acc_old = state
    m_new = jnp.maximum(m_old, col_max)
    alpha = jnp.exp(m_old - m_new)
    e = jnp.exp(s_ref[...] - m_new).astype(BF16)
    return m_new, alpha * acc_old + _dot(v_aug_t, e)


def _nsa_kernel(q_ref, kc_ref, vc_ref, ks_ref, vs_ref, kw_ref, vw_ref, gn_ref, gm_ref,
                qal_ref, cpos_ref, wsel_ref,
                o_ref, q_t, ksaug, kwaug, kcaug, kstage, vstage, s_win, s_a, s_b, flags, *, tq, nk, nkw, seq):
    b = pl.program_id(0)
    g = pl.program_id(1)
    qt = pl.program_id(2)
    r_heads = HEADS_PER_GROUP
    dh = HEAD_DIM
    m_cols = r_heads * tq
    nb = seq // SEL_LEN
    nc = kc_ref.shape[1]
    a_w = 2 * dh
    k_w = a_w + nb

    @pl.when((b == 0) & (g == 0) & (qt == 0))
    def _():
        def fill(c, carry):
            off = pl.multiple_of(c * nk, nk)
            pos = off + lax.broadcasted_iota(jnp.int32, (nk, k_w), 0)
            lane = lax.broadcasted_iota(jnp.int32, (nk, k_w), 1)
            blk_of = jnp.right_shift(pos, SEL_SHIFT)
            cols = jnp.where((lane >= dh) & (lane < dh + 3), blk_of * SEL_LEN,
                             jnp.where((lane >= dh + 3) & (lane < dh + 6), pos & (SEL_LEN - 1),
                                       jnp.where(lane == a_w + blk_of, 1, 0)))
            cols = cols.astype(F32).astype(BF16)
            ksaug[pl.ds(off, nk), :] = cols
            kwaug[pl.ds(off, nk), :] = cols[:, 0:a_w]
            return carry

        lax.fori_loop(0, seq // nk, fill, 0)
        kcaug[...] = cpos_ref[...]
        vstage[...] = jnp.ones(vstage.shape, BF16)

    @pl.when(qt == 0)
    def _():
        ksaug[:, 0:dh] = ks_ref[0]
        kwaug[:, 0:dh] = kw_ref[0]
        kcaug[:, 0:dh] = kc_ref[0]
        q_t[dh:a_w, :] = qal_ref[0]

    v_rows = dh + ONES_ROWS
    ri = lax.broadcasted_iota(jnp.int32, (v_rows, 2 * dh), 0)
    ci = lax.broadcasted_iota(jnp.int32, (v_rows, 2 * dh), 1)
    pick = (((ri < dh) & (ri == ci)) | ((ri >= dh) & (ci == dh))).astype(F32).astype(BF16)
    transposed = lambda v: _dot_nt(pick[0:dh, 0:dh], v).astype(BF16)
    staged_aug_t = lambda v: _dot_nt(pick, v).astype(BF16)
    ones_rows = jnp.ones((ONES_ROWS, nkw), BF16)
    window_aug_t = lambda v: jnp.concatenate([transposed(v), ones_rows], axis=0)
    init = (jnp.full((1, m_cols), MASK_NEG, F32), jnp.zeros((v_rows, m_cols), F32))

    t0 = qt * tq
    col = lax.broadcasted_iota(jnp.int32, (1, m_cols), 1)
    trow = t0 + (col & (tq - 1))

    scale = np.float32(HEAD_DIM ** -0.5)
    qf = jnp.transpose(q_ref[...].astype(F32) * scale)
    for r in range(r_heads):
        q_t[0:dh, r * tq:(r + 1) * tq] = qf[r * dh:(r + 1) * dh, :].astype(BF16)

    cend = lax.broadcasted_iota(jnp.int32, (nc, 1), 0) * CMP_STRIDE + (CMP_LEN - 1)
    bias_c = jnp.where(cend <= trow, 0.0, MASK_NEG).astype(F32)
    s = _dot(kcaug[...], q_t[0:a_w, :]) + bias_c
    m = jnp.max(s, axis=0, keepdims=True)
    e = jnp.exp(s - m)
    has_key = (trow >= CMP_LEN - 1).astype(F32)
    p = e * (has_key / jnp.sum(e, axis=0, keepdims=True))
    o_cmp = _dot(transposed(vc_ref[0]), p.astype(BF16))
    imp = p[:, 0:tq]
    for r in range(1, r_heads):
        imp = imp + p[:, r * tq:(r + 1) * tq]

    hi = imp.astype(BF16)
    r1 = imp - hi.astype(F32)
    mid = r1.astype(BF16)
    lo = (r1 - mid.astype(F32)).astype(BF16)
    wsel = wsel_ref[...]
    imp_t = _dot(wsel, hi) + _dot(wsel, mid) + _dot(wsel, lo)

    n_wt = (WINDOW + tq) // nkw
    win_off, win_max = [], []
    win_s = [s_win.at[i * nkw:(i + 1) * nkw, :] for i in range(n_wt)]
    for i in range(n_wt):
        start = t0 + tq - (i + 1) * nkw
        off_w = pl.multiple_of(jnp.maximum(start, 0), nkw)
        kpos_w = start + lax.broadcasted_iota(jnp.int32, (nkw, 1), 0)
        ok = (kpos_w <= trow) & (kpos_w > trow - WINDOW) & (start >= 0)
        bias_w = jnp.where(ok, 0.0, MASK_NEG).astype(F32)
        win_off.append(off_w)
        win_max.append(_scores(kwaug[pl.ds(off_w, nkw), :], q_t[0:a_w, :], bias_w, win_s[i]))

    blk = lax.broadcasted_iota(jnp.int32, (nb, tq), 0).astype(F32)
    tq_l = t0 + lax.broadcasted_iota(jnp.int32, (nb, tq), 1)
    cur = jnp.right_shift(tq_l, SEL_SHIFT).astype(F32)
    valid = blk <= cur
    forced = (blk == 0.0) | (blk == cur) | (blk == cur - 1.0)
    score = jnp.where(valid & jnp.logical_not(forced), imp_t, -1.0)
    sel = jnp.where(forced, 1.0, 0.0)
    for _ in range(min(SEL_TOPK, nb) - N_FORCED):
        mx = jnp.max(score, axis=0, keepdims=True)
        idx = jnp.min(jnp.where(score == mx, blk, float(nb)), axis=0, keepdims=True)
        hit = blk == idx
        sel = jnp.where(hit, 1.0, sel)
        score = jnp.where(hit, -2.0, score)
    selv = (sel > 0.0) & valid
    selneg_t = jnp.where(selv, 0.0, MASK_NEG).astype(BF16)
    for r in range(r_heads):
        q_t[a_w:, r * tq:(r + 1) * tq] = selneg_t

    used = jnp.max(jnp.where(selv, 1.0, 0.0), axis=1, keepdims=True)
    bit_id = lax.broadcasted_iota(jnp.int32, (nb, 1), 0) & (WORD_BITS - 1)
    weighted = used * jnp.left_shift(1, bit_id).astype(F32)
    for i in range(nb // WORD_BITS):
        flags[i] = jnp.sum(weighted[i * WORD_BITS:(i + 1) * WORD_BITS, :]).astype(jnp.int32)

    n_own = tq // SEL_LEN
    off_q = pl.multiple_of(t0, tq)
    kstage[0:tq, :] = ksaug[pl.ds(off_q, tq), :]
    vstage[0:tq, 0:dh] = vs_ref[0, pl.ds(off_q, tq), :]

    def gather(j, cnt):
        bit = jnp.right_shift(flags[jnp.right_shift(j, WORD_SHIFT)], j & (WORD_BITS - 1)) & 1

        @pl.when(bit == 1)
        def _():
            src = pl.multiple_of(j * SEL_LEN, SEL_LEN)
            dst = pl.multiple_of(cnt * SEL_LEN, SEL_LEN)
            kstage[pl.ds(dst, SEL_LEN), :] = ksaug[pl.ds(src, SEL_LEN), :]
            vstage[pl.ds(dst, SEL_LEN), 0:dh] = vs_ref[0, pl.ds(src, SEL_LEN), :]

        return cnt + bit

    n_blocks = lax.fori_loop(0, jnp.right_shift(t0, SEL_SHIFT), gather, n_own)
    per_tile = nk // SEL_LEN
    n_tiles = (n_blocks + per_tile - 1) // per_tile
    pad_block = jnp.where(lax.broadcasted_iota(jnp.int32, (SEL_LEN, k_w), 1) == PAD_COL, 1.0, 0.0).astype(BF16)

    def pad(j, carry):
        dst = pl.multiple_of(j * SEL_LEN, SEL_LEN)
        kstage[pl.ds(dst, SEL_LEN), :] = pad_block
        return carry

    lax.fori_loop(n_blocks, n_tiles * per_tile, pad, 0)

    srow = lax.broadcasted_iota(jnp.int32, (nk, 1), 0)
    bias_0 = jnp.where((srow >= tq) | (t0 + srow <= trow), 0.0, MASK_NEG).astype(F32)
    max_a = _scores(kstage[0:nk, :], q_t[...], bias_0, s_a)

    state = init
    for i in range(n_wt):
        state = _accumulate(win_s[i], win_max[i], window_aug_t(vw_ref[0, pl.ds(win_off[i], nkw), :]), state)
    o_win = state[1][0:dh] / state[1][dh:dh + 1]

    def tile_scores(i, s_ref):
        return _scores(kstage[pl.ds(pl.multiple_of(i * nk, nk), nk), :], q_t[...], None, s_ref)

    def tile_accumulate(i, s_ref, col_max, state):
        return _accumulate(s_ref, col_max, staged_aug_t(vstage[pl.ds(pl.multiple_of(i * nk, nk), nk), :]), state)

    def slc_pair(j, carry):
        state, max_a = carry
        i = 2 * j
        max_b = tile_scores(i + 1, s_b)
        state = tile_accumulate(i, s_a, max_a, state)
        max_a = tile_scores(i + 2, s_a)
        state = tile_accumulate(i + 1, s_b, max_b, state)
        return state, max_a

    n_pairs = (n_tiles - 1) // 2
    state, max_a = lax.fori_loop(0, n_pairs, slc_pair, (init, max_a))
    i_a = 2 * n_pairs

    def two_left(state):
        max_b = tile_scores(i_a + 1, s_b)
        state = tile_accumulate(i_a, s_a, max_a, state)
        return tile_accumulate(i_a + 1, s_b, max_b, state)

    def one_left(state):
        return tile_accumulate(i_a, s_a, max_a, state)

    _, acc = lax.cond(n_tiles - i_a == 2, two_left, one_left, state)
    o_slc = acc[0:dh] / acc[dh:dh + 1]

    gates = jnp.transpose(_sigmoid(gn_ref[...]))
    heads = []
    for r in range(r_heads):
        cs = slice(r * tq, (r + 1) * tq)
        heads.append(gates[3 * r:3 * r + 1, :] * o_cmp[:, cs] + gates[3 * r + 1:3 * r + 2, :] * o_slc[:, cs]
                     + gates[3 * r + 2:3 * r + 3, :] * o_win[:, cs])
    o = jnp.transpose(jnp.concatenate(heads, axis=0))
    o_ref[...] = (_sigmoid(gm_ref[...]) * o).astype(o_ref.dtype)


def _alibi_tables(seq, nc, tq):
    import ml_dtypes
    bf = ml_dtypes.bfloat16
    h = np.arange(1, N_HEADS + 1, dtype=np.float32)
    slopes = np.exp2(-ALIBI_MAX_BIAS * h / N_HEADS).astype(np.float32)
    s1 = slopes.astype(bf).astype(np.float32)
    s2 = (slopes - s1).astype(bf).astype(np.float32)
    s3 = (slopes - s1 - s2).astype(bf).astype(np.float32)
    dh = HEAD_DIM
    qal = np.zeros((N_HEADS, dh), np.float32)
    for rep in range(3):
        qal[:, 3 * rep + 0] = s1
        qal[:, 3 * rep + 1] = s2
        qal[:, 3 * rep + 2] = s3
    qal[:, PAD_COL - dh] = MASK_NEG
    qal = qal.reshape(N_KV_GROUPS, HEADS_PER_GROUP, dh).transpose(0, 2, 1)
    qal_p = np.repeat(qal, tq, axis=2)

    nb = seq // SEL_LEN
    c = np.arange(nc)
    cpos = np.zeros((nc, 2 * dh), np.float32)
    cpos[:, dh:dh + 3] = ((c // 16) * 16 * CMP_STRIDE)[:, None]
    cpos[:, dh + 3:dh + 6] = ((c % 16) * CMP_STRIDE)[:, None]
    cpos[:, dh + 6:dh + 9] = CMP_LEN - 1

    r_sel = SEL_LEN // CMP_STRIDE
    r_cmp = CMP_LEN // CMP_STRIDE
    wsel = np.zeros((nb, nc), np.float32)
    for j in range(nb):
        for mm in range(r_sel):
            for nn in range(r_cmp):
                ci = r_sel * j + mm - nn
                if 0 <= ci < nc - 1:
                    wsel[j, ci] += 1.0
    as_bf = lambda a: jnp.asarray(a.astype(bf))
    return as_bf(qal_p), as_bf(cpos), as_bf(wsel)


def _nsa(qm, kvm, kcm, vcm, gnm, fm, batch, seq, tq, nk, nkw, gm_col0):
    t = qm.shape[0]
    g = N_KV_GROUPS
    dh = HEAD_DIM
    nqt = seq // tq
    nc = kcm.shape[1]
    nb = seq // SEL_LEN
    gw = HEADS_PER_GROUP * dh
    m_cols = HEADS_PER_GROUP * tq
    qal, cpos, wsel = _alibi_tables(seq, nc, tq)
    n_wt = (WINDOW + tq) // nkw
    row = lambda b, gg, i: b * nqt + i
    slab = lambda base: pl.BlockSpec((1, seq, dh), lambda b, gg, i: (base + gg, b, 0))
    cmp_slab = pl.BlockSpec((1, nc, dh), lambda b, gg, i: (b * g + gg, 0, 0))
    const = lambda a: pl.BlockSpec(a.shape, lambda b, gg, i: (0,) * a.ndim)
    return pl.pallas_call(
        functools.partial(_nsa_kernel, tq=tq, nk=nk, nkw=nkw, seq=seq),
        grid=(batch, g, nqt),
        in_specs=[
            pl.BlockSpec((tq, gw), lambda b, gg, i: (row(b, gg, i), gg)),
            cmp_slab, cmp_slab,
            slab(2 * g), slab(3 * g), slab(4 * g), slab(5 * g),
            pl.BlockSpec((tq, LANES), lambda b, gg, i: (row(b, gg, i), gg)),
            pl.BlockSpec((tq, gw), lambda b, gg, i: (row(b, gg, i), gm_col0 + gg)),
            pl.BlockSpec((1, dh, m_cols), lambda b, gg, i: (gg, 0, 0)),
            const(cpos), const(wsel),
        ],
        out_specs=pl.BlockSpec((tq, gw), lambda b, gg, i: (row(b, gg, i), gg)),
        out_shape=jax.ShapeDtypeStruct((t, g * gw), F32),
        scratch_shapes=[
            pltpu.VMEM((2 * dh + nb, m_cols), BF16),
            pltpu.VMEM((seq, 2 * dh + nb), BF16),
            pltpu.VMEM((seq, 2 * dh), BF16),
            pltpu.VMEM((nc, 2 * dh), BF16),
            pltpu.VMEM((seq, 2 * dh + nb), BF16),
            pltpu.VMEM((seq, 2 * dh), BF16),
            pltpu.VMEM((n_wt * nkw, m_cols), F32),
            pltpu.VMEM((nk, m_cols), F32),
            pltpu.VMEM((nk, m_cols), F32),
            pltpu.SMEM((nb // WORD_BITS,), jnp.int32),
        ],
        compiler_params=_params(("arbitrary", "arbitrary", "arbitrary")),
        name="nsa",
    )(qm, kcm, vcm, kvm, kvm, kvm, kvm, gnm, fm, qal, cpos, wsel)


def _out_proj_kernel(x_ref, yr_ref, ya_ref, w_ref, g_ref, o_ref):
    y = (yr_ref[...] + ya_ref[...]).astype(BF16)
    o_ref[...] = x_ref[...] + _rmsnorm(_dot(y, w_ref[...]), g_ref[...])


def _out_proj(x, yr, ya, w, g, tm):
    t, d = x.shape
    rows = pl.BlockSpec((tm, d), lambda i: (i, 0))
    return pl.pallas_call(
        _out_proj_kernel,
        grid=(t // tm,),
        in_specs=[rows, rows, rows, pl.BlockSpec((d, d), lambda i: (0, 0)), pl.BlockSpec((1, d), lambda i: (0, 0))],
        out_specs=rows,
        out_shape=jax.ShapeDtypeStruct((t, d), F32),
        compiler_params=_params(("parallel",)),
        name="out_proj",
    )(x, yr, ya, w, g)


def _ffn_kernel(x_ref, gpre_ref, wg_ref, wu_ref, wd_ref, gpost_ref, p_ref, wpg_ref, bpg_ref, wpp_ref,
                o_ref, h_ref, acc_ref):
    j = pl.program_id(1)

    @pl.when(j == 0)
    def _():
        h_ref[...] = _rmsnorm(x_ref[...], gpre_ref[...]).astype(BF16)
        acc_ref[...] = jnp.zeros(acc_ref.shape, F32)

    h = h_ref[...]
    gate = _dot(h, wg_ref[...])
    up = _dot(h, wu_ref[...])
    act = (gate * _sigmoid(gate) * up).astype(BF16)
    acc_ref[...] += _dot(act, wd_ref[...])

    @pl.when(j == pl.num_programs(1) - 1)
    def _():
        x2 = x_ref[...] + _rmsnorm(acc_ref[...], gpost_ref[...])
        gate_p = _sigmoid(_dot(x2.astype(BF16), wpg_ref[...]) + bpg_ref[...])
        o_ref[...] = x2 + gate_p * _dot(p_ref[...].astype(BF16), wpp_ref[...])


def _ffn(x, gpre, wgu, wd, gpost, p, wpg, bpg, wpp, tm, tf):
    t, d = x.shape
    dff = wd.shape[0]
    nf = dff // tf
    dp = p.shape[1]
    rows = pl.BlockSpec((tm, d), lambda i, j: (i, 0))
    vec = pl.BlockSpec((1, d), lambda i, j: (0, 0))
    return pl.pallas_call(
        _ffn_kernel,
        grid=(t // tm, nf),
        in_specs=[
            rows, vec,
            pl.BlockSpec((d, tf), lambda i, j: (0, j)),
            pl.BlockSpec((d, tf), lambda i, j: (0, nf + j)),
            pl.BlockSpec((tf, d), lambda i, j: (j, 0)),
            vec,
            pl.BlockSpec((tm, dp), lambda i, j: (i, 0)),
            pl.BlockSpec((d, d), lambda i, j: (0, 0)),
            vec,
            pl.BlockSpec((dp, d), lambda i, j: (0, 0)),
        ],
        out_specs=rows,
        out_shape=jax.ShapeDtypeStruct((t, d), F32),
        scratch_shapes=[pltpu.VMEM((tm, d), BF16), pltpu.VMEM((tm, d), F32)],
        compiler_params=_params(("parallel", "arbitrary")),
        name="ffn_ple",
    )(x, gpre, wgu, wgu, wd, gpost, p, wpg, bpg, wpp)


def _block_diag_chunks(w):
    n, bs, _ = w.shape
    per = LANES // bs
    w = w.reshape(n // per, per, bs, bs)
    eye = jnp.eye(per, dtype=w.dtype)
    return jnp.einsum('cpij,pq->cpiqj', w, eye).reshape(n // per, LANES, LANES)


def _layer(x, p, norm_mix_pre, norm_mix_post, w_in, conv_w, conv_b, lru_wa, lru_ba, lru_wx, lru_bx,
           lru_lambda, cmp_pos_k, cmp_pos_v, cmp_k_w1, cmp_k_w2, cmp_v_w1, cmp_v_w2, w_out,
           norm_ffn_pre, norm_ffn_post, ffn_w_gate_up, ffn_w_down, ple_w_proj, ple_w_gate, ple_b_gate,
           batch, seq):
    t, d = x.shape
    d_attn = N_HEADS * HEAD_DIM
    d_kv = N_KV_GROUPS * HEAD_DIM
    row2 = lambda v: v.reshape(1, -1)

    o_q = 2 * d
    o_kv = o_q + d_attn
    o_gn = o_kv + 6 * d_kv
    o_gm = o_gn + 3 * N_HEADS
    w_f = jnp.concatenate([w_in[:, 0:o_q], w_in[:, o_gm:o_gm + 2 * d]], axis=1).astype(BF16)
    w_q = w_in[:, o_q:o_kv].astype(BF16)
    w_kv = w_in[:, o_kv:o_gn].astype(BF16)
    per_g = 3 * HEADS_PER_GROUP
    w_gn = w_in[:, o_gn:o_gm].reshape(d, N_KV_GROUPS, per_g)
    w_gn = jnp.pad(w_gn, ((0, 0), (0, 0), (0, LANES - per_g))).reshape(d, N_KV_GROUPS * LANES).astype(BF16)

    g_pre = row2(norm_mix_pre)
    tm = min(512, t)
    fm = _norm_matmul(x, g_pre, w_f, F32, tm, 1024)
    qm = _norm_matmul(x, g_pre, w_q, BF16, tm, d_attn)
    gnm = _norm_matmul(x, g_pre, w_gn, F32, tm, N_KV_GROUPS * LANES)
    kvm = _norm_kv(x, g_pre, w_kv, tm)

    yr = _rnn(fm, conv_w, row2(conv_b), _block_diag_chunks(lru_wa).astype(BF16),
              _block_diag_chunks(lru_wx).astype(BF16), row2(lru_ba), row2(lru_bx), row2(lru_lambda),
              batch, seq, min(512, seq))

    nchunk = seq // CMP_STRIDE
    kv_chunks = kvm[0:2 * N_KV_GROUPS].reshape(2 * N_KV_GROUPS, t // CMP_STRIDE, CMP_STRIDE * HEAD_DIM)
    kcm, vcm = _compress(kv_chunks, cmp_k_w1.astype(BF16), cmp_k_w2.astype(BF16), cmp_pos_k.reshape(1, -1),
                         cmp_v_w1.astype(BF16), cmp_v_w2.astype(BF16), cmp_pos_v.reshape(1, -1), batch, nchunk)

    tq = min(256, seq)
    ya = _nsa(qm, kvm, kcm, vcm, gnm, fm, batch, seq, tq, min(512, seq), min(256, seq),
              gm_col0=(3 * d) // (HEADS_PER_GROUP * HEAD_DIM))

    x1 = _out_proj(x, yr, ya, w_out.astype(BF16), row2(norm_mix_post), tm)
    dff = ffn_w_down.shape[0]
    tf = dff // 2 if (dff // 2) % LANES == 0 else dff
    return _ffn(x1, row2(norm_ffn_pre), ffn_w_gate_up.astype(BF16), ffn_w_down.astype(BF16), row2(norm_ffn_post),
                p, ple_w_gate.astype(BF16), row2(ple_b_gate), ple_w_proj.astype(BF16), tm, tf)


def kernel(x, p, norm_mix_pre, norm_mix_post, w_in, conv_w, conv_b, lru_wa, lru_ba, lru_wx, lru_bx, lru_lambda, cmp_pos_k, cmp_pos_v, cmp_k_w1, cmp_k_w2, cmp_v_w1, cmp_v_w2, w_out, norm_ffn_pre, norm_ffn_post, ffn_w_gate_up, ffn_w_down, ple_w_proj, ple_w_gate, ple_b_gate):
    batch, seq, d = x.shape
    depth = w_in.shape[0]
    xf = x.reshape(batch * seq, d)
    for i in range(depth):
        xf = _layer(xf, p[i].reshape(batch * seq, -1), norm_mix_pre[i], norm_mix_post[i], w_in[i], conv_w[i],
                    conv_b[i], lru_wa[i], lru_ba[i], lru_wx[i], lru_bx[i], lru_lambda[i], cmp_pos_k[i],
                    cmp_pos_v[i], cmp_k_w1[i], cmp_k_w2[i], cmp_v_w1[i], cmp_v_w2[i], w_out[i],
                    norm_ffn_pre[i], norm_ffn_post[i], ffn_w_gate_up[i], ffn_w_down[i], ple_w_proj[i],
                    ple_w_gate[i], ple_b_gate[i], batch, seq)
    return xf.reshape(batch, seq, d)
```

```python
import functools

import numpy as np
import jax
import jax.numpy as jnp
from jax import lax
from jax.experimental import pallas as pl
from jax.experimental.pallas import tpu as pltpu

N_LRU_BLOCKS = 16
CONV_WIDTH = 4
LRU_C = 8.0
N_HEADS = 16
HEAD_DIM = 64
N_KV_GROUPS = 4
HEADS_PER_GROUP = N_HEADS // N_KV_GROUPS
CMP_LEN = 32
CMP_STRIDE = 16
SEL_LEN = 64
SEL_TOPK = 16
WINDOW = 512
FORCE_SCORE = 1e4
ALIBI_MAX_BIAS = 8.0
NORM_EPS = 1e-6

LANES = 128
SUBLANES = 8
VMEM_LIMIT_BYTES = 56 * 1024 * 1024

MASK_NEG = -1e30
N_ALIBI_COLS = 9
PAD_COL = HEAD_DIM + N_ALIBI_COLS
WORD_BITS = 16
WORD_SHIFT = 4
SEL_SHIFT = 6
N_FORCED = 3
BF16 = jnp.bfloat16
F32 = jnp.float32


def _dot(a, b):
    return jnp.dot(a, b, preferred_element_type=F32)


def _dot_nt(a, b):
    return lax.dot_general(a, b, (((1,), (1,)), ((), ())), preferred_element_type=F32)


def _sigmoid(x):
    return 1.0 / (1.0 + jnp.exp(-x))


def _gelu_tanh(x):
    c = np.float32(np.sqrt(2.0 / np.pi))
    return 0.5 * x * (1.0 + jnp.tanh(c * (x + 0.044715 * (x * x * x))))


def _rmsnorm(x, g):
    ms = jnp.mean(x * x, axis=-1, keepdims=True)
    return x * lax.rsqrt(ms + NORM_EPS) * g


def _params(sem):
    return pltpu.CompilerParams(dimension_semantics=sem, vmem_limit_bytes=VMEM_LIMIT_BYTES)


def _norm_matmul_kernel(x_ref, g_ref, w_ref, o_ref, h_ref):
    @pl.when(pl.program_id(1) == 0)
    def _():
        h_ref[...] = _rmsnorm(x_ref[...], g_ref[...]).astype(BF16)

    o_ref[...] = _dot(h_ref[...], w_ref[...]).astype(o_ref.dtype)


def _norm_matmul(x, g, w, out_dtype, tm, tn):
    t, k = x.shape
    n = w.shape[1]
    return pl.pallas_call(
        _norm_matmul_kernel,
        grid=(t // tm, n // tn),
        in_specs=[
            pl.BlockSpec((tm, k), lambda i, j: (i, 0)),
            pl.BlockSpec((1, k), lambda i, j: (0, 0)),
            pl.BlockSpec((k, tn), lambda i, j: (0, j)),
        ],
        out_specs=pl.BlockSpec((tm, tn), lambda i, j: (i, j)),
        out_shape=jax.ShapeDtypeStruct((t, n), out_dtype),
        scratch_shapes=[pltpu.VMEM((tm, k), BF16)],
        compiler_params=_params(("parallel", "arbitrary")),
        name="norm_matmul",
    )(x, g, w)


def _norm_qkv_kernel(x_ref, g_ref, w_ref, q_ref, gn_ref, kv_ref):
    h = _rmsnorm(x_ref[...], g_ref[...]).astype(BF16)
    z = _dot(h, w_ref[...])
    nq = q_ref.shape[1]
    ngn = gn_ref.shape[1]
    q_ref[...] = z[:, 0:nq].astype(q_ref.dtype)
    gn_ref[...] = z[:, nq:nq + ngn]
    for c in range(kv_ref.shape[0]):
        lo = nq + ngn + c * HEAD_DIM
        kv_ref[c] = z[:, lo:lo + HEAD_DIM].astype(kv_ref.dtype)


def _norm_qkv(x, g, w, nq, ngn, tm):
    t, k = x.shape
    n = w.shape[1]
    n_slabs = (n - nq - ngn) // HEAD_DIM
    return pl.pallas_call(
        _norm_qkv_kernel,
        grid=(t // tm,),
        in_specs=[
            pl.BlockSpec((tm, k), lambda i: (i, 0)),
            pl.BlockSpec((1, k), lambda i: (0, 0)),
            pl.BlockSpec((k, n), lambda i: (0, 0)),
        ],
        out_specs=[
            pl.BlockSpec((tm, nq), lambda i: (i, 0)),
            pl.BlockSpec((tm, ngn), lambda i: (i, 0)),
            pl.BlockSpec((n_slabs, tm, HEAD_DIM), lambda i: (0, i, 0)),
        ],
        out_shape=[
            jax.ShapeDtypeStruct((t, nq), BF16),
            jax.ShapeDtypeStruct((t, ngn), F32),
            jax.ShapeDtypeStruct((n_slabs, t, HEAD_DIM), BF16),
        ],
        compiler_params=_params(("parallel",)),
        name="norm_qkv",
    )(x, g, w)


def _rnn_kernel(xr_ref, gr_ref, gm_ref, cw_ref, cb_ref, wa_ref, wx_ref, ba_ref, bx_ref, lam_ref,
                o_ref, xbuf, a_scr, b_scr, h_scr, *, ts):
    s_idx = pl.program_id(1)
    d = xr_ref.shape[1]
    halo = SUBLANES

    @pl.when(s_idx == 0)
    def _():
        xbuf[0:halo, :] = jnp.zeros((halo, d), F32)
        h_scr[...] = jnp.zeros(h_scr.shape, F32)

    xbuf[halo:halo + ts, :] = xr_ref[...].astype(F32)
    xc = cb_ref[...] + xbuf[halo:halo + ts, :] * cw_ref[CONV_WIDTH - 1:CONV_WIDTH, :]
    for k in range(1, CONV_WIDTH):
        xc = xc + xbuf[halo - k:halo - k + ts, :] * cw_ref[CONV_WIDTH - 1 - k:CONV_WIDTH - k, :]
    xbuf[0:halo, :] = xbuf[ts:ts + halo, :]

    xcb = xc.astype(BF16)
    n_chunks = d // LANES
    neg_sp = -LRU_C * (jnp.maximum(-lam_ref[...], 0.0) + jnp.log(1.0 + jnp.exp(-jnp.abs(lam_ref[...]))))
    for c in range(n_chunks):
        sl = slice(c * LANES, (c + 1) * LANES)
        xk = xcb[:, sl]
        r = _sigmoid(_dot(xk, wa_ref[c]) + ba_ref[:, sl])
        i = _sigmoid(_dot(xk, wx_ref[c]) + bx_ref[:, sl])
        log_a = r * neg_sp[:, sl]
        a = jnp.exp(log_a)
        a_scr[:, sl] = a
        b_scr[:, sl] = jnp.sqrt(1.0 - a * a) * (i * xc[:, sl])

    def group(gi, h):
        base = pl.multiple_of(gi * SUBLANES, SUBLANES)
        a8 = a_scr[pl.ds(base, SUBLANES), :]
        b8 = b_scr[pl.ds(base, SUBLANES), :]
        rows = []
        for j in range(SUBLANES):
            h = a8[j:j + 1, :] * h + b8[j:j + 1, :]
            rows.append(h)
        a_scr[pl.ds(base, SUBLANES), :] = jnp.concatenate(rows, axis=0)
        return h

    h_last = lax.fori_loop(0, ts // SUBLANES, group, h_scr[0:1, :])
    h_scr[0:1, :] = h_last
    o_ref[...] = (_sigmoid(gm_ref[...].astype(F32)) * a_scr[...]
                  * _gelu_tanh(gr_ref[...].astype(F32))).astype(o_ref.dtype)


def _rnn(f, conv_w, conv_b, wa, wx, ba, bx, lam, batch, seq, ts):
    t = f.shape[0]
    d = conv_w.shape[1]
    ns = seq // ts
    nd = d // d
    del nd
    row = lambda b, s: b * ns + s
    vec = lambda r: pl.BlockSpec((r, d), lambda b, s: (0, 0))
    return pl.pallas_call(
        functools.partial(_rnn_kernel, ts=ts),
        grid=(batch, ns),
        in_specs=[
            pl.BlockSpec((ts, d), lambda b, s: (row(b, s), 0)),
            pl.BlockSpec((ts, d), lambda b, s: (row(b, s), 1)),
            pl.BlockSpec((ts, d), lambda b, s: (row(b, s), 2)),
            vec(CONV_WIDTH), vec(1),
            pl.BlockSpec(wa.shape, lambda b, s: (0, 0, 0)),
            pl.BlockSpec(wx.shape, lambda b, s: (0, 0, 0)),
            vec(1), vec(1), vec(1),
        ],
        out_specs=pl.BlockSpec((ts, d), lambda b, s: (row(b, s), 0)),
        out_shape=jax.ShapeDtypeStruct((t, d), BF16),
        scratch_shapes=[
            pltpu.VMEM((ts + SUBLANES, d), F32),
            pltpu.VMEM((ts, d), F32),
            pltpu.VMEM((ts, d), F32),
            pltpu.VMEM((SUBLANES, d), F32),
        ],
        compiler_params=_params(("parallel", "arbitrary")),
        name="rnn_mixer",
    )(f, f, f, conv_w, conv_b, wa, wx, ba, bx, lam)


def _compress_kernel(ak_ref, av_ref, w1k_ref, w2k_ref, pk_ref, w1v_ref, w2v_ref, pv_ref, ok_ref, ov_ref):
    def one(a_ref, w1_ref, w2_ref, p_ref, o_ref):
        a = a_ref[0]
        half = a.shape[1]
        nchunk = a.shape[0]
        lo = _dot(a, w1_ref[0:half, :])
        hi = _dot(a, w1_ref[half:2 * half, :])
        pb = _dot(jnp.broadcast_to(p_ref[...], (SUBLANES, 2 * half)).astype(BF16), w1_ref[...])[0:1, :]
        h = lo + pltpu.roll(hi, nchunk - 1, 0) + pb
        o_ref[0] = _dot(_gelu_tanh(h).astype(BF16), w2_ref[...]).astype(o_ref.dtype)

    one(ak_ref, w1k_ref, w2k_ref, pk_ref, ok_ref)
    one(av_ref, w1v_ref, w2v_ref, pv_ref, ov_ref)


def _compress(kv_chunks, w1k, w2k, pk, w1v, w2v, pv, batch, nchunk):
    g = N_KV_GROUPS
    width = kv_chunks.shape[2]
    full = lambda a: pl.BlockSpec(a.shape, lambda b, gg: (0,) * a.ndim)
    out = jax.ShapeDtypeStruct((batch * g, nchunk, HEAD_DIM), BF16)
    return pl.pallas_call(
        _compress_kernel,
        grid=(batch, g),
        in_specs=[
            pl.BlockSpec((1, nchunk, width), lambda b, gg: (gg, b, 0)),
            pl.BlockSpec((1, nchunk, width), lambda b, gg: (g + gg, b, 0)),
            full(w1k), full(w2k), full(pk), full(w1v), full(w2v), full(pv),
        ],
        out_specs=[
            pl.BlockSpec((1, nchunk, HEAD_DIM), lambda b, gg: (b * g + gg, 0, 0)),
            pl.BlockSpec((1, nchunk, HEAD_DIM), lambda b, gg: (b * g + gg, 0, 0)),
        ],
        out_shape=[out, out],
        compiler_params=_params(("parallel", "parallel")),
        name="compress",
    )(kv_chunks, kv_chunks, w1k, w2k, pk, w1v, w2v, pv)


ONES_ROWS = 16


def _scores(k, q_rows, bias, s_ref):
    s = _dot(k, q_rows)
    if bias is not None:
        s = s + bias
    s_ref[...] = s
    return jnp.max(s, axis=0, keepdims=True)


def _accumulate(s_ref, col_max, v_aug_t, state):
    m_old, acc_old = state
    m_new = jnp.maximum(m_old, col_max)
    alpha = jnp.exp(m_old - m_new)
    e = jnp.exp(s_ref[...] - m_new).astype(BF16)
    return m_new, alpha * acc_old + _dot(v_aug_t, e)


def _nsa_kernel(q_ref, kc_ref, vc_ref, ks_ref, vs_ref, kw_ref, vw_ref, gn_ref, gm_ref,
                qal_ref, cpos_ref, wsel_ref,
                o_ref, q_t, ksaug, kwaug, kcaug, kstage, vstage, s_win, s_a, s_b, flags, *, tq, nk, nkw, seq):
    b = pl.program_id(0)
    g = pl.program_id(1)
    qt = pl.program_id(2)
    r_heads = HEADS_PER_GROUP
    dh = HEAD_DIM
    m_cols = r_heads * tq
    nb = seq // SEL_LEN
    nc = kc_ref.shape[1]
    a_w = 2 * dh
    k_w = a_w + nb

    @pl.when((b == 0) & (g == 0) & (qt == 0))
    def _():
        def fill(c, carry):
            off = pl.multiple_of(c * nk, nk)
            pos = off + lax.broadcasted_iota(jnp.int32, (nk, k_w), 0)
            lane = lax.broadcasted_iota(jnp.int32, (nk, k_w), 1)
            blk_of = jnp.right_shift(pos, SEL_SHIFT)
            cols = jnp.where((lane >= dh) & (lane < dh + 3), blk_of * SEL_LEN,
                             jnp.where((lane >= dh + 3) & (lane < dh + 6), pos & (SEL_LEN - 1),
                                       jnp.where(lane == a_w + blk_of, 1, 0)))
            cols = cols.astype(F32).astype(BF16)
            ksaug[pl.ds(off, nk), :] = cols
            kwaug[pl.ds(off, nk), :] = cols[:, 0:a_w]
            return carry

        lax.fori_loop(0, seq // nk, fill, 0)
        kcaug[...] = cpos_ref[...]
        vstage[...] = jnp.ones(vstage.shape, BF16)

    @pl.when(qt == 0)
    def _():
        ksaug[:, 0:dh] = ks_ref[0]
        kwaug[:, 0:dh] = kw_ref[0]
        kcaug[:, 0:dh] = kc_ref[0]
        q_t[dh:a_w, :] = qal_ref[0]

    v_rows = dh + ONES_ROWS
    ri = lax.broadcasted_iota(jnp.int32, (v_rows, 2 * dh), 0)
    ci = lax.broadcasted_iota(jnp.int32, (v_rows, 2 * dh), 1)
    pick = (((ri < dh) & (ri == ci)) | ((ri >= dh) & (ci == dh))).astype(F32).astype(BF16)
    transposed = lambda v: _dot_nt(pick[0:dh, 0:dh], v).astype(BF16)
    staged_aug_t = lambda v: _dot_nt(pick, v).astype(BF16)
    ones_rows = jnp.ones((ONES_ROWS, nkw), BF16)
    window_aug_t = lambda v: jnp.concatenate([transposed(v), ones_rows], axis=0)
    init = (jnp.full((1, m_cols), MASK_NEG, F32), jnp.zeros((v_rows, m_cols), F32))

    t0 = qt * tq
    col = lax.broadcasted_iota(jnp.int32, (1, m_cols), 1)
    trow = t0 + (col & (tq - 1))

    scale = np.float32(HEAD_DIM ** -0.5)
    qf = jnp.transpose(q_ref[...].astype(F32) * scale)
    for r in range(r_heads):
        q_t[0:dh, r * tq:(r + 1) * tq] = qf[r * dh:(r + 1) * dh, :].astype(BF16)

    cend = lax.broadcasted_iota(jnp.int32, (nc, 1), 0) * CMP_STRIDE + (CMP_LEN - 1)
    bias_c = jnp.where(cend <= trow, 0.0, MASK_NEG).astype(F32)
    s = _dot(kcaug[...], q_t[0:a_w, :]) + bias_c
    m = jnp.max(s, axis=0, keepdims=True)
    e = jnp.exp(s - m)
    has_key = (trow >= CMP_LEN - 1).astype(F32)
    p = e * (has_key / jnp.sum(e, axis=0, keepdims=True))
    o_cmp = _dot(transposed(vc_ref[0]), p.astype(BF16))
    imp = p[:, 0:tq]
    for r in range(1, r_heads):
        imp = imp + p[:, r * tq:(r + 1) * tq]

    hi = imp.astype(BF16)
    r1 = imp - hi.astype(F32)
    mid = r1.astype(BF16)
    lo = (r1 - mid.astype(F32)).astype(BF16)
    wsel = wsel_ref[...]
    imp_t = _dot(wsel, hi) + _dot(wsel, mid) + _dot(wsel, lo)

    n_wt = (WINDOW + tq) // nkw
    win_off, win_max = [], []
    win_s = [s_win.at[i * nkw:(i + 1) * nkw, :] for i in range(n_wt)]
    for i in range(n_wt):
        start = t0 + tq - (i + 1) * nkw
        off_w = pl.multiple_of(jnp.maximum(start, 0), nkw)
        kpos_w = start + lax.broadcasted_iota(jnp.int32, (nkw, 1), 0)
        ok = (kpos_w <= trow) & (kpos_w > trow - WINDOW) & (start >= 0)
        bias_w = jnp.where(ok, 0.0, MASK_NEG).astype(F32)
        win_off.append(off_w)
        win_max.append(_scores(kwaug[pl.ds(off_w, nkw), :], q_t[0:a_w, :], bias_w, win_s[i]))

    blk = lax.broadcasted_iota(jnp.int32, (nb, tq), 0).astype(F32)
    tq_l = t0 + lax.broadcasted_iota(jnp.int32, (nb, tq), 1)
    cur = jnp.right_shift(tq_l, SEL_SHIFT).astype(F32)
    valid = blk <= cur
    forced = (blk == 0.0) | (blk == cur) | (blk == cur - 1.0)
    score = jnp.where(valid & jnp.logical_not(forced), imp_t, -1.0)
    sel = jnp.where(forced, 1.0, 0.0)
    for _ in range(min(SEL_TOPK, nb) - N_FORCED):
        mx = jnp.max(score, axis=0, keepdims=True)
        idx = jnp.min(jnp.where(score == mx, blk, float(nb)), axis=0, keepdims=True)
        hit = blk == idx
        sel = jnp.where(hit, 1.0, sel)
        score = jnp.where(hit, -2.0, score)
    selv = (sel > 0.0) & valid
    selneg_t = jnp.where(selv, 0.0, MASK_NEG).astype(BF16)
    for r in range(r_heads):
        q_t[a_w:, r * tq:(r + 1) * tq] = selneg_t

    used = jnp.max(jnp.where(selv, 1.0, 0.0), axis=1, keepdims=True)
    bit_id = lax.broadcasted_iota(jnp.int32, (nb, 1), 0) & (WORD_BITS - 1)
    weighted = used * jnp.left_shift(1, bit_id).astype(F32)
    for i in range(nb // WORD_BITS):
        flags[i] = jnp.sum(weighted[i * WORD_BITS:(i + 1) * WORD_BITS, :]).astype(jnp.int32)

    n_own = tq // SEL_LEN
    off_q = pl.multiple_of(t0, tq)
    kstage[0:tq, :] = ksaug[pl.ds(off_q, tq), :]
    vstage[0:tq, 0:dh] = vs_ref[0, pl.ds(off_q, tq), :]

    def gather(j, cnt):
        bit = jnp.right_shift(flags[jnp.right_shift(j, WORD_SHIFT)], j & (WORD_BITS - 1)) & 1

        @pl.when(bit == 1)
        def _():
            src = pl.multiple_of(j * SEL_LEN, SEL_LEN)
            dst = pl.multiple_of(cnt * SEL_LEN, SEL_LEN)
            kstage[pl.ds(dst, SEL_LEN), :] = ksaug[pl.ds(src, SEL_LEN), :]
            vstage[pl.ds(dst, SEL_LEN), 0:dh] = vs_ref[0, pl.ds(src, SEL_LEN), :]

        return cnt + bit

    n_blocks = lax.fori_loop(0, jnp.right_shift(t0, SEL_SHIFT), gather, n_own)
    per_tile = nk // SEL_LEN
    n_tiles = (n_blocks + per_tile - 1) // per_tile
    pad_block = jnp.where(lax.broadcasted_iota(jnp.int32, (SEL_LEN, k_w), 1) == PAD_COL, 1.0, 0.0).astype(BF16)

    def pad(j, carry):
        dst = pl.multiple_of(j * SEL_LEN, SEL_LEN)
        kstage[pl.ds(dst, SEL_LEN), :] = pad_block
        return carry

    lax.fori_loop(n_blocks, n_tiles * per_tile, pad, 0)

    srow = lax.broadcasted_iota(jnp.int32, (nk, 1), 0)
    bias_0 = jnp.where((srow >= tq) | (t0 + srow <= trow), 0.0, MASK_NEG).astype(F32)
    max_a = _scores(kstage[0:nk, :], q_t[...], bias_0, s_a)

    state = init
    for i in range(n_wt):
        state = _accumulate(win_s[i], win_max[i], window_aug_t(vw_ref[0, pl.ds(win_off[i], nkw), :]), state)
    o_win = state[1][0:dh] / state[1][dh:dh + 1]

    def tile_scores(i, s_ref):
        return _scores(kstage[pl.ds(pl.multiple_of(i * nk, nk), nk), :], q_t[...], None, s_ref)

    def tile_accumulate(i, s_ref, col_max, state):
        return _accumulate(s_ref, col_max, staged_aug_t(vstage[pl.ds(pl.multiple_of(i * nk, nk), nk), :]), state)

    def slc_pair(j, carry):
        state, max_a = carry
        i = 2 * j
        max_b = tile_scores(i + 1, s_b)
        state = tile_accumulate(i, s_a, max_a, state)
        max_a = tile_scores(i + 2, s_a)
        state = tile_accumulate(i + 1, s_b, max_b, state)
        return state, max_a

    n_pairs = (n_tiles - 1) // 2
    state, max_a = lax.fori_loop(0, n_pairs, slc_pair, (init, max_a))
    i_a = 2 * n_pairs

    def two_left(state):
        max_b = tile_scores(i_a + 1, s_b)
        state = tile_accumulate(i_a, s_a, max_a, state)
        return tile_accumulate(i_a + 1, s_b, max_b, state)

    def one_left(state):
        return tile_accumulate(i_a, s_a, max_a, state)

    _, acc = lax.cond(n_tiles - i_a == 2, two_left, one_left, state)
    o_slc = acc[0:dh] / acc[dh:dh + 1]

    gates = jnp.transpose(_sigmoid(gn_ref[...]))
    heads = []
    for r in range(r_heads):
        cs = slice(r * tq, (r + 1) * tq)
        heads.append(gates[3 * r:3 * r + 1, :] * o_cmp[:, cs] + gates[3 * r + 1:3 * r + 2, :] * o_slc[:, cs]
                     + gates[3 * r + 2:3 * r + 3, :] * o_win[:, cs])
    o = jnp.transpose(jnp.concatenate(heads, axis=0))
    o_ref[...] = (_sigmoid(gm_ref[...].astype(F32)) * o).astype(o_ref.dtype)


def _alibi_tables(seq, nc, tq):
    import ml_dtypes
    bf = ml_dtypes.bfloat16
    h = np.arange(1, N_HEADS + 1, dtype=np.float32)
    slopes = np.exp2(-ALIBI_MAX_BIAS * h / N_HEADS).astype(np.float32)
    s1 = slopes.astype(bf).astype(np.float32)
    s2 = (slopes - s1).astype(bf).astype(np.float32)
    s3 = (slopes - s1 - s2).astype(bf).astype(np.float32)
    dh = HEAD_DIM
    qal = np.zeros((N_HEADS, dh), np.float32)
    for rep in range(3):
        qal[:, 3 * rep + 0] = s1
        qal[:, 3 * rep + 1] = s2
        qal[:, 3 * rep + 2] = s3
    qal[:, PAD_COL - dh] = MASK_NEG
    qal = qal.reshape(N_KV_GROUPS, HEADS_PER_GROUP, dh).transpose(0, 2, 1)
    qal_p = np.repeat(qal, tq, axis=2)

    nb = seq // SEL_LEN
    c = np.arange(nc)
    cpos = np.zeros((nc, 2 * dh), np.float32)
    cpos[:, dh:dh + 3] = ((c // 16) * 16 * CMP_STRIDE)[:, None]
    cpos[:, dh + 3:dh + 6] = ((c % 16) * CMP_STRIDE)[:, None]
    cpos[:, dh + 6:dh + 9] = CMP_LEN - 1

    r_sel = SEL_LEN // CMP_STRIDE
    r_cmp = CMP_LEN // CMP_STRIDE
    wsel = np.zeros((nb, nc), np.float32)
    for j in range(nb):
        for mm in range(r_sel):
            for nn in range(r_cmp):
                ci = r_sel * j + mm - nn
                if 0 <= ci < nc - 1:
                    wsel[j, ci] += 1.0
    as_bf = lambda a: jnp.asarray(a.astype(bf))
    return as_bf(qal_p), as_bf(cpos), as_bf(wsel)


def _nsa(qm, kvm, kcm, vcm, gnm, fm, batch, seq, tq, nk, nkw, gm_col0):
    t = qm.shape[0]
    g = N_KV_GROUPS
    dh = HEAD_DIM
    nqt = seq // tq
    nc = kcm.shape[1]
    nb = seq // SEL_LEN
    gw = HEADS_PER_GROUP * dh
    m_cols = HEADS_PER_GROUP * tq
    qal, cpos, wsel = _alibi_tables(seq, nc, tq)
    n_wt = (WINDOW + tq) // nkw
    row = lambda b, gg, i: b * nqt + i
    slab = lambda base: pl.BlockSpec((1, seq, dh), lambda b, gg, i: (base + gg, b, 0))
    cmp_slab = pl.BlockSpec((1, nc, dh), lambda b, gg, i: (b * g + gg, 0, 0))
    const = lambda a: pl.BlockSpec(a.shape, lambda b, gg, i: (0,) * a.ndim)
    return pl.pallas_call(
        functools.partial(_nsa_kernel, tq=tq, nk=nk, nkw=nkw, seq=seq),
        grid=(batch, g, nqt),
        in_specs=[
            pl.BlockSpec((tq, gw), lambda b, gg, i: (row(b, gg, i), gg)),
            cmp_slab, cmp_slab,
            slab(2 * g), slab(3 * g), slab(4 * g), slab(5 * g),
            pl.BlockSpec((tq, LANES), lambda b, gg, i: (row(b, gg, i), gg)),
            pl.BlockSpec((tq, gw), lambda b, gg, i: (row(b, gg, i), gm_col0 + gg)),
            pl.BlockSpec((1, dh, m_cols), lambda b, gg, i: (gg, 0, 0)),
            const(cpos), const(wsel),
        ],
        out_specs=pl.BlockSpec((tq, gw), lambda b, gg, i: (row(b, gg, i), gg)),
        out_shape=jax.ShapeDtypeStruct((t, g * gw), BF16),
        scratch_shapes=[
            pltpu.VMEM((2 * dh + nb, m_cols), BF16),
            pltpu.VMEM((seq, 2 * dh + nb), BF16),
            pltpu.VMEM((seq, 2 * dh), BF16),
            pltpu.VMEM((nc, 2 * dh), BF16),
            pltpu.VMEM((seq, 2 * dh + nb), BF16),
            pltpu.VMEM((seq, 2 * dh), BF16),
            pltpu.VMEM((n_wt * nkw, m_cols), F32),
            pltpu.VMEM((nk, m_cols), F32),
            pltpu.VMEM((nk, m_cols), F32),
            pltpu.SMEM((nb // WORD_BITS,), jnp.int32),
        ],
        compiler_params=_params(("arbitrary", "arbitrary", "arbitrary")),
        name="nsa",
    )(qm, kcm, vcm, kvm, kvm, kvm, kvm, gnm, fm, qal, cpos, wsel)


def _ffn_kernel(x_ref, yr_ref, ya_ref, wo_ref, gmix_ref, gpre_ref, wg_ref, wu_ref, wd_ref, gpost_ref,
                p_ref, wpg_ref, bpg_ref, wpp_ref, o_ref, x1_ref, h_ref, acc_ref):
    j = pl.program_id(1)

    @pl.when(j == 0)
    def _():
        y = (yr_ref[...].astype(F32) + ya_ref[...].astype(F32)).astype(BF16)
        x1 = x_ref[...] + _rmsnorm(_dot(y, wo_ref[...]), gmix_ref[...])
        x1_ref[...] = x1
        h_ref[...] = _rmsnorm(x1, gpre_ref[...]).astype(BF16)
        acc_ref[...] = jnp.zeros(acc_ref.shape, F32)

    h = h_ref[...]
    gate = _dot(h, wg_ref[...])
    up = _dot(h, wu_ref[...])
    act = (gate * _sigmoid(gate) * up).astype(BF16)
    acc_ref[...] += _dot(act, wd_ref[...])

    @pl.when(j == pl.num_programs(1) - 1)
    def _():
        x2 = x1_ref[...] + _rmsnorm(acc_ref[...], gpost_ref[...])
        gate_p = _sigmoid(_dot(x2.astype(BF16), wpg_ref[...]) + bpg_ref[...])
        o_ref[...] = x2 + gate_p * _dot(p_ref[...].astype(BF16), wpp_ref[...])


def _ffn(x, yr, ya, wo, gmix, gpre, wgu, wd, gpost, p, wpg, bpg, wpp, tm, tf):
    t, d = x.shape
    dff = wd.shape[0]
    nf = dff // tf
    dp = p.shape[1]
    rows = pl.BlockSpec((tm, d), lambda i, j: (i, 0))
    vec = pl.BlockSpec((1, d), lambda i, j: (0, 0))
    square = pl.BlockSpec((d, d), lambda i, j: (0, 0))
    return pl.pallas_call(
        _ffn_kernel,
        grid=(t // tm, nf),
        in_specs=[
            rows, rows, rows, square, vec, vec,
            pl.BlockSpec((d, tf), lambda i, j: (0, j)),
            pl.BlockSpec((d, tf), lambda i, j: (0, nf + j)),
            pl.BlockSpec((tf, d), lambda i, j: (j, 0)),
            vec,
            pl.BlockSpec((tm, dp), lambda i, j: (i, 0)),
            square,
            vec,
            pl.BlockSpec((dp, d), lambda i, j: (0, 0)),
        ],
        out_specs=rows,
        out_shape=jax.ShapeDtypeStruct((t, d), F32),
        scratch_shapes=[pltpu.VMEM((tm, d), F32), pltpu.VMEM((tm, d), BF16), pltpu.VMEM((tm, d), F32)],
        compiler_params=_params(("parallel", "arbitrary")),
        name="out_ffn_ple",
    )(x, yr, ya, wo, gmix, gpre, wgu, wgu, wd, gpost, p, wpg, bpg, wpp)


def _block_diag_chunks(w):
    n, bs, _ = w.shape
    per = LANES // bs
    w = w.reshape(n // per, per, bs, bs)
    eye = jnp.eye(per, dtype=w.dtype)
    return jnp.einsum('cpij,pq->cpiqj', w, eye).reshape(n // per, LANES, LANES)


def _layer(x, p, norm_mix_pre, norm_mix_post, w_in, conv_w, conv_b, lru_wa, lru_ba, lru_wx, lru_bx,
           lru_lambda, cmp_pos_k, cmp_pos_v, cmp_k_w1, cmp_k_w2, cmp_v_w1, cmp_v_w2, w_out,
           norm_ffn_pre, norm_ffn_post, ffn_w_gate_up, ffn_w_down, ple_w_proj, ple_w_gate, ple_b_gate,
           batch, seq):
    t, d = x.shape
    d_attn = N_HEADS * HEAD_DIM
    d_kv = N_KV_GROUPS * HEAD_DIM
    row2 = lambda v: v.reshape(1, -1)

    o_q = 2 * d
    o_kv = o_q + d_attn
    o_gn = o_kv + 6 * d_kv
    o_gm = o_gn + 3 * N_HEADS
    w_f = jnp.concatenate([w_in[:, 0:o_q], w_in[:, o_gm:o_gm + 2 * d]], axis=1).astype(BF16)
    w_q = w_in[:, o_q:o_kv].astype(BF16)
    w_kv = w_in[:, o_kv:o_gn].astype(BF16)
    per_g = 3 * HEADS_PER_GROUP
    w_gn = w_in[:, o_gn:o_gm].reshape(d, N_KV_GROUPS, per_g)
    w_gn = jnp.pad(w_gn, ((0, 0), (0, 0), (0, LANES - per_g))).reshape(d, N_KV_GROUPS * LANES).astype(BF16)

    g_pre = row2(norm_mix_pre)
    tm = min(512, t)
    fm = _norm_matmul(x, g_pre, w_f, BF16, tm, w_f.shape[1])
    qm, gnm, kvm = _norm_qkv(x, g_pre, jnp.concatenate([w_q, w_gn, w_kv], axis=1), d_attn, N_KV_GROUPS * LANES, tm)

    yr = _rnn(fm, conv_w, row2(conv_b), _block_diag_chunks(lru_wa).astype(BF16),
              _block_diag_chunks(lru_wx).astype(BF16), row2(lru_ba), row2(lru_bx), row2(lru_lambda),
              batch, seq, min(512, seq))

    nchunk = seq // CMP_STRIDE
    kv_chunks = kvm[0:2 * N_KV_GROUPS].reshape(2 * N_KV_GROUPS, t // CMP_STRIDE, CMP_STRIDE * HEAD_DIM)
    kcm, vcm = _compress(kv_chunks, cmp_k_w1.astype(BF16), cmp_k_w2.astype(BF16), cmp_pos_k.reshape(1, -1),
                         cmp_v_w1.astype(BF16), cmp_v_w2.astype(BF16), cmp_pos_v.reshape(1, -1), batch, nchunk)

    tq = min(256, seq)
    ya = _nsa(qm, kvm, kcm, vcm, gnm, fm, batch, seq, tq, min(512, seq), min(256, seq),
              gm_col0=(3 * d) // (HEADS_PER_GROUP * HEAD_DIM))

    dff = ffn_w_down.shape[0]
    tf = dff // 2 if (dff // 2) % LANES == 0 else dff
    return _ffn(x, yr, ya, w_out.astype(BF16), row2(norm_mix_post), row2(norm_ffn_pre),
                ffn_w_gate_up.astype(BF16), ffn_w_down.astype(BF16), row2(norm_ffn_post),
                p, ple_w_gate.astype(BF16), row2(ple_b_gate), ple_w_proj.astype(BF16), tm, tf)


def kernel(x, p, norm_mix_pre, norm_mix_post, w_in, conv_w, conv_b, lru_wa, lru_ba, lru_wx, lru_bx, lru_lambda, cmp_pos_k, cmp_pos_v, cmp_k_w1, cmp_k_w2, cmp_v_w1, cmp_v_w2, w_out, norm_ffn_pre, norm_ffn_post, ffn_w_gate_up, ffn_w_down, ple_w_proj, ple_w_gate, ple_b_gate):
    batch, seq, d = x.shape
    depth = w_in.shape[0]
    xf = x.reshape(batch * seq, d)
    for i in range(depth):
        xf = _layer(xf, p[i].reshape(batch * seq, -1), norm_mix_pre[i], norm_mix_post[i], w_in[i], conv_w[i],
                    conv_b[i], lru_wa[i], lru_ba[i], lru_wx[i], lru_bx[i], lru_lambda[i], cmp_pos_k[i],
                    cmp_pos_v[i], cmp_k_w1[i], cmp_k_w2[i], cmp_v_w1[i], cmp_v_w2[i], w_out[i],
                    norm_ffn_pre[i], norm_ffn_post[i], ffn_w_gate_up[i], ffn_w_down[i], ple_w_proj[i],
                    ple_w_gate[i], ple_b_gate[i], batch, seq)
    return xf.reshape(batch, seq, d)
```

```python
import functools

import numpy as np
import jax
import jax.numpy as jnp
from jax import lax
from jax.experimental import pallas as pl
from jax.experimental.pallas import tpu as pltpu

N_LRU_BLOCKS = 16
CONV_WIDTH = 4
LRU_C = 8.0
N_HEADS = 16
HEAD_DIM = 64
N_KV_GROUPS = 4
HEADS_PER_GROUP = N_HEADS // N_KV_GROUPS
CMP_LEN = 32
CMP_STRIDE = 16
SEL_LEN = 64
SEL_TOPK = 16
WINDOW = 512
FORCE_SCORE = 1e4
ALIBI_MAX_BIAS = 8.0
NORM_EPS = 1e-6

LANES = 128
SUBLANES = 8
VMEM_LIMIT_BYTES = 56 * 1024 * 1024

MASK_NEG = -1e30
N_ALIBI_COLS = 9
PAD_COL = HEAD_DIM + N_ALIBI_COLS
WORD_BITS = 16
WORD_SHIFT = 4
SEL_SHIFT = 6
N_FORCED = 3
NEAR_BLOCKS = 24
FILL_ROWS = 512
NSA_KEY_TILE = 11 * SEL_LEN
BF16 = jnp.bfloat16
F32 = jnp.float32


def _dot(a, b):
    return jnp.dot(a, b, preferred_element_type=F32)


def _dot_nt(a, b):
    return lax.dot_general(a, b, (((1,), (1,)), ((), ())), preferred_element_type=F32)


def _sigmoid(x):
    return 1.0 / (1.0 + jnp.exp(-x))


def _gelu_tanh(x):
    c = np.float32(np.sqrt(2.0 / np.pi))
    return 0.5 * x * (1.0 + jnp.tanh(c * (x + 0.044715 * (x * x * x))))


def _rmsnorm(x, g):
    ms = jnp.mean(x * x, axis=-1, keepdims=True)
    return x * lax.rsqrt(ms + NORM_EPS) * g


def _params(sem):
    return pltpu.CompilerParams(dimension_semantics=sem, vmem_limit_bytes=VMEM_LIMIT_BYTES)


def _norm_matmul_kernel(x_ref, g_ref, w_ref, o_ref, h_ref):
    @pl.when(pl.program_id(1) == 0)
    def _():
        h_ref[...] = _rmsnorm(x_ref[...], g_ref[...]).astype(BF16)

    o_ref[...] = _dot(h_ref[...], w_ref[...]).astype(o_ref.dtype)


def _norm_matmul(x, g, w, out_dtype, tm, tn):
    t, k = x.shape
    n = w.shape[1]
    return pl.pallas_call(
        _norm_matmul_kernel,
        grid=(t // tm, n // tn),
        in_specs=[
            pl.BlockSpec((tm, k), lambda i, j: (i, 0)),
            pl.BlockSpec((1, k), lambda i, j: (0, 0)),
            pl.BlockSpec((k, tn), lambda i, j: (0, j)),
        ],
        out_specs=pl.BlockSpec((tm, tn), lambda i, j: (i, j)),
        out_shape=jax.ShapeDtypeStruct((t, n), out_dtype),
        scratch_shapes=[pltpu.VMEM((tm, k), BF16)],
        compiler_params=_params(("parallel", "arbitrary")),
        name="norm_matmul",
    )(x, g, w)


def _norm_qkv_kernel(x_ref, g_ref, w_ref, q_ref, gn_ref, kv_ref):
    h = _rmsnorm(x_ref[...], g_ref[...]).astype(BF16)
    z = _dot(h, w_ref[...])
    nq = q_ref.shape[1]
    ngn = gn_ref.shape[1]
    q_ref[...] = z[:, 0:nq].astype(q_ref.dtype)
    gn_ref[...] = z[:, nq:nq + ngn]
    for c in range(kv_ref.shape[0]):
        lo = nq + ngn + c * HEAD_DIM
        kv_ref[c] = z[:, lo:lo + HEAD_DIM].astype(kv_ref.dtype)


def _norm_qkv(x, g, w, nq, ngn, tm):
    t, k = x.shape
    n = w.shape[1]
    n_slabs = (n - nq - ngn) // HEAD_DIM
    return pl.pallas_call(
        _norm_qkv_kernel,
        grid=(t // tm,),
        in_specs=[
            pl.BlockSpec((tm, k), lambda i: (i, 0)),
            pl.BlockSpec((1, k), lambda i: (0, 0)),
            pl.BlockSpec((k, n), lambda i: (0, 0)),
        ],
        out_specs=[
            pl.BlockSpec((tm, nq), lambda i: (i, 0)),
            pl.BlockSpec((tm, ngn), lambda i: (i, 0)),
            pl.BlockSpec((n_slabs, tm, HEAD_DIM), lambda i: (0, i, 0)),
        ],
        out_shape=[
            jax.ShapeDtypeStruct((t, nq), BF16),
            jax.ShapeDtypeStruct((t, ngn), F32),
            jax.ShapeDtypeStruct((n_slabs, t, HEAD_DIM), BF16),
        ],
        compiler_params=_params(("parallel",)),
        name="norm_qkv",
    )(x, g, w)


def _rnn_kernel(xr_ref, gr_ref, gm_ref, cw_ref, cb_ref, wa_ref, wx_ref, ba_ref, bx_ref, lam_ref,
                o_ref, xbuf, a_scr, b_scr, h_scr, *, ts):
    s_idx = pl.program_id(1)
    d = xr_ref.shape[1]
    halo = SUBLANES

    @pl.when(s_idx == 0)
    def _():
        xbuf[0:halo, :] = jnp.zeros((halo, d), F32)
        h_scr[...] = jnp.zeros(h_scr.shape, F32)

    xbuf[halo:halo + ts, :] = xr_ref[...].astype(F32)
    xc = cb_ref[...] + xbuf[halo:halo + ts, :] * cw_ref[CONV_WIDTH - 1:CONV_WIDTH, :]
    for k in range(1, CONV_WIDTH):
        xc = xc + xbuf[halo - k:halo - k + ts, :] * cw_ref[CONV_WIDTH - 1 - k:CONV_WIDTH - k, :]
    xbuf[0:halo, :] = xbuf[ts:ts + halo, :]

    xcb = xc.astype(BF16)
    n_chunks = d // LANES
    neg_sp = -LRU_C * (jnp.maximum(-lam_ref[...], 0.0) + jnp.log(1.0 + jnp.exp(-jnp.abs(lam_ref[...]))))
    for c in range(n_chunks):
        sl = slice(c * LANES, (c + 1) * LANES)
        xk = xcb[:, sl]
        r = _sigmoid(_dot(xk, wa_ref[c]) + ba_ref[:, sl])
        i = _sigmoid(_dot(xk, wx_ref[c]) + bx_ref[:, sl])
        log_a = r * neg_sp[:, sl]
        a = jnp.exp(log_a)
        a_scr[:, sl] = a
        b_scr[:, sl] = jnp.sqrt(1.0 - a * a) * (i * xc[:, sl])

    def group(gi, h):
        base = pl.multiple_of(gi * SUBLANES, SUBLANES)
        a8 = a_scr[pl.ds(base, SUBLANES), :]
        b8 = b_scr[pl.ds(base, SUBLANES), :]
        rows = []
        for j in range(SUBLANES):
            h = a8[j:j + 1, :] * h + b8[j:j + 1, :]
            rows.append(h)
        a_scr[pl.ds(base, SUBLANES), :] = jnp.concatenate(rows, axis=0)
        return h

    h_last = lax.fori_loop(0, ts // SUBLANES, group, h_scr[0:1, :])
    h_scr[0:1, :] = h_last
    o_ref[...] = (_sigmoid(gm_ref[...].astype(F32)) * a_scr[...]
                  * _gelu_tanh(gr_ref[...].astype(F32))).astype(o_ref.dtype)


def _rnn(f, conv_w, conv_b, wa, wx, ba, bx, lam, batch, seq, ts):
    t = f.shape[0]
    d = conv_w.shape[1]
    ns = seq // ts
    nd = d // d
    del nd
    row = lambda b, s: b * ns + s
    vec = lambda r: pl.BlockSpec((r, d), lambda b, s: (0, 0))
    return pl.pallas_call(
        functools.partial(_rnn_kernel, ts=ts),
        grid=(batch, ns),
        in_specs=[
            pl.BlockSpec((ts, d), lambda b, s: (row(b, s), 0)),
            pl.BlockSpec((ts, d), lambda b, s: (row(b, s), 1)),
            pl.BlockSpec((ts, d), lambda b, s: (row(b, s), 2)),
            vec(CONV_WIDTH), vec(1),
            pl.BlockSpec(wa.shape, lambda b, s: (0, 0, 0)),
            pl.BlockSpec(wx.shape, lambda b, s: (0, 0, 0)),
            vec(1), vec(1), vec(1),
        ],
        out_specs=pl.BlockSpec((ts, d), lambda b, s: (row(b, s), 0)),
        out_shape=jax.ShapeDtypeStruct((t, d), BF16),
        scratch_shapes=[
            pltpu.VMEM((ts + SUBLANES, d), F32),
            pltpu.VMEM((ts, d), F32),
            pltpu.VMEM((ts, d), F32),
            pltpu.VMEM((SUBLANES, d), F32),
        ],
        compiler_params=_params(("parallel", "arbitrary")),
        name="rnn_mixer",
    )(f, f, f, conv_w, conv_b, wa, wx, ba, bx, lam)


def _compress_kernel(ak_ref, av_ref, w1k_ref, w2k_ref, pk_ref, w1v_ref, w2v_ref, pv_ref, ok_ref, ov_ref):
    def one(a_ref, w1_ref, w2_ref, p_ref, o_ref):
        a = a_ref[0]
        half = a.shape[1]
        nchunk = a.shape[0]
        lo = _dot(a, w1_ref[0:half, :])
        hi = _dot(a, w1_ref[half:2 * half, :])
        pb = _dot(jnp.broadcast_to(p_ref[...], (SUBLANES, 2 * half)).astype(BF16), w1_ref[...])[0:1, :]
        h = lo + pltpu.roll(hi, nchunk - 1, 0) + pb
        o_ref[0] = _dot(_gelu_tanh(h).astype(BF16), w2_ref[...]).astype(o_ref.dtype)

    one(ak_ref, w1k_ref, w2k_ref, pk_ref, ok_ref)
    one(av_ref, w1v_ref, w2v_ref, pv_ref, ov_ref)


def _compress(kv_chunks, w1k, w2k, pk, w1v, w2v, pv, batch, nchunk):
    g = N_KV_GROUPS
    width = kv_chunks.shape[2]
    full = lambda a: pl.BlockSpec(a.shape, lambda b, gg: (0,) * a.ndim)
    out = jax.ShapeDtypeStruct((batch * g, nchunk, HEAD_DIM), BF16)
    return pl.pallas_call(
        _compress_kernel,
        grid=(batch, g),
        in_specs=[
            pl.BlockSpec((1, nchunk, width), lambda b, gg: (gg, b, 0)),
            pl.BlockSpec((1, nchunk, width), lambda b, gg: (g + gg, b, 0)),
            full(w1k), full(w2k), full(pk), full(w1v), full(w2v), full(pv),
        ],
        out_specs=[
            pl.BlockSpec((1, nchunk, HEAD_DIM), lambda b, gg: (b * g + gg, 0, 0)),
            pl.BlockSpec((1, nchunk, HEAD_DIM), lambda b, gg: (b * g + gg, 0, 0)),
        ],
        out_shape=[out, out],
        compiler_params=_params(("parallel", "parallel")),
        name="compress",
    )(kv_chunks, kv_chunks, w1k, w2k, pk, w1v, w2v, pv)


ONES_ROWS = 16


def _scores(k, q_rows, bias, s_ref):
    s = _dot(k, q_rows)
    if bias is not None:
        s = s + bias
    s_ref[...] = s
    return jnp.max(s, axis=0, keepdims=True)


def _accumulate(s_ref, col_max, v_aug_t, state):
    m_old, acc_old = state
    m_new = jnp.maximum(m_old, col_max)
    alpha = jnp.exp(m_old - m_new)
    e = jnp.exp(s_ref[...] - m_new).astype(BF16)
    return m_new, alpha * acc_old + _dot(v_aug_t, e)


def _nsa_kernel(q_ref, kc_ref, vc_ref, ks_ref, vs_ref, kw_ref, vw_ref, gn_ref, gm_ref,
                qal_ref, cpos_ref, wsel_ref,
                o_ref, q_t, ksaug, kwaug, kcaug, kstage, vstage, s_win, s_a, s_b, win_max, o_cmp_scr, flags,
                *, tq, nk, nkw, seq, sel_chunk):
    b = pl.program_id(0)
    g = pl.program_id(1)
    qt = pl.program_id(2)
    r_heads = HEADS_PER_GROUP
    dh = HEAD_DIM
    m_cols = r_heads * tq
    nb = seq // SEL_LEN
    nc = kc_ref.shape[1]
    a_w = 2 * dh
    k_w = a_w + nb

    @pl.when((b == 0) & (g == 0) & (qt == 0))
    def _():
        rows = min(FILL_ROWS, seq)

        def fill(c, carry):
            off = pl.multiple_of(c * rows, rows)
            pos = off + lax.broadcasted_iota(jnp.int32, (rows, k_w), 0)
            lane = lax.broadcasted_iota(jnp.int32, (rows, k_w), 1)
            blk_of = jnp.right_shift(pos, SEL_SHIFT)
            cols = jnp.where((lane >= dh) & (lane < dh + 3), blk_of * SEL_LEN,
                             jnp.where((lane >= dh + 3) & (lane < dh + 6), pos & (SEL_LEN - 1),
                                       jnp.where(lane == a_w + blk_of, 1, 0)))
            cols = cols.astype(F32).astype(BF16)
            ksaug[pl.ds(off, rows), :] = cols
            kwaug[pl.ds(off, rows), :] = cols[:, 0:a_w]
            return carry

        lax.fori_loop(0, seq // rows, fill, 0)
        kcaug[...] = cpos_ref[...]
        vstage[...] = jnp.ones(vstage.shape, BF16)

    @pl.when(qt == 0)
    def _():
        ksaug[:, 0:dh] = ks_ref[0]
        kwaug[:, 0:dh] = kw_ref[0]
        kcaug[:, 0:dh] = kc_ref[0]
        q_t[dh:a_w, :] = qal_ref[0]

    v_rows = dh + ONES_ROWS
    ri = lax.broadcasted_iota(jnp.int32, (v_rows, 2 * dh), 0)
    ci = lax.broadcasted_iota(jnp.int32, (v_rows, 2 * dh), 1)
    pick = (((ri < dh) & (ri == ci)) | ((ri >= dh) & (ci == dh))).astype(F32).astype(BF16)
    transposed = lambda v: _dot_nt(pick[0:dh, 0:dh], v).astype(BF16)
    staged_aug_t = lambda v: _dot_nt(pick, v).astype(BF16)
    ones_rows = jnp.ones((ONES_ROWS, nkw), BF16)
    window_aug_t = lambda v: jnp.concatenate([transposed(v), ones_rows], axis=0)
    init = (jnp.full((1, m_cols), MASK_NEG, F32), jnp.zeros((v_rows, m_cols), F32))

    t0 = qt * tq
    col = lax.broadcasted_iota(jnp.int32, (1, m_cols), 1)
    trow = t0 + (col & (tq - 1))

    scale = np.float32(HEAD_DIM ** -0.5)
    qf = jnp.transpose(q_ref[...].astype(F32) * scale)
    for r in range(r_heads):
        q_t[0:dh, r * tq:(r + 1) * tq] = qf[r * dh:(r + 1) * dh, :].astype(BF16)

    n_wt = (WINDOW + tq) // nkw
    win_s = [s_win.at[i * nkw:(i + 1) * nkw, :] for i in range(n_wt)]
    win_off = [pl.multiple_of(jnp.maximum(t0 + tq - (i + 1) * nkw, 0), nkw) for i in range(n_wt)]

    def window_scores():
        for i in range(n_wt):
            start = t0 + tq - (i + 1) * nkw
            kpos_w = start + lax.broadcasted_iota(jnp.int32, (nkw, 1), 0)
            ok = (kpos_w <= trow) & (kpos_w > trow - WINDOW) & (start >= 0)
            bias_w = jnp.where(ok, 0.0, MASK_NEG).astype(F32)
            win_max[i:i + 1, :] = _scores(kwaug[pl.ds(win_off[i], nkw), :], q_t[0:a_w, :], bias_w, win_s[i])

    def compressed_and_select(rows_c, rows_b):
        cend = lax.broadcasted_iota(jnp.int32, (rows_c, 1), 0) * CMP_STRIDE + (CMP_LEN - 1)
        bias_c = jnp.where(cend <= trow, 0.0, MASK_NEG).astype(F32)
        s = _dot(kcaug[0:rows_c, :], q_t[0:a_w, :]) + bias_c
        m = jnp.max(s, axis=0, keepdims=True)
        e = jnp.exp(s - m)
        has_key = (trow >= CMP_LEN - 1).astype(F32)
        p = e * (has_key / jnp.sum(e, axis=0, keepdims=True))
        o_cmp_scr[...] = _dot(transposed(vc_ref[0, 0:rows_c, :]), p.astype(BF16))
        imp = p[:, 0:tq]
        for r in range(1, r_heads):
            imp = imp + p[:, r * tq:(r + 1) * tq]

        hi = imp.astype(BF16)
        r1 = imp - hi.astype(F32)
        mid = r1.astype(BF16)
        lo = (r1 - mid.astype(F32)).astype(BF16)
        wsel = wsel_ref[0:rows_b, 0:rows_c]
        imp_t = _dot(wsel, hi) + _dot(wsel, mid) + _dot(wsel, lo)

        window_scores()

        blk = lax.broadcasted_iota(jnp.int32, (rows_b, tq), 0).astype(F32)
        tq_l = t0 + lax.broadcasted_iota(jnp.int32, (rows_b, tq), 1)
        cur = jnp.right_shift(tq_l, SEL_SHIFT).astype(F32)
        valid = blk <= cur
        forced = (blk == 0.0) | (blk == cur) | (blk == cur - 1.0)
        score = jnp.where(valid & jnp.logical_not(forced), imp_t, -1.0)
        sel = jnp.where(forced, 1.0, 0.0)
        for _ in range(min(SEL_TOPK, nb) - N_FORCED):
            mx = jnp.max(score, axis=0, keepdims=True)
            idx = jnp.min(jnp.where(score == mx, blk, float(nb)), axis=0, keepdims=True)
            hit = blk == idx
            sel = jnp.where(hit, 1.0, sel)
            score = jnp.where(hit, -2.0, score)
        selv = (sel > 0.0) & valid
        selneg_t = jnp.where(selv, 0.0, MASK_NEG).astype(BF16)
        unseen = jnp.full((nb - rows_b, tq), MASK_NEG, BF16)
        for r in range(r_heads):
            q_t[a_w:a_w + rows_b, r * tq:(r + 1) * tq] = selneg_t
            if rows_b < nb:
                q_t[a_w + rows_b:, r * tq:(r + 1) * tq] = unseen

        used = jnp.max(jnp.where(selv, 1.0, 0.0), axis=1, keepdims=True)
        bit_id = lax.broadcasted_iota(jnp.int32, (rows_b, 1), 0) & (WORD_BITS - 1)
        weighted = used * jnp.left_shift(1, bit_id).astype(F32)
        for i in range(nb // WORD_BITS):
            if (i + 1) * WORD_BITS <= rows_b:
                flags[i] = jnp.sum(weighted[i * WORD_BITS:(i + 1) * WORD_BITS, :]).astype(jnp.int32)
            else:
                flags[i] = 0

    n_var = nb // sel_chunk
    cmp_chunk = sel_chunk * (SEL_LEN // CMP_STRIDE)
    seen_c = (t0 + tq - CMP_LEN) // CMP_STRIDE + 1
    seen_b = (t0 + tq) // SEL_LEN
    variant = jnp.maximum((seen_c + cmp_chunk - 1) // cmp_chunk, (seen_b + sel_chunk - 1) // sel_chunk)
    for v in range(1, n_var + 1):
        pl.when(variant == v)(functools.partial(compressed_and_select, min(v * cmp_chunk, nc), v * sel_chunk))
    o_cmp = o_cmp_scr[...]

    n_own = tq // SEL_LEN
    off_q = pl.multiple_of(t0, tq)
    kstage[0:tq, :] = ksaug[pl.ds(off_q, tq), :]
    vstage[0:tq, 0:dh] = vs_ref[0, pl.ds(off_q, tq), :]

    def stage_block(j, cnt):
        src = pl.multiple_of(j * SEL_LEN, SEL_LEN)
        dst = pl.multiple_of(cnt * SEL_LEN, SEL_LEN)
        kstage[pl.ds(dst, SEL_LEN), :] = ksaug[pl.ds(src, SEL_LEN), :]
        vstage[pl.ds(dst, SEL_LEN), 0:dh] = vs_ref[0, pl.ds(src, SEL_LEN), :]

    def gather(j, cnt):
        bit = jnp.right_shift(flags[jnp.right_shift(j, WORD_SHIFT)], j & (WORD_BITS - 1)) & 1
        pl.when(bit == 1)(functools.partial(stage_block, j, cnt))
        return cnt + bit

    n_past = jnp.right_shift(t0, SEL_SHIFT)
    near_lo = jnp.maximum(n_past - NEAR_BLOCKS, 0)
    first = jnp.where(near_lo > 0, flags[0] & 1, 0)
    pl.when(first == 1)(functools.partial(stage_block, 0, n_own))

    def far_word(w, cnt):
        lo = jnp.maximum(w * WORD_BITS, 1)
        hi = jnp.minimum((w + 1) * WORD_BITS, near_lo)
        rest = jnp.where(w == 0, flags[w] & -2, flags[w])
        return lax.cond(rest != 0, lambda c: lax.fori_loop(lo, hi, gather, c), lambda c: c, cnt)

    n_blocks = lax.fori_loop(0, jnp.right_shift(near_lo + WORD_BITS - 1, WORD_SHIFT), far_word, n_own + first)
    n_blocks = lax.fori_loop(near_lo, n_past, gather, n_blocks)
    per_tile = nk // SEL_LEN
    n_tiles = (n_blocks + per_tile - 1) // per_tile
    pad_block = jnp.where(lax.broadcasted_iota(jnp.int32, (SEL_LEN, k_w), 1) == PAD_COL, 1.0, 0.0).astype(BF16)

    def pad(j, carry):
        dst = pl.multiple_of(j * SEL_LEN, SEL_LEN)
        kstage[pl.ds(dst, SEL_LEN), :] = pad_block
        return carry

    lax.fori_loop(n_blocks, n_tiles * per_tile, pad, 0)

    srow = lax.broadcasted_iota(jnp.int32, (nk, 1), 0)
    bias_0 = jnp.where((srow >= tq) | (t0 + srow <= trow), 0.0, MASK_NEG).astype(F32)
    max_a = _scores(kstage[0:nk, :], q_t[...], bias_0, s_a)

    state = init
    for i in range(n_wt):
        state = _accumulate(win_s[i], win_max[i:i + 1, :], window_aug_t(vw_ref[0, pl.ds(win_off[i], nkw), :]), state)
    o_win = state[1][0:dh] / state[1][dh:dh + 1]

    def tile_scores(i, s_ref):
        return _scores(kstage[pl.ds(pl.multiple_of(i * nk, nk), nk), :], q_t[...], None, s_ref)

    def tile_accumulate(i, s_ref, col_max, state):
        return _accumulate(s_ref, col_max, staged_aug_t(vstage[pl.ds(pl.multiple_of(i * nk, nk), nk), :]), state)

    def slc_pair(j, carry):
        state, max_a = carry
        i = 2 * j
        max_b = tile_scores(i + 1, s_b)
        state = tile_accumulate(i, s_a, max_a, state)
        max_a = tile_scores(i + 2, s_a)
        state = tile_accumulate(i + 1, s_b, max_b, state)
        return state, max_a

    n_pairs = (n_tiles - 1) // 2
    state, max_a = lax.fori_loop(0, n_pairs, slc_pair, (init, max_a))
    i_a = 2 * n_pairs

    def two_left(state):
        max_b = tile_scores(i_a + 1, s_b)
        state = tile_accumulate(i_a, s_a, max_a, state)
        return tile_accumulate(i_a + 1, s_b, max_b, state)

    def one_left(state):
        return tile_accumulate(i_a, s_a, max_a, state)

    _, acc = lax.cond(n_tiles - i_a == 2, two_left, one_left, state)
    o_slc = acc[0:dh] / acc[dh:dh + 1]

    gates = jnp.transpose(_sigmoid(gn_ref[...]))
    heads = []
    for r in range(r_heads):
        cs = slice(r * tq, (r + 1) * tq)
        heads.append(gates[3 * r:3 * r + 1, :] * o_cmp[:, cs] + gates[3 * r + 1:3 * r + 2, :] * o_slc[:, cs]
                     + gates[3 * r + 2:3 * r + 3, :] * o_win[:, cs])
    o = jnp.transpose(jnp.concatenate(heads, axis=0))
    o_ref[...] = (_sigmoid(gm_ref[...].astype(F32)) * o).astype(o_ref.dtype)


def _alibi_tables(seq, nc, tq):
    import ml_dtypes
    bf = ml_dtypes.bfloat16
    h = np.arange(1, N_HEADS + 1, dtype=np.float32)
    slopes = np.exp2(-ALIBI_MAX_BIAS * h / N_HEADS).astype(np.float32)
    s1 = slopes.astype(bf).astype(np.float32)
    s2 = (slopes - s1).astype(bf).astype(np.float32)
    s3 = (slopes - s1 - s2).astype(bf).astype(np.float32)
    dh = HEAD_DIM
    qal = np.zeros((N_HEADS, dh), np.float32)
    for rep in range(3):
        qal[:, 3 * rep + 0] = s1
        qal[:, 3 * rep + 1] = s2
        qal[:, 3 * rep + 2] = s3
    qal[:, PAD_COL - dh] = MASK_NEG
    qal = qal.reshape(N_KV_GROUPS, HEADS_PER_GROUP, dh).transpose(0, 2, 1)
    qal_p = np.repeat(qal, tq, axis=2)

    nb = seq // SEL_LEN
    c = np.arange(nc)
    cpos = np.zeros((nc, 2 * dh), np.float32)
    cpos[:, dh:dh + 3] = ((c // 16) * 16 * CMP_STRIDE)[:, None]
    cpos[:, dh + 3:dh + 6] = ((c % 16) * CMP_STRIDE)[:, None]
    cpos[:, dh + 6:dh + 9] = CMP_LEN - 1

    r_sel = SEL_LEN // CMP_STRIDE
    r_cmp = CMP_LEN // CMP_STRIDE
    wsel = np.zeros((nb, nc), np.float32)
    for j in range(nb):
        for mm in range(r_sel):
            for nn in range(r_cmp):
                ci = r_sel * j + mm - nn
                if 0 <= ci < nc - 1:
                    wsel[j, ci] += 1.0
    as_bf = lambda a: jnp.asarray(a.astype(bf))
    return as_bf(qal_p), as_bf(cpos), as_bf(wsel)


def _nsa(qm, kvm, kcm, vcm, gnm, fm, batch, seq, tq, nk, nkw, gm_col0):
    t = qm.shape[0]
    g = N_KV_GROUPS
    dh = HEAD_DIM
    nqt = seq // tq
    nc = kcm.shape[1]
    nb = seq // SEL_LEN
    gw = HEADS_PER_GROUP * dh
    m_cols = HEADS_PER_GROUP * tq
    qal, cpos, wsel = _alibi_tables(seq, nc, tq)
    n_wt = (WINDOW + tq) // nkw
    sel_chunk = max(WORD_BITS, nb // 4)
    per_tile = nk // SEL_LEN
    stage_rows = -(-nb // per_tile) * per_tile * SEL_LEN
    row = lambda b, gg, i: b * nqt + i
    slab = lambda base: pl.BlockSpec((1, seq, dh), lambda b, gg, i: (base + gg, b, 0))
    cmp_slab = pl.BlockSpec((1, nc, dh), lambda b, gg, i: (b * g + gg, 0, 0))
    const = lambda a: pl.BlockSpec(a.shape, lambda b, gg, i: (0,) * a.ndim)
    return pl.pallas_call(
        functools.partial(_nsa_kernel, tq=tq, nk=nk, nkw=nkw, seq=seq, sel_chunk=sel_chunk),
        grid=(batch, g, nqt),
        in_specs=[
            pl.BlockSpec((tq, gw), lambda b, gg, i: (row(b, gg, i), gg)),
            cmp_slab, cmp_slab,
            slab(2 * g), slab(3 * g), slab(4 * g), slab(5 * g),
            pl.BlockSpec((tq, LANES), lambda b, gg, i: (row(b, gg, i), gg)),
            pl.BlockSpec((tq, gw), lambda b, gg, i: (row(b, gg, i), gm_col0 + gg)),
            pl.BlockSpec((1, dh, m_cols), lambda b, gg, i: (gg, 0, 0)),
            const(cpos), const(wsel),
        ],
        out_specs=pl.BlockSpec((tq, gw), lambda b, gg, i: (row(b, gg, i), gg)),
        out_shape=jax.ShapeDtypeStruct((t, g * gw), BF16),
        scratch_shapes=[
            pltpu.VMEM((2 * dh + nb, m_cols), BF16),
            pltpu.VMEM((seq, 2 * dh + nb), BF16),
            pltpu.VMEM((seq, 2 * dh), BF16),
            pltpu.VMEM((nc, 2 * dh), BF16),
            pltpu.VMEM((stage_rows, 2 * dh + nb), BF16),
            pltpu.VMEM((stage_rows, 2 * dh), BF16),
            pltpu.VMEM((n_wt * nkw, m_cols), F32),
            pltpu.VMEM((nk, m_cols), F32),
            pltpu.VMEM((nk, m_cols), F32),
            pltpu.VMEM((SUBLANES, m_cols), F32),
            pltpu.VMEM((dh, m_cols), F32),
            pltpu.SMEM((nb // WORD_BITS,), jnp.int32),
        ],
        compiler_params=_params(("arbitrary", "arbitrary", "arbitrary")),
        name="nsa",
    )(qm, kcm, vcm, kvm, kvm, kvm, kvm, gnm, fm, qal, cpos, wsel)


def _ffn_kernel(x_ref, yr_ref, ya_ref, wo_ref, gmix_ref, gpre_ref, wg_ref, wu_ref, wd_ref, gpost_ref,
                p_ref, wpg_ref, bpg_ref, wpp_ref, o_ref, x1_ref, h_ref, acc_ref):
    j = pl.program_id(1)

    @pl.when(j == 0)
    def _():
        y = (yr_ref[...].astype(F32) + ya_ref[...].astype(F32)).astype(BF16)
        x1 = x_ref[...] + _rmsnorm(_dot(y, wo_ref[...]), gmix_ref[...])
        x1_ref[...] = x1
        h_ref[...] = _rmsnorm(x1, gpre_ref[...]).astype(BF16)
        acc_ref[...] = jnp.zeros(acc_ref.shape, F32)

    h = h_ref[...]
    gate = _dot(h, wg_ref[...])
    up = _dot(h, wu_ref[...])
    act = (gate * _sigmoid(gate) * up).astype(BF16)
    acc_ref[...] += _dot(act, wd_ref[...])

    @pl.when(j == pl.num_programs(1) - 1)
    def _():
        x2 = x1_ref[...] + _rmsnorm(acc_ref[...], gpost_ref[...])
        gate_p = _sigmoid(_dot(x2.astype(BF16), wpg_ref[...]) + bpg_ref[...])
        o_ref[...] = x2 + gate_p * _dot(p_ref[...].astype(BF16), wpp_ref[...])


def _ffn(x, yr, ya, wo, gmix, gpre, wgu, wd, gpost, p, wpg, bpg, wpp, tm, tf):
    t, d = x.shape
    dff = wd.shape[0]
    nf = dff // tf
    dp = p.shape[1]
    rows = pl.BlockSpec((tm, d), lambda i, j: (i, 0))
    vec = pl.BlockSpec((1, d), lambda i, j: (0, 0))
    square = pl.BlockSpec((d, d), lambda i, j: (0, 0))
    return pl.pallas_call(
        _ffn_kernel,
        grid=(t // tm, nf),
        in_specs=[
            rows, rows, rows, square, vec, vec,
            pl.BlockSpec((d, tf), lambda i, j: (0, j)),
            pl.BlockSpec((d, tf), lambda i, j: (0, nf + j)),
            pl.BlockSpec((tf, d), lambda i, j: (j, 0)),
            vec,
            pl.BlockSpec((tm, dp), lambda i, j: (i, 0)),
            square,
            vec,
            pl.BlockSpec((dp, d), lambda i, j: (0, 0)),
        ],
        out_specs=rows,
        out_shape=jax.ShapeDtypeStruct((t, d), F32),
        scratch_shapes=[pltpu.VMEM((tm, d), F32), pltpu.VMEM((tm, d), BF16), pltpu.VMEM((tm, d), F32)],
        compiler_params=_params(("parallel", "arbitrary")),
        name="out_ffn_ple",
    )(x, yr, ya, wo, gmix, gpre, wgu, wgu, wd, gpost, p, wpg, bpg, wpp)


def _block_diag_chunks(w):
    n, bs, _ = w.shape
    per = LANES // bs
    w = w.reshape(n // per, per, bs, bs)
    eye = jnp.eye(per, dtype=w.dtype)
    return jnp.einsum('cpij,pq->cpiqj', w, eye).reshape(n // per, LANES, LANES)


def _layer(x, p, norm_mix_pre, norm_mix_post, w_in, conv_w, conv_b, lru_wa, lru_ba, lru_wx, lru_bx,
           lru_lambda, cmp_pos_k, cmp_pos_v, cmp_k_w1, cmp_k_w2, cmp_v_w1, cmp_v_w2, w_out,
           norm_ffn_pre, norm_ffn_post, ffn_w_gate_up, ffn_w_down, ple_w_proj, ple_w_gate, ple_b_gate,
           batch, seq):
    t, d = x.shape
    d_attn = N_HEADS * HEAD_DIM
    d_kv = N_KV_GROUPS * HEAD_DIM
    row2 = lambda v: v.reshape(1, -1)

    o_q = 2 * d
    o_kv = o_q + d_attn
    o_gn = o_kv + 6 * d_kv
    o_gm = o_gn + 3 * N_HEADS
    w_f = jnp.concatenate([w_in[:, 0:o_q], w_in[:, o_gm:o_gm + 2 * d]], axis=1).astype(BF16)
    w_q = w_in[:, o_q:o_kv].astype(BF16)
    w_kv = w_in[:, o_kv:o_gn].astype(BF16)
    per_g = 3 * HEADS_PER_GROUP
    w_gn = w_in[:, o_gn:o_gm].reshape(d, N_KV_GROUPS, per_g)
    w_gn = jnp.pad(w_gn, ((0, 0), (0, 0), (0, LANES - per_g))).reshape(d, N_KV_GROUPS * LANES).astype(BF16)

    g_pre = row2(norm_mix_pre)
    tm = min(512, t)
    fm = _norm_matmul(x, g_pre, w_f, BF16, tm, w_f.shape[1])
    qm, gnm, kvm = _norm_qkv(x, g_pre, jnp.concatenate([w_q, w_gn, w_kv], axis=1), d_attn, N_KV_GROUPS * LANES, tm)

    yr = _rnn(fm, conv_w, row2(conv_b), _block_diag_chunks(lru_wa).astype(BF16),
              _block_diag_chunks(lru_wx).astype(BF16), row2(lru_ba), row2(lru_bx), row2(lru_lambda),
              batch, seq, min(512, seq))

    nchunk = seq // CMP_STRIDE
    kv_chunks = kvm[0:2 * N_KV_GROUPS].reshape(2 * N_KV_GROUPS, t // CMP_STRIDE, CMP_STRIDE * HEAD_DIM)
    kcm, vcm = _compress(kv_chunks, cmp_k_w1.astype(BF16), cmp_k_w2.astype(BF16), cmp_pos_k.reshape(1, -1),
                         cmp_v_w1.astype(BF16), cmp_v_w2.astype(BF16), cmp_pos_v.reshape(1, -1), batch, nchunk)

    tq = min(256, seq)
    ya = _nsa(qm, kvm, kcm, vcm, gnm, fm, batch, seq, tq, min(NSA_KEY_TILE, seq), min(256, seq),
              gm_col0=(3 * d) // (HEADS_PER_GROUP * HEAD_DIM))

    dff = ffn_w_down.shape[0]
    tf = dff // 2 if (dff // 2) % LANES == 0 else dff
    return _ffn(x, yr, ya, w_out.astype(BF16), row2(norm_mix_post), row2(norm_ffn_pre),
                ffn_w_gate_up.astype(BF16), ffn_w_down.astype(BF16), row2(norm_ffn_post),
                p, ple_w_gate.astype(BF16), row2(ple_b_gate), ple_w_proj.astype(BF16), tm, tf)


def kernel(x, p, norm_mix_pre, norm_mix_post, w_in, conv_w, conv_b, lru_wa, lru_ba, lru_wx, lru_bx, lru_lambda, cmp_pos_k, cmp_pos_v, cmp_k_w1, cmp_k_w2, cmp_v_w1, cmp_v_w2, w_out, norm_ffn_pre, norm_ffn_post, ffn_w_gate_up, ffn_w_down, ple_w_proj, ple_w_gate, ple_b_gate):
    batch, seq, d = x.shape
    depth = w_in.shape[0]
    xf = x.reshape(batch * seq, d)
    for i in range(depth):
        xf = _layer(xf, p[i].reshape(batch * seq, -1), norm_mix_pre[i], norm_mix_post[i], w_in[i], conv_w[i],
                    conv_b[i], lru_wa[i], lru_ba[i], lru_wx[i], lru_bx[i], lru_lambda[i], cmp_pos_k[i],
                    cmp_pos_v[i], cmp_k_w1[i], cmp_k_w2[i], cmp_v_w1[i], cmp_v_w2[i], w_out[i],
                    norm_ffn_pre[i], norm_ffn_post[i], ffn_w_gate_up[i], ffn_w_down[i], ple_w_proj[i],
                    ple_w_gate[i], ple_b_gate[i], batch, seq)
    return xf.reshape(batch, seq, d)
```

```python
import functools

import numpy as np
import jax
import jax.numpy as jnp
from jax import lax
from jax.experimental import pallas as pl
from jax.experimental.pallas import tpu as pltpu

N_LRU_BLOCKS = 16
CONV_WIDTH = 4
LRU_C = 8.0
N_HEADS = 16
HEAD_DIM = 64
N_KV_GROUPS = 4
HEADS_PER_GROUP = N_HEADS // N_KV_GROUPS
CMP_LEN = 32
CMP_STRIDE = 16
SEL_LEN = 64
SEL_TOPK = 16
WINDOW = 512
FORCE_SCORE = 1e4
ALIBI_MAX_BIAS = 8.0
NORM_EPS = 1e-6

LANES = 128
SUBLANES = 8
VMEM_LIMIT_BYTES = 56 * 1024 * 1024

MASK_NEG = -1e30
N_ALIBI_COLS = 9
PAD_COL = HEAD_DIM + N_ALIBI_COLS
WORD_BITS = 16
WORD_SHIFT = 4
SEL_SHIFT = 6
N_FORCED = 3
NEAR_BLOCKS = 24
FILL_ROWS = 512
NSA_KEY_TILE = 11 * SEL_LEN
BF16 = jnp.bfloat16
F32 = jnp.float32


def _dot(a, b):
    return jnp.dot(a, b, preferred_element_type=F32)


def _dot_nt(a, b):
    return lax.dot_general(a, b, (((1,), (1,)), ((), ())), preferred_element_type=F32)


def _sigmoid(x):
    return 0.5 * jnp.tanh(0.5 * x) + 0.5


def _gelu_tanh(x):
    c = np.float32(np.sqrt(2.0 / np.pi))
    half = 0.5 * x
    return half + half * jnp.tanh(x * (c + (c * 0.044715) * (x * x)))


def _rmsnorm(x, g):
    ms = jnp.mean(x * x, axis=-1, keepdims=True)
    return x * lax.rsqrt(ms + NORM_EPS) * g


def _params(sem):
    return pltpu.CompilerParams(dimension_semantics=sem, vmem_limit_bytes=VMEM_LIMIT_BYTES)


def _norm_matmul_kernel(x_ref, g_ref, w_ref, o_ref, h_ref):
    @pl.when(pl.program_id(1) == 0)
    def _():
        h_ref[...] = _rmsnorm(x_ref[...], g_ref[...]).astype(BF16)

    o_ref[...] = _dot(h_ref[...], w_ref[...]).astype(o_ref.dtype)


def _norm_matmul(x, g, w, out_dtype, tm, tn):
    t, k = x.shape
    n = w.shape[1]
    return pl.pallas_call(
        _norm_matmul_kernel,
        grid=(t // tm, n // tn),
        in_specs=[
            pl.BlockSpec((tm, k), lambda i, j: (i, 0)),
            pl.BlockSpec((1, k), lambda i, j: (0, 0)),
            pl.BlockSpec((k, tn), lambda i, j: (0, j)),
        ],
        out_specs=pl.BlockSpec((tm, tn), lambda i, j: (i, j)),
        out_shape=jax.ShapeDtypeStruct((t, n), out_dtype),
        scratch_shapes=[pltpu.VMEM((tm, k), BF16)],
        compiler_params=_params(("parallel", "arbitrary")),
        name="norm_matmul",
    )(x, g, w)


def _norm_qkv_kernel(x_ref, g_ref, w_ref, q_ref, gn_ref, cv_ref, kv_ref, slab_scr):
    h = _rmsnorm(x_ref[...], g_ref[...]).astype(BF16)
    z = _dot(h, w_ref[...])
    nq = q_ref.shape[1]
    ngn = gn_ref.shape[1]
    q_ref[...] = z[:, 0:nq].astype(q_ref.dtype)
    gn_ref[...] = z[:, nq:nq + ngn]
    n_cv = cv_ref.shape[0]
    rows = cv_ref.shape[1]
    for c in range(n_cv):
        lo = nq + ngn + c * HEAD_DIM
        slab_scr[...] = z[:, lo:lo + HEAD_DIM]
        for l in range(CMP_STRIDE):
            cv_ref[c, :, l * HEAD_DIM:(l + 1) * HEAD_DIM] = (
                slab_scr[pl.ds(l, rows, stride=CMP_STRIDE), :].astype(cv_ref.dtype))
    for c in range(kv_ref.shape[0]):
        lo = nq + ngn + (n_cv + c) * HEAD_DIM
        kv_ref[c] = z[:, lo:lo + HEAD_DIM].astype(kv_ref.dtype)


def _norm_qkv(x, g, w, nq, ngn, n_cv, tm):
    t, k = x.shape
    n = w.shape[1]
    n_kv = (n - nq - ngn) // HEAD_DIM - n_cv
    return pl.pallas_call(
        _norm_qkv_kernel,
        grid=(t // tm,),
        in_specs=[
            pl.BlockSpec((tm, k), lambda i: (i, 0)),
            pl.BlockSpec((1, k), lambda i: (0, 0)),
            pl.BlockSpec((k, n), lambda i: (0, 0)),
        ],
        out_specs=[
            pl.BlockSpec((tm, nq), lambda i: (i, 0)),
            pl.BlockSpec((tm, ngn), lambda i: (i, 0)),
            pl.BlockSpec((n_cv, tm // CMP_STRIDE, CMP_STRIDE * HEAD_DIM), lambda i: (0, i, 0)),
            pl.BlockSpec((n_kv, tm, HEAD_DIM), lambda i: (0, i, 0)),
        ],
        out_shape=[
            jax.ShapeDtypeStruct((t, nq), BF16),
            jax.ShapeDtypeStruct((t, ngn), F32),
            jax.ShapeDtypeStruct((n_cv, t // CMP_STRIDE, CMP_STRIDE * HEAD_DIM), BF16),
            jax.ShapeDtypeStruct((n_kv, t, HEAD_DIM), BF16),
        ],
        scratch_shapes=[pltpu.VMEM((tm, HEAD_DIM), F32)],
        compiler_params=_params(("parallel",)),
        name="norm_qkv",
    )(x, g, w)


def _rnn_kernel(xr_ref, gr_ref, gm_ref, cw_ref, cb_ref, wa_ref, wx_ref, ba_ref, bx_ref, lam_ref,
                o_ref, xbuf, a_scr, b_scr, h_scr, *, ts):
    s_idx = pl.program_id(1)
    d = xr_ref.shape[1]
    halo = SUBLANES

    @pl.when(s_idx == 0)
    def _():
        xbuf[0:halo, :] = jnp.zeros((halo, d), F32)
        h_scr[...] = jnp.zeros(h_scr.shape, F32)

    xbuf[halo:halo + ts, :] = xr_ref[...].astype(F32)
    xc = cb_ref[...] + xbuf[halo:halo + ts, :] * cw_ref[CONV_WIDTH - 1:CONV_WIDTH, :]
    for k in range(1, CONV_WIDTH):
        xc = xc + xbuf[halo - k:halo - k + ts, :] * cw_ref[CONV_WIDTH - 1 - k:CONV_WIDTH - k, :]
    xbuf[0:halo, :] = xbuf[ts:ts + halo, :]

    xcb = xc.astype(BF16)
    n_chunks = d // LANES
    neg_sp = -LRU_C * (jnp.maximum(-lam_ref[...], 0.0) + jnp.log(1.0 + jnp.exp(-jnp.abs(lam_ref[...]))))
    for c in range(n_chunks):
        sl = slice(c * LANES, (c + 1) * LANES)
        xk = xcb[:, sl]
        r = _sigmoid(_dot(xk, wa_ref[c]) + ba_ref[:, sl])
        i = _sigmoid(_dot(xk, wx_ref[c]) + bx_ref[:, sl])
        log_a = r * neg_sp[:, sl]
        a = jnp.exp(log_a)
        a_scr[:, sl] = a
        u = 1.0 - a * a
        root = jnp.where(u > 0.0, u * lax.rsqrt(u), 0.0)
        b_scr[:, sl] = root * (i * xc[:, sl])

    def group(gi, h):
        base = pl.multiple_of(gi * SUBLANES, SUBLANES)
        a8 = a_scr[pl.ds(base, SUBLANES), :]
        b8 = b_scr[pl.ds(base, SUBLANES), :]
        rows = []
        for j in range(SUBLANES):
            h = a8[j:j + 1, :] * h + b8[j:j + 1, :]
            rows.append(h)
        a_scr[pl.ds(base, SUBLANES), :] = jnp.concatenate(rows, axis=0)
        return h

    h_last = lax.fori_loop(0, ts // SUBLANES, group, h_scr[0:1, :])
    h_scr[0:1, :] = h_last
    o_ref[...] = (_sigmoid(gm_ref[...].astype(F32)) * a_scr[...]
                  * _gelu_tanh(gr_ref[...].astype(F32))).astype(o_ref.dtype)


def _rnn(f, conv_w, conv_b, wa, wx, ba, bx, lam, batch, seq, ts):
    t = f.shape[0]
    d = conv_w.shape[1]
    ns = seq // ts
    nd = d // d
    del nd
    row = lambda b, s: b * ns + s
    vec = lambda r: pl.BlockSpec((r, d), lambda b, s: (0, 0))
    return pl.pallas_call(
        functools.partial(_rnn_kernel, ts=ts),
        grid=(batch, ns),
        in_specs=[
            pl.BlockSpec((ts, d), lambda b, s: (row(b, s), 0)),
            pl.BlockSpec((ts, d), lambda b, s: (row(b, s), 1)),
            pl.BlockSpec((ts, d), lambda b, s: (row(b, s), 2)),
            vec(CONV_WIDTH), vec(1),
            pl.BlockSpec(wa.shape, lambda b, s: (0, 0, 0)),
            pl.BlockSpec(wx.shape, lambda b, s: (0, 0, 0)),
            vec(1), vec(1), vec(1),
        ],
        out_specs=pl.BlockSpec((ts, d), lambda b, s: (row(b, s), 0)),
        out_shape=jax.ShapeDtypeStruct((t, d), BF16),
        scratch_shapes=[
            pltpu.VMEM((ts + SUBLANES, d), F32),
            pltpu.VMEM((ts, d), F32),
            pltpu.VMEM((ts, d), F32),
            pltpu.VMEM((SUBLANES, d), F32),
        ],
        compiler_params=_params(("parallel", "arbitrary")),
        name="rnn_mixer",
    )(f, f, f, conv_w, conv_b, wa, wx, ba, bx, lam)


def _compress_kernel(ak_ref, av_ref, w1k_ref, w2k_ref, pk_ref, w1v_ref, w2v_ref, pv_ref, ok_ref, ov_ref):
    def one(a_ref, w1_ref, w2_ref, p_ref, o_ref):
        a = a_ref[0]
        half = a.shape[1]
        nchunk = a.shape[0]
        lo = _dot(a, w1_ref[0:half, :])
        hi = _dot(a, w1_ref[half:2 * half, :])
        pb = _dot(jnp.broadcast_to(p_ref[...], (SUBLANES, 2 * half)).astype(BF16), w1_ref[...])[0:1, :]
        h = lo + pltpu.roll(hi, nchunk - 1, 0) + pb
        o_ref[0] = _dot(_gelu_tanh(h).astype(BF16), w2_ref[...]).astype(o_ref.dtype)

    one(ak_ref, w1k_ref, w2k_ref, pk_ref, ok_ref)
    one(av_ref, w1v_ref, w2v_ref, pv_ref, ov_ref)


def _compress(kv_chunks, w1k, w2k, pk, w1v, w2v, pv, batch, nchunk):
    g = N_KV_GROUPS
    width = kv_chunks.shape[2]
    full = lambda a: pl.BlockSpec(a.shape, lambda b, gg: (0,) * a.ndim)
    out = jax.ShapeDtypeStruct((batch * g, nchunk, HEAD_DIM), BF16)
    return pl.pallas_call(
        _compress_kernel,
        grid=(batch, g),
        in_specs=[
            pl.BlockSpec((1, nchunk, width), lambda b, gg: (gg, b, 0)),
            pl.BlockSpec((1, nchunk, width), lambda b, gg: (g + gg, b, 0)),
            full(w1k), full(w2k), full(pk), full(w1v), full(w2v), full(pv),
        ],
        out_specs=[
            pl.BlockSpec((1, nchunk, HEAD_DIM), lambda b, gg: (b * g + gg, 0, 0)),
            pl.BlockSpec((1, nchunk, HEAD_DIM), lambda b, gg: (b * g + gg, 0, 0)),
        ],
        out_shape=[out, out],
        compiler_params=_params(("parallel", "parallel")),
        name="compress",
    )(kv_chunks, kv_chunks, w1k, w2k, pk, w1v, w2v, pv)


ONES_ROWS = 16


def _scores(k, q_rows, bias, s_ref):
    s = _dot(k, q_rows)
    if bias is not None:
        s = s + bias
    s_ref[...] = s
    return jnp.max(s, axis=0, keepdims=True)


def _accumulate(s_ref, col_max, v_aug_t, state):
    m_old, acc_old = state
    m_new = jnp.maximum(m_old, col_max)
    alpha = jnp.exp(m_old - m_new)
    e = jnp.exp(s_ref[...] - m_new).astype(BF16)
    return m_new, alpha * acc_old + _dot(v_aug_t, e)


def _nsa_kernel(q_ref, kc_ref, vc_ref, ks_ref, vs_ref, kw_ref, vw_ref, gn_ref, gm_ref,
                qal_ref, cpos_ref, wsel_ref,
                o_ref, q_t, ksaug, kwaug, kcaug, kstage, vstage, s_win, s_a, s_b, win_max, o_cmp_scr, flags,
                *, tq, nk, nkw, seq, sel_chunk):
    b = pl.program_id(0)
    g = pl.program_id(1)
    qt = pl.program_id(2)
    r_heads = HEADS_PER_GROUP
    dh = HEAD_DIM
    m_cols = r_heads * tq
    nb = seq // SEL_LEN
    nc = kc_ref.shape[1]
    a_w = 2 * dh
    k_w = a_w + nb

    @pl.when((b == 0) & (g == 0) & (qt == 0))
    def _():
        rows = min(FILL_ROWS, seq)

        def fill(c, carry):
            off = pl.multiple_of(c * rows, rows)
            pos = off + lax.broadcasted_iota(jnp.int32, (rows, k_w), 0)
            lane = lax.broadcasted_iota(jnp.int32, (rows, k_w), 1)
            blk_of = jnp.right_shift(pos, SEL_SHIFT)
            cols = jnp.where((lane >= dh) & (lane < dh + 3), blk_of * SEL_LEN,
                             jnp.where((lane >= dh + 3) & (lane < dh + 6), pos & (SEL_LEN - 1),
                                       jnp.where(lane == a_w + blk_of, 1, 0)))
            cols = cols.astype(F32).astype(BF16)
            ksaug[pl.ds(off, rows), :] = cols
            kwaug[pl.ds(off, rows), :] = cols[:, 0:a_w]
            return carry

        lax.fori_loop(0, seq // rows, fill, 0)
        kcaug[...] = cpos_ref[...]
        vstage[...] = jnp.ones(vstage.shape, BF16)

    @pl.when(qt == 0)
    def _():
        ksaug[:, 0:dh] = ks_ref[0]
        kwaug[:, 0:dh] = kw_ref[0]
        kcaug[:, 0:dh] = kc_ref[0]
        q_t[dh:a_w, :] = qal_ref[0]

    v_rows = dh + ONES_ROWS
    ri = lax.broadcasted_iota(jnp.int32, (v_rows, 2 * dh), 0)
    ci = lax.broadcasted_iota(jnp.int32, (v_rows, 2 * dh), 1)
    pick = (((ri < dh) & (ri == ci)) | ((ri >= dh) & (ci == dh))).astype(F32).astype(BF16)
    transposed = lambda v: _dot_nt(pick[0:dh, 0:dh], v).astype(BF16)
    staged_aug_t = lambda v: _dot_nt(pick, v).astype(BF16)
    ones_rows = jnp.ones((ONES_ROWS, nkw), BF16)
    window_aug_t = lambda v: jnp.concatenate([transposed(v), ones_rows], axis=0)
    init = (jnp.full((1, m_cols), MASK_NEG, F32), jnp.zeros((v_rows, m_cols), F32))

    t0 = qt * tq
    col = lax.broadcasted_iota(jnp.int32, (1, m_cols), 1)
    trow = t0 + (col & (tq - 1))

    scale = np.float32(HEAD_DIM ** -0.5)
    qf = jnp.transpose(q_ref[...].astype(F32) * scale)
    for r in range(r_heads):
        q_t[0:dh, r * tq:(r + 1) * tq] = qf[r * dh:(r + 1) * dh, :].astype(BF16)

    n_wt = (WINDOW + tq) // nkw
    win_s = [s_win.at[i * nkw:(i + 1) * nkw, :] for i in range(n_wt)]
    win_off = [pl.multiple_of(jnp.maximum(t0 + tq - (i + 1) * nkw, 0), nkw) for i in range(n_wt)]

    def window_scores():
        for i in range(n_wt):
            start = t0 + tq - (i + 1) * nkw
            kpos_w = start + lax.broadcasted_iota(jnp.int32, (nkw, 1), 0)
            ok = (kpos_w <= trow) & (kpos_w > trow - WINDOW) & (start >= 0)
            bias_w = jnp.where(ok, 0.0, MASK_NEG).astype(F32)
            win_max[i:i + 1, :] = _scores(kwaug[pl.ds(win_off[i], nkw), :], q_t[0:a_w, :], bias_w, win_s[i])

    def compressed_and_select(rows_c, rows_b):
        cend = lax.broadcasted_iota(jnp.int32, (rows_c, 1), 0) * CMP_STRIDE + (CMP_LEN - 1)
        bias_c = jnp.where(cend <= trow, 0.0, MASK_NEG).astype(F32)
        s = _dot(kcaug[0:rows_c, :], q_t[0:a_w, :]) + bias_c
        m = jnp.max(s, axis=0, keepdims=True)
        e = jnp.exp(s - m)
        has_key = (trow >= CMP_LEN - 1).astype(F32)
        p = e * (has_key / jnp.sum(e, axis=0, keepdims=True))
        o_cmp_scr[...] = _dot(transposed(vc_ref[0, 0:rows_c, :]), p.astype(BF16))
        imp = p[:, 0:tq]
        for r in range(1, r_heads):
            imp = imp + p[:, r * tq:(r + 1) * tq]

        hi = imp.astype(BF16)
        r1 = imp - hi.astype(F32)
        mid = r1.astype(BF16)
        lo = (r1 - mid.astype(F32)).astype(BF16)
        wsel = wsel_ref[0:rows_b, 0:rows_c]
        imp_t = _dot(wsel, hi) + _dot(wsel, mid) + _dot(wsel, lo)

        window_scores()

        blk = lax.broadcasted_iota(jnp.int32, (rows_b, tq), 0).astype(F32)
        tq_l = t0 + lax.broadcasted_iota(jnp.int32, (rows_b, tq), 1)
        cur = jnp.right_shift(tq_l, SEL_SHIFT).astype(F32)
        valid = blk <= cur
        forced = (blk == 0.0) | (blk == cur) | (blk == cur - 1.0)
        score = jnp.where(valid & jnp.logical_not(forced), imp_t, -1.0)
        sel = jnp.where(forced, 1.0, 0.0)
        for _ in range(min(SEL_TOPK, nb) - N_FORCED):
            mx = jnp.max(score, axis=0, keepdims=True)
            idx = jnp.min(jnp.where(score == mx, blk, float(nb)), axis=0, keepdims=True)
            hit = blk == idx
            sel = jnp.where(hit, 1.0, sel)
            score = jnp.where(hit, -2.0, score)
        selv = (sel > 0.0) & valid
        selneg_t = jnp.where(selv, 0.0, MASK_NEG).astype(BF16)
        unseen = jnp.full((nb - rows_b, tq), MASK_NEG, BF16)
        for r in range(r_heads):
            q_t[a_w:a_w + rows_b, r * tq:(r + 1) * tq] = selneg_t
            if rows_b < nb:
                q_t[a_w + rows_b:, r * tq:(r + 1) * tq] = unseen

        used = jnp.max(jnp.where(selv, 1.0, 0.0), axis=1, keepdims=True)
        bit_id = lax.broadcasted_iota(jnp.int32, (rows_b, 1), 0) & (WORD_BITS - 1)
        weighted = used * jnp.left_shift(1, bit_id).astype(F32)
        for i in range(nb // WORD_BITS):
            if (i + 1) * WORD_BITS <= rows_b:
                flags[i] = jnp.sum(weighted[i * WORD_BITS:(i + 1) * WORD_BITS, :]).astype(jnp.int32)
            else:
                flags[i] = 0

    n_var = nb // sel_chunk
    cmp_chunk = sel_chunk * (SEL_LEN // CMP_STRIDE)
    seen_c = (t0 + tq - CMP_LEN) // CMP_STRIDE + 1
    seen_b = (t0 + tq) // SEL_LEN
    variant = jnp.maximum((seen_c + cmp_chunk - 1) // cmp_chunk, (seen_b + sel_chunk - 1) // sel_chunk)
    for v in range(1, n_var + 1):
        pl.when(variant == v)(functools.partial(compressed_and_select, min(v * cmp_chunk, nc), v * sel_chunk))
    o_cmp = o_cmp_scr[...]

    n_own = tq // SEL_LEN
    off_q = pl.multiple_of(t0, tq)
    kstage[0:tq, :] = ksaug[pl.ds(off_q, tq), :]
    vstage[0:tq, 0:dh] = vs_ref[0, pl.ds(off_q, tq), :]

    def stage_block(j, cnt):
        src = pl.multiple_of(j * SEL_LEN, SEL_LEN)
        dst = pl.multiple_of(cnt * SEL_LEN, SEL_LEN)
        kstage[pl.ds(dst, SEL_LEN), :] = ksaug[pl.ds(src, SEL_LEN), :]
        vstage[pl.ds(dst, SEL_LEN), 0:dh] = vs_ref[0, pl.ds(src, SEL_LEN), :]

    def gather(j, cnt):
        bit = jnp.right_shift(flags[jnp.right_shift(j, WORD_SHIFT)], j & (WORD_BITS - 1)) & 1
        pl.when(bit == 1)(functools.partial(stage_block, j, cnt))
        return cnt + bit

    n_past = jnp.right_shift(t0, SEL_SHIFT)
    near_lo = jnp.maximum(n_past - NEAR_BLOCKS, 0)
    first = jnp.where(near_lo > 0, flags[0] & 1, 0)
    pl.when(first == 1)(functools.partial(stage_block, 0, n_own))

    def far_word(w, cnt):
        lo = jnp.maximum(w * WORD_BITS, 1)
        hi = jnp.minimum((w + 1) * WORD_BITS, near_lo)
        rest = jnp.where(w == 0, flags[w] & -2, flags[w])
        return lax.cond(rest != 0, lambda c: lax.fori_loop(lo, hi, gather, c), lambda c: c, cnt)

    n_blocks = lax.fori_loop(0, jnp.right_shift(near_lo + WORD_BITS - 1, WORD_SHIFT), far_word, n_own + first)
    n_blocks = lax.fori_loop(near_lo, n_past, gather, n_blocks)
    per_tile = nk // SEL_LEN
    n_tiles = (n_blocks + per_tile - 1) // per_tile
    pad_block = jnp.where(lax.broadcasted_iota(jnp.int32, (SEL_LEN, k_w), 1) == PAD_COL, 1.0, 0.0).astype(BF16)

    def pad(j, carry):
        dst = pl.multiple_of(j * SEL_LEN, SEL_LEN)
        kstage[pl.ds(dst, SEL_LEN), :] = pad_block
        return carry

    lax.fori_loop(n_blocks, n_tiles * per_tile, pad, 0)

    srow = lax.broadcasted_iota(jnp.int32, (nk, 1), 0)
    bias_0 = jnp.where((srow >= tq) | (t0 + srow <= trow), 0.0, MASK_NEG).astype(F32)
    max_a = _scores(kstage[0:nk, :], q_t[...], bias_0, s_a)

    state = init
    for i in range(n_wt):
        state = _accumulate(win_s[i], win_max[i:i + 1, :], window_aug_t(vw_ref[0, pl.ds(win_off[i], nkw), :]), state)
    o_win = state[1][0:dh] / state[1][dh:dh + 1]

    def tile_scores(i, s_ref):
        return _scores(kstage[pl.ds(pl.multiple_of(i * nk, nk), nk), :], q_t[...], None, s_ref)

    def tile_accumulate(i, s_ref, col_max, state):
        return _accumulate(s_ref, col_max, staged_aug_t(vstage[pl.ds(pl.multiple_of(i * nk, nk), nk), :]), state)

    def slc_pair(j, carry):
        state, max_a = carry
        i = 2 * j
        max_b = tile_scores(i + 1, s_b)
        state = tile_accumulate(i, s_a, max_a, state)
        max_a = tile_scores(i + 2, s_a)
        state = tile_accumulate(i + 1, s_b, max_b, state)
        return state, max_a

    n_pairs = (n_tiles - 1) // 2
    state, max_a = lax.fori_loop(0, n_pairs, slc_pair, (init, max_a))
    i_a = 2 * n_pairs

    def two_left(state):
        max_b = tile_scores(i_a + 1, s_b)
        state = tile_accumulate(i_a, s_a, max_a, state)
        return tile_accumulate(i_a + 1, s_b, max_b, state)

    def one_left(state):
        return tile_accumulate(i_a, s_a, max_a, state)

    _, acc = lax.cond(n_tiles - i_a == 2, two_left, one_left, state)
    o_slc = acc[0:dh] / acc[dh:dh + 1]

    gates = jnp.transpose(_sigmoid(gn_ref[...]))
    heads = []
    for r in range(r_heads):
        cs = slice(r * tq, (r + 1) * tq)
        heads.append(gates[3 * r:3 * r + 1, :] * o_cmp[:, cs] + gates[3 * r + 1:3 * r + 2, :] * o_slc[:, cs]
                     + gates[3 * r + 2:3 * r + 3, :] * o_win[:, cs])
    o = jnp.transpose(jnp.concatenate(heads, axis=0))
    o_ref[...] = (_sigmoid(gm_ref[...].astype(F32)) * o).astype(o_ref.dtype)


def _alibi_tables(seq, nc, tq):
    import ml_dtypes
    bf = ml_dtypes.bfloat16
    h = np.arange(1, N_HEADS + 1, dtype=np.float32)
    slopes = np.exp2(-ALIBI_MAX_BIAS * h / N_HEADS).astype(np.float32)
    s1 = slopes.astype(bf).astype(np.float32)
    s2 = (slopes - s1).astype(bf).astype(np.float32)
    s3 = (slopes - s1 - s2).astype(bf).astype(np.float32)
    dh = HEAD_DIM
    qal = np.zeros((N_HEADS, dh), np.float32)
    for rep in range(3):
        qal[:, 3 * rep + 0] = s1
        qal[:, 3 * rep + 1] = s2
        qal[:, 3 * rep + 2] = s3
    qal[:, PAD_COL - dh] = MASK_NEG
    qal = qal.reshape(N_KV_GROUPS, HEADS_PER_GROUP, dh).transpose(0, 2, 1)
    qal_p = np.repeat(qal, tq, axis=2)

    nb = seq // SEL_LEN
    c = np.arange(nc)
    cpos = np.zeros((nc, 2 * dh), np.float32)
    cpos[:, dh:dh + 3] = ((c // 16) * 16 * CMP_STRIDE)[:, None]
    cpos[:, dh + 3:dh + 6] = ((c % 16) * CMP_STRIDE)[:, None]
    cpos[:, dh + 6:dh + 9] = CMP_LEN - 1

    r_sel = SEL_LEN // CMP_STRIDE
    r_cmp = CMP_LEN // CMP_STRIDE
    wsel = np.zeros((nb, nc), np.float32)
    for j in range(nb):
        for mm in range(r_sel):
            for nn in range(r_cmp):
                ci = r_sel * j + mm - nn
                if 0 <= ci < nc - 1:
                    wsel[j, ci] += 1.0
    as_bf = lambda a: jnp.asarray(a.astype(bf))
    return as_bf(qal_p), as_bf(cpos), as_bf(wsel)


def _nsa(qm, kvm, kcm, vcm, gnm, fm, batch, seq, tq, nk, nkw, gm_col0):
    t = qm.shape[0]
    g = N_KV_GROUPS
    dh = HEAD_DIM
    nqt = seq // tq
    nc = kcm.shape[1]
    nb = seq // SEL_LEN
    gw = HEADS_PER_GROUP * dh
    m_cols = HEADS_PER_GROUP * tq
    qal, cpos, wsel = _alibi_tables(seq, nc, tq)
    n_wt = (WINDOW + tq) // nkw
    sel_chunk = max(WORD_BITS, nb // 4)
    per_tile = nk // SEL_LEN
    stage_rows = -(-nb // per_tile) * per_tile * SEL_LEN
    row = lambda b, gg, i: b * nqt + i
    slab = lambda base: pl.BlockSpec((1, seq, dh), lambda b, gg, i: (base + gg, b, 0))
    cmp_slab = pl.BlockSpec((1, nc, dh), lambda b, gg, i: (b * g + gg, 0, 0))
    const = lambda a: pl.BlockSpec(a.shape, lambda b, gg, i: (0,) * a.ndim)
    return pl.pallas_call(
        functools.partial(_nsa_kernel, tq=tq, nk=nk, nkw=nkw, seq=seq, sel_chunk=sel_chunk),
        grid=(batch, g, nqt),
        in_specs=[
            pl.BlockSpec((tq, gw), lambda b, gg, i: (row(b, gg, i), gg)),
            cmp_slab, cmp_slab,
            slab(0), slab(g), slab(2 * g), slab(3 * g),
            pl.BlockSpec((tq, LANES), lambda b, gg, i: (row(b, gg, i), gg)),
            pl.BlockSpec((tq, gw), lambda b, gg, i: (row(b, gg, i), gm_col0 + gg)),
            pl.BlockSpec((1, dh, m_cols), lambda b, gg, i: (gg, 0, 0)),
            const(cpos), const(wsel),
        ],
        out_specs=pl.BlockSpec((tq, gw), lambda b, gg, i: (row(b, gg, i), gg)),
        out_shape=jax.ShapeDtypeStruct((t, g * gw), BF16),
        scratch_shapes=[
            pltpu.VMEM((2 * dh + nb, m_cols), BF16),
            pltpu.VMEM((seq, 2 * dh + nb), BF16),
            pltpu.VMEM((seq, 2 * dh), BF16),
            pltpu.VMEM((nc, 2 * dh), BF16),
            pltpu.VMEM((stage_rows, 2 * dh + nb), BF16),
            pltpu.VMEM((stage_rows, 2 * dh), BF16),
            pltpu.VMEM((n_wt * nkw, m_cols), F32),
            pltpu.VMEM((nk, m_cols), F32),
            pltpu.VMEM((nk, m_cols), F32),
            pltpu.VMEM((SUBLANES, m_cols), F32),
            pltpu.VMEM((dh, m_cols), F32),
            pltpu.SMEM((nb // WORD_BITS,), jnp.int32),
        ],
        compiler_params=_params(("arbitrary", "arbitrary", "arbitrary")),
        name="nsa",
    )(qm, kcm, vcm, kvm, kvm, kvm, kvm, gnm, fm, qal, cpos, wsel)


def _ffn_kernel(x_ref, yr_ref, ya_ref, wo_ref, gmix_ref, gpre_ref, wg_ref, wu_ref, wd_ref, gpost_ref,
                p_ref, wpg_ref, bpg_ref, wpp_ref, o_ref, x1_ref, h_ref, acc_ref):
    j = pl.program_id(1)

    @pl.when(j == 0)
    def _():
        y = (yr_ref[...].astype(F32) + ya_ref[...].astype(F32)).astype(BF16)
        x1 = x_ref[...] + _rmsnorm(_dot(y, wo_ref[...]), gmix_ref[...])
        x1_ref[...] = x1
        h_ref[...] = _rmsnorm(x1, gpre_ref[...]).astype(BF16)
        acc_ref[...] = jnp.zeros(acc_ref.shape, F32)

    h = h_ref[...]
    gate = _dot(h, wg_ref[...])
    up = _dot(h, wu_ref[...])
    act = (gate * _sigmoid(gate) * up).astype(BF16)
    acc_ref[...] += _dot(act, wd_ref[...])

    @pl.when(j == pl.num_programs(1) - 1)
    def _():
        x2 = x1_ref[...] + _rmsnorm(acc_ref[...], gpost_ref[...])
        gate_p = _sigmoid(_dot(x2.astype(BF16), wpg_ref[...]) + bpg_ref[...])
        o_ref[...] = x2 + gate_p * _dot(p_ref[...].astype(BF16), wpp_ref[...])


def _ffn(x, yr, ya, wo, gmix, gpre, wgu, wd, gpost, p, wpg, bpg, wpp, tm, tf):
    t, d = x.shape
    dff = wd.shape[0]
    nf = dff // tf
    dp = p.shape[1]
    rows = pl.BlockSpec((tm, d), lambda i, j: (i, 0))
    vec = pl.BlockSpec((1, d), lambda i, j: (0, 0))
    square = pl.BlockSpec((d, d), lambda i, j: (0, 0))
    return pl.pallas_call(
        _ffn_kernel,
        grid=(t // tm, nf),
        in_specs=[
            rows, rows, rows, square, vec, vec,
            pl.BlockSpec((d, tf), lambda i, j: (0, j)),
            pl.BlockSpec((d, tf), lambda i, j: (0, nf + j)),
            pl.BlockSpec((tf, d), lambda i, j: (j, 0)),
            vec,
            pl.BlockSpec((tm, dp), lambda i, j: (i, 0)),
            square,
            vec,
            pl.BlockSpec((dp, d), lambda i, j: (0, 0)),
        ],
        out_specs=rows,
        out_shape=jax.ShapeDtypeStruct((t, d), F32),
        scratch_shapes=[pltpu.VMEM((tm, d), F32), pltpu.VMEM((tm, d), BF16), pltpu.VMEM((tm, d), F32)],
        compiler_params=_params(("parallel", "arbitrary")),
        name="out_ffn_ple",
    )(x, yr, ya, wo, gmix, gpre, wgu, wgu, wd, gpost, p, wpg, bpg, wpp)


def _block_diag_chunks(w):
    n, bs, _ = w.shape
    per = LANES // bs
    w = w.reshape(n // per, per, bs, bs)
    eye = jnp.eye(per, dtype=w.dtype)
    return jnp.einsum('cpij,pq->cpiqj', w, eye).reshape(n // per, LANES, LANES)


def _layer(x, p, norm_mix_pre, norm_mix_post, w_in, conv_w, conv_b, lru_wa, lru_ba, lru_wx, lru_bx,
           lru_lambda, cmp_pos_k, cmp_pos_v, cmp_k_w1, cmp_k_w2, cmp_v_w1, cmp_v_w2, w_out,
           norm_ffn_pre, norm_ffn_post, ffn_w_gate_up, ffn_w_down, ple_w_proj, ple_w_gate, ple_b_gate,
           batch, seq):
    t, d = x.shape
    d_attn = N_HEADS * HEAD_DIM
    d_kv = N_KV_GROUPS * HEAD_DIM
    row2 = lambda v: v.reshape(1, -1)

    o_q = 2 * d
    o_kv = o_q + d_attn
    o_gn = o_kv + 6 * d_kv
    o_gm = o_gn + 3 * N_HEADS
    w_f = jnp.concatenate([w_in[:, 0:o_q], w_in[:, o_gm:o_gm + 2 * d]], axis=1).astype(BF16)
    w_q = w_in[:, o_q:o_kv].astype(BF16)
    w_kv = w_in[:, o_kv:o_gn].astype(BF16)
    per_g = 3 * HEADS_PER_GROUP
    w_gn = w_in[:, o_gn:o_gm].reshape(d, N_KV_GROUPS, per_g)
    w_gn = jnp.pad(w_gn, ((0, 0), (0, 0), (0, LANES - per_g))).reshape(d, N_KV_GROUPS * LANES).astype(BF16)

    g_pre = row2(norm_mix_pre)
    tm = min(512, t)
    fm = _norm_matmul(x, g_pre, w_f, BF16, tm, w_f.shape[1])
    qm, gnm, kv_chunks, kvm = _norm_qkv(x, g_pre, jnp.concatenate([w_q, w_gn, w_kv], axis=1), d_attn,
                                        N_KV_GROUPS * LANES, 2 * N_KV_GROUPS, tm)

    yr = _rnn(fm, conv_w, row2(conv_b), _block_diag_chunks(lru_wa).astype(BF16),
              _block_diag_chunks(lru_wx).astype(BF16), row2(lru_ba), row2(lru_bx), row2(lru_lambda),
              batch, seq, min(512, seq))

    nchunk = seq // CMP_STRIDE
    kcm, vcm = _compress(kv_chunks, cmp_k_w1.astype(BF16), cmp_k_w2.astype(BF16), cmp_pos_k.reshape(1, -1),
                         cmp_v_w1.astype(BF16), cmp_v_w2.astype(BF16), cmp_pos_v.reshape(1, -1), batch, nchunk)

    tq = min(256, seq)
    ya = _nsa(qm, kvm, kcm, vcm, gnm, fm, batch, seq, tq, min(NSA_KEY_TILE, seq), min(256, seq),
              gm_col0=(3 * d) // (HEADS_PER_GROUP * HEAD_DIM))

    dff = ffn_w_down.shape[0]
    tf = dff // 2 if (dff // 2) % LANES == 0 else dff
    return _ffn(x, yr, ya, w_out.astype(BF16), row2(norm_mix_post), row2(norm_ffn_pre),
                ffn_w_gate_up.astype(BF16), ffn_w_down.astype(BF16), row2(norm_ffn_post),
                p, ple_w_gate.astype(BF16), row2(ple_b_gate), ple_w_proj.astype(BF16), tm, tf)


def kernel(x, p, norm_mix_pre, norm_mix_post, w_in, conv_w, conv_b, lru_wa, lru_ba, lru_wx, lru_bx, lru_lambda, cmp_pos_k, cmp_pos_v, cmp_k_w1, cmp_k_w2, cmp_v_w1, cmp_v_w2, w_out, norm_ffn_pre, norm_ffn_post, ffn_w_gate_up, ffn_w_down, ple_w_proj, ple_w_gate, ple_b_gate):
    batch, seq, d = x.shape
    depth = w_in.shape[0]
    xf = x.reshape(batch * seq, d)
    for i in range(depth):
        xf = _layer(xf, p[i].reshape(batch * seq, -1), norm_mix_pre[i], norm_mix_post[i], w_in[i], conv_w[i],
                    conv_b[i], lru_wa[i], lru_ba[i], lru_wx[i], lru_bx[i], lru_lambda[i], cmp_pos_k[i],
                    cmp_pos_v[i], cmp_k_w1[i], cmp_k_w2[i], cmp_v_w1[i], cmp_v_w2[i], w_out[i],
                    norm_ffn_pre[i], norm_ffn_post[i], ffn_w_gate_up[i], ffn_w_down[i], ple_w_proj[i],
                    ple_w_gate[i], ple_b_gate[i], batch, seq)
    return xf.reshape(batch, seq, d)
```

```python
import functools

import numpy as np
import jax
import jax.numpy as jnp
from jax import lax
from jax.experimental import pallas as pl
from jax.experimental.pallas import tpu as pltpu

N_LRU_BLOCKS = 16
CONV_WIDTH = 4
LRU_C = 8.0
N_HEADS = 16
HEAD_DIM = 64
N_KV_GROUPS = 4
HEADS_PER_GROUP = N_HEADS // N_KV_GROUPS
CMP_LEN = 32
CMP_STRIDE = 16
SEL_LEN = 64
SEL_TOPK = 16
WINDOW = 512
FORCE_SCORE = 1e4
ALIBI_MAX_BIAS = 8.0
NORM_EPS = 1e-6

LANES = 128
SUBLANES = 8
VMEM_LIMIT_BYTES = 56 * 1024 * 1024

MASK_NEG = -1e30
N_ALIBI_COLS = 9
PAD_COL = HEAD_DIM + N_ALIBI_COLS
WORD_BITS = 16
WORD_SHIFT = 4
SEL_SHIFT = 6
N_FORCED = 3
NEAR_BLOCKS = 24
FILL_ROWS = 512
NSA_KEY_TILE = 11 * SEL_LEN
BF16 = jnp.bfloat16
F32 = jnp.float32


def _dot(a, b):
    return jnp.dot(a, b, preferred_element_type=F32)


def _dot_nt(a, b):
    return lax.dot_general(a, b, (((1,), (1,)), ((), ())), preferred_element_type=F32)


def _sigmoid(x):
    return 0.5 * jnp.tanh(0.5 * x) + 0.5


def _gelu_tanh(x):
    c = np.float32(np.sqrt(2.0 / np.pi))
    half = 0.5 * x
    return half + half * jnp.tanh(x * (c + (c * 0.044715) * (x * x)))


def _rmsnorm(x, g):
    ms = jnp.mean(x * x, axis=-1, keepdims=True)
    return x * lax.rsqrt(ms + NORM_EPS) * g


def _params(sem):
    return pltpu.CompilerParams(dimension_semantics=sem, vmem_limit_bytes=VMEM_LIMIT_BYTES)


def _norm_matmul_kernel(x_ref, g_ref, w_ref, o_ref, h_ref):
    @pl.when(pl.program_id(1) == 0)
    def _():
        h_ref[...] = _rmsnorm(x_ref[...], g_ref[...]).astype(BF16)

    o_ref[...] = _dot(h_ref[...], w_ref[...]).astype(o_ref.dtype)


def _norm_matmul(x, g, w, out_dtype, tm, tn):
    t, k = x.shape
    n = w.shape[1]
    return pl.pallas_call(
        _norm_matmul_kernel,
        grid=(t // tm, n // tn),
        in_specs=[
            pl.BlockSpec((tm, k), lambda i, j: (i, 0)),
            pl.BlockSpec((1, k), lambda i, j: (0, 0)),
            pl.BlockSpec((k, tn), lambda i, j: (0, j)),
        ],
        out_specs=pl.BlockSpec((tm, tn), lambda i, j: (i, j)),
        out_shape=jax.ShapeDtypeStruct((t, n), out_dtype),
        scratch_shapes=[pltpu.VMEM((tm, k), BF16)],
        compiler_params=_params(("parallel", "arbitrary")),
        name="norm_matmul",
    )(x, g, w)


def _norm_qkv_kernel(x_ref, g_ref, w_ref, q_ref, gn_ref, cv_ref, kv_ref, slab_scr):
    h = _rmsnorm(x_ref[...], g_ref[...]).astype(BF16)
    z = _dot(h, w_ref[...])
    nq = q_ref.shape[1]
    ngn = gn_ref.shape[1]
    q_ref[...] = z[:, 0:nq].astype(q_ref.dtype)
    gn_ref[...] = z[:, nq:nq + ngn]
    n_cv = cv_ref.shape[0]
    rows = cv_ref.shape[1]
    for c in range(n_cv):
        lo = nq + ngn + c * HEAD_DIM
        slab_scr[...] = z[:, lo:lo + HEAD_DIM]
        for l in range(CMP_STRIDE):
            cv_ref[c, :, l * HEAD_DIM:(l + 1) * HEAD_DIM] = (
                slab_scr[pl.ds(l, rows, stride=CMP_STRIDE), :].astype(cv_ref.dtype))
    for c in range(kv_ref.shape[0]):
        lo = nq + ngn + (n_cv + c) * HEAD_DIM
        kv_ref[c] = z[:, lo:lo + HEAD_DIM].astype(kv_ref.dtype)


def _norm_qkv(x, g, w, nq, ngn, n_cv, tm):
    t, k = x.shape
    n = w.shape[1]
    n_kv = (n - nq - ngn) // HEAD_DIM - n_cv
    return pl.pallas_call(
        _norm_qkv_kernel,
        grid=(t // tm,),
        in_specs=[
            pl.BlockSpec((tm, k), lambda i: (i, 0)),
            pl.BlockSpec((1, k), lambda i: (0, 0)),
            pl.BlockSpec((k, n), lambda i: (0, 0)),
        ],
        out_specs=[
            pl.BlockSpec((tm, nq), lambda i: (i, 0)),
            pl.BlockSpec((tm, ngn), lambda i: (i, 0)),
            pl.BlockSpec((n_cv, tm // CMP_STRIDE, CMP_STRIDE * HEAD_DIM), lambda i: (0, i, 0)),
            pl.BlockSpec((n_kv, tm, HEAD_DIM), lambda i: (0, i, 0)),
        ],
        out_shape=[
            jax.ShapeDtypeStruct((t, nq), BF16),
            jax.ShapeDtypeStruct((t, ngn), F32),
            jax.ShapeDtypeStruct((n_cv, t // CMP_STRIDE, CMP_STRIDE * HEAD_DIM), BF16),
            jax.ShapeDtypeStruct((n_kv, t, HEAD_DIM), BF16),
        ],
        scratch_shapes=[pltpu.VMEM((tm, HEAD_DIM), F32)],
        compiler_params=_params(("parallel",)),
        name="norm_qkv",
    )(x, g, w)


def _rnn_kernel(xr_ref, gr_ref, gm_ref, cw_ref, cb_ref, wa_ref, wx_ref, ba_ref, bx_ref, lam_ref,
                o_ref, xbuf, a_scr, b_scr, h_scr, *, ts):
    s_idx = pl.program_id(1)
    d = xr_ref.shape[1]
    halo = SUBLANES

    @pl.when(s_idx == 0)
    def _():
        xbuf[0:halo, :] = jnp.zeros((halo, d), F32)
        h_scr[...] = jnp.zeros(h_scr.shape, F32)

    xbuf[halo:halo + ts, :] = xr_ref[...].astype(F32)
    xc = cb_ref[...] + xbuf[halo:halo + ts, :] * cw_ref[CONV_WIDTH - 1:CONV_WIDTH, :]
    for k in range(1, CONV_WIDTH):
        xc = xc + xbuf[halo - k:halo - k + ts, :] * cw_ref[CONV_WIDTH - 1 - k:CONV_WIDTH - k, :]
    xbuf[0:halo, :] = xbuf[ts:ts + halo, :]

    xcb = xc.astype(BF16)
    n_chunks = d // LANES
    neg_sp = -LRU_C * (jnp.maximum(-lam_ref[...], 0.0) + jnp.log(1.0 + jnp.exp(-jnp.abs(lam_ref[...]))))
    for c in range(n_chunks):
        sl = slice(c * LANES, (c + 1) * LANES)
        xk = xcb[:, sl]
        r = _sigmoid(_dot(xk, wa_ref[c]) + ba_ref[:, sl])
        i = _sigmoid(_dot(xk, wx_ref[c]) + bx_ref[:, sl])
        log_a = r * neg_sp[:, sl]
        a = jnp.exp(log_a)
        a_scr[:, sl] = a
        u = 1.0 - a * a
        root = jnp.where(u > 0.0, u * lax.rsqrt(u), 0.0)
        b_scr[:, sl] = root * (i * xc[:, sl])

    def group(gi, h):
        base = pl.multiple_of(gi * SUBLANES, SUBLANES)
        a8 = a_scr[pl.ds(base, SUBLANES), :]
        b8 = b_scr[pl.ds(base, SUBLANES), :]
        rows = []
        for j in range(SUBLANES):
            h = a8[j:j + 1, :] * h + b8[j:j + 1, :]
            rows.append(h)
        a_scr[pl.ds(base, SUBLANES), :] = jnp.concatenate(rows, axis=0)
        return h

    h_last = lax.fori_loop(0, ts // SUBLANES, group, h_scr[0:1, :])
    h_scr[0:1, :] = h_last
    o_ref[...] = (_sigmoid(gm_ref[...].astype(F32)) * a_scr[...]
                  * _gelu_tanh(gr_ref[...].astype(F32))).astype(o_ref.dtype)


def _rnn(f, conv_w, conv_b, wa, wx, ba, bx, lam, batch, seq, ts):
    t = f.shape[0]
    d = conv_w.shape[1]
    ns = seq // ts
    nd = d // d
    del nd
    row = lambda b, s: b * ns + s
    vec = lambda r: pl.BlockSpec((r, d), lambda b, s: (0, 0))
    return pl.pallas_call(
        functools.partial(_rnn_kernel, ts=ts),
        grid=(batch, ns),
        in_specs=[
            pl.BlockSpec((ts, d), lambda b, s: (row(b, s), 0)),
            pl.BlockSpec((ts, d), lambda b, s: (row(b, s), 1)),
            pl.BlockSpec((ts, d), lambda b, s: (row(b, s), 2)),
            vec(CONV_WIDTH), vec(1),
            pl.BlockSpec(wa.shape, lambda b, s: (0, 0, 0)),
            pl.BlockSpec(wx.shape, lambda b, s: (0, 0, 0)),
            vec(1), vec(1), vec(1),
        ],
        out_specs=pl.BlockSpec((ts, d), lambda b, s: (row(b, s), 0)),
        out_shape=jax.ShapeDtypeStruct((t, d), BF16),
        scratch_shapes=[
            pltpu.VMEM((ts + SUBLANES, d), F32),
            pltpu.VMEM((ts, d), F32),
            pltpu.VMEM((ts, d), F32),
            pltpu.VMEM((SUBLANES, d), F32),
        ],
        compiler_params=_params(("parallel", "arbitrary")),
        name="rnn_mixer",
    )(f, f, f, conv_w, conv_b, wa, wx, ba, bx, lam)


def _compress_kernel(ak_ref, av_ref, w1k_ref, w2k_ref, pk_ref, w1v_ref, w2v_ref, pv_ref, ok_ref, ov_ref):
    def one(a_ref, w1_ref, w2_ref, p_ref, o_ref):
        a = a_ref[0]
        half = a.shape[1]
        nchunk = a.shape[0]
        lo = _dot(a, w1_ref[0:half, :])
        hi = _dot(a, w1_ref[half:2 * half, :])
        pb = _dot(jnp.broadcast_to(p_ref[...], (SUBLANES, 2 * half)).astype(BF16), w1_ref[...])[0:1, :]
        h = lo + pltpu.roll(hi, nchunk - 1, 0) + pb
        o_ref[0] = _dot(_gelu_tanh(h).astype(BF16), w2_ref[...]).astype(o_ref.dtype)

    one(ak_ref, w1k_ref, w2k_ref, pk_ref, ok_ref)
    one(av_ref, w1v_ref, w2v_ref, pv_ref, ov_ref)


def _compress(kv_chunks, w1k, w2k, pk, w1v, w2v, pv, batch, nchunk):
    g = N_KV_GROUPS
    width = kv_chunks.shape[2]
    full = lambda a: pl.BlockSpec(a.shape, lambda b, gg: (0,) * a.ndim)
    out = jax.ShapeDtypeStruct((batch * g, nchunk, HEAD_DIM), BF16)
    return pl.pallas_call(
        _compress_kernel,
        grid=(batch, g),
        in_specs=[
            pl.BlockSpec((1, nchunk, width), lambda b, gg: (gg, b, 0)),
            pl.BlockSpec((1, nchunk, width), lambda b, gg: (g + gg, b, 0)),
            full(w1k), full(w2k), full(pk), full(w1v), full(w2v), full(pv),
        ],
        out_specs=[
            pl.BlockSpec((1, nchunk, HEAD_DIM), lambda b, gg: (b * g + gg, 0, 0)),
            pl.BlockSpec((1, nchunk, HEAD_DIM), lambda b, gg: (b * g + gg, 0, 0)),
        ],
        out_shape=[out, out],
        compiler_params=_params(("parallel", "parallel")),
        name="compress",
    )(kv_chunks, kv_chunks, w1k, w2k, pk, w1v, w2v, pv)


ONES_ROWS = 16


def _scores(k, q_rows, s_ref, keep=None, keep_rows=None, shift=None):
    s = _dot(k, q_rows)
    if shift is not None:
        s = s + shift
    if keep is None:
        s_ref[...] = s
        return jnp.max(s, axis=0, keepdims=True)
    n = s.shape[0] if keep_rows is None else keep_rows
    top = s[0:n]
    for mask in keep:
        top = jnp.where(mask, top, MASK_NEG)
    s_ref[0:n, :] = top
    col_max = jnp.max(top, axis=0, keepdims=True)
    if n < s.shape[0]:
        s_ref[n:, :] = s[n:]
        col_max = jnp.maximum(col_max, jnp.max(s[n:], axis=0, keepdims=True))
    return col_max


def _accumulate(s_ref, col_max, v_aug_t, state):
    m_old, acc_old = state
    m_new = jnp.maximum(m_old, col_max)
    alpha = jnp.exp(m_old - m_new)
    e = jnp.exp(s_ref[...] - m_new).astype(BF16)
    return m_new, alpha * acc_old + _dot(v_aug_t, e)


def _nsa_kernel(q_ref, kc_ref, vc_ref, ks_ref, vs_ref, kw_ref, vw_ref, gn_ref, gm_ref,
                qal_ref, cpos_ref, wsel_ref,
                o_ref, q_t, ksaug, kwaug, kcaug, kstage, vstage, s_win, s_a, s_b, win_max, o_cmp_scr, flags,
                *, tq, nk, nkw, seq, sel_chunk):
    b = pl.program_id(0)
    g = pl.program_id(1)
    qt = pl.program_id(2)
    r_heads = HEADS_PER_GROUP
    dh = HEAD_DIM
    m_cols = r_heads * tq
    nb = seq // SEL_LEN
    nc = kc_ref.shape[1]
    a_w = 2 * dh
    k_w = a_w + nb

    @pl.when((b == 0) & (g == 0) & (qt == 0))
    def _():
        rows = min(FILL_ROWS, seq)

        def fill(c, carry):
            off = pl.multiple_of(c * rows, rows)
            pos = off + lax.broadcasted_iota(jnp.int32, (rows, k_w), 0)
            lane = lax.broadcasted_iota(jnp.int32, (rows, k_w), 1)
            blk_of = jnp.right_shift(pos, SEL_SHIFT)
            cols = jnp.where((lane >= dh) & (lane < dh + 3), blk_of * SEL_LEN,
                             jnp.where((lane >= dh + 3) & (lane < dh + 6), pos & (SEL_LEN - 1),
                                       jnp.where(lane == a_w + blk_of, 1, 0)))
            cols = cols.astype(F32).astype(BF16)
            ksaug[pl.ds(off, rows), :] = cols
            kwaug[pl.ds(off, rows), :] = cols[:, 0:a_w]
            return carry

        lax.fori_loop(0, seq // rows, fill, 0)
        kcaug[...] = cpos_ref[...]
        vstage[...] = jnp.ones(vstage.shape, BF16)

    @pl.when(qt == 0)
    def _():
        ksaug[:, 0:dh] = ks_ref[0]
        kwaug[:, 0:dh] = kw_ref[0]
        kcaug[:, 0:dh] = kc_ref[0]
        q_t[dh:a_w, :] = qal_ref[0]

    v_rows = dh + ONES_ROWS
    ri = lax.broadcasted_iota(jnp.int32, (v_rows, 2 * dh), 0)
    ci = lax.broadcasted_iota(jnp.int32, (v_rows, 2 * dh), 1)
    pick = (((ri < dh) & (ri == ci)) | ((ri >= dh) & (ci == dh))).astype(F32).astype(BF16)
    transposed = lambda v: _dot_nt(pick[0:dh, 0:dh], v).astype(BF16)
    staged_aug_t = lambda v: _dot_nt(pick, v).astype(BF16)
    ones_rows = jnp.ones((ONES_ROWS, nkw), BF16)
    window_aug_t = lambda v: jnp.concatenate([transposed(v), ones_rows], axis=0)
    init = (jnp.full((1, m_cols), MASK_NEG, F32), jnp.zeros((v_rows, m_cols), F32))

    t0 = qt * tq
    col = lax.broadcasted_iota(jnp.int32, (1, m_cols), 1)
    trow = t0 + (col & (tq - 1))

    scale = np.float32(HEAD_DIM ** -0.5)
    qf = jnp.transpose(q_ref[...].astype(F32) * scale)
    for r in range(r_heads):
        q_t[0:dh, r * tq:(r + 1) * tq] = qf[r * dh:(r + 1) * dh, :].astype(BF16)

    n_wt = (WINDOW + tq) // nkw
    win_s = [s_win.at[i * nkw:(i + 1) * nkw, :] for i in range(n_wt)]
    win_off = [pl.multiple_of(jnp.maximum(t0 + tq - (i + 1) * nkw, 0), nkw) for i in range(n_wt)]

    def window_scores():
        for i in range(n_wt):
            start = t0 + tq - (i + 1) * nkw
            kpos_w = start + lax.broadcasted_iota(jnp.int32, (nkw, 1), 0)
            keep = []
            if i * nkw < tq - 1:
                keep.append(kpos_w <= trow)
            if (i + 1) * nkw > WINDOW:
                keep.append(kpos_w > trow - WINDOW)
            shift = jnp.where(start >= 0, 0.0, MASK_NEG) if (i + 1) * nkw > tq else None
            win_max[i:i + 1, :] = _scores(kwaug[pl.ds(win_off[i], nkw), :], q_t[0:a_w, :], win_s[i],
                                          keep=keep or None, shift=shift)

    def compressed_and_select(rows_c, rows_b):
        cend = lax.broadcasted_iota(jnp.int32, (rows_c, 1), 0) * CMP_STRIDE + (CMP_LEN - 1)
        s = jnp.where(cend <= trow, _dot(kcaug[0:rows_c, :], q_t[0:a_w, :]), MASK_NEG)
        m = jnp.max(s, axis=0, keepdims=True)
        e = jnp.exp(s - m)
        has_key = (trow >= CMP_LEN - 1).astype(F32)
        p = e * (has_key / jnp.sum(e, axis=0, keepdims=True))
        o_cmp_scr[...] = _dot(transposed(vc_ref[0, 0:rows_c, :]), p.astype(BF16))
        imp = p[:, 0:tq]
        for r in range(1, r_heads):
            imp = imp + p[:, r * tq:(r + 1) * tq]

        hi = imp.astype(BF16)
        r1 = imp - hi.astype(F32)
        mid = r1.astype(BF16)
        lo = (r1 - mid.astype(F32)).astype(BF16)
        wsel = wsel_ref[0:rows_b, 0:rows_c]
        imp_t = _dot(wsel, hi) + _dot(wsel, mid) + _dot(wsel, lo)

        window_scores()

        blk = lax.broadcasted_iota(jnp.int32, (rows_b, tq), 0).astype(F32)
        tq_l = t0 + lax.broadcasted_iota(jnp.int32, (rows_b, tq), 1)
        cur = jnp.right_shift(tq_l, SEL_SHIFT).astype(F32)
        valid = blk <= cur
        sel = jnp.where(blk == 0.0, 1.0, jnp.where(blk == cur, 1.0, jnp.where(blk == cur - 1.0, 1.0, 0.0)))
        score = jnp.where(valid, jnp.where(sel > 0.0, -1.0, imp_t), -1.0)
        for _ in range(min(SEL_TOPK, nb) - N_FORCED):
            mx = jnp.max(score, axis=0, keepdims=True)
            idx = jnp.min(jnp.where(score == mx, blk, float(nb)), axis=0, keepdims=True)
            hit = blk == idx
            sel = jnp.where(hit, 1.0, sel)
            score = jnp.where(hit, -2.0, score)
        selv = jnp.where(valid, sel, 0.0) > 0.0
        selneg_t = jnp.where(selv, 0.0, MASK_NEG).astype(BF16)
        unseen = jnp.full((nb - rows_b, tq), MASK_NEG, BF16)
        for r in range(r_heads):
            q_t[a_w:a_w + rows_b, r * tq:(r + 1) * tq] = selneg_t
            if rows_b < nb:
                q_t[a_w + rows_b:, r * tq:(r + 1) * tq] = unseen

        used = jnp.max(jnp.where(selv, 1.0, 0.0), axis=1, keepdims=True)
        bit_id = lax.broadcasted_iota(jnp.int32, (rows_b, 1), 0) & (WORD_BITS - 1)
        weighted = used * jnp.left_shift(1, bit_id).astype(F32)
        for i in range(nb // WORD_BITS):
            if (i + 1) * WORD_BITS <= rows_b:
                flags[i] = jnp.sum(weighted[i * WORD_BITS:(i + 1) * WORD_BITS, :]).astype(jnp.int32)
            else:
                flags[i] = 0

    n_var = nb // sel_chunk
    cmp_chunk = sel_chunk * (SEL_LEN // CMP_STRIDE)
    seen_c = (t0 + tq - CMP_LEN) // CMP_STRIDE + 1
    seen_b = (t0 + tq) // SEL_LEN
    variant = jnp.maximum((seen_c + cmp_chunk - 1) // cmp_chunk, (seen_b + sel_chunk - 1) // sel_chunk)
    for v in range(1, n_var + 1):
        pl.when(variant == v)(functools.partial(compressed_and_select, min(v * cmp_chunk, nc), v * sel_chunk))
    o_cmp = o_cmp_scr[...]

    n_own = tq // SEL_LEN
    off_q = pl.multiple_of(t0, tq)
    kstage[0:tq, :] = ksaug[pl.ds(off_q, tq), :]
    vstage[0:tq, 0:dh] = vs_ref[0, pl.ds(off_q, tq), :]

    def stage_block(j, cnt):
        src = pl.multiple_of(j * SEL_LEN, SEL_LEN)
        dst = pl.multiple_of(cnt * SEL_LEN, SEL_LEN)
        kstage[pl.ds(dst, SEL_LEN), :] = ksaug[pl.ds(src, SEL_LEN), :]
        vstage[pl.ds(dst, SEL_LEN), 0:dh] = vs_ref[0, pl.ds(src, SEL_LEN), :]

    def gather(j, cnt):
        bit = jnp.right_shift(flags[jnp.right_shift(j, WORD_SHIFT)], j & (WORD_BITS - 1)) & 1
        pl.when(bit == 1)(functools.partial(stage_block, j, cnt))
        return cnt + bit

    n_past = jnp.right_shift(t0, SEL_SHIFT)
    near_lo = jnp.maximum(n_past - NEAR_BLOCKS, 0)
    first = jnp.where(near_lo > 0, flags[0] & 1, 0)
    pl.when(first == 1)(functools.partial(stage_block, 0, n_own))

    def far_word(w, cnt):
        lo = jnp.maximum(w * WORD_BITS, 1)
        hi = jnp.minimum((w + 1) * WORD_BITS, near_lo)
        rest = jnp.where(w == 0, flags[w] & -2, flags[w])
        return lax.cond(rest != 0, lambda c: lax.fori_loop(lo, hi, gather, c), lambda c: c, cnt)

    n_blocks = lax.fori_loop(0, jnp.right_shift(near_lo + WORD_BITS - 1, WORD_SHIFT), far_word, n_own + first)
    n_blocks = lax.fori_loop(near_lo, n_past, gather, n_blocks)
    per_tile = nk // SEL_LEN
    n_tiles = (n_blocks + per_tile - 1) // per_tile
    pad_block = jnp.where(lax.broadcasted_iota(jnp.int32, (SEL_LEN, k_w), 1) == PAD_COL, 1.0, 0.0).astype(BF16)

    def pad(j, carry):
        dst = pl.multiple_of(j * SEL_LEN, SEL_LEN)
        kstage[pl.ds(dst, SEL_LEN), :] = pad_block
        return carry

    lax.fori_loop(n_blocks, n_tiles * per_tile, pad, 0)

    own_pos = t0 + lax.broadcasted_iota(jnp.int32, (tq, 1), 0)
    max_a = _scores(kstage[0:nk, :], q_t[...], s_a, keep=[own_pos <= trow], keep_rows=tq)

    state = init
    for i in range(n_wt):
        state = _accumulate(win_s[i], win_max[i:i + 1, :], window_aug_t(vw_ref[0, pl.ds(win_off[i], nkw), :]), state)
    o_win = state[1][0:dh] / state[1][dh:dh + 1]

    def tile_scores(i, s_ref):
        return _scores(kstage[pl.ds(pl.multiple_of(i * nk, nk), nk), :], q_t[...], s_ref)

    def tile_accumulate(i, s_ref, col_max, state):
        return _accumulate(s_ref, col_max, staged_aug_t(vstage[pl.ds(pl.multiple_of(i * nk, nk), nk), :]), state)

    def slc_pair(j, carry):
        state, max_a = carry
        i = 2 * j
        max_b = tile_scores(i + 1, s_b)
        state = tile_accumulate(i, s_a, max_a, state)
        max_a = tile_scores(i + 2, s_a)
        state = tile_accumulate(i + 1, s_b, max_b, state)
        return state, max_a

    n_pairs = (n_tiles - 1) // 2
    state, max_a = lax.fori_loop(0, n_pairs, slc_pair, (init, max_a))
    i_a = 2 * n_pairs

    def two_left(state):
        max_b = tile_scores(i_a + 1, s_b)
        state = tile_accumulate(i_a, s_a, max_a, state)
        return tile_accumulate(i_a + 1, s_b, max_b, state)

    def one_left(state):
        return tile_accumulate(i_a, s_a, max_a, state)

    _, acc = lax.cond(n_tiles - i_a == 2, two_left, one_left, state)
    o_slc = acc[0:dh] / acc[dh:dh + 1]

    gates = jnp.transpose(_sigmoid(gn_ref[...]))
    heads = []
    for r in range(r_heads):
        cs = slice(r * tq, (r + 1) * tq)
        heads.append(gates[3 * r:3 * r + 1, :] * o_cmp[:, cs] + gates[3 * r + 1:3 * r + 2, :] * o_slc[:, cs]
                     + gates[3 * r + 2:3 * r + 3, :] * o_win[:, cs])
    o = jnp.transpose(jnp.concatenate(heads, axis=0))
    o_ref[...] = (_sigmoid(gm_ref[...].astype(F32)) * o).astype(o_ref.dtype)


def _alibi_tables(seq, nc, tq):
    import ml_dtypes
    bf = ml_dtypes.bfloat16
    h = np.arange(1, N_HEADS + 1, dtype=np.float32)
    slopes = np.exp2(-ALIBI_MAX_BIAS * h / N_HEADS).astype(np.float32)
    s1 = slopes.astype(bf).astype(np.float32)
    s2 = (slopes - s1).astype(bf).astype(np.float32)
    s3 = (slopes - s1 - s2).astype(bf).astype(np.float32)
    dh = HEAD_DIM
    qal = np.zeros((N_HEADS, dh), np.float32)
    for rep in range(3):
        qal[:, 3 * rep + 0] = s1
        qal[:, 3 * rep + 1] = s2
        qal[:, 3 * rep + 2] = s3
    qal[:, PAD_COL - dh] = MASK_NEG
    qal = qal.reshape(N_KV_GROUPS, HEADS_PER_GROUP, dh).transpose(0, 2, 1)
    qal_p = np.repeat(qal, tq, axis=2)

    nb = seq // SEL_LEN
    c = np.arange(nc)
    cpos = np.zeros((nc, 2 * dh), np.float32)
    cpos[:, dh:dh + 3] = ((c // 16) * 16 * CMP_STRIDE)[:, None]
    cpos[:, dh + 3:dh + 6] = ((c % 16) * CMP_STRIDE)[:, None]
    cpos[:, dh + 6:dh + 9] = CMP_LEN - 1

    r_sel = SEL_LEN // CMP_STRIDE
    r_cmp = CMP_LEN // CMP_STRIDE
    wsel = np.zeros((nb, nc), np.float32)
    for j in range(nb):
        for mm in range(r_sel):
            for nn in range(r_cmp):
                ci = r_sel * j + mm - nn
                if 0 <= ci < nc - 1:
                    wsel[j, ci] += 1.0
    as_bf = lambda a: jnp.asarray(a.astype(bf))
    return as_bf(qal_p), as_bf(cpos), as_bf(wsel)


def _nsa(qm, kvm, kcm, vcm, gnm, fm, batch, seq, tq, nk, nkw, gm_col0):
    t = qm.shape[0]
    g = N_KV_GROUPS
    dh = HEAD_DIM
    nqt = seq // tq
    nc = kcm.shape[1]
    nb = seq // SEL_LEN
    gw = HEADS_PER_GROUP * dh
    m_cols = HEADS_PER_GROUP * tq
    qal, cpos, wsel = _alibi_tables(seq, nc, tq)
    n_wt = (WINDOW + tq) // nkw
    sel_chunk = max(WORD_BITS, nb // 4)
    per_tile = nk // SEL_LEN
    stage_rows = -(-nb // per_tile) * per_tile * SEL_LEN
    row = lambda b, gg, i: b * nqt + i
    slab = lambda base: pl.BlockSpec((1, seq, dh), lambda b, gg, i: (base + gg, b, 0))
    cmp_slab = pl.BlockSpec((1, nc, dh), lambda b, gg, i: (b * g + gg, 0, 0))
    const = lambda a: pl.BlockSpec(a.shape, lambda b, gg, i: (0,) * a.ndim)
    return pl.pallas_call(
        functools.partial(_nsa_kernel, tq=tq, nk=nk, nkw=nkw, seq=seq, sel_chunk=sel_chunk),
        grid=(batch, g, nqt),
        in_specs=[
            pl.BlockSpec((tq, gw), lambda b, gg, i: (row(b, gg, i), gg)),
            cmp_slab, cmp_slab,
            slab(0), slab(g), slab(2 * g), slab(3 * g),
            pl.BlockSpec((tq, LANES), lambda b, gg, i: (row(b, gg, i), gg)),
            pl.BlockSpec((tq, gw), lambda b, gg, i: (row(b, gg, i), gm_col0 + gg)),
            pl.BlockSpec((1, dh, m_cols), lambda b, gg, i: (gg, 0, 0)),
            const(cpos), const(wsel),
        ],
        out_specs=pl.BlockSpec((tq, gw), lambda b, gg, i: (row(b, gg, i), gg)),
        out_shape=jax.ShapeDtypeStruct((t, g * gw), BF16),
        scratch_shapes=[
            pltpu.VMEM((2 * dh + nb, m_cols), BF16),
            pltpu.VMEM((seq, 2 * dh + nb), BF16),
            pltpu.VMEM((seq, 2 * dh), BF16),
            pltpu.VMEM((nc, 2 * dh), BF16),
            pltpu.VMEM((stage_rows, 2 * dh + nb), BF16),
            pltpu.VMEM((stage_rows, 2 * dh), BF16),
            pltpu.VMEM((n_wt * nkw, m_cols), F32),
            pltpu.VMEM((nk, m_cols), F32),
            pltpu.VMEM((nk, m_cols), F32),
            pltpu.VMEM((SUBLANES, m_cols), F32),
            pltpu.VMEM((dh, m_cols), F32),
            pltpu.SMEM((nb // WORD_BITS,), jnp.int32),
        ],
        compiler_params=_params(("arbitrary", "arbitrary", "arbitrary")),
        name="nsa",
    )(qm, kcm, vcm, kvm, kvm, kvm, kvm, gnm, fm, qal, cpos, wsel)


def _ffn_kernel(x_ref, yr_ref, ya_ref, wo_ref, gmix_ref, gpre_ref, wg_ref, wu_ref, wd_ref, gpost_ref,
                p_ref, wpg_ref, bpg_ref, wpp_ref, o_ref, x1_ref, h_ref, acc_ref):
    j = pl.program_id(1)

    @pl.when(j == 0)
    def _():
        y = (yr_ref[...].astype(F32) + ya_ref[...].astype(F32)).astype(BF16)
        x1 = x_ref[...] + _rmsnorm(_dot(y, wo_ref[...]), gmix_ref[...])
        x1_ref[...] = x1
        h_ref[...] = _rmsnorm(x1, gpre_ref[...]).astype(BF16)
        acc_ref[...] = jnp.zeros(acc_ref.shape, F32)

    h = h_ref[...]
    gate = _dot(h, wg_ref[...])
    up = _dot(h, wu_ref[...])
    act = (gate * _sigmoid(gate) * up).astype(BF16)
    acc_ref[...] += _dot(act, wd_ref[...])

    @pl.when(j == pl.num_programs(1) - 1)
    def _():
        x2 = x1_ref[...] + _rmsnorm(acc_ref[...], gpost_ref[...])
        gate_p = _sigmoid(_dot(x2.astype(BF16), wpg_ref[...]) + bpg_ref[...])
        o_ref[...] = x2 + gate_p * _dot(p_ref[...].astype(BF16), wpp_ref[...])


def _ffn(x, yr, ya, wo, gmix, gpre, wgu, wd, gpost, p, wpg, bpg, wpp, tm, tf):
    t, d = x.shape
    dff = wd.shape[0]
    nf = dff // tf
    dp = p.shape[1]
    rows = pl.BlockSpec((tm, d), lambda i, j: (i, 0))
    vec = pl.BlockSpec((1, d), lambda i, j: (0, 0))
    square = pl.BlockSpec((d, d), lambda i, j: (0, 0))
    return pl.pallas_call(
        _ffn_kernel,
        grid=(t // tm, nf),
        in_specs=[
            rows, rows, rows, square, vec, vec,
            pl.BlockSpec((d, tf), lambda i, j: (0, j)),
            pl.BlockSpec((d, tf), lambda i, j: (0, nf + j)),
            pl.BlockSpec((tf, d), lambda i, j: (j, 0)),
            vec,
            pl.BlockSpec((tm, dp), lambda i, j: (i, 0)),
            square,
            vec,
            pl.BlockSpec((dp, d), lambda i, j: (0, 0)),
        ],
        out_specs=rows,
        out_shape=jax.ShapeDtypeStruct((t, d), F32),
        scratch_shapes=[pltpu.VMEM((tm, d), F32), pltpu.VMEM((tm, d), BF16), pltpu.VMEM((tm, d), F32)],
        compiler_params=_params(("parallel", "arbitrary")),
        name="out_ffn_ple",
    )(x, yr, ya, wo, gmix, gpre, wgu, wgu, wd, gpost, p, wpg, bpg, wpp)


def _block_diag_chunks(w):
    n, bs, _ = w.shape
    per = LANES // bs
    w = w.reshape(n // per, per, bs, bs)
    eye = jnp.eye(per, dtype=w.dtype)
    return jnp.einsum('cpij,pq->cpiqj', w, eye).reshape(n // per, LANES, LANES)


def _layer(x, p, norm_mix_pre, norm_mix_post, w_in, conv_w, conv_b, lru_wa, lru_ba, lru_wx, lru_bx,
           lru_lambda, cmp_pos_k, cmp_pos_v, cmp_k_w1, cmp_k_w2, cmp_v_w1, cmp_v_w2, w_out,
           norm_ffn_pre, norm_ffn_post, ffn_w_gate_up, ffn_w_down, ple_w_proj, ple_w_gate, ple_b_gate,
           batch, seq):
    t, d = x.shape
    d_attn = N_HEADS * HEAD_DIM
    d_kv = N_KV_GROUPS * HEAD_DIM
    row2 = lambda v: v.reshape(1, -1)

    o_q = 2 * d
    o_kv = o_q + d_attn
    o_gn = o_kv + 6 * d_kv
    o_gm = o_gn + 3 * N_HEADS
    w_f = jnp.concatenate([w_in[:, 0:o_q], w_in[:, o_gm:o_gm + 2 * d]], axis=1).astype(BF16)
    w_q = w_in[:, o_q:o_kv].astype(BF16)
    w_kv = w_in[:, o_kv:o_gn].astype(BF16)
    per_g = 3 * HEADS_PER_GROUP
    w_gn = w_in[:, o_gn:o_gm].reshape(d, N_KV_GROUPS, per_g)
    w_gn = jnp.pad(w_gn, ((0, 0), (0, 0), (0, LANES - per_g))).reshape(d, N_KV_GROUPS * LANES).astype(BF16)

    g_pre = row2(norm_mix_pre)
    tm = min(512, t)
    fm = _norm_matmul(x, g_pre, w_f, BF16, tm, w_f.shape[1])
    qm, gnm, kv_chunks, kvm = _norm_qkv(x, g_pre, jnp.concatenate([w_q, w_gn, w_kv], axis=1), d_attn,
                                        N_KV_GROUPS * LANES, 2 * N_KV_GROUPS, tm)

    yr = _rnn(fm, conv_w, row2(conv_b), _block_diag_chunks(lru_wa).astype(BF16),
              _block_diag_chunks(lru_wx).astype(BF16), row2(lru_ba), row2(lru_bx), row2(lru_lambda),
              batch, seq, min(512, seq))

    nchunk = seq // CMP_STRIDE
    kcm, vcm = _compress(kv_chunks, cmp_k_w1.astype(BF16), cmp_k_w2.astype(BF16), cmp_pos_k.reshape(1, -1),
                         cmp_v_w1.astype(BF16), cmp_v_w2.astype(BF16), cmp_pos_v.reshape(1, -1), batch, nchunk)

    tq = min(256, seq)
    ya = _nsa(qm, kvm, kcm, vcm, gnm, fm, batch, seq, tq, min(NSA_KEY_TILE, seq), min(256, seq),
              gm_col0=(3 * d) // (HEADS_PER_GROUP * HEAD_DIM))

    dff = ffn_w_down.shape[0]
    tf = dff // 2 if (dff // 2) % LANES == 0 else dff
    return _ffn(x, yr, ya, w_out.astype(BF16), row2(norm_mix_post), row2(norm_ffn_pre),
                ffn_w_gate_up.astype(BF16), ffn_w_down.astype(BF16), row2(norm_ffn_post),
                p, ple_w_gate.astype(BF16), row2(ple_b_gate), ple_w_proj.astype(BF16), tm, tf)


def kernel(x, p, norm_mix_pre, norm_mix_post, w_in, conv_w, conv_b, lru_wa, lru_ba, lru_wx, lru_bx, lru_lambda, cmp_pos_k, cmp_pos_v, cmp_k_w1, cmp_k_w2, cmp_v_w1, cmp_v_w2, w_out, norm_ffn_pre, norm_ffn_post, ffn_w_gate_up, ffn_w_down, ple_w_proj, ple_w_gate, ple_b_gate):
    batch, seq, d = x.shape
    depth = w_in.shape[0]
    xf = x.reshape(batch * seq, d)
    for i in range(depth):
        xf = _layer(xf, p[i].reshape(batch * seq, -1), norm_mix_pre[i], norm_mix_post[i], w_in[i], conv_w[i],
                    conv_b[i], lru_wa[i], lru_ba[i], lru_wx[i], lru_bx[i], lru_lambda[i], cmp_pos_k[i],
                    cmp_pos_v[i], cmp_k_w1[i], cmp_k_w2[i], cmp_v_w1[i], cmp_v_w2[i], w_out[i],
                    norm_ffn_pre[i], norm_ffn_post[i], ffn_w_gate_up[i], ffn_w_down[i], ple_w_proj[i],
                    ple_w_gate[i], ple_b_gate[i], batch, seq)
    return xf.reshape(batch, seq, d)
```

```python
import functools

import numpy as np
import jax
import jax.numpy as jnp
from jax import lax
from jax.experimental import pallas as pl
from jax.experimental.pallas import tpu as pltpu

CONV_WIDTH = 4
LRU_C = 8.0
N_HEADS = 16
HEAD_DIM = 64
N_KV_GROUPS = 4
HEADS_PER_GROUP = N_HEADS // N_KV_GROUPS
CMP_LEN = 32
CMP_STRIDE = 16
SEL_LEN = 64
SEL_TOPK = 16
WINDOW = 512
ALIBI_MAX_BIAS = 8.0
NORM_EPS = 1e-6

LANES = 128
SUBLANES = 8
VMEM_LIMIT_BYTES = 56 * 1024 * 1024

MASK_NEG = -1e30
LOG2_E = 1.4426950408889634
N_ALIBI_COLS = 9
PAD_COL = HEAD_DIM + N_ALIBI_COLS
WORD_BITS = 16
WORD_SHIFT = 4
SEL_SHIFT = 6
N_FORCED = 3
NEAR_BLOCKS = 24
FILL_ROWS = 512
NSA_KEY_TILE = 11 * SEL_LEN
OVERLAP_CHUNKS = 4
BF16 = jnp.bfloat16
F32 = jnp.float32


def _dot(a, b):
    return jnp.dot(a, b, preferred_element_type=F32)


def _dot_nt(a, b):
    return lax.dot_general(a, b, (((1,), (1,)), ((), ())), preferred_element_type=F32)


def _sigmoid(x):
    return 0.5 * jnp.tanh(0.5 * x) + 0.5


def _gelu_tanh(x):
    c = np.float32(np.sqrt(2.0 / np.pi))
    half = 0.5 * x
    return half + half * jnp.tanh(x * (c + (c * 0.044715) * (x * x)))


def _rmsnorm(x, g):
    ms = jnp.mean(x * x, axis=-1, keepdims=True)
    return x * lax.rsqrt(ms + NORM_EPS) * g


def _params(sem):
    return pltpu.CompilerParams(dimension_semantics=sem, vmem_limit_bytes=VMEM_LIMIT_BYTES)


def _norm_matmul_kernel(x_ref, g_ref, w_ref, o_ref, h_ref):
    @pl.when(pl.program_id(1) == 0)
    def _():
        h_ref[...] = _rmsnorm(x_ref[...], g_ref[...]).astype(BF16)

    o_ref[...] = _dot(h_ref[...], w_ref[...]).astype(o_ref.dtype)


def _norm_matmul(x, g, w, out_dtype, tm, tn):
    t, k = x.shape
    n = w.shape[1]
    return pl.pallas_call(
        _norm_matmul_kernel,
        grid=(t // tm, n // tn),
        in_specs=[
            pl.BlockSpec((tm, k), lambda i, j: (i, 0)),
            pl.BlockSpec((1, k), lambda i, j: (0, 0)),
            pl.BlockSpec((k, tn), lambda i, j: (0, j)),
        ],
        out_specs=pl.BlockSpec((tm, tn), lambda i, j: (i, j)),
        out_shape=jax.ShapeDtypeStruct((t, n), out_dtype),
        scratch_shapes=[pltpu.VMEM((tm, k), BF16)],
        compiler_params=_params(("parallel", "arbitrary")),
        name="norm_matmul",
    )(x, g, w)


def _norm_qkv_kernel(x_ref, g_ref, w_ref, q_ref, gn_ref, cv_ref, kv_ref, slab_scr):
    h = _rmsnorm(x_ref[...], g_ref[...]).astype(BF16)
    z = _dot(h, w_ref[...])
    nq = q_ref.shape[1]
    ngn = gn_ref.shape[1]
    q_ref[...] = z[:, 0:nq].astype(q_ref.dtype)
    gn_ref[...] = z[:, nq:nq + ngn]
    n_cv = cv_ref.shape[0]
    rows = cv_ref.shape[1]
    for c in range(n_cv):
        lo = nq + ngn + c * HEAD_DIM
        slab_scr[...] = z[:, lo:lo + HEAD_DIM]
        for l in range(CMP_STRIDE):
            cv_ref[c, :, l * HEAD_DIM:(l + 1) * HEAD_DIM] = (
                slab_scr[pl.ds(l, rows, stride=CMP_STRIDE), :].astype(cv_ref.dtype))
    for c in range(kv_ref.shape[0]):
        lo = nq + ngn + (n_cv + c) * HEAD_DIM
        kv_ref[c] = z[:, lo:lo + HEAD_DIM].astype(kv_ref.dtype)


def _norm_qkv(x, g, w, nq, ngn, n_cv, tm):
    t, k = x.shape
    n = w.shape[1]
    n_kv = (n - nq - ngn) // HEAD_DIM - n_cv
    return pl.pallas_call(
        _norm_qkv_kernel,
        grid=(t // tm,),
        in_specs=[
            pl.BlockSpec((tm, k), lambda i: (i, 0)),
            pl.BlockSpec((1, k), lambda i: (0, 0)),
            pl.BlockSpec((k, n), lambda i: (0, 0)),
        ],
        out_specs=[
            pl.BlockSpec((tm, nq), lambda i: (i, 0)),
            pl.BlockSpec((tm, ngn), lambda i: (i, 0)),
            pl.BlockSpec((n_cv, tm // CMP_STRIDE, CMP_STRIDE * HEAD_DIM), lambda i: (0, i, 0)),
            pl.BlockSpec((n_kv, tm, HEAD_DIM), lambda i: (0, i, 0)),
        ],
        out_shape=[
            jax.ShapeDtypeStruct((t, nq), BF16),
            jax.ShapeDtypeStruct((t, ngn), F32),
            jax.ShapeDtypeStruct((n_cv, t // CMP_STRIDE, CMP_STRIDE * HEAD_DIM), BF16),
            jax.ShapeDtypeStruct((n_kv, t, HEAD_DIM), BF16),
        ],
        scratch_shapes=[pltpu.VMEM((tm, HEAD_DIM), F32)],
        compiler_params=_params(("parallel",)),
        name="norm_qkv",
    )(x, g, w)


def _rnn_kernel(xr_ref, gr_ref, gm_ref, cw_ref, cb_ref, wa_ref, wx_ref, ba_ref, bx_ref, lam_ref,
                o_ref, xbuf, a_scr, b_scr, h_scr, *, ts):
    s_idx = pl.program_id(1)
    d = xr_ref.shape[1]
    halo = SUBLANES

    @pl.when(s_idx == 0)
    def _():
        xbuf[0:halo, :] = jnp.zeros((halo, d), F32)
        h_scr[...] = jnp.zeros(h_scr.shape, F32)

    xbuf[halo:halo + ts, :] = xr_ref[...].astype(F32)
    xc = cb_ref[...] + xbuf[halo:halo + ts, :] * cw_ref[CONV_WIDTH - 1:CONV_WIDTH, :]
    for k in range(1, CONV_WIDTH):
        xc = xc + xbuf[halo - k:halo - k + ts, :] * cw_ref[CONV_WIDTH - 1 - k:CONV_WIDTH - k, :]
    xbuf[0:halo, :] = xbuf[ts:ts + halo, :]

    xcb = xc.astype(BF16)
    n_chunks = d // LANES
    neg_sp = -LRU_C * (jnp.maximum(-lam_ref[...], 0.0) + jnp.log(1.0 + jnp.exp(-jnp.abs(lam_ref[...]))))
    for c in range(n_chunks):
        sl = slice(c * LANES, (c + 1) * LANES)
        xk = xcb[:, sl]
        r = _sigmoid(_dot(xk, wa_ref[c]) + ba_ref[:, sl])
        i = _sigmoid(_dot(xk, wx_ref[c]) + bx_ref[:, sl])
        log_a = r * neg_sp[:, sl]
        a = jnp.exp(log_a)
        a_scr[:, sl] = a
        u = 1.0 - a * a
        root = jnp.where(u > 0.0, u * lax.rsqrt(u), 0.0)
        b_scr[:, sl] = root * (i * xc[:, sl])

    def group(gi, h):
        base = pl.multiple_of(gi * SUBLANES, SUBLANES)
        a8 = a_scr[pl.ds(base, SUBLANES), :]
        b8 = b_scr[pl.ds(base, SUBLANES), :]
        rows = []
        for j in range(SUBLANES):
            h = a8[j:j + 1, :] * h + b8[j:j + 1, :]
            rows.append(h)
        a_scr[pl.ds(base, SUBLANES), :] = jnp.concatenate(rows, axis=0)
        return h

    h_last = lax.fori_loop(0, ts // SUBLANES, group, h_scr[0:1, :])
    h_scr[0:1, :] = h_last
    o_ref[...] = (_sigmoid(gm_ref[...].astype(F32)) * a_scr[...]
                  * _gelu_tanh(gr_ref[...].astype(F32))).astype(o_ref.dtype)


def _rnn(f, conv_w, conv_b, wa, wx, ba, bx, lam, batch, seq, ts):
    t = f.shape[0]
    d = conv_w.shape[1]
    ns = seq // ts
    nd = d // d
    del nd
    row = lambda b, s: b * ns + s
    vec = lambda r: pl.BlockSpec((r, d), lambda b, s: (0, 0))
    return pl.pallas_call(
        functools.partial(_rnn_kernel, ts=ts),
        grid=(batch, ns),
        in_specs=[
            pl.BlockSpec((ts, d), lambda b, s: (row(b, s), 0)),
            pl.BlockSpec((ts, d), lambda b, s: (row(b, s), 1)),
            pl.BlockSpec((ts, d), lambda b, s: (row(b, s), 2)),
            vec(CONV_WIDTH), vec(1),
            pl.BlockSpec(wa.shape, lambda b, s: (0, 0, 0)),
            pl.BlockSpec(wx.shape, lambda b, s: (0, 0, 0)),
            vec(1), vec(1), vec(1),
        ],
        out_specs=pl.BlockSpec((ts, d), lambda b, s: (row(b, s), 0)),
        out_shape=jax.ShapeDtypeStruct((t, d), BF16),
        scratch_shapes=[
            pltpu.VMEM((ts + SUBLANES, d), F32),
            pltpu.VMEM((ts, d), F32),
            pltpu.VMEM((ts, d), F32),
            pltpu.VMEM((SUBLANES, d), F32),
        ],
        compiler_params=_params(("parallel", "arbitrary")),
        name="rnn_mixer",
    )(f, f, f, conv_w, conv_b, wa, wx, ba, bx, lam)


def _compress_kernel(ak_ref, av_ref, w1k_ref, w2k_ref, pk_ref, w1v_ref, w2v_ref, pv_ref, ok_ref, ov_ref):
    def one(a_ref, w1_ref, w2_ref, p_ref, o_ref):
        a = a_ref[0]
        half = a.shape[1]
        nchunk = a.shape[0]
        lo = _dot(a, w1_ref[0:half, :])
        hi = _dot(a, w1_ref[half:2 * half, :])
        pb = _dot(jnp.broadcast_to(p_ref[...], (SUBLANES, 2 * half)).astype(BF16), w1_ref[...])[0:1, :]
        h = lo + pltpu.roll(hi, nchunk - 1, 0) + pb
        o_ref[0] = _dot(_gelu_tanh(h).astype(BF16), w2_ref[...]).astype(o_ref.dtype)

    one(ak_ref, w1k_ref, w2k_ref, pk_ref, ok_ref)
    one(av_ref, w1v_ref, w2v_ref, pv_ref, ov_ref)


def _compress(kv_chunks, w1k, w2k, pk, w1v, w2v, pv, batch, nchunk):
    g = N_KV_GROUPS
    width = kv_chunks.shape[2]
    full = lambda a: pl.BlockSpec(a.shape, lambda b, gg: (0,) * a.ndim)
    out = jax.ShapeDtypeStruct((batch * g, nchunk, HEAD_DIM), BF16)
    return pl.pallas_call(
        _compress_kernel,
        grid=(batch, g),
        in_specs=[
            pl.BlockSpec((1, nchunk, width), lambda b, gg: (gg, b, 0)),
            pl.BlockSpec((1, nchunk, width), lambda b, gg: (g + gg, b, 0)),
            full(w1k), full(w2k), full(pk), full(w1v), full(w2v), full(pv),
        ],
        out_specs=[
            pl.BlockSpec((1, nchunk, HEAD_DIM), lambda b, gg: (b * g + gg, 0, 0)),
            pl.BlockSpec((1, nchunk, HEAD_DIM), lambda b, gg: (b * g + gg, 0, 0)),
        ],
        out_shape=[out, out],
        compiler_params=_params(("parallel", "parallel")),
        name="compress",
    )(kv_chunks, kv_chunks, w1k, w2k, pk, w1v, w2v, pv)


ONES_ROWS = 16


def _scores(k, q_rows, s_ref, keep=None, keep_rows=None, shift=None):
    s = _dot(k, q_rows)
    if shift is not None:
        s = s + shift
    if keep is None:
        s_ref[...] = s
        return jnp.max(s, axis=0, keepdims=True)
    n = s.shape[0] if keep_rows is None else keep_rows
    top = s[0:n]
    for mask in keep:
        top = jnp.where(mask, top, MASK_NEG)
    s_ref[0:n, :] = top
    col_max = jnp.max(top, axis=0, keepdims=True)
    if n < s.shape[0]:
        s_ref[n:, :] = s[n:]
        col_max = jnp.maximum(col_max, jnp.max(s[n:], axis=0, keepdims=True))
    return col_max


def _accumulate(s_ref, col_max, v_aug_t, state):
    m_old, acc_old = state
    m_new = jnp.maximum(m_old, col_max)
    alpha = jnp.exp2(m_old - m_new)
    e = jnp.exp2(s_ref[...] - m_new).astype(BF16)
    return m_new, alpha * acc_old + _dot(v_aug_t, e)


def _nsa_kernel(q_ref, kc_ref, vc_ref, ks_ref, vs_ref, kw_ref, vw_ref, gn_ref, gm_ref,
                qal_ref, cpos_ref, wsel_ref,
                o_ref, q_t, ksaug, kwaug, kcaug, kstage, vstage, s_win, s_a, s_b, win_max, o_cmp_scr, flags,
                *, tq, nk, nkw, seq, sel_chunk):
    b = pl.program_id(0)
    g = pl.program_id(1)
    qt = pl.program_id(2)
    r_heads = HEADS_PER_GROUP
    dh = HEAD_DIM
    m_cols = r_heads * tq
    nb = seq // SEL_LEN
    nc = kc_ref.shape[1]
    a_w = 2 * dh
    k_w = a_w + nb

    @pl.when((b == 0) & (g == 0) & (qt == 0))
    def _():
        rows = min(FILL_ROWS, seq)

        def fill(c, carry):
            off = pl.multiple_of(c * rows, rows)
            pos = off + lax.broadcasted_iota(jnp.int32, (rows, k_w), 0)
            lane = lax.broadcasted_iota(jnp.int32, (rows, k_w), 1)
            blk_of = jnp.right_shift(pos, SEL_SHIFT)
            cols = jnp.where((lane >= dh) & (lane < dh + 3), blk_of * SEL_LEN,
                             jnp.where((lane >= dh + 3) & (lane < dh + 6), pos & (SEL_LEN - 1),
                                       jnp.where(lane == a_w + blk_of, 1, 0)))
            cols = cols.astype(F32).astype(BF16)
            ksaug[pl.ds(off, rows), :] = cols
            kwaug[pl.ds(off, rows), :] = cols[:, 0:a_w]
            return carry

        lax.fori_loop(0, seq // rows, fill, 0)
        kcaug[...] = cpos_ref[...]
        vstage[...] = jnp.ones(vstage.shape, BF16)

    @pl.when(qt == 0)
    def _():
        ksaug[:, 0:dh] = ks_ref[0]
        kwaug[:, 0:dh] = kw_ref[0]
        kcaug[:, 0:dh] = kc_ref[0]
        q_t[dh:a_w, :] = qal_ref[0]

    v_rows = dh + ONES_ROWS
    ri = lax.broadcasted_iota(jnp.int32, (v_rows, 2 * dh), 0)
    ci = lax.broadcasted_iota(jnp.int32, (v_rows, 2 * dh), 1)
    pick = (((ri < dh) & (ri == ci)) | ((ri >= dh) & (ci == dh))).astype(F32).astype(BF16)
    transposed = lambda v: _dot_nt(pick[0:dh, 0:dh], v).astype(BF16)
    staged_aug_t = lambda v: _dot_nt(pick, v).astype(BF16)
    ones_rows = jnp.ones((ONES_ROWS, nkw), BF16)
    window_aug_t = lambda v: jnp.concatenate([transposed(v), ones_rows], axis=0)
    init = (jnp.full((1, m_cols), MASK_NEG, F32), jnp.zeros((v_rows, m_cols), F32))

    t0 = qt * tq
    col = lax.broadcasted_iota(jnp.int32, (1, m_cols), 1)
    trow = t0 + (col & (tq - 1))

    scale = np.float32(HEAD_DIM ** -0.5 * LOG2_E)
    qf = jnp.transpose(q_ref[...].astype(F32) * scale)
    for r in range(r_heads):
        q_t[0:dh, r * tq:(r + 1) * tq] = qf[r * dh:(r + 1) * dh, :].astype(BF16)

    n_wt = (WINDOW + tq) // nkw
    win_s = [s_win.at[i * nkw:(i + 1) * nkw, :] for i in range(n_wt)]
    win_off = [pl.multiple_of(jnp.maximum(t0 + tq - (i + 1) * nkw, 0), nkw) for i in range(n_wt)]

    def window_scores():
        for i in range(n_wt):
            start = t0 + tq - (i + 1) * nkw
            kpos_w = start + lax.broadcasted_iota(jnp.int32, (nkw, 1), 0)
            keep = []
            if i * nkw < tq - 1:
                keep.append(kpos_w <= trow)
            if (i + 1) * nkw > WINDOW:
                keep.append(kpos_w > trow - WINDOW)
            shift = jnp.where(start >= 0, 0.0, MASK_NEG) if (i + 1) * nkw > tq else None
            win_max[i:i + 1, :] = _scores(kwaug[pl.ds(win_off[i], nkw), :], q_t[0:a_w, :], win_s[i],
                                          keep=keep or None, shift=shift)

    def compressed_and_select(rows_c, rows_b):
        cend = lax.broadcasted_iota(jnp.int32, (rows_c, 1), 0) * CMP_STRIDE + (CMP_LEN - 1)
        s = jnp.where(cend <= trow, _dot(kcaug[0:rows_c, :], q_t[0:a_w, :]), MASK_NEG)
        m = jnp.max(s, axis=0, keepdims=True)
        e = jnp.exp2(s - m)
        has_key = (trow >= CMP_LEN - 1).astype(F32)
        p = e * (has_key / jnp.sum(e, axis=0, keepdims=True))
        o_cmp_scr[...] = _dot(transposed(vc_ref[0, 0:rows_c, :]), p.astype(BF16))
        imp = p[:, 0:tq]
        for r in range(1, r_heads):
            imp = imp + p[:, r * tq:(r + 1) * tq]

        hi = imp.astype(BF16)
        r1 = imp - hi.astype(F32)
        mid = r1.astype(BF16)
        lo = (r1 - mid.astype(F32)).astype(BF16)
        wsel = wsel_ref[0:rows_b, 0:rows_c]
        imp_t = _dot(wsel, hi) + _dot(wsel, mid) + _dot(wsel, lo)

        window_scores()

        blk = lax.broadcasted_iota(jnp.int32, (rows_b, tq), 0).astype(F32)
        tq_l = t0 + lax.broadcasted_iota(jnp.int32, (rows_b, tq), 1)
        cur = jnp.right_shift(tq_l, SEL_SHIFT).astype(F32)
        valid = blk <= cur
        sel = jnp.where(blk == 0.0, 1.0, jnp.where(blk == cur, 1.0, jnp.where(blk == cur - 1.0, 1.0, 0.0)))
        score = jnp.where(valid, jnp.where(sel > 0.0, -1.0, imp_t), -1.0)
        for _ in range(min(SEL_TOPK, nb) - N_FORCED):
            mx = jnp.max(score, axis=0, keepdims=True)
            idx = jnp.min(jnp.where(score == mx, blk, float(nb)), axis=0, keepdims=True)
            hit = blk == idx
            sel = jnp.where(hit, 1.0, sel)
            score = jnp.where(hit, -2.0, score)
        selv = jnp.where(valid, sel, 0.0) > 0.0
        selneg_t = jnp.where(selv, 0.0, MASK_NEG).astype(BF16)
        unseen = jnp.full((nb - rows_b, tq), MASK_NEG, BF16)
        for r in range(r_heads):
            q_t[a_w:a_w + rows_b, r * tq:(r + 1) * tq] = selneg_t
            if rows_b < nb:
                q_t[a_w + rows_b:, r * tq:(r + 1) * tq] = unseen

        used = jnp.max(jnp.where(selv, 1.0, 0.0), axis=1, keepdims=True)
        bit_id = lax.broadcasted_iota(jnp.int32, (rows_b, 1), 0) & (WORD_BITS - 1)
        weighted = used * jnp.left_shift(1, bit_id).astype(F32)
        for i in range(nb // WORD_BITS):
            if (i + 1) * WORD_BITS <= rows_b:
                flags[i] = jnp.sum(weighted[i * WORD_BITS:(i + 1) * WORD_BITS, :]).astype(jnp.int32)
            else:
                flags[i] = 0

    n_var = nb // sel_chunk
    cmp_chunk = sel_chunk * (SEL_LEN // CMP_STRIDE)
    seen_c = (t0 + tq - CMP_LEN) // CMP_STRIDE + 1
    seen_b = (t0 + tq) // SEL_LEN
    variant = jnp.maximum((seen_c + cmp_chunk - 1) // cmp_chunk, (seen_b + sel_chunk - 1) // sel_chunk)
    for v in range(1, n_var + 1):
        pl.when(variant == v)(functools.partial(compressed_and_select, min(v * cmp_chunk, nc), v * sel_chunk))
    o_cmp = o_cmp_scr[...]

    n_own = tq // SEL_LEN
    off_q = pl.multiple_of(t0, tq)
    kstage[0:tq, :] = ksaug[pl.ds(off_q, tq), :]
    vstage[0:tq, 0:dh] = vs_ref[0, pl.ds(off_q, tq), :]

    def stage_block(j, cnt):
        src = pl.multiple_of(j * SEL_LEN, SEL_LEN)
        dst = pl.multiple_of(cnt * SEL_LEN, SEL_LEN)
        kstage[pl.ds(dst, SEL_LEN), :] = ksaug[pl.ds(src, SEL_LEN), :]
        vstage[pl.ds(dst, SEL_LEN), 0:dh] = vs_ref[0, pl.ds(src, SEL_LEN), :]

    def gather(j, cnt):
        bit = jnp.right_shift(flags[jnp.right_shift(j, WORD_SHIFT)], j & (WORD_BITS - 1)) & 1
        pl.when(bit == 1)(functools.partial(stage_block, j, cnt))
        return cnt + bit

    n_past = jnp.right_shift(t0, SEL_SHIFT)
    near_lo = jnp.maximum(n_past - NEAR_BLOCKS, 0)
    first = jnp.where(near_lo > 0, flags[0] & 1, 0)
    pl.when(first == 1)(functools.partial(stage_block, 0, n_own))

    def far_word(w, cnt):
        lo = jnp.maximum(w * WORD_BITS, 1)
        hi = jnp.minimum((w + 1) * WORD_BITS, near_lo)
        rest = jnp.where(w == 0, flags[w] & -2, flags[w])
        return lax.cond(rest != 0, lambda c: lax.fori_loop(lo, hi, gather, c), lambda c: c, cnt)

    n_blocks = lax.fori_loop(0, jnp.right_shift(near_lo + WORD_BITS - 1, WORD_SHIFT), far_word, n_own + first)
    n_blocks = lax.fori_loop(near_lo, n_past, gather, n_blocks)
    per_tile = nk // SEL_LEN
    n_tiles = (n_blocks + per_tile - 1) // per_tile
    pad_block = jnp.where(lax.broadcasted_iota(jnp.int32, (SEL_LEN, k_w), 1) == PAD_COL, 1.0, 0.0).astype(BF16)

    def pad(j, carry):
        dst = pl.multiple_of(j * SEL_LEN, SEL_LEN)
        kstage[pl.ds(dst, SEL_LEN), :] = pad_block
        return carry

    lax.fori_loop(n_blocks, n_tiles * per_tile, pad, 0)

    own_pos = t0 + lax.broadcasted_iota(jnp.int32, (tq, 1), 0)
    s_first = _dot(kstage[0:nk, :], q_t[...])
    own = jnp.where(own_pos <= trow, s_first[0:tq], MASK_NEG)
    s_a[0:tq, :] = own
    s_a[tq:, :] = s_first[tq:]
    step = (nk - tq) // max(n_wt - 1, 1) // SUBLANES * SUBLANES
    bounds = [0, tq] + [tq + step * i for i in range(1, n_wt - 1)] + [nk]
    m_w = functools.reduce(jnp.maximum, [win_max[i:i + 1, :] for i in range(n_wt)])
    parts, weights = [], []
    for i in range(n_wt):
        rows = own if i == 0 else s_first[bounds[i]:bounds[i + 1]]
        parts.append(jnp.max(rows, axis=0, keepdims=True))
        weights.append(jnp.exp2(win_s[i][...] - (m_w + parts[i] * 0.0)).astype(BF16))
    max_a = functools.reduce(jnp.maximum, parts)
    v_win = jnp.concatenate([window_aug_t(vw_ref[0, pl.ds(win_off[i], nkw), :]) for i in range(n_wt)], axis=1)
    acc_w = _dot(v_win, jnp.concatenate(weights, axis=0))
    o_win = acc_w[0:dh] / acc_w[dh:dh + 1]

    def tile_scores(i, s_ref):
        return _scores(kstage[pl.ds(pl.multiple_of(i * nk, nk), nk), :], q_t[...], s_ref)

    def tile_accumulate(i, s_ref, col_max, state):
        return _accumulate(s_ref, col_max, staged_aug_t(vstage[pl.ds(pl.multiple_of(i * nk, nk), nk), :]), state)

    def scores_and_accumulate(i_next, s_next_ref, i_cur, s_cur_ref, max_cur, state):
        m_old, acc_old = state
        m_new = jnp.maximum(m_old, max_cur)
        alpha = jnp.exp2(m_old - m_new)
        s_next = _dot(kstage[pl.ds(pl.multiple_of(i_next * nk, nk), nk), :], q_t[...])
        s_next_ref[...] = s_next
        rows = nk // OVERLAP_CHUNKS
        partial, weights = [], []
        for c in range(OVERLAP_CHUNKS):
            rs = slice(c * rows, (c + 1) * rows)
            partial.append(jnp.max(s_next[rs], axis=0, keepdims=True))
            m_c = m_new + partial[c] * 0.0
            weights.append(jnp.exp2(s_cur_ref[rs, :] - m_c).astype(BF16))
        v_t = staged_aug_t(vstage[pl.ds(pl.multiple_of(i_cur * nk, nk), nk), :])
        acc = alpha * acc_old + _dot(v_t, jnp.concatenate(weights, axis=0))
        return (m_new, acc), functools.reduce(jnp.maximum, partial)

    def slc_pair(j, carry):
        state, max_a = carry
        i = 2 * j
        state, max_b = scores_and_accumulate(i + 1, s_b, i, s_a, max_a, state)
        state, max_a = scores_and_accumulate(i + 2, s_a, i + 1, s_b, max_b, state)
        return state, max_a

    n_pairs = (n_tiles - 1) // 2
    state, max_a = lax.fori_loop(0, n_pairs, slc_pair, (init, max_a))
    i_a = 2 * n_pairs

    def two_left(state):
        state, max_b = scores_and_accumulate(i_a + 1, s_b, i_a, s_a, max_a, state)
        return tile_accumulate(i_a + 1, s_b, max_b, state)

    def one_left(state):
        return tile_accumulate(i_a, s_a, max_a, state)

    _, acc = lax.cond(n_tiles - i_a == 2, two_left, one_left, state)
    o_slc = acc[0:dh] / acc[dh:dh + 1]

    gates = jnp.transpose(_sigmoid(gn_ref[...]))
    heads = []
    for r in range(r_heads):
        cs = slice(r * tq, (r + 1) * tq)
        heads.append(gates[3 * r:3 * r + 1, :] * o_cmp[:, cs] + gates[3 * r + 1:3 * r + 2, :] * o_slc[:, cs]
                     + gates[3 * r + 2:3 * r + 3, :] * o_win[:, cs])
    o = jnp.transpose(jnp.concatenate(heads, axis=0))
    o_ref[...] = (_sigmoid(gm_ref[...].astype(F32)) * o).astype(o_ref.dtype)


def _alibi_tables(seq, nc, tq):
    import ml_dtypes
    bf = ml_dtypes.bfloat16
    h = np.arange(1, N_HEADS + 1, dtype=np.float32)
    slopes = (np.exp2(-ALIBI_MAX_BIAS * h / N_HEADS) * LOG2_E).astype(np.float32)
    s1 = slopes.astype(bf).astype(np.float32)
    s2 = (slopes - s1).astype(bf).astype(np.float32)
    s3 = (slopes - s1 - s2).astype(bf).astype(np.float32)
    dh = HEAD_DIM
    qal = np.zeros((N_HEADS, dh), np.float32)
    for rep in range(3):
        qal[:, 3 * rep + 0] = s1
        qal[:, 3 * rep + 1] = s2
        qal[:, 3 * rep + 2] = s3
    qal[:, PAD_COL - dh] = MASK_NEG
    qal = qal.reshape(N_KV_GROUPS, HEADS_PER_GROUP, dh).transpose(0, 2, 1)
    qal_p = np.repeat(qal, tq, axis=2)

    nb = seq // SEL_LEN
    c = np.arange(nc)
    cpos = np.zeros((nc, 2 * dh), np.float32)
    cpos[:, dh:dh + 3] = ((c // 16) * 16 * CMP_STRIDE)[:, None]
    cpos[:, dh + 3:dh + 6] = ((c % 16) * CMP_STRIDE)[:, None]
    cpos[:, dh + 6:dh + 9] = CMP_LEN - 1

    r_sel = SEL_LEN // CMP_STRIDE
    r_cmp = CMP_LEN // CMP_STRIDE
    wsel = np.zeros((nb, nc), np.float32)
    for j in range(nb):
        for mm in range(r_sel):
            for nn in range(r_cmp):
                ci = r_sel * j + mm - nn
                if 0 <= ci < nc - 1:
                    wsel[j, ci] += 1.0
    as_bf = lambda a: jnp.asarray(a.astype(bf))
    return as_bf(qal_p), as_bf(cpos), as_bf(wsel)


def _nsa(qm, kvm, kcm, vcm, gnm, fm, batch, seq, tq, nk, nkw, gm_col0):
    t = qm.shape[0]
    g = N_KV_GROUPS
    dh = HEAD_DIM
    nqt = seq // tq
    nc = kcm.shape[1]
    nb = seq // SEL_LEN
    gw = HEADS_PER_GROUP * dh
    m_cols = HEADS_PER_GROUP * tq
    qal, cpos, wsel = _alibi_tables(seq, nc, tq)
    n_wt = (WINDOW + tq) // nkw
    sel_chunk = max(WORD_BITS, nb // 4)
    per_tile = nk // SEL_LEN
    stage_rows = -(-nb // per_tile) * per_tile * SEL_LEN
    row = lambda b, gg, i: b * nqt + i
    slab = lambda base: pl.BlockSpec((1, seq, dh), lambda b, gg, i: (base + gg, b, 0))
    cmp_slab = pl.BlockSpec((1, nc, dh), lambda b, gg, i: (b * g + gg, 0, 0))
    const = lambda a: pl.BlockSpec(a.shape, lambda b, gg, i: (0,) * a.ndim)
    return pl.pallas_call(
        functools.partial(_nsa_kernel, tq=tq, nk=nk, nkw=nkw, seq=seq, sel_chunk=sel_chunk),
        grid=(batch, g, nqt),
        in_specs=[
            pl.BlockSpec((tq, gw), lambda b, gg, i: (row(b, gg, i), gg)),
            cmp_slab, cmp_slab,
            slab(0), slab(g), slab(2 * g), slab(3 * g),
            pl.BlockSpec((tq, LANES), lambda b, gg, i: (row(b, gg, i), gg)),
            pl.BlockSpec((tq, gw), lambda b, gg, i: (row(b, gg, i), gm_col0 + gg)),
            pl.BlockSpec((1, dh, m_cols), lambda b, gg, i: (gg, 0, 0)),
            const(cpos), const(wsel),
        ],
        out_specs=pl.BlockSpec((tq, gw), lambda b, gg, i: (row(b, gg, i), gg)),
        out_shape=jax.ShapeDtypeStruct((t, g * gw), BF16),
        scratch_shapes=[
            pltpu.VMEM((2 * dh + nb, m_cols), BF16),
            pltpu.VMEM((seq, 2 * dh + nb), BF16),
            pltpu.VMEM((seq, 2 * dh), BF16),
            pltpu.VMEM((nc, 2 * dh), BF16),
            pltpu.VMEM((stage_rows, 2 * dh + nb), BF16),
            pltpu.VMEM((stage_rows, 2 * dh), BF16),
            pltpu.VMEM((n_wt * nkw, m_cols), F32),
            pltpu.VMEM((nk, m_cols), F32),
            pltpu.VMEM((nk, m_cols), F32),
            pltpu.VMEM((SUBLANES, m_cols), F32),
            pltpu.VMEM((dh, m_cols), F32),
            pltpu.SMEM((nb // WORD_BITS,), jnp.int32),
        ],
        compiler_params=_params(("arbitrary", "arbitrary", "arbitrary")),
        name="nsa",
    )(qm, kcm, vcm, kvm, kvm, kvm, kvm, gnm, fm, qal, cpos, wsel)


def _ffn_kernel(x_ref, yr_ref, ya_ref, wo_ref, gmix_ref, gpre_ref, wg_ref, wu_ref, wd_ref, gpost_ref,
                p_ref, wpg_ref, bpg_ref, wpp_ref, o_ref, x1_ref, h_ref, acc_ref):
    j = pl.program_id(1)

    @pl.when(j == 0)
    def _():
        y = (yr_ref[...].astype(F32) + ya_ref[...].astype(F32)).astype(BF16)
        x1 = x_ref[...] + _rmsnorm(_dot(y, wo_ref[...]), gmix_ref[...])
        x1_ref[...] = x1
        h_ref[...] = _rmsnorm(x1, gpre_ref[...]).astype(BF16)
        acc_ref[...] = jnp.zeros(acc_ref.shape, F32)

    h = h_ref[...]
    gate = _dot(h, wg_ref[...])
    up = _dot(h, wu_ref[...])
    act = (gate * _sigmoid(gate) * up).astype(BF16)
    acc_ref[...] += _dot(act, wd_ref[...])

    @pl.when(j == pl.num_programs(1) - 1)
    def _():
        x2 = x1_ref[...] + _rmsnorm(acc_ref[...], gpost_ref[...])
        gate_p = _sigmoid(_dot(x2.astype(BF16), wpg_ref[...]) + bpg_ref[...])
        o_ref[...] = x2 + gate_p * _dot(p_ref[...].astype(BF16), wpp_ref[...])


def _ffn(x, yr, ya, wo, gmix, gpre, wgu, wd, gpost, p, wpg, bpg, wpp, tm, tf):
    t, d = x.shape
    dff = wd.shape[0]
    nf = dff // tf
    dp = p.shape[1]
    rows = pl.BlockSpec((tm, d), lambda i, j: (i, 0))
    vec = pl.BlockSpec((1, d), lambda i, j: (0, 0))
    square = pl.BlockSpec((d, d), lambda i, j: (0, 0))
    return pl.pallas_call(
        _ffn_kernel,
        grid=(t // tm, nf),
        in_specs=[
            rows, rows, rows, square, vec, vec,
            pl.BlockSpec((d, tf), lambda i, j: (0, j)),
            pl.BlockSpec((d, tf), lambda i, j: (0, nf + j)),
            pl.BlockSpec((tf, d), lambda i, j: (j, 0)),
            vec,
            pl.BlockSpec((tm, dp), lambda i, j: (i, 0)),
            square,
            vec,
            pl.BlockSpec((dp, d), lambda i, j: (0, 0)),
        ],
        out_specs=rows,
        out_shape=jax.ShapeDtypeStruct((t, d), F32),
        scratch_shapes=[pltpu.VMEM((tm, d), F32), pltpu.VMEM((tm, d), BF16), pltpu.VMEM((tm, d), F32)],
        compiler_params=_params(("parallel", "arbitrary")),
        name="out_ffn_ple",
    )(x, yr, ya, wo, gmix, gpre, wgu, wgu, wd, gpost, p, wpg, bpg, wpp)


def _block_diag_chunks(w):
    n, bs, _ = w.shape
    per = LANES // bs
    w = w.reshape(n // per, per, bs, bs)
    eye = jnp.eye(per, dtype=w.dtype)
    return jnp.einsum('cpij,pq->cpiqj', w, eye).reshape(n // per, LANES, LANES)


def _layer(x, p, norm_mix_pre, norm_mix_post, w_in, conv_w, conv_b, lru_wa, lru_ba, lru_wx, lru_bx,
           lru_lambda, cmp_pos_k, cmp_pos_v, cmp_k_w1, cmp_k_w2, cmp_v_w1, cmp_v_w2, w_out,
           norm_ffn_pre, norm_ffn_post, ffn_w_gate_up, ffn_w_down, ple_w_proj, ple_w_gate, ple_b_gate,
           batch, seq):
    t, d = x.shape
    d_attn = N_HEADS * HEAD_DIM
    d_kv = N_KV_GROUPS * HEAD_DIM
    row2 = lambda v: v.reshape(1, -1)

    o_q = 2 * d
    o_kv = o_q + d_attn
    o_gn = o_kv + 6 * d_kv
    o_gm = o_gn + 3 * N_HEADS
    w_f = jnp.concatenate([w_in[:, 0:o_q], w_in[:, o_gm:o_gm + 2 * d]], axis=1).astype(BF16)
    w_q = w_in[:, o_q:o_kv].astype(BF16)
    w_kv = w_in[:, o_kv:o_gn].astype(BF16)
    per_g = 3 * HEADS_PER_GROUP
    w_gn = w_in[:, o_gn:o_gm].reshape(d, N_KV_GROUPS, per_g)
    w_gn = jnp.pad(w_gn, ((0, 0), (0, 0), (0, LANES - per_g))).reshape(d, N_KV_GROUPS * LANES).astype(BF16)

    g_pre = row2(norm_mix_pre)
    tm = min(512, t)
    fm = _norm_matmul(x, g_pre, w_f, BF16, tm, w_f.shape[1])
    qm, gnm, kv_chunks, kvm = _norm_qkv(x, g_pre, jnp.concatenate([w_q, w_gn, w_kv], axis=1), d_attn,
                                        N_KV_GROUPS * LANES, 2 * N_KV_GROUPS, tm)

    yr = _rnn(fm, conv_w, row2(conv_b), _block_diag_chunks(lru_wa).astype(BF16),
              _block_diag_chunks(lru_wx).astype(BF16), row2(lru_ba), row2(lru_bx), row2(lru_lambda),
              batch, seq, min(512, seq))

    nchunk = seq // CMP_STRIDE
    kcm, vcm = _compress(kv_chunks, cmp_k_w1.astype(BF16), cmp_k_w2.astype(BF16), cmp_pos_k.reshape(1, -1),
                         cmp_v_w1.astype(BF16), cmp_v_w2.astype(BF16), cmp_pos_v.reshape(1, -1), batch, nchunk)

    tq = min(256, seq)
    ya = _nsa(qm, kvm, kcm, vcm, gnm, fm, batch, seq, tq, min(NSA_KEY_TILE, seq), min(256, seq),
              gm_col0=(3 * d) // (HEADS_PER_GROUP * HEAD_DIM))

    dff = ffn_w_down.shape[0]
    tf = dff // 2 if (dff // 2) % LANES == 0 else dff
    return _ffn(x, yr, ya, w_out.astype(BF16), row2(norm_mix_post), row2(norm_ffn_pre),
                ffn_w_gate_up.astype(BF16), ffn_w_down.astype(BF16), row2(norm_ffn_post),
                p, ple_w_gate.astype(BF16), row2(ple_b_gate), ple_w_proj.astype(BF16), tm, tf)


def kernel(x, p, norm_mix_pre, norm_mix_post, w_in, conv_w, conv_b, lru_wa, lru_ba, lru_wx, lru_bx, lru_lambda, cmp_pos_k, cmp_pos_v, cmp_k_w1, cmp_k_w2, cmp_v_w1, cmp_v_w2, w_out, norm_ffn_pre, norm_ffn_post, ffn_w_gate_up, ffn_w_down, ple_w_proj, ple_w_gate, ple_b_gate):
    batch, seq, d = x.shape
    depth = w_in.shape[0]
    xf = x.reshape(batch * seq, d)
    for i in range(depth):
        xf = _layer(xf, p[i].reshape(batch * seq, -1), norm_mix_pre[i], norm_mix_post[i], w_in[i], conv_w[i],
                    conv_b[i], lru_wa[i], lru_ba[i], lru_wx[i], lru_bx[i], lru_lambda[i], cmp_pos_k[i],
                    cmp_pos_v[i], cmp_k_w1[i], cmp_k_w2[i], cmp_v_w1[i], cmp_v_w2[i], w_out[i],
                    norm_ffn_pre[i], norm_ffn_post[i], ffn_w_gate_up[i], ffn_w_down[i], ple_w_proj[i],
                    ple_w_gate[i], ple_b_gate[i], batch, seq)
    return xf.reshape(batch, seq, d)
```

```python
import functools

import numpy as np
import jax
import jax.numpy as jnp
from jax import lax
from jax.experimental import pallas as pl
from jax.experimental.pallas import tpu as pltpu

CONV_WIDTH = 4
LRU_C = 8.0
N_HEADS = 16
HEAD_DIM = 64
N_KV_GROUPS = 4
HEADS_PER_GROUP = N_HEADS // N_KV_GROUPS
CMP_LEN = 32
CMP_STRIDE = 16
SEL_LEN = 64
SEL_TOPK = 16
WINDOW = 512
ALIBI_MAX_BIAS = 8.0
NORM_EPS = 1e-6

LANES = 128
SUBLANES = 8
VMEM_LIMIT_BYTES = 56 * 1024 * 1024

MASK_NEG = -1e30
LOG2_E = 1.4426950408889634
N_ALIBI_COLS = 9
PAD_COL = HEAD_DIM + N_ALIBI_COLS
WORD_BITS = 16
WORD_SHIFT = 4
SEL_SHIFT = 6
N_FORCED = 3
NEAR_BLOCKS = 24
FILL_ROWS = 512
NSA_KEY_TILE = 11 * SEL_LEN
NSA_TILES_PER_STEP = 2
OVERLAP_CHUNKS = 4
BF16 = jnp.bfloat16
F32 = jnp.float32


def _dot(a, b):
    return jnp.dot(a, b, preferred_element_type=F32)


def _dot_nt(a, b):
    return lax.dot_general(a, b, (((1,), (1,)), ((), ())), preferred_element_type=F32)


def _sigmoid(x):
    return 0.5 * jnp.tanh(0.5 * x) + 0.5


def _gelu_tanh(x):
    c = np.float32(np.sqrt(2.0 / np.pi))
    half = 0.5 * x
    return half + half * jnp.tanh(x * (c + (c * 0.044715) * (x * x)))


def _rmsnorm(x, g):
    ms = jnp.mean(x * x, axis=-1, keepdims=True)
    return x * lax.rsqrt(ms + NORM_EPS) * g


def _params(sem):
    return pltpu.CompilerParams(dimension_semantics=sem, vmem_limit_bytes=VMEM_LIMIT_BYTES)


def _norm_matmul_kernel(x_ref, g_ref, w_ref, o_ref, h_ref):
    @pl.when(pl.program_id(1) == 0)
    def _():
        h_ref[...] = _rmsnorm(x_ref[...], g_ref[...]).astype(BF16)

    o_ref[...] = _dot(h_ref[...], w_ref[...]).astype(o_ref.dtype)


def _norm_matmul(x, g, w, out_dtype, tm, tn):
    t, k = x.shape
    n = w.shape[1]
    return pl.pallas_call(
        _norm_matmul_kernel,
        grid=(t // tm, n // tn),
        in_specs=[
            pl.BlockSpec((tm, k), lambda i, j: (i, 0)),
            pl.BlockSpec((1, k), lambda i, j: (0, 0)),
            pl.BlockSpec((k, tn), lambda i, j: (0, j)),
        ],
        out_specs=pl.BlockSpec((tm, tn), lambda i, j: (i, j)),
        out_shape=jax.ShapeDtypeStruct((t, n), out_dtype),
        scratch_shapes=[pltpu.VMEM((tm, k), BF16)],
        compiler_params=_params(("parallel", "arbitrary")),
        name="norm_matmul",
    )(x, g, w)


def _norm_qkv_kernel(x_ref, g_ref, w_ref, q_ref, gn_ref, cv_ref, kv_ref, slab_scr):
    h = _rmsnorm(x_ref[...], g_ref[...]).astype(BF16)
    z = _dot(h, w_ref[...])
    nq = q_ref.shape[1]
    ngn = gn_ref.shape[1]
    q_ref[...] = z[:, 0:nq].astype(q_ref.dtype)
    gn_ref[...] = z[:, nq:nq + ngn]
    n_cv = cv_ref.shape[0]
    rows = cv_ref.shape[1]
    for c in range(n_cv):
        lo = nq + ngn + c * HEAD_DIM
        slab_scr[...] = z[:, lo:lo + HEAD_DIM]
        for l in range(CMP_STRIDE):
            cv_ref[c, :, l * HEAD_DIM:(l + 1) * HEAD_DIM] = (
                slab_scr[pl.ds(l, rows, stride=CMP_STRIDE), :].astype(cv_ref.dtype))
    for c in range(kv_ref.shape[0]):
        lo = nq + ngn + (n_cv + c) * HEAD_DIM
        kv_ref[c] = z[:, lo:lo + HEAD_DIM].astype(kv_ref.dtype)


def _norm_qkv(x, g, w, nq, ngn, n_cv, tm):
    t, k = x.shape
    n = w.shape[1]
    n_kv = (n - nq - ngn) // HEAD_DIM - n_cv
    return pl.pallas_call(
        _norm_qkv_kernel,
        grid=(t // tm,),
        in_specs=[
            pl.BlockSpec((tm, k), lambda i: (i, 0)),
            pl.BlockSpec((1, k), lambda i: (0, 0)),
            pl.BlockSpec((k, n), lambda i: (0, 0)),
        ],
        out_specs=[
            pl.BlockSpec((tm, nq), lambda i: (i, 0)),
            pl.BlockSpec((tm, ngn), lambda i: (i, 0)),
            pl.BlockSpec((n_cv, tm // CMP_STRIDE, CMP_STRIDE * HEAD_DIM), lambda i: (0, i, 0)),
            pl.BlockSpec((n_kv, tm, HEAD_DIM), lambda i: (0, i, 0)),
        ],
        out_shape=[
            jax.ShapeDtypeStruct((t, nq), BF16),
            jax.ShapeDtypeStruct((t, ngn), F32),
            jax.ShapeDtypeStruct((n_cv, t // CMP_STRIDE, CMP_STRIDE * HEAD_DIM), BF16),
            jax.ShapeDtypeStruct((n_kv, t, HEAD_DIM), BF16),
        ],
        scratch_shapes=[pltpu.VMEM((tm, HEAD_DIM), F32)],
        compiler_params=_params(("parallel",)),
        name="norm_qkv",
    )(x, g, w)


def _rnn_kernel(xr_ref, gr_ref, gm_ref, cw_ref, cb_ref, wa_ref, wx_ref, ba_ref, bx_ref, lam_ref,
                o_ref, xbuf, a_scr, b_scr, h_scr, *, ts):
    s_idx = pl.program_id(1)
    d = xr_ref.shape[1]
    halo = SUBLANES

    @pl.when(s_idx == 0)
    def _():
        xbuf[0:halo, :] = jnp.zeros((halo, d), F32)
        h_scr[...] = jnp.zeros(h_scr.shape, F32)

    xbuf[halo:halo + ts, :] = xr_ref[...].astype(F32)
    xc = cb_ref[...] + xbuf[halo:halo + ts, :] * cw_ref[CONV_WIDTH - 1:CONV_WIDTH, :]
    for k in range(1, CONV_WIDTH):
        xc = xc + xbuf[halo - k:halo - k + ts, :] * cw_ref[CONV_WIDTH - 1 - k:CONV_WIDTH - k, :]
    xbuf[0:halo, :] = xbuf[ts:ts + halo, :]

    xcb = xc.astype(BF16)
    n_chunks = d // LANES
    neg_sp = -LRU_C * (jnp.maximum(-lam_ref[...], 0.0) + jnp.log(1.0 + jnp.exp(-jnp.abs(lam_ref[...]))))
    for c in range(n_chunks):
        sl = slice(c * LANES, (c + 1) * LANES)
        xk = xcb[:, sl]
        r = _sigmoid(_dot(xk, wa_ref[c]) + ba_ref[:, sl])
        i = _sigmoid(_dot(xk, wx_ref[c]) + bx_ref[:, sl])
        log_a = r * neg_sp[:, sl]
        a = jnp.exp(log_a)
        a_scr[:, sl] = a
        u = 1.0 - a * a
        root = jnp.where(u > 0.0, u * lax.rsqrt(u), 0.0)
        b_scr[:, sl] = root * (i * xc[:, sl])

    def group(gi, h):
        base = pl.multiple_of(gi * SUBLANES, SUBLANES)
        a8 = a_scr[pl.ds(base, SUBLANES), :]
        b8 = b_scr[pl.ds(base, SUBLANES), :]
        rows = []
        for j in range(SUBLANES):
            h = a8[j:j + 1, :] * h + b8[j:j + 1, :]
            rows.append(h)
        a_scr[pl.ds(base, SUBLANES), :] = jnp.concatenate(rows, axis=0)
        return h

    h_last = lax.fori_loop(0, ts // SUBLANES, group, h_scr[0:1, :])
    h_scr[0:1, :] = h_last
    o_ref[...] = (_sigmoid(gm_ref[...].astype(F32)) * a_scr[...]
                  * _gelu_tanh(gr_ref[...].astype(F32))).astype(o_ref.dtype)


def _rnn(f, conv_w, conv_b, wa, wx, ba, bx, lam, batch, seq, ts):
    t = f.shape[0]
    d = conv_w.shape[1]
    ns = seq // ts
    nd = d // d
    del nd
    row = lambda b, s: b * ns + s
    vec = lambda r: pl.BlockSpec((r, d), lambda b, s: (0, 0))
    return pl.pallas_call(
        functools.partial(_rnn_kernel, ts=ts),
        grid=(batch, ns),
        in_specs=[
            pl.BlockSpec((ts, d), lambda b, s: (row(b, s), 0)),
            pl.BlockSpec((ts, d), lambda b, s: (row(b, s), 1)),
            pl.BlockSpec((ts, d), lambda b, s: (row(b, s), 2)),
            vec(CONV_WIDTH), vec(1),
            pl.BlockSpec(wa.shape, lambda b, s: (0, 0, 0)),
            pl.BlockSpec(wx.shape, lambda b, s: (0, 0, 0)),
            vec(1), vec(1), vec(1),
        ],
        out_specs=pl.BlockSpec((ts, d), lambda b, s: (row(b, s), 0)),
        out_shape=jax.ShapeDtypeStruct((t, d), BF16),
        scratch_shapes=[
            pltpu.VMEM((ts + SUBLANES, d), F32),
            pltpu.VMEM((ts, d), F32),
            pltpu.VMEM((ts, d), F32),
            pltpu.VMEM((SUBLANES, d), F32),
        ],
        compiler_params=_params(("parallel", "arbitrary")),
        name="rnn_mixer",
    )(f, f, f, conv_w, conv_b, wa, wx, ba, bx, lam)


def _compress_kernel(ak_ref, av_ref, w1k_ref, w2k_ref, pk_ref, w1v_ref, w2v_ref, pv_ref, ok_ref, ov_ref):
    def one(a_ref, w1_ref, w2_ref, p_ref, o_ref):
        a = a_ref[0]
        half = a.shape[1]
        nchunk = a.shape[0]
        lo = _dot(a, w1_ref[0:half, :])
        hi = _dot(a, w1_ref[half:2 * half, :])
        pb = _dot(jnp.broadcast_to(p_ref[...], (SUBLANES, 2 * half)).astype(BF16), w1_ref[...])[0:1, :]
        h = lo + pltpu.roll(hi, nchunk - 1, 0) + pb
        o_ref[0] = _dot(_gelu_tanh(h).astype(BF16), w2_ref[...]).astype(o_ref.dtype)

    one(ak_ref, w1k_ref, w2k_ref, pk_ref, ok_ref)
    one(av_ref, w1v_ref, w2v_ref, pv_ref, ov_ref)


def _compress(kv_chunks, w1k, w2k, pk, w1v, w2v, pv, batch, nchunk):
    g = N_KV_GROUPS
    width = kv_chunks.shape[2]
    full = lambda a: pl.BlockSpec(a.shape, lambda b, gg: (0,) * a.ndim)
    out = jax.ShapeDtypeStruct((batch * g, nchunk, HEAD_DIM), BF16)
    return pl.pallas_call(
        _compress_kernel,
        grid=(batch, g),
        in_specs=[
            pl.BlockSpec((1, nchunk, width), lambda b, gg: (gg, b, 0)),
            pl.BlockSpec((1, nchunk, width), lambda b, gg: (g + gg, b, 0)),
            full(w1k), full(w2k), full(pk), full(w1v), full(w2v), full(pv),
        ],
        out_specs=[
            pl.BlockSpec((1, nchunk, HEAD_DIM), lambda b, gg: (b * g + gg, 0, 0)),
            pl.BlockSpec((1, nchunk, HEAD_DIM), lambda b, gg: (b * g + gg, 0, 0)),
        ],
        out_shape=[out, out],
        compiler_params=_params(("parallel", "parallel")),
        name="compress",
    )(kv_chunks, kv_chunks, w1k, w2k, pk, w1v, w2v, pv)


ONES_ROWS = 16


def _scores(k, q_rows, s_ref, keep=None, keep_rows=None, shift=None):
    s = _dot(k, q_rows)
    if shift is not None:
        s = s + shift
    if keep is None:
        s_ref[...] = s
        return jnp.max(s, axis=0, keepdims=True)
    n = s.shape[0] if keep_rows is None else keep_rows
    top = s[0:n]
    for mask in keep:
        top = jnp.where(mask, top, MASK_NEG)
    s_ref[0:n, :] = top
    col_max = jnp.max(top, axis=0, keepdims=True)
    if n < s.shape[0]:
        s_ref[n:, :] = s[n:]
        col_max = jnp.maximum(col_max, jnp.max(s[n:], axis=0, keepdims=True))
    return col_max


def _accumulate(s_ref, col_max, v_aug_t, state):
    m_old, acc_old = state
    m_new = jnp.maximum(m_old, col_max)
    alpha = jnp.exp2(m_old - m_new)
    e = jnp.exp2(s_ref[...] - m_new).astype(BF16)
    return m_new, alpha * acc_old + _dot(v_aug_t, e)


def _nsa_kernel(q_ref, kc_ref, vc_ref, ks_ref, vs_ref, kw_ref, vw_ref, gn_ref, gm_ref,
                qal_ref, cpos_ref, wsel_ref, o_ref, *scratch, tq, tiles_per_step, **static):
    def tile(j, carry):
        rows = pl.ds(pl.multiple_of(j * tq, tq), tq)
        _nsa_tile(pl.program_id(2) * tiles_per_step + j, q_ref.at[rows, :], kc_ref, vc_ref, ks_ref, vs_ref,
                  kw_ref, vw_ref, gn_ref.at[rows, :], gm_ref.at[rows, :], qal_ref, cpos_ref, wsel_ref,
                  o_ref.at[rows, :], *scratch, tq=tq, **static)
        return carry

    lax.fori_loop(0, tiles_per_step, tile, 0)


def _nsa_tile(qt, q_ref, kc_ref, vc_ref, ks_ref, vs_ref, kw_ref, vw_ref, gn_ref, gm_ref,
              qal_ref, cpos_ref, wsel_ref,
              o_ref, q_t, ksaug, kwaug, kcaug, kstage, vstage, s_win, s_a, s_b, win_max, o_cmp_scr, flags,
              *, tq, nk, nkw, seq, sel_chunk):
    b = pl.program_id(0)
    g = pl.program_id(1)
    r_heads = HEADS_PER_GROUP
    dh = HEAD_DIM
    m_cols = r_heads * tq
    nb = seq // SEL_LEN
    nc = kc_ref.shape[1]
    a_w = 2 * dh
    k_w = a_w + nb

    @pl.when((b == 0) & (g == 0) & (qt == 0))
    def _():
        rows = min(FILL_ROWS, seq)

        def fill(c, carry):
            off = pl.multiple_of(c * rows, rows)
            pos = off + lax.broadcasted_iota(jnp.int32, (rows, k_w), 0)
            lane = lax.broadcasted_iota(jnp.int32, (rows, k_w), 1)
            blk_of = jnp.right_shift(pos, SEL_SHIFT)
            cols = jnp.where((lane >= dh) & (lane < dh + 3), blk_of * SEL_LEN,
                             jnp.where((lane >= dh + 3) & (lane < dh + 6), pos & (SEL_LEN - 1),
                                       jnp.where(lane == a_w + blk_of, 1, 0)))
            cols = cols.astype(F32).astype(BF16)
            ksaug[pl.ds(off, rows), :] = cols
            kwaug[pl.ds(off, rows), :] = cols[:, 0:a_w]
            return carry

        lax.fori_loop(0, seq // rows, fill, 0)
        kcaug[...] = cpos_ref[...]
        vstage[...] = jnp.ones(vstage.shape, BF16)

    @pl.when(qt == 0)
    def _():
        ksaug[:, 0:dh] = ks_ref[0]
        kwaug[:, 0:dh] = kw_ref[0]
        kcaug[:, 0:dh] = kc_ref[0]
        q_t[dh:a_w, :] = qal_ref[0]

    v_rows = dh + ONES_ROWS
    ri = lax.broadcasted_iota(jnp.int32, (v_rows, 2 * dh), 0)
    ci = lax.broadcasted_iota(jnp.int32, (v_rows, 2 * dh), 1)
    pick = (((ri < dh) & (ri == ci)) | ((ri >= dh) & (ci == dh))).astype(F32).astype(BF16)
    transposed = lambda v: _dot_nt(pick[0:dh, 0:dh], v).astype(BF16)
    staged_aug_t = lambda v: _dot_nt(pick, v).astype(BF16)
    ones_rows = jnp.ones((ONES_ROWS, nkw), BF16)
    window_aug_t = lambda v: jnp.concatenate([transposed(v), ones_rows], axis=0)
    init = (jnp.full((1, m_cols), MASK_NEG, F32), jnp.zeros((v_rows, m_cols), F32))

    t0 = qt * tq
    col = lax.broadcasted_iota(jnp.int32, (1, m_cols), 1)
    trow = t0 + (col & (tq - 1))

    scale = np.float32(HEAD_DIM ** -0.5 * LOG2_E)
    qf = jnp.transpose(q_ref[...].astype(F32) * scale)
    for r in range(r_heads):
        q_t[0:dh, r * tq:(r + 1) * tq] = qf[r * dh:(r + 1) * dh, :].astype(BF16)

    n_wt = (WINDOW + tq) // nkw
    win_s = [s_win.at[i * nkw:(i + 1) * nkw, :] for i in range(n_wt)]
    win_off = [pl.multiple_of(jnp.maximum(t0 + tq - (i + 1) * nkw, 0), nkw) for i in range(n_wt)]

    def window_scores():
        for i in range(n_wt):
            start = t0 + tq - (i + 1) * nkw
            kpos_w = start + lax.broadcasted_iota(jnp.int32, (nkw, 1), 0)
            keep = []
            if i * nkw < tq - 1:
                keep.append(kpos_w <= trow)
            if (i + 1) * nkw > WINDOW:
                keep.append(kpos_w > trow - WINDOW)
            shift = jnp.where(start >= 0, 0.0, MASK_NEG) if (i + 1) * nkw > tq else None
            win_max[i:i + 1, :] = _scores(kwaug[pl.ds(win_off[i], nkw), :], q_t[0:a_w, :], win_s[i],
                                          keep=keep or None, shift=shift)

    def compressed_and_select(rows_c, rows_b):
        cend = lax.broadcasted_iota(jnp.int32, (rows_c, 1), 0) * CMP_STRIDE + (CMP_LEN - 1)
        s = jnp.where(cend <= trow, _dot(kcaug[0:rows_c, :], q_t[0:a_w, :]), MASK_NEG)
        m = jnp.max(s, axis=0, keepdims=True)
        e = jnp.exp2(s - m)
        has_key = (trow >= CMP_LEN - 1).astype(F32)
        p = e * (has_key / jnp.sum(e, axis=0, keepdims=True))
        o_cmp_scr[...] = _dot(transposed(vc_ref[0, 0:rows_c, :]), p.astype(BF16))
        imp = p[:, 0:tq]
        for r in range(1, r_heads):
            imp = imp + p[:, r * tq:(r + 1) * tq]

        hi = imp.astype(BF16)
        r1 = imp - hi.astype(F32)
        mid = r1.astype(BF16)
        lo = (r1 - mid.astype(F32)).astype(BF16)
        wsel = wsel_ref[0:rows_b, 0:rows_c]
        imp_t = _dot(wsel, hi) + _dot(wsel, mid) + _dot(wsel, lo)

        window_scores()

        blk = lax.broadcasted_iota(jnp.int32, (rows_b, tq), 0).astype(F32)
        tq_l = t0 + lax.broadcasted_iota(jnp.int32, (rows_b, tq), 1)
        cur = jnp.right_shift(tq_l, SEL_SHIFT).astype(F32)
        valid = blk <= cur
        sel = jnp.where(blk == 0.0, 1.0, jnp.where(blk == cur, 1.0, jnp.where(blk == cur - 1.0, 1.0, 0.0)))
        score = jnp.where(valid, jnp.where(sel > 0.0, -1.0, imp_t), -1.0)
        for _ in range(min(SEL_TOPK, nb) - N_FORCED):
            mx = jnp.max(score, axis=0, keepdims=True)
            idx = jnp.min(jnp.where(score == mx, blk, float(nb)), axis=0, keepdims=True)
            hit = blk == idx
            sel = jnp.where(hit, 1.0, sel)
            score = jnp.where(hit, -2.0, score)
        selv = jnp.where(valid, sel, 0.0) > 0.0
        selneg_t = jnp.where(selv, 0.0, MASK_NEG).astype(BF16)
        unseen = jnp.full((nb - rows_b, tq), MASK_NEG, BF16)
        for r in range(r_heads):
            q_t[a_w:a_w + rows_b, r * tq:(r + 1) * tq] = selneg_t
            if rows_b < nb:
                q_t[a_w + rows_b:, r * tq:(r + 1) * tq] = unseen

        used = jnp.max(jnp.where(selv, 1.0, 0.0), axis=1, keepdims=True)
        bit_id = lax.broadcasted_iota(jnp.int32, (rows_b, 1), 0) & (WORD_BITS - 1)
        weighted = used * jnp.left_shift(1, bit_id).astype(F32)
        for i in range(nb // WORD_BITS):
            if (i + 1) * WORD_BITS <= rows_b:
                flags[i] = jnp.sum(weighted[i * WORD_BITS:(i + 1) * WORD_BITS, :]).astype(jnp.int32)
            else:
                flags[i] = 0

    n_var = nb // sel_chunk
    cmp_chunk = sel_chunk * (SEL_LEN // CMP_STRIDE)
    seen_c = (t0 + tq - CMP_LEN) // CMP_STRIDE + 1
    seen_b = (t0 + tq) // SEL_LEN
    variant = jnp.maximum((seen_c + cmp_chunk - 1) // cmp_chunk, (seen_b + sel_chunk - 1) // sel_chunk)
    for v in range(1, n_var + 1):
        pl.when(variant == v)(functools.partial(compressed_and_select, min(v * cmp_chunk, nc), v * sel_chunk))
    o_cmp = o_cmp_scr[...]

    n_own = tq // SEL_LEN
    off_q = pl.multiple_of(t0, tq)
    kstage[0:tq, :] = ksaug[pl.ds(off_q, tq), :]
    vstage[0:tq, 0:dh] = vs_ref[0, pl.ds(off_q, tq), :]

    def stage_block(j, cnt):
        src = pl.multiple_of(j * SEL_LEN, SEL_LEN)
        dst = pl.multiple_of(cnt * SEL_LEN, SEL_LEN)
        kstage[pl.ds(dst, SEL_LEN), :] = ksaug[pl.ds(src, SEL_LEN), :]
        vstage[pl.ds(dst, SEL_LEN), 0:dh] = vs_ref[0, pl.ds(src, SEL_LEN), :]

    def gather(j, cnt):
        bit = jnp.right_shift(flags[jnp.right_shift(j, WORD_SHIFT)], j & (WORD_BITS - 1)) & 1
        pl.when(bit == 1)(functools.partial(stage_block, j, cnt))
        return cnt + bit

    n_past = jnp.right_shift(t0, SEL_SHIFT)
    near_lo = jnp.maximum(n_past - NEAR_BLOCKS, 0)
    first = jnp.where(near_lo > 0, flags[0] & 1, 0)
    pl.when(first == 1)(functools.partial(stage_block, 0, n_own))

    def far_word(w, cnt):
        lo = jnp.maximum(w * WORD_BITS, 1)
        hi = jnp.minimum((w + 1) * WORD_BITS, near_lo)
        rest = jnp.where(w == 0, flags[w] & -2, flags[w])
        return lax.cond(rest != 0, lambda c: lax.fori_loop(lo, hi, gather, c), lambda c: c, cnt)

    n_blocks = lax.fori_loop(0, jnp.right_shift(near_lo + WORD_BITS - 1, WORD_SHIFT), far_word, n_own + first)
    n_blocks = lax.fori_loop(near_lo, n_past, gather, n_blocks)
    per_tile = nk // SEL_LEN
    n_tiles = (n_blocks + per_tile - 1) // per_tile
    pad_block = jnp.where(lax.broadcasted_iota(jnp.int32, (SEL_LEN, k_w), 1) == PAD_COL, 1.0, 0.0).astype(BF16)

    def pad(j, carry):
        dst = pl.multiple_of(j * SEL_LEN, SEL_LEN)
        kstage[pl.ds(dst, SEL_LEN), :] = pad_block
        return carry

    lax.fori_loop(n_blocks, n_tiles * per_tile, pad, 0)

    own_pos = t0 + lax.broadcasted_iota(jnp.int32, (tq, 1), 0)
    s_first = _dot(kstage[0:nk, :], q_t[...])
    own = jnp.where(own_pos <= trow, s_first[0:tq], MASK_NEG)
    s_a[0:tq, :] = own
    s_a[tq:, :] = s_first[tq:]
    step = (nk - tq) // max(n_wt - 1, 1) // SUBLANES * SUBLANES
    bounds = [0, tq] + [tq + step * i for i in range(1, n_wt - 1)] + [nk]
    m_w = functools.reduce(jnp.maximum, [win_max[i:i + 1, :] for i in range(n_wt)])
    parts, weights = [], []
    for i in range(n_wt):
        rows = own if i == 0 else s_first[bounds[i]:bounds[i + 1]]
        parts.append(jnp.max(rows, axis=0, keepdims=True))
        weights.append(jnp.exp2(win_s[i][...] - (m_w + parts[i] * 0.0)).astype(BF16))
    max_a = functools.reduce(jnp.maximum, parts)
    v_win = jnp.concatenate([window_aug_t(vw_ref[0, pl.ds(win_off[i], nkw), :]) for i in range(n_wt)], axis=1)
    acc_w = _dot(v_win, jnp.concatenate(weights, axis=0))
    o_win = acc_w[0:dh] / acc_w[dh:dh + 1]

    def tile_scores(i, s_ref):
        return _scores(kstage[pl.ds(pl.multiple_of(i * nk, nk), nk), :], q_t[...], s_ref)

    def tile_accumulate(i, s_ref, col_max, state):
        return _accumulate(s_ref, col_max, staged_aug_t(vstage[pl.ds(pl.multiple_of(i * nk, nk), nk), :]), state)

    def scores_and_accumulate(i_next, s_next_ref, i_cur, s_cur_ref, max_cur, state):
        m_old, acc_old = state
        m_new = jnp.maximum(m_old, max_cur)
        alpha = jnp.exp2(m_old - m_new)
        s_next = _dot(kstage[pl.ds(pl.multiple_of(i_next * nk, nk), nk), :], q_t[...])
        s_next_ref[...] = s_next
        rows = nk // OVERLAP_CHUNKS
        partial, weights = [], []
        for c in range(OVERLAP_CHUNKS):
            rs = slice(c * rows, (c + 1) * rows)
            partial.append(jnp.max(s_next[rs], axis=0, keepdims=True))
            m_c = m_new + partial[c] * 0.0
            weights.append(jnp.exp2(s_cur_ref[rs, :] - m_c).astype(BF16))
        v_t = staged_aug_t(vstage[pl.ds(pl.multiple_of(i_cur * nk, nk), nk), :])
        acc = alpha * acc_old + _dot(v_t, jnp.concatenate(weights, axis=0))
        return (m_new, acc), functools.reduce(jnp.maximum, partial)

    def slc_pair(j, carry):
        state, max_a = carry
        i = 2 * j
        state, max_b = scores_and_accumulate(i + 1, s_b, i, s_a, max_a, state)
        state, max_a = scores_and_accumulate(i + 2, s_a, i + 1, s_b, max_b, state)
        return state, max_a

    n_pairs = (n_tiles - 1) // 2
    state, max_a = lax.fori_loop(0, n_pairs, slc_pair, (init, max_a))
    i_a = 2 * n_pairs

    def two_left(state):
        state, max_b = scores_and_accumulate(i_a + 1, s_b, i_a, s_a, max_a, state)
        return tile_accumulate(i_a + 1, s_b, max_b, state)

    def one_left(state):
        return tile_accumulate(i_a, s_a, max_a, state)

    _, acc = lax.cond(n_tiles - i_a == 2, two_left, one_left, state)
    o_slc = acc[0:dh] / acc[dh:dh + 1]

    gates = jnp.transpose(_sigmoid(gn_ref[...]))
    heads = []
    for r in range(r_heads):
        cs = slice(r * tq, (r + 1) * tq)
        heads.append(gates[3 * r:3 * r + 1, :] * o_cmp[:, cs] + gates[3 * r + 1:3 * r + 2, :] * o_slc[:, cs]
                     + gates[3 * r + 2:3 * r + 3, :] * o_win[:, cs])
    o = jnp.transpose(jnp.concatenate(heads, axis=0))
    o_ref[...] = (_sigmoid(gm_ref[...].astype(F32)) * o).astype(o_ref.dtype)


def _alibi_tables(seq, nc, tq):
    import ml_dtypes
    bf = ml_dtypes.bfloat16
    h = np.arange(1, N_HEADS + 1, dtype=np.float32)
    slopes = (np.exp2(-ALIBI_MAX_BIAS * h / N_HEADS) * LOG2_E).astype(np.float32)
    s1 = slopes.astype(bf).astype(np.float32)
    s2 = (slopes - s1).astype(bf).astype(np.float32)
    s3 = (slopes - s1 - s2).astype(bf).astype(np.float32)
    dh = HEAD_DIM
    qal = np.zeros((N_HEADS, dh), np.float32)
    for rep in range(3):
        qal[:, 3 * rep + 0] = s1
        qal[:, 3 * rep + 1] = s2
        qal[:, 3 * rep + 2] = s3
    qal[:, PAD_COL - dh] = MASK_NEG
    qal = qal.reshape(N_KV_GROUPS, HEADS_PER_GROUP, dh).transpose(0, 2, 1)
    qal_p = np.repeat(qal, tq, axis=2)

    nb = seq // SEL_LEN
    c = np.arange(nc)
    cpos = np.zeros((nc, 2 * dh), np.float32)
    cpos[:, dh:dh + 3] = ((c // 16) * 16 * CMP_STRIDE)[:, None]
    cpos[:, dh + 3:dh + 6] = ((c % 16) * CMP_STRIDE)[:, None]
    cpos[:, dh + 6:dh + 9] = CMP_LEN - 1

    r_sel = SEL_LEN // CMP_STRIDE
    r_cmp = CMP_LEN // CMP_STRIDE
    wsel = np.zeros((nb, nc), np.float32)
    for j in range(nb):
        for mm in range(r_sel):
            for nn in range(r_cmp):
                ci = r_sel * j + mm - nn
                if 0 <= ci < nc - 1:
                    wsel[j, ci] += 1.0
    as_bf = lambda a: jnp.asarray(a.astype(bf))
    return as_bf(qal_p), as_bf(cpos), as_bf(wsel)


def _nsa(qm, kvm, kcm, vcm, gnm, fm, batch, seq, tq, nk, nkw, gm_col0):
    t = qm.shape[0]
    g = N_KV_GROUPS
    dh = HEAD_DIM
    nqt = seq // tq
    nc = kcm.shape[1]
    nb = seq // SEL_LEN
    gw = HEADS_PER_GROUP * dh
    m_cols = HEADS_PER_GROUP * tq
    qal, cpos, wsel = _alibi_tables(seq, nc, tq)
    n_wt = (WINDOW + tq) // nkw
    sel_chunk = max(WORD_BITS, nb // 4)
    per_tile = nk // SEL_LEN
    stage_rows = -(-nb // per_tile) * per_tile * SEL_LEN
    tps = NSA_TILES_PER_STEP if nqt % NSA_TILES_PER_STEP == 0 else 1
    n_steps = nqt // tps
    rq = tps * tq
    row = lambda b, gg, i: b * n_steps + i
    slab = lambda base: pl.BlockSpec((1, seq, dh), lambda b, gg, i: (base + gg, b, 0))
    cmp_slab = pl.BlockSpec((1, nc, dh), lambda b, gg, i: (b * g + gg, 0, 0))
    const = lambda a: pl.BlockSpec(a.shape, lambda b, gg, i: (0,) * a.ndim)
    return pl.pallas_call(
        functools.partial(_nsa_kernel, tq=tq, tiles_per_step=tps, nk=nk, nkw=nkw, seq=seq, sel_chunk=sel_chunk),
        grid=(batch, g, n_steps),
        in_specs=[
            pl.BlockSpec((rq, gw), lambda b, gg, i: (row(b, gg, i), gg)),
            cmp_slab, cmp_slab,
            slab(0), slab(g), slab(2 * g), slab(3 * g),
            pl.BlockSpec((rq, LANES), lambda b, gg, i: (row(b, gg, i), gg)),
            pl.BlockSpec((rq, gw), lambda b, gg, i: (row(b, gg, i), gm_col0 + gg)),
            pl.BlockSpec((1, dh, m_cols), lambda b, gg, i: (gg, 0, 0)),
            const(cpos), const(wsel),
        ],
        out_specs=pl.BlockSpec((rq, gw), lambda b, gg, i: (row(b, gg, i), gg)),
        out_shape=jax.ShapeDtypeStruct((t, g * gw), BF16),
        scratch_shapes=[
            pltpu.VMEM((2 * dh + nb, m_cols), BF16),
            pltpu.VMEM((seq, 2 * dh + nb), BF16),
            pltpu.VMEM((seq, 2 * dh), BF16),
            pltpu.VMEM((nc, 2 * dh), BF16),
            pltpu.VMEM((stage_rows, 2 * dh + nb), BF16),
            pltpu.VMEM((stage_rows, 2 * dh), BF16),
            pltpu.VMEM((n_wt * nkw, m_cols), F32),
            pltpu.VMEM((nk, m_cols), F32),
            pltpu.VMEM((nk, m_cols), F32),
            pltpu.VMEM((SUBLANES, m_cols), F32),
            pltpu.VMEM((dh, m_cols), F32),
            pltpu.SMEM((nb // WORD_BITS,), jnp.int32),
        ],
        compiler_params=_params(("arbitrary", "arbitrary", "arbitrary")),
        name="nsa",
    )(qm, kcm, vcm, kvm, kvm, kvm, kvm, gnm, fm, qal, cpos, wsel)


def _ffn_kernel(x_ref, yr_ref, ya_ref, wo_ref, gmix_ref, gpre_ref, wg_ref, wu_ref, wd_ref, gpost_ref,
                p_ref, wpg_ref, bpg_ref, wpp_ref, o_ref, x1_ref, h_ref, acc_ref):
    j = pl.program_id(1)

    @pl.when(j == 0)
    def _():
        y = (yr_ref[...].astype(F32) + ya_ref[...].astype(F32)).astype(BF16)
        x1 = x_ref[...] + _rmsnorm(_dot(y, wo_ref[...]), gmix_ref[...])
        x1_ref[...] = x1
        h_ref[...] = _rmsnorm(x1, gpre_ref[...]).astype(BF16)
        acc_ref[...] = jnp.zeros(acc_ref.shape, F32)

    h = h_ref[...]
    gate = _dot(h, wg_ref[...])
    up = _dot(h, wu_ref[...])
    act = (gate * _sigmoid(gate) * up).astype(BF16)
    acc_ref[...] += _dot(act, wd_ref[...])

    @pl.when(j == pl.num_programs(1) - 1)
    def _():
        x2 = x1_ref[...] + _rmsnorm(acc_ref[...], gpost_ref[...])
        gate_p = _sigmoid(_dot(x2.astype(BF16), wpg_ref[...]) + bpg_ref[...])
        o_ref[...] = x2 + gate_p * _dot(p_ref[...].astype(BF16), wpp_ref[...])


def _ffn(x, yr, ya, wo, gmix, gpre, wgu, wd, gpost, p, wpg, bpg, wpp, tm, tf):
    t, d = x.shape
    dff = wd.shape[0]
    nf = dff // tf
    dp = p.shape[1]
    rows = pl.BlockSpec((tm, d), lambda i, j: (i, 0))
    vec = pl.BlockSpec((1, d), lambda i, j: (0, 0))
    square = pl.BlockSpec((d, d), lambda i, j: (0, 0))
    return pl.pallas_call(
        _ffn_kernel,
        grid=(t // tm, nf),
        in_specs=[
            rows, rows, rows, square, vec, vec,
            pl.BlockSpec((d, tf), lambda i, j: (0, j)),
            pl.BlockSpec((d, tf), lambda i, j: (0, nf + j)),
            pl.BlockSpec((tf, d), lambda i, j: (j, 0)),
            vec,
            pl.BlockSpec((tm, dp), lambda i, j: (i, 0)),
            square,
            vec,
            pl.BlockSpec((dp, d), lambda i, j: (0, 0)),
        ],
        out_specs=rows,
        out_shape=jax.ShapeDtypeStruct((t, d), F32),
        scratch_shapes=[pltpu.VMEM((tm, d), F32), pltpu.VMEM((tm, d), BF16), pltpu.VMEM((tm, d), F32)],
        compiler_params=_params(("parallel", "arbitrary")),
        name="out_ffn_ple",
    )(x, yr, ya, wo, gmix, gpre, wgu, wgu, wd, gpost, p, wpg, bpg, wpp)


def _block_diag_chunks(w):
    n, bs, _ = w.shape
    per = LANES // bs
    w = w.reshape(n // per, per, bs, bs)
    eye = jnp.eye(per, dtype=w.dtype)
    return jnp.einsum('cpij,pq->cpiqj', w, eye).reshape(n // per, LANES, LANES)


def _layer(x, p, norm_mix_pre, norm_mix_post, w_in, conv_w, conv_b, lru_wa, lru_ba, lru_wx, lru_bx,
           lru_lambda, cmp_pos_k, cmp_pos_v, cmp_k_w1, cmp_k_w2, cmp_v_w1, cmp_v_w2, w_out,
           norm_ffn_pre, norm_ffn_post, ffn_w_gate_up, ffn_w_down, ple_w_proj, ple_w_gate, ple_b_gate,
           batch, seq):
    t, d = x.shape
    d_attn = N_HEADS * HEAD_DIM
    d_kv = N_KV_GROUPS * HEAD_DIM
    row2 = lambda v: v.reshape(1, -1)

    o_q = 2 * d
    o_kv = o_q + d_attn
    o_gn = o_kv + 6 * d_kv
    o_gm = o_gn + 3 * N_HEADS
    w_f = jnp.concatenate([w_in[:, 0:o_q], w_in[:, o_gm:o_gm + 2 * d]], axis=1).astype(BF16)
    w_q = w_in[:, o_q:o_kv].astype(BF16)
    w_kv = w_in[:, o_kv:o_gn].astype(BF16)
    per_g = 3 * HEADS_PER_GROUP
    w_gn = w_in[:, o_gn:o_gm].reshape(d, N_KV_GROUPS, per_g)
    w_gn = jnp.pad(w_gn, ((0, 0), (0, 0), (0, LANES - per_g))).reshape(d, N_KV_GROUPS * LANES).astype(BF16)

    g_pre = row2(norm_mix_pre)
    tm = min(512, t)
    fm = _norm_matmul(x, g_pre, w_f, BF16, tm, w_f.shape[1])
    qm, gnm, kv_chunks, kvm = _norm_qkv(x, g_pre, jnp.concatenate([w_q, w_gn, w_kv], axis=1), d_attn,
                                        N_KV_GROUPS * LANES, 2 * N_KV_GROUPS, tm)

    yr = _rnn(fm, conv_w, row2(conv_b), _block_diag_chunks(lru_wa).astype(BF16),
              _block_diag_chunks(lru_wx).astype(BF16), row2(lru_ba), row2(lru_bx), row2(lru_lambda),
              batch, seq, min(512, seq))

    nchunk = seq // CMP_STRIDE
    kcm, vcm = _compress(kv_chunks, cmp_k_w1.astype(BF16), cmp_k_w2.astype(BF16), cmp_pos_k.reshape(1, -1),
                         cmp_v_w1.astype(BF16), cmp_v_w2.astype(BF16), cmp_pos_v.reshape(1, -1), batch, nchunk)

    tq = min(256, seq)
    ya = _nsa(qm, kvm, kcm, vcm, gnm, fm, batch, seq, tq, min(NSA_KEY_TILE, seq), min(256, seq),
              gm_col0=(3 * d) // (HEADS_PER_GROUP * HEAD_DIM))

    dff = ffn_w_down.shape[0]
    tf = dff
    return _ffn(x, yr, ya, w_out.astype(BF16), row2(norm_mix_post), row2(norm_ffn_pre),
                ffn_w_gate_up.astype(BF16), ffn_w_down.astype(BF16), row2(norm_ffn_post),
                p, ple_w_gate.astype(BF16), row2(ple_b_gate), ple_w_proj.astype(BF16), tm, tf)


def kernel(x, p, norm_mix_pre, norm_mix_post, w_in, conv_w, conv_b, lru_wa, lru_ba, lru_wx, lru_bx, lru_lambda, cmp_pos_k, cmp_pos_v, cmp_k_w1, cmp_k_w2, cmp_v_w1, cmp_v_w2, w_out, norm_ffn_pre, norm_ffn_post, ffn_w_gate_up, ffn_w_down, ple_w_proj, ple_w_gate, ple_b_gate):
    batch, seq, d = x.shape
    depth = w_in.shape[0]
    xf = x.reshape(batch * seq, d)
    for i in range(depth):
        xf = _layer(xf, p[i].reshape(batch * seq, -1), norm_mix_pre[i], norm_mix_post[i], w_in[i], conv_w[i],
                    conv_b[i], lru_wa[i], lru_ba[i], lru_wx[i], lru_bx[i], lru_lambda[i], cmp_pos_k[i],
                    cmp_pos_v[i], cmp_k_w1[i], cmp_k_w2[i], cmp_v_w1[i], cmp_v_w2[i], w_out[i],
                    norm_ffn_pre[i], norm_ffn_post[i], ffn_w_gate_up[i], ffn_w_down[i], ple_w_proj[i],
                    ple_w_gate[i], ple_b_gate[i], batch, seq)
    return xf.reshape(batch, seq, d)
```

```python
import functools

import numpy as np
import jax
import jax.numpy as jnp
from jax import lax
from jax.experimental import pallas as pl
from jax.experimental.pallas import tpu as pltpu

CONV_WIDTH = 4
LRU_C = 8.0
N_HEADS = 16
HEAD_DIM = 64
N_KV_GROUPS = 4
HEADS_PER_GROUP = N_HEADS // N_KV_GROUPS
CMP_LEN = 32
CMP_STRIDE = 16
SEL_LEN = 64
SEL_TOPK = 16
WINDOW = 512
ALIBI_MAX_BIAS = 8.0
NORM_EPS = 1e-6

LANES = 128
SUBLANES = 8
VMEM_LIMIT_BYTES = 56 * 1024 * 1024

MASK_NEG = -1e30
LOG2_E = 1.4426950408889634
N_ALIBI_COLS = 9
PAD_COL = HEAD_DIM + N_ALIBI_COLS
WORD_BITS = 16
WORD_SHIFT = 4
SEL_SHIFT = 6
N_FORCED = 3
NEAR_BLOCKS = 24
FILL_ROWS = 512
NSA_KEY_TILE = 11 * SEL_LEN
SCAN_UNROLL = 4
NSA_TILES_PER_STEP = 4
OVERLAP_CHUNKS = 4
BF16 = jnp.bfloat16
F32 = jnp.float32


def _dot(a, b):
    return jnp.dot(a, b, preferred_element_type=F32)


def _dot_nt(a, b):
    return lax.dot_general(a, b, (((1,), (1,)), ((), ())), preferred_element_type=F32)


def _sigmoid(x):
    return 0.5 * jnp.tanh(0.5 * x) + 0.5


def _gelu_tanh(x):
    c = np.float32(np.sqrt(2.0 / np.pi))
    half = 0.5 * x
    return half + half * jnp.tanh(x * (c + (c * 0.044715) * (x * x)))


def _rmsnorm(x, g):
    ms = jnp.mean(x * x, axis=-1, keepdims=True)
    return x * lax.rsqrt(ms + NORM_EPS) * g


def _params(sem):
    return pltpu.CompilerParams(dimension_semantics=sem, vmem_limit_bytes=VMEM_LIMIT_BYTES)


def _norm_matmul_kernel(x_ref, g_ref, w_ref, o_ref, h_ref):
    @pl.when(pl.program_id(1) == 0)
    def _():
        h_ref[...] = _rmsnorm(x_ref[...], g_ref[...]).astype(BF16)

    o_ref[...] = _dot(h_ref[...], w_ref[...]).astype(o_ref.dtype)


def _norm_matmul(x, g, w, out_dtype, tm, tn):
    t, k = x.shape
    n = w.shape[1]
    return pl.pallas_call(
        _norm_matmul_kernel,
        grid=(t // tm, n // tn),
        in_specs=[
            pl.BlockSpec((tm, k), lambda i, j: (i, 0)),
            pl.BlockSpec((1, k), lambda i, j: (0, 0)),
            pl.BlockSpec((k, tn), lambda i, j: (0, j)),
        ],
        out_specs=pl.BlockSpec((tm, tn), lambda i, j: (i, j)),
        out_shape=jax.ShapeDtypeStruct((t, n), out_dtype),
        scratch_shapes=[pltpu.VMEM((tm, k), BF16)],
        compiler_params=_params(("parallel", "arbitrary")),
        name="norm_matmul",
    )(x, g, w)


def _norm_qkv_kernel(x_ref, g_ref, w_ref, q_ref, gn_ref, cv_ref, kv_ref, slab_scr):
    h = _rmsnorm(x_ref[...], g_ref[...]).astype(BF16)
    z = _dot(h, w_ref[...])
    nq = q_ref.shape[1]
    ngn = gn_ref.shape[1]
    q_ref[...] = z[:, 0:nq].astype(q_ref.dtype)
    gn_ref[...] = z[:, nq:nq + ngn]
    n_cv = cv_ref.shape[0]
    rows = cv_ref.shape[1]
    for c in range(n_cv):
        lo = nq + ngn + c * HEAD_DIM
        slab_scr[...] = z[:, lo:lo + HEAD_DIM]
        for l in range(CMP_STRIDE):
            cv_ref[c, :, l * HEAD_DIM:(l + 1) * HEAD_DIM] = (
                slab_scr[pl.ds(l, rows, stride=CMP_STRIDE), :].astype(cv_ref.dtype))
    for c in range(kv_ref.shape[0]):
        lo = nq + ngn + (n_cv + c) * HEAD_DIM
        kv_ref[c] = z[:, lo:lo + HEAD_DIM].astype(kv_ref.dtype)


def _norm_qkv(x, g, w, nq, ngn, n_cv, tm):
    t, k = x.shape
    n = w.shape[1]
    n_kv = (n - nq - ngn) // HEAD_DIM - n_cv
    return pl.pallas_call(
        _norm_qkv_kernel,
        grid=(t // tm,),
        in_specs=[
            pl.BlockSpec((tm, k), lambda i: (i, 0)),
            pl.BlockSpec((1, k), lambda i: (0, 0)),
            pl.BlockSpec((k, n), lambda i: (0, 0)),
        ],
        out_specs=[
            pl.BlockSpec((tm, nq), lambda i: (i, 0)),
            pl.BlockSpec((tm, ngn), lambda i: (i, 0)),
            pl.BlockSpec((n_cv, tm // CMP_STRIDE, CMP_STRIDE * HEAD_DIM), lambda i: (0, i, 0)),
            pl.BlockSpec((n_kv, tm, HEAD_DIM), lambda i: (0, i, 0)),
        ],
        out_shape=[
            jax.ShapeDtypeStruct((t, nq), BF16),
            jax.ShapeDtypeStruct((t, ngn), F32),
            jax.ShapeDtypeStruct((n_cv, t // CMP_STRIDE, CMP_STRIDE * HEAD_DIM), BF16),
            jax.ShapeDtypeStruct((n_kv, t, HEAD_DIM), BF16),
        ],
        scratch_shapes=[pltpu.VMEM((tm, HEAD_DIM), F32)],
        compiler_params=_params(("parallel",)),
        name="norm_qkv",
    )(x, g, w)


def _rnn_kernel(xr_ref, gr_ref, gm_ref, cw_ref, cb_ref, wa_ref, wx_ref, ba_ref, bx_ref, lam_ref,
                o_ref, xbuf, a_scr, b_scr, h_scr, *, ts):
    s_idx = pl.program_id(1)
    d = xr_ref.shape[1]
    halo = SUBLANES

    @pl.when(s_idx == 0)
    def _():
        xbuf[0:halo, :] = jnp.zeros((halo, d), F32)
        h_scr[...] = jnp.zeros(h_scr.shape, F32)

    xbuf[halo:halo + ts, :] = xr_ref[...].astype(F32)
    xc = cb_ref[...] + xbuf[halo:halo + ts, :] * cw_ref[CONV_WIDTH - 1:CONV_WIDTH, :]
    for k in range(1, CONV_WIDTH):
        xc = xc + xbuf[halo - k:halo - k + ts, :] * cw_ref[CONV_WIDTH - 1 - k:CONV_WIDTH - k, :]
    xbuf[0:halo, :] = xbuf[ts:ts + halo, :]

    xcb = xc.astype(BF16)
    n_chunks = d // LANES
    neg_sp = -LRU_C * (jnp.maximum(-lam_ref[...], 0.0) + jnp.log(1.0 + jnp.exp(-jnp.abs(lam_ref[...]))))
    for c in range(n_chunks):
        sl = slice(c * LANES, (c + 1) * LANES)
        xk = xcb[:, sl]
        r = _sigmoid(_dot(xk, wa_ref[c]) + ba_ref[:, sl])
        i = _sigmoid(_dot(xk, wx_ref[c]) + bx_ref[:, sl])
        log_a = r * neg_sp[:, sl]
        a = jnp.exp(log_a)
        a_scr[:, sl] = a
        u = 1.0 - a * a
        root = jnp.where(u > 0.0, u * lax.rsqrt(u), 0.0)
        b_scr[:, sl] = root * (i * xc[:, sl])

    row_id = lax.broadcasted_iota(jnp.int32, (SUBLANES, d), 0)

    def group(gi, h):
        base = pl.multiple_of(gi * SUBLANES, SUBLANES)
        a = a_scr[pl.ds(base, SUBLANES), :]
        b = b_scr[pl.ds(base, SUBLANES), :]
        s = 1
        while s < SUBLANES:
            reach = row_id >= s
            b = jnp.where(reach, a * pltpu.roll(b, s, 0) + b, b)
            a = jnp.where(reach, a * pltpu.roll(a, s, 0), a)
            s *= 2
        hs = a * h + b
        a_scr[pl.ds(base, SUBLANES), :] = hs
        return jnp.broadcast_to(hs[SUBLANES - 1:SUBLANES, :], (SUBLANES, d))

    h_scr[...] = lax.fori_loop(0, ts // SUBLANES, group, h_scr[...], unroll=SCAN_UNROLL)
    o_ref[...] = (_sigmoid(gm_ref[...].astype(F32)) * a_scr[...]
                  * _gelu_tanh(gr_ref[...].astype(F32))).astype(o_ref.dtype)


def _rnn(f, conv_w, conv_b, wa, wx, ba, bx, lam, batch, seq, ts):
    t = f.shape[0]
    d = conv_w.shape[1]
    ns = seq // ts
    nd = d // d
    del nd
    row = lambda b, s: b * ns + s
    vec = lambda r: pl.BlockSpec((r, d), lambda b, s: (0, 0))
    return pl.pallas_call(
        functools.partial(_rnn_kernel, ts=ts),
        grid=(batch, ns),
        in_specs=[
            pl.BlockSpec((ts, d), lambda b, s: (row(b, s), 0)),
            pl.BlockSpec((ts, d), lambda b, s: (row(b, s), 1)),
            pl.BlockSpec((ts, d), lambda b, s: (row(b, s), 2)),
            vec(CONV_WIDTH), vec(1),
            pl.BlockSpec(wa.shape, lambda b, s: (0, 0, 0)),
            pl.BlockSpec(wx.shape, lambda b, s: (0, 0, 0)),
            vec(1), vec(1), vec(1),
        ],
        out_specs=pl.BlockSpec((ts, d), lambda b, s: (row(b, s), 0)),
        out_shape=jax.ShapeDtypeStruct((t, d), BF16),
        scratch_shapes=[
            pltpu.VMEM((ts + SUBLANES, d), F32),
            pltpu.VMEM((ts, d), F32),
            pltpu.VMEM((ts, d), F32),
            pltpu.VMEM((SUBLANES, d), F32),
        ],
        compiler_params=_params(("parallel", "arbitrary")),
        name="rnn_mixer",
    )(f, f, f, conv_w, conv_b, wa, wx, ba, bx, lam)


def _compress_kernel(ak_ref, av_ref, w1k_ref, w2k_ref, pk_ref, w1v_ref, w2v_ref, pv_ref, ok_ref, ov_ref):
    def one(a_ref, w1_ref, w2_ref, p_ref, o_ref):
        a = a_ref[0]
        half = a.shape[1]
        nchunk = a.shape[0]
        lo = _dot(a, w1_ref[0:half, :])
        hi = _dot(a, w1_ref[half:2 * half, :])
        pb = _dot(jnp.broadcast_to(p_ref[...], (SUBLANES, 2 * half)).astype(BF16), w1_ref[...])[0:1, :]
        h = lo + pltpu.roll(hi, nchunk - 1, 0) + pb
        o_ref[0] = _dot(_gelu_tanh(h).astype(BF16), w2_ref[...]).astype(o_ref.dtype)

    one(ak_ref, w1k_ref, w2k_ref, pk_ref, ok_ref)
    one(av_ref, w1v_ref, w2v_ref, pv_ref, ov_ref)


def _compress(kv_chunks, w1k, w2k, pk, w1v, w2v, pv, batch, nchunk):
    g = N_KV_GROUPS
    width = kv_chunks.shape[2]
    full = lambda a: pl.BlockSpec(a.shape, lambda b, gg: (0,) * a.ndim)
    out = jax.ShapeDtypeStruct((batch * g, nchunk, HEAD_DIM), BF16)
    return pl.pallas_call(
        _compress_kernel,
        grid=(batch, g),
        in_specs=[
            pl.BlockSpec((1, nchunk, width), lambda b, gg: (gg, b, 0)),
            pl.BlockSpec((1, nchunk, width), lambda b, gg: (g + gg, b, 0)),
            full(w1k), full(w2k), full(pk), full(w1v), full(w2v), full(pv),
        ],
        out_specs=[
            pl.BlockSpec((1, nchunk, HEAD_DIM), lambda b, gg: (b * g + gg, 0, 0)),
            pl.BlockSpec((1, nchunk, HEAD_DIM), lambda b, gg: (b * g + gg, 0, 0)),
        ],
        out_shape=[out, out],
        compiler_params=_params(("parallel", "parallel")),
        name="compress",
    )(kv_chunks, kv_chunks, w1k, w2k, pk, w1v, w2v, pv)


ONES_ROWS = 16


def _scores(k, q_rows, s_ref, keep=None, keep_rows=None, shift=None):
    s = _dot(k, q_rows)
    if shift is not None:
        s = s + shift
    if keep is None:
        s_ref[...] = s
        return jnp.max(s, axis=0, keepdims=True)
    n = s.shape[0] if keep_rows is None else keep_rows
    top = s[0:n]
    for mask in keep:
        top = jnp.where(mask, top, MASK_NEG)
    s_ref[0:n, :] = top
    col_max = jnp.max(top, axis=0, keepdims=True)
    if n < s.shape[0]:
        s_ref[n:, :] = s[n:]
        col_max = jnp.maximum(col_max, jnp.max(s[n:], axis=0, keepdims=True))
    return col_max


def _accumulate(s_ref, col_max, v_aug_t, state):
    m_old, acc_old = state
    m_new = jnp.maximum(m_old, col_max)
    alpha = jnp.exp2(m_old - m_new)
    e = jnp.exp2(s_ref[...] - m_new).astype(BF16)
    return m_new, alpha * acc_old + _dot(v_aug_t, e)


def _nsa_kernel(q_ref, kc_ref, vc_ref, ks_ref, vs_ref, kw_ref, vw_ref, gn_ref, gm_ref,
                qal_ref, cpos_ref, wsel_ref, o_ref, *scratch, tq, tiles_per_step, **static):
    def tile(j, carry):
        rows = pl.ds(pl.multiple_of(j * tq, tq), tq)
        _nsa_tile(pl.program_id(2) * tiles_per_step + j, q_ref.at[rows, :], kc_ref, vc_ref, ks_ref, vs_ref,
                  kw_ref, vw_ref, gn_ref.at[rows, :], gm_ref.at[rows, :], qal_ref, cpos_ref, wsel_ref,
                  o_ref.at[rows, :], *scratch, tq=tq, **static)
        return carry

    lax.fori_loop(0, tiles_per_step, tile, 0)


def _nsa_tile(qt, q_ref, kc_ref, vc_ref, ks_ref, vs_ref, kw_ref, vw_ref, gn_ref, gm_ref,
              qal_ref, cpos_ref, wsel_ref,
              o_ref, q_t, ksaug, kwaug, kcaug, kstage, vstage, s_win, s_a, s_b, win_max, o_cmp_scr, flags,
              *, tq, nk, nkw, seq, sel_chunk):
    b = pl.program_id(0)
    g = pl.program_id(1)
    r_heads = HEADS_PER_GROUP
    dh = HEAD_DIM
    m_cols = r_heads * tq
    nb = seq // SEL_LEN
    nc = kc_ref.shape[1]
    a_w = 2 * dh
    k_w = a_w + nb

    @pl.when((b == 0) & (g == 0) & (qt == 0))
    def _():
        rows = min(FILL_ROWS, seq)

        def fill(c, carry):
            off = pl.multiple_of(c * rows, rows)
            pos = off + lax.broadcasted_iota(jnp.int32, (rows, k_w), 0)
            lane = lax.broadcasted_iota(jnp.int32, (rows, k_w), 1)
            blk_of = jnp.right_shift(pos, SEL_SHIFT)
            cols = jnp.where((lane >= dh) & (lane < dh + 3), blk_of * SEL_LEN,
                             jnp.where((lane >= dh + 3) & (lane < dh + 6), pos & (SEL_LEN - 1),
                                       jnp.where(lane == a_w + blk_of, 1, 0)))
            cols = cols.astype(F32).astype(BF16)
            ksaug[pl.ds(off, rows), :] = cols
            kwaug[pl.ds(off, rows), :] = cols[:, 0:a_w]
            return carry

        lax.fori_loop(0, seq // rows, fill, 0)
        kcaug[...] = cpos_ref[...]
        vstage[...] = jnp.ones(vstage.shape, BF16)

    @pl.when(qt == 0)
    def _():
        ksaug[:, 0:dh] = ks_ref[0]
        kwaug[:, 0:dh] = kw_ref[0]
        kcaug[:, 0:dh] = kc_ref[0]
        q_t[dh:a_w, :] = qal_ref[0]

    v_rows = dh + ONES_ROWS
    ri = lax.broadcasted_iota(jnp.int32, (v_rows, 2 * dh), 0)
    ci = lax.broadcasted_iota(jnp.int32, (v_rows, 2 * dh), 1)
    pick = (((ri < dh) & (ri == ci)) | ((ri >= dh) & (ci == dh))).astype(F32).astype(BF16)
    transposed = lambda v: _dot_nt(pick[0:dh, 0:dh], v).astype(BF16)
    staged_aug_t = lambda v: _dot_nt(pick, v).astype(BF16)
    ones_rows = jnp.ones((ONES_ROWS, nkw), BF16)
    window_aug_t = lambda v: jnp.concatenate([transposed(v), ones_rows], axis=0)
    init = (jnp.full((1, m_cols), MASK_NEG, F32), jnp.zeros((v_rows, m_cols), F32))

    t0 = qt * tq
    col = lax.broadcasted_iota(jnp.int32, (1, m_cols), 1)
    trow = t0 + (col & (tq - 1))

    scale = np.float32(HEAD_DIM ** -0.5 * LOG2_E)
    qf = jnp.transpose(q_ref[...].astype(F32) * scale)
    for r in range(r_heads):
        q_t[0:dh, r * tq:(r + 1) * tq] = qf[r * dh:(r + 1) * dh, :].astype(BF16)

    n_wt = (WINDOW + tq) // nkw
    win_s = [s_win.at[i * nkw:(i + 1) * nkw, :] for i in range(n_wt)]
    win_off = [pl.multiple_of(jnp.maximum(t0 + tq - (i + 1) * nkw, 0), nkw) for i in range(n_wt)]

    def window_scores():
        for i in range(n_wt):
            start = t0 + tq - (i + 1) * nkw
            kpos_w = start + lax.broadcasted_iota(jnp.int32, (nkw, 1), 0)
            keep = []
            if i * nkw < tq - 1:
                keep.append(kpos_w <= trow)
            if (i + 1) * nkw > WINDOW:
                keep.append(kpos_w > trow - WINDOW)
            shift = jnp.where(start >= 0, 0.0, MASK_NEG) if (i + 1) * nkw > tq else None
            win_max[i:i + 1, :] = _scores(kwaug[pl.ds(win_off[i], nkw), :], q_t[0:a_w, :], win_s[i],
                                          keep=keep or None, shift=shift)

    def compressed_and_select(rows_c, rows_b):
        cend = lax.broadcasted_iota(jnp.int32, (rows_c, 1), 0) * CMP_STRIDE + (CMP_LEN - 1)
        s = jnp.where(cend <= trow, _dot(kcaug[0:rows_c, :], q_t[0:a_w, :]), MASK_NEG)
        m = jnp.max(s, axis=0, keepdims=True)
        e = jnp.exp2(s - m)
        has_key = (trow >= CMP_LEN - 1).astype(F32)
        p = e * (has_key / jnp.sum(e, axis=0, keepdims=True))
        o_cmp_scr[...] = _dot(transposed(vc_ref[0, 0:rows_c, :]), p.astype(BF16))
        imp = p[:, 0:tq]
        for r in range(1, r_heads):
            imp = imp + p[:, r * tq:(r + 1) * tq]

        hi = imp.astype(BF16)
        r1 = imp - hi.astype(F32)
        mid = r1.astype(BF16)
        lo = (r1 - mid.astype(F32)).astype(BF16)
        wsel = wsel_ref[0:rows_b, 0:rows_c]
        imp_t = _dot(wsel, hi) + _dot(wsel, mid) + _dot(wsel, lo)

        window_scores()

        blk = lax.broadcasted_iota(jnp.int32, (rows_b, tq), 0).astype(F32)
        tq_l = t0 + lax.broadcasted_iota(jnp.int32, (rows_b, tq), 1)
        cur = jnp.right_shift(tq_l, SEL_SHIFT).astype(F32)
        valid = blk <= cur
        sel = jnp.where(blk == 0.0, 1.0, jnp.where(blk == cur, 1.0, jnp.where(blk == cur - 1.0, 1.0, 0.0)))
        score = jnp.where(valid, jnp.where(sel > 0.0, -1.0, imp_t), -1.0)
        for _ in range(min(SEL_TOPK, nb) - N_FORCED):
            mx = jnp.max(score, axis=0, keepdims=True)
            idx = jnp.min(jnp.where(score == mx, blk, float(nb)), axis=0, keepdims=True)
            hit = blk == idx
            sel = jnp.where(hit, 1.0, sel)
            score = jnp.where(hit, -2.0, score)
        selv = jnp.where(valid, sel, 0.0) > 0.0
        selneg_t = jnp.where(selv, 0.0, MASK_NEG).astype(BF16)
        unseen = jnp.full((nb - rows_b, tq), MASK_NEG, BF16)
        for r in range(r_heads):
            q_t[a_w:a_w + rows_b, r * tq:(r + 1) * tq] = selneg_t
            if rows_b < nb:
                q_t[a_w + rows_b:, r * tq:(r + 1) * tq] = unseen

        used = jnp.max(jnp.where(selv, 1.0, 0.0), axis=1, keepdims=True)
        bit_id = lax.broadcasted_iota(jnp.int32, (rows_b, 1), 0) & (WORD_BITS - 1)
        weighted = used * jnp.left_shift(1, bit_id).astype(F32)
        for i in range(nb // WORD_BITS):
            if (i + 1) * WORD_BITS <= rows_b:
                flags[i] = jnp.sum(weighted[i * WORD_BITS:(i + 1) * WORD_BITS, :]).astype(jnp.int32)
            else:
                flags[i] = 0

    n_var = nb // sel_chunk
    cmp_chunk = sel_chunk * (SEL_LEN // CMP_STRIDE)
    seen_c = (t0 + tq - CMP_LEN) // CMP_STRIDE + 1
    seen_b = (t0 + tq) // SEL_LEN
    variant = jnp.maximum((seen_c + cmp_chunk - 1) // cmp_chunk, (seen_b + sel_chunk - 1) // sel_chunk)
    for v in range(1, n_var + 1):
        pl.when(variant == v)(functools.partial(compressed_and_select, min(v * cmp_chunk, nc), v * sel_chunk))
    o_cmp = o_cmp_scr[...]

    n_own = tq // SEL_LEN
    off_q = pl.multiple_of(t0, tq)
    kstage[0:tq, :] = ksaug[pl.ds(off_q, tq), :]
    vstage[0:tq, 0:dh] = vs_ref[0, pl.ds(off_q, tq), :]

    def stage_block(j, cnt):
        src = pl.multiple_of(j * SEL_LEN, SEL_LEN)
        dst = pl.multiple_of(cnt * SEL_LEN, SEL_LEN)
        kstage[pl.ds(dst, SEL_LEN), :] = ksaug[pl.ds(src, SEL_LEN), :]
        vstage[pl.ds(dst, SEL_LEN), 0:dh] = vs_ref[0, pl.ds(src, SEL_LEN), :]

    def gather(j, cnt):
        bit = jnp.right_shift(flags[jnp.right_shift(j, WORD_SHIFT)], j & (WORD_BITS - 1)) & 1
        pl.when(bit == 1)(functools.partial(stage_block, j, cnt))
        return cnt + bit

    n_past = jnp.right_shift(t0, SEL_SHIFT)
    near_lo = jnp.maximum(n_past - NEAR_BLOCKS, 0)
    first = jnp.where(near_lo > 0, flags[0] & 1, 0)
    pl.when(first == 1)(functools.partial(stage_block, 0, n_own))

    def far_word(w, cnt):
        lo = jnp.maximum(w * WORD_BITS, 1)
        hi = jnp.minimum((w + 1) * WORD_BITS, near_lo)
        rest = jnp.where(w == 0, flags[w] & -2, flags[w])
        return lax.cond(rest != 0, lambda c: lax.fori_loop(lo, hi, gather, c), lambda c: c, cnt)

    n_blocks = lax.fori_loop(0, jnp.right_shift(near_lo + WORD_BITS - 1, WORD_SHIFT), far_word, n_own + first)
    n_blocks = lax.fori_loop(near_lo, n_past, gather, n_blocks)
    per_tile = nk // SEL_LEN
    n_tiles = (n_blocks + per_tile - 1) // per_tile
    pad_block = jnp.where(lax.broadcasted_iota(jnp.int32, (SEL_LEN, k_w), 1) == PAD_COL, 1.0, 0.0).astype(BF16)

    def pad(j, carry):
        dst = pl.multiple_of(j * SEL_LEN, SEL_LEN)
        kstage[pl.ds(dst, SEL_LEN), :] = pad_block
        return carry

    lax.fori_loop(n_blocks, n_tiles * per_tile, pad, 0)

    own_pos = t0 + lax.broadcasted_iota(jnp.int32, (tq, 1), 0)
    s_first = _dot(kstage[0:nk, :], q_t[...])
    own = jnp.where(own_pos <= trow, s_first[0:tq], MASK_NEG)
    s_a[0:tq, :] = own
    s_a[tq:, :] = s_first[tq:]
    step = (nk - tq) // max(n_wt - 1, 1) // SUBLANES * SUBLANES
    bounds = [0, tq] + [tq + step * i for i in range(1, n_wt - 1)] + [nk]
    m_w = functools.reduce(jnp.maximum, [win_max[i:i + 1, :] for i in range(n_wt)])
    parts, weights = [], []
    for i in range(n_wt):
        rows = own if i == 0 else s_first[bounds[i]:bounds[i + 1]]
        parts.append(jnp.max(rows, axis=0, keepdims=True))
        weights.append(jnp.exp2(win_s[i][...] - (m_w + parts[i] * 0.0)).astype(BF16))
    max_a = functools.reduce(jnp.maximum, parts)
    v_win = jnp.concatenate([window_aug_t(vw_ref[0, pl.ds(win_off[i], nkw), :]) for i in range(n_wt)], axis=1)
    acc_w = _dot(v_win, jnp.concatenate(weights, axis=0))
    o_win = acc_w[0:dh] / acc_w[dh:dh + 1]

    def tile_scores(i, s_ref):
        return _scores(kstage[pl.ds(pl.multiple_of(i * nk, nk), nk), :], q_t[...], s_ref)

    def tile_accumulate(i, s_ref, col_max, state):
        return _accumulate(s_ref, col_max, staged_aug_t(vstage[pl.ds(pl.multiple_of(i * nk, nk), nk), :]), state)

    def scores_and_accumulate(i_next, s_next_ref, i_cur, s_cur_ref, max_cur, state):
        m_old, acc_old = state
        m_new = jnp.maximum(m_old, max_cur)
        alpha = jnp.exp2(m_old - m_new)
        s_next = _dot(kstage[pl.ds(pl.multiple_of(i_next * nk, nk), nk), :], q_t[...])
        s_next_ref[...] = s_next
        rows = nk // OVERLAP_CHUNKS
        partial, weights = [], []
        for c in range(OVERLAP_CHUNKS):
            rs = slice(c * rows, (c + 1) * rows)
            partial.append(jnp.max(s_next[rs], axis=0, keepdims=True))
            m_c = m_new + partial[c] * 0.0
            weights.append(jnp.exp2(s_cur_ref[rs, :] - m_c).astype(BF16))
        v_t = staged_aug_t(vstage[pl.ds(pl.multiple_of(i_cur * nk, nk), nk), :])
        acc = alpha * acc_old + _dot(v_t, jnp.concatenate(weights, axis=0))
        return (m_new, acc), functools.reduce(jnp.maximum, partial)

    def slc_pair(j, carry):
        state, max_a = carry
        i = 2 * j
        state, max_b = scores_and_accumulate(i + 1, s_b, i, s_a, max_a, state)
        state, max_a = scores_and_accumulate(i + 2, s_a, i + 1, s_b, max_b, state)
        return state, max_a

    n_pairs = (n_tiles - 1) // 2
    state, max_a = lax.fori_loop(0, n_pairs, slc_pair, (init, max_a))
    i_a = 2 * n_pairs

    def two_left(state):
        state, max_b = scores_and_accumulate(i_a + 1, s_b, i_a, s_a, max_a, state)
        return tile_accumulate(i_a + 1, s_b, max_b, state)

    def one_left(state):
        return tile_accumulate(i_a, s_a, max_a, state)

    _, acc = lax.cond(n_tiles - i_a == 2, two_left, one_left, state)
    o_slc = acc[0:dh] / acc[dh:dh + 1]

    gates = jnp.transpose(_sigmoid(gn_ref[...]))
    heads = []
    for r in range(r_heads):
        cs = slice(r * tq, (r + 1) * tq)
        heads.append(gates[3 * r:3 * r + 1, :] * o_cmp[:, cs] + gates[3 * r + 1:3 * r + 2, :] * o_slc[:, cs]
                     + gates[3 * r + 2:3 * r + 3, :] * o_win[:, cs])
    o = jnp.transpose(jnp.concatenate(heads, axis=0))
    o_ref[...] = (_sigmoid(gm_ref[...].astype(F32)) * o).astype(o_ref.dtype)


def _alibi_tables(seq, nc, tq):
    import ml_dtypes
    bf = ml_dtypes.bfloat16
    h = np.arange(1, N_HEADS + 1, dtype=np.float32)
    slopes = (np.exp2(-ALIBI_MAX_BIAS * h / N_HEADS) * LOG2_E).astype(np.float32)
    s1 = slopes.astype(bf).astype(np.float32)
    s2 = (slopes - s1).astype(bf).astype(np.float32)
    s3 = (slopes - s1 - s2).astype(bf).astype(np.float32)
    dh = HEAD_DIM
    qal = np.zeros((N_HEADS, dh), np.float32)
    for rep in range(3):
        qal[:, 3 * rep + 0] = s1
        qal[:, 3 * rep + 1] = s2
        qal[:, 3 * rep + 2] = s3
    qal[:, PAD_COL - dh] = MASK_NEG
    qal = qal.reshape(N_KV_GROUPS, HEADS_PER_GROUP, dh).transpose(0, 2, 1)
    qal_p = np.repeat(qal, tq, axis=2)

    nb = seq // SEL_LEN
    c = np.arange(nc)
    cpos = np.zeros((nc, 2 * dh), np.float32)
    cpos[:, dh:dh + 3] = ((c // 16) * 16 * CMP_STRIDE)[:, None]
    cpos[:, dh + 3:dh + 6] = ((c % 16) * CMP_STRIDE)[:, None]
    cpos[:, dh + 6:dh + 9] = CMP_LEN - 1

    r_sel = SEL_LEN // CMP_STRIDE
    r_cmp = CMP_LEN // CMP_STRIDE
    wsel = np.zeros((nb, nc), np.float32)
    for j in range(nb):
        for mm in range(r_sel):
            for nn in range(r_cmp):
                ci = r_sel * j + mm - nn
                if 0 <= ci < nc - 1:
                    wsel[j, ci] += 1.0
    as_bf = lambda a: jnp.asarray(a.astype(bf))
    return as_bf(qal_p), as_bf(cpos), as_bf(wsel)


def _nsa(qm, kvm, kcm, vcm, gnm, fm, batch, seq, tq, nk, nkw, gm_col0):
    t = qm.shape[0]
    g = N_KV_GROUPS
    dh = HEAD_DIM
    nqt = seq // tq
    nc = kcm.shape[1]
    nb = seq // SEL_LEN
    gw = HEADS_PER_GROUP * dh
    m_cols = HEADS_PER_GROUP * tq
    qal, cpos, wsel = _alibi_tables(seq, nc, tq)
    n_wt = (WINDOW + tq) // nkw
    sel_chunk = max(WORD_BITS, nb // 4)
    per_tile = nk // SEL_LEN
    stage_rows = -(-nb // per_tile) * per_tile * SEL_LEN
    tps = NSA_TILES_PER_STEP if nqt % NSA_TILES_PER_STEP == 0 else 1
    n_steps = nqt // tps
    rq = tps * tq
    row = lambda b, gg, i: b * n_steps + i
    slab = lambda base: pl.BlockSpec((1, seq, dh), lambda b, gg, i: (base + gg, b, 0))
    cmp_slab = pl.BlockSpec((1, nc, dh), lambda b, gg, i: (b * g + gg, 0, 0))
    const = lambda a: pl.BlockSpec(a.shape, lambda b, gg, i: (0,) * a.ndim)
    return pl.pallas_call(
        functools.partial(_nsa_kernel, tq=tq, tiles_per_step=tps, nk=nk, nkw=nkw, seq=seq, sel_chunk=sel_chunk),
        grid=(batch, g, n_steps),
        in_specs=[
            pl.BlockSpec((rq, gw), lambda b, gg, i: (row(b, gg, i), gg)),
            cmp_slab, cmp_slab,
            slab(0), slab(g), slab(2 * g), slab(3 * g),
            pl.BlockSpec((rq, LANES), lambda b, gg, i: (row(b, gg, i), gg)),
            pl.BlockSpec((rq, gw), lambda b, gg, i: (row(b, gg, i), gm_col0 + gg)),
            pl.BlockSpec((1, dh, m_cols), lambda b, gg, i: (gg, 0, 0)),
            const(cpos), const(wsel),
        ],
        out_specs=pl.BlockSpec((rq, gw), lambda b, gg, i: (row(b, gg, i), gg)),
        out_shape=jax.ShapeDtypeStruct((t, g * gw), BF16),
        scratch_shapes=[
            pltpu.VMEM((2 * dh + nb, m_cols), BF16),
            pltpu.VMEM((seq, 2 * dh + nb), BF16),
            pltpu.VMEM((seq, 2 * dh), BF16),
            pltpu.VMEM((nc, 2 * dh), BF16),
            pltpu.VMEM((stage_rows, 2 * dh + nb), BF16),
            pltpu.VMEM((stage_rows, 2 * dh), BF16),
            pltpu.VMEM((n_wt * nkw, m_cols), F32),
            pltpu.VMEM((nk, m_cols), F32),
            pltpu.VMEM((nk, m_cols), F32),
            pltpu.VMEM((SUBLANES, m_cols), F32),
            pltpu.VMEM((dh, m_cols), F32),
            pltpu.SMEM((nb // WORD_BITS,), jnp.int32),
        ],
        compiler_params=_params(("arbitrary", "arbitrary", "arbitrary")),
        name="nsa",
    )(qm, kcm, vcm, kvm, kvm, kvm, kvm, gnm, fm, qal, cpos, wsel)


def _ffn_kernel(x_ref, yr_ref, ya_ref, wo_ref, gmix_ref, gpre_ref, wg_ref, wu_ref, wd_ref, gpost_ref,
                p_ref, wpg_ref, bpg_ref, wpp_ref, o_ref, x1_ref, h_ref, acc_ref):
    j = pl.program_id(1)

    @pl.when(j == 0)
    def _():
        y = (yr_ref[...].astype(F32) + ya_ref[...].astype(F32)).astype(BF16)
        x1 = x_ref[...] + _rmsnorm(_dot(y, wo_ref[...]), gmix_ref[...])
        x1_ref[...] = x1
        h_ref[...] = _rmsnorm(x1, gpre_ref[...]).astype(BF16)
        acc_ref[...] = jnp.zeros(acc_ref.shape, F32)

    h = h_ref[...]
    gate = _dot(h, wg_ref[...])
    up = _dot(h, wu_ref[...])
    act = (gate * _sigmoid(gate) * up).astype(BF16)
    acc_ref[...] += _dot(act, wd_ref[...])

    @pl.when(j == pl.num_programs(1) - 1)
    def _():
        x2 = x1_ref[...] + _rmsnorm(acc_ref[...], gpost_ref[...])
        gate_p = _sigmoid(_dot(x2.astype(BF16), wpg_ref[...]) + bpg_ref[...])
        o_ref[...] = x2 + gate_p * _dot(p_ref[...].astype(BF16), wpp_ref[...])


def _ffn(x, yr, ya, wo, gmix, gpre, wgu, wd, gpost, p, wpg, bpg, wpp, tm, tf):
    t, d = x.shape
    dff = wd.shape[0]
    nf = dff // tf
    dp = p.shape[1]
    rows = pl.BlockSpec((tm, d), lambda i, j: (i, 0))
    vec = pl.BlockSpec((1, d), lambda i, j: (0, 0))
    square = pl.BlockSpec((d, d), lambda i, j: (0, 0))
    return pl.pallas_call(
        _ffn_kernel,
        grid=(t // tm, nf),
        in_specs=[
            rows, rows, rows, square, vec, vec,
            pl.BlockSpec((d, tf), lambda i, j: (0, j)),
            pl.BlockSpec((d, tf), lambda i, j: (0, nf + j)),
            pl.BlockSpec((tf, d), lambda i, j: (j, 0)),
            vec,
            pl.BlockSpec((tm, dp), lambda i, j: (i, 0)),
            square,
            vec,
            pl.BlockSpec((dp, d), lambda i, j: (0, 0)),
        ],
        out_specs=rows,
        out_shape=jax.ShapeDtypeStruct((t, d), F32),
        scratch_shapes=[pltpu.VMEM((tm, d), F32), pltpu.VMEM((tm, d), BF16), pltpu.VMEM((tm, d), F32)],
        compiler_params=_params(("parallel", "arbitrary")),
        name="out_ffn_ple",
    )(x, yr, ya, wo, gmix, gpre, wgu, wgu, wd, gpost, p, wpg, bpg, wpp)


def _block_diag_chunks(w):
    n, bs, _ = w.shape
    per = LANES // bs
    w = w.reshape(n // per, per, bs, bs)
    eye = jnp.eye(per, dtype=w.dtype)
    return jnp.einsum('cpij,pq->cpiqj', w, eye).reshape(n // per, LANES, LANES)


def _layer(x, p, norm_mix_pre, norm_mix_post, w_in, conv_w, conv_b, lru_wa, lru_ba, lru_wx, lru_bx,
           lru_lambda, cmp_pos_k, cmp_pos_v, cmp_k_w1, cmp_k_w2, cmp_v_w1, cmp_v_w2, w_out,
           norm_ffn_pre, norm_ffn_post, ffn_w_gate_up, ffn_w_down, ple_w_proj, ple_w_gate, ple_b_gate,
           batch, seq):
    t, d = x.shape
    d_attn = N_HEADS * HEAD_DIM
    d_kv = N_KV_GROUPS * HEAD_DIM
    row2 = lambda v: v.reshape(1, -1)

    o_q = 2 * d
    o_kv = o_q + d_attn
    o_gn = o_kv + 6 * d_kv
    o_gm = o_gn + 3 * N_HEADS
    w_f = jnp.concatenate([w_in[:, 0:o_q], w_in[:, o_gm:o_gm + 2 * d]], axis=1).astype(BF16)
    w_q = w_in[:, o_q:o_kv].astype(BF16)
    w_kv = w_in[:, o_kv:o_gn].astype(BF16)
    per_g = 3 * HEADS_PER_GROUP
    w_gn = w_in[:, o_gn:o_gm].reshape(d, N_KV_GROUPS, per_g)
    w_gn = jnp.pad(w_gn, ((0, 0), (0, 0), (0, LANES - per_g))).reshape(d, N_KV_GROUPS * LANES).astype(BF16)

    g_pre = row2(norm_mix_pre)
    tm = min(512, t)
    fm = _norm_matmul(x, g_pre, w_f, BF16, tm, w_f.shape[1])
    qm, gnm, kv_chunks, kvm = _norm_qkv(x, g_pre, jnp.concatenate([w_q, w_gn, w_kv], axis=1), d_attn,
                                        N_KV_GROUPS * LANES, 2 * N_KV_GROUPS, tm)

    yr = _rnn(fm, conv_w, row2(conv_b), _block_diag_chunks(lru_wa).astype(BF16),
              _block_diag_chunks(lru_wx).astype(BF16), row2(lru_ba), row2(lru_bx), row2(lru_lambda),
              batch, seq, min(512, seq))

    nchunk = seq // CMP_STRIDE
    kcm, vcm = _compress(kv_chunks, cmp_k_w1.astype(BF16), cmp_k_w2.astype(BF16), cmp_pos_k.reshape(1, -1),
                         cmp_v_w1.astype(BF16), cmp_v_w2.astype(BF16), cmp_pos_v.reshape(1, -1), batch, nchunk)

    tq = min(256, seq)
    ya = _nsa(qm, kvm, kcm, vcm, gnm, fm, batch, seq, tq, min(NSA_KEY_TILE, seq), min(256, seq),
              gm_col0=(3 * d) // (HEADS_PER_GROUP * HEAD_DIM))

    dff = ffn_w_down.shape[0]
    tf = dff
    return _ffn(x, yr, ya, w_out.astype(BF16), row2(norm_mix_post), row2(norm_ffn_pre),
                ffn_w_gate_up.astype(BF16), ffn_w_down.astype(BF16), row2(norm_ffn_post),
                p, ple_w_gate.astype(BF16), row2(ple_b_gate), ple_w_proj.astype(BF16), tm, tf)


def kernel(x, p, norm_mix_pre, norm_mix_post, w_in, conv_w, conv_b, lru_wa, lru_ba, lru_wx, lru_bx, lru_lambda, cmp_pos_k, cmp_pos_v, cmp_k_w1, cmp_k_w2, cmp_v_w1, cmp_v_w2, w_out, norm_ffn_pre, norm_ffn_post, ffn_w_gate_up, ffn_w_down, ple_w_proj, ple_w_gate, ple_b_gate):
    batch, seq, d = x.shape
    depth = w_in.shape[0]
    xf = x.reshape(batch * seq, d)
    for i in range(depth):
        xf = _layer(xf, p[i].reshape(batch * seq, -1), norm_mix_pre[i], norm_mix_post[i], w_in[i], conv_w[i],
                    conv_b[i], lru_wa[i], lru_ba[i], lru_wx[i], lru_bx[i], lru_lambda[i], cmp_pos_k[i],
                    cmp_pos_v[i], cmp_k_w1[i], cmp_k_w2[i], cmp_v_w1[i], cmp_v_w2[i], w_out[i],
                    norm_ffn_pre[i], norm_ffn_post[i], ffn_w_gate_up[i], ffn_w_down[i], ple_w_proj[i],
                    ple_w_gate[i], ple_b_gate[i], batch, seq)
    return xf.reshape(batch, seq, d)
```

```python
import functools

import numpy as np
import jax
import jax.numpy as jnp
from jax import lax
from jax.experimental import pallas as pl
from jax.experimental.pallas import tpu as pltpu

CONV_WIDTH = 4
LRU_C = 8.0
N_HEADS = 16
HEAD_DIM = 64
N_KV_GROUPS = 4
HEADS_PER_GROUP = N_HEADS // N_KV_GROUPS
CMP_LEN = 32
CMP_STRIDE = 16
SEL_LEN = 64
SEL_TOPK = 16
WINDOW = 512
ALIBI_MAX_BIAS = 8.0
NORM_EPS = 1e-6

LANES = 128
SUBLANES = 8
VMEM_LIMIT_BYTES = 56 * 1024 * 1024

MASK_NEG = -1e30
LOG2_E = 1.4426950408889634
N_ALIBI_COLS = 9
PAD_COL = HEAD_DIM + N_ALIBI_COLS
WORD_BITS = 16
WORD_SHIFT = 4
SEL_SHIFT = 6
N_FORCED = 3
NEAR_BLOCKS = 24
FILL_ROWS = 512
NSA_KEY_TILE = 11 * SEL_LEN
SCAN_UNROLL = 4
NSA_TILES_PER_STEP = 4
OVERLAP_CHUNKS = 4
BF16 = jnp.bfloat16
F32 = jnp.float32


def _dot(a, b):
    return jnp.dot(a, b, preferred_element_type=F32)


def _dot_nt(a, b):
    return lax.dot_general(a, b, (((1,), (1,)), ((), ())), preferred_element_type=F32)


def _sigmoid(x):
    return 0.5 * jnp.tanh(0.5 * x) + 0.5


def _gelu_tanh(x):
    c = np.float32(np.sqrt(2.0 / np.pi))
    half = 0.5 * x
    return half + half * jnp.tanh(x * (c + (c * 0.044715) * (x * x)))


def _rmsnorm(x, g):
    ms = jnp.mean(x * x, axis=-1, keepdims=True)
    return x * lax.rsqrt(ms + NORM_EPS) * g


def _params(sem):
    return pltpu.CompilerParams(dimension_semantics=sem, vmem_limit_bytes=VMEM_LIMIT_BYTES)


def _norm_matmul_kernel(x_ref, g_ref, w_ref, o_ref, h_ref):
    @pl.when(pl.program_id(1) == 0)
    def _():
        h_ref[...] = _rmsnorm(x_ref[...], g_ref[...]).astype(BF16)

    o_ref[...] = _dot(h_ref[...], w_ref[...]).astype(o_ref.dtype)


def _norm_matmul(x, g, w, out_dtype, tm, tn):
    t, k = x.shape
    n = w.shape[1]
    return pl.pallas_call(
        _norm_matmul_kernel,
        grid=(t // tm, n // tn),
        in_specs=[
            pl.BlockSpec((tm, k), lambda i, j: (i, 0)),
            pl.BlockSpec((1, k), lambda i, j: (0, 0)),
            pl.BlockSpec((k, tn), lambda i, j: (0, j)),
        ],
        out_specs=pl.BlockSpec((tm, tn), lambda i, j: (i, j)),
        out_shape=jax.ShapeDtypeStruct((t, n), out_dtype),
        scratch_shapes=[pltpu.VMEM((tm, k), BF16)],
        compiler_params=_params(("parallel", "arbitrary")),
        name="norm_matmul",
    )(x, g, w)


def _norm_qkv_kernel(x_ref, g_ref, w_ref, q_ref, gn_ref, cv_ref, kv_ref, slab_scr):
    h = _rmsnorm(x_ref[...], g_ref[...]).astype(BF16)
    z = _dot(h, w_ref[...])
    nq = q_ref.shape[1]
    ngn = gn_ref.shape[1]
    q_ref[...] = z[:, 0:nq].astype(q_ref.dtype)
    gn_ref[...] = z[:, nq:nq + ngn]
    n_cv = cv_ref.shape[0]
    rows = cv_ref.shape[1]
    for c in range(n_cv):
        lo = nq + ngn + c * HEAD_DIM
        slab_scr[...] = z[:, lo:lo + HEAD_DIM]
        for l in range(CMP_STRIDE):
            cv_ref[c, :, l * HEAD_DIM:(l + 1) * HEAD_DIM] = (
                slab_scr[pl.ds(l, rows, stride=CMP_STRIDE), :].astype(cv_ref.dtype))
    for c in range(kv_ref.shape[0]):
        lo = nq + ngn + (n_cv + c) * HEAD_DIM
        kv_ref[c] = z[:, lo:lo + HEAD_DIM].astype(kv_ref.dtype)


def _norm_qkv(x, g, w, nq, ngn, n_cv, tm):
    t, k = x.shape
    n = w.shape[1]
    n_kv = (n - nq - ngn) // HEAD_DIM - n_cv
    return pl.pallas_call(
        _norm_qkv_kernel,
        grid=(t // tm,),
        in_specs=[
            pl.BlockSpec((tm, k), lambda i: (i, 0)),
            pl.BlockSpec((1, k), lambda i: (0, 0)),
            pl.BlockSpec((k, n), lambda i: (0, 0)),
        ],
        out_specs=[
            pl.BlockSpec((tm, nq), lambda i: (i, 0)),
            pl.BlockSpec((tm, ngn), lambda i: (i, 0)),
            pl.BlockSpec((n_cv, tm // CMP_STRIDE, CMP_STRIDE * HEAD_DIM), lambda i: (0, i, 0)),
            pl.BlockSpec((n_kv, tm, HEAD_DIM), lambda i: (0, i, 0)),
        ],
        out_shape=[
            jax.ShapeDtypeStruct((t, nq), BF16),
            jax.ShapeDtypeStruct((t, ngn), F32),
            jax.ShapeDtypeStruct((n_cv, t // CMP_STRIDE, CMP_STRIDE * HEAD_DIM), BF16),
            jax.ShapeDtypeStruct((n_kv, t, HEAD_DIM), BF16),
        ],
        scratch_shapes=[pltpu.VMEM((tm, HEAD_DIM), F32)],
        compiler_params=_params(("parallel",)),
        name="norm_qkv",
    )(x, g, w)


def _rnn_kernel(xr_ref, gr_ref, gm_ref, cw_ref, cb_ref, wa_ref, wx_ref, ba_ref, bx_ref, lam_ref,
                o_ref, xbuf, a_scr, b_scr, h_scr, *, ts):
    s_idx = pl.program_id(1)
    d = xr_ref.shape[1]
    halo = SUBLANES

    @pl.when(s_idx == 0)
    def _():
        xbuf[0:halo, :] = jnp.zeros((halo, d), F32)
        h_scr[...] = jnp.zeros(h_scr.shape, F32)

    xbuf[halo:halo + ts, :] = xr_ref[...].astype(F32)
    xc = cb_ref[...] + xbuf[halo:halo + ts, :] * cw_ref[CONV_WIDTH - 1:CONV_WIDTH, :]
    for k in range(1, CONV_WIDTH):
        xc = xc + xbuf[halo - k:halo - k + ts, :] * cw_ref[CONV_WIDTH - 1 - k:CONV_WIDTH - k, :]
    xbuf[0:halo, :] = xbuf[ts:ts + halo, :]

    xcb = xc.astype(BF16)
    n_chunks = d // LANES
    neg_sp = -LRU_C * (jnp.maximum(-lam_ref[...], 0.0) + jnp.log(1.0 + jnp.exp(-jnp.abs(lam_ref[...]))))
    for c in range(n_chunks):
        sl = slice(c * LANES, (c + 1) * LANES)
        xk = xcb[:, sl]
        r = _sigmoid(_dot(xk, wa_ref[c]) + ba_ref[:, sl])
        i = _sigmoid(_dot(xk, wx_ref[c]) + bx_ref[:, sl])
        log_a = r * neg_sp[:, sl]
        a = jnp.exp(log_a)
        a_scr[:, sl] = a
        u = 1.0 - a * a
        root = jnp.where(u > 0.0, u * lax.rsqrt(u), 0.0)
        b_scr[:, sl] = root * (i * xc[:, sl])

    row_id = lax.broadcasted_iota(jnp.int32, (SUBLANES, d), 0)

    def group(gi, h):
        base = pl.multiple_of(gi * SUBLANES, SUBLANES)
        a = a_scr[pl.ds(base, SUBLANES), :]
        b = b_scr[pl.ds(base, SUBLANES), :]
        s = 1
        while s < SUBLANES:
            reach = row_id >= s
            b = jnp.where(reach, a * pltpu.roll(b, s, 0) + b, b)
            a = jnp.where(reach, a * pltpu.roll(a, s, 0), a)
            s *= 2
        hs = a * h + b
        a_scr[pl.ds(base, SUBLANES), :] = hs
        return jnp.broadcast_to(hs[SUBLANES - 1:SUBLANES, :], (SUBLANES, d))

    h_scr[...] = lax.fori_loop(0, ts // SUBLANES, group, h_scr[...], unroll=SCAN_UNROLL)
    o_ref[...] = (_sigmoid(gm_ref[...].astype(F32)) * a_scr[...]
                  * _gelu_tanh(gr_ref[...].astype(F32))).astype(o_ref.dtype)


def _rnn(f, conv_w, conv_b, wa, wx, ba, bx, lam, batch, seq, ts):
    t = f.shape[0]
    d = conv_w.shape[1]
    ns = seq // ts
    nd = d // d
    del nd
    row = lambda b, s: b * ns + s
    vec = lambda r: pl.BlockSpec((r, d), lambda b, s: (0, 0))
    return pl.pallas_call(
        functools.partial(_rnn_kernel, ts=ts),
        grid=(batch, ns),
        in_specs=[
            pl.BlockSpec((ts, d), lambda b, s: (row(b, s), 0)),
            pl.BlockSpec((ts, d), lambda b, s: (row(b, s), 1)),
            pl.BlockSpec((ts, d), lambda b, s: (row(b, s), 2)),
            vec(CONV_WIDTH), vec(1),
            pl.BlockSpec(wa.shape, lambda b, s: (0, 0, 0)),
            pl.BlockSpec(wx.shape, lambda b, s: (0, 0, 0)),
            vec(1), vec(1), vec(1),
        ],
        out_specs=pl.BlockSpec((ts, d), lambda b, s: (row(b, s), 0)),
        out_shape=jax.ShapeDtypeStruct((t, d), BF16),
        scratch_shapes=[
            pltpu.VMEM((ts + SUBLANES, d), F32),
            pltpu.VMEM((ts, d), F32),
            pltpu.VMEM((ts, d), F32),
            pltpu.VMEM((SUBLANES, d), F32),
        ],
        compiler_params=_params(("parallel", "arbitrary")),
        name="rnn_mixer",
    )(f, f, f, conv_w, conv_b, wa, wx, ba, bx, lam)


def _compress_kernel(ak_ref, av_ref, w1k_ref, w2k_ref, pk_ref, w1v_ref, w2v_ref, pv_ref, ok_ref, ov_ref):
    def one(a_ref, w1_ref, w2_ref, p_ref, o_ref):
        a = a_ref[0]
        half = a.shape[1]
        nchunk = a.shape[0]
        lo = _dot(a, w1_ref[0:half, :])
        hi = _dot(a, w1_ref[half:2 * half, :])
        pb = _dot(jnp.broadcast_to(p_ref[...], (SUBLANES, 2 * half)).astype(BF16), w1_ref[...])[0:1, :]
        h = lo + pltpu.roll(hi, nchunk - 1, 0) + pb
        o_ref[0] = _dot(_gelu_tanh(h).astype(BF16), w2_ref[...]).astype(o_ref.dtype)

    one(ak_ref, w1k_ref, w2k_ref, pk_ref, ok_ref)
    one(av_ref, w1v_ref, w2v_ref, pv_ref, ov_ref)


def _compress(kv_chunks, w1k, w2k, pk, w1v, w2v, pv, batch, nchunk):
    g = N_KV_GROUPS
    width = kv_chunks.shape[2]
    full = lambda a: pl.BlockSpec(a.shape, lambda b, gg: (0,) * a.ndim)
    out = jax.ShapeDtypeStruct((batch * g, nchunk, HEAD_DIM), BF16)
    return pl.pallas_call(
        _compress_kernel,
        grid=(batch, g),
        in_specs=[
            pl.BlockSpec((1, nchunk, width), lambda b, gg: (gg, b, 0)),
            pl.BlockSpec((1, nchunk, width), lambda b, gg: (g + gg, b, 0)),
            full(w1k), full(w2k), full(pk), full(w1v), full(w2v), full(pv),
        ],
        out_specs=[
            pl.BlockSpec((1, nchunk, HEAD_DIM), lambda b, gg: (b * g + gg, 0, 0)),
            pl.BlockSpec((1, nchunk, HEAD_DIM), lambda b, gg: (b * g + gg, 0, 0)),
        ],
        out_shape=[out, out],
        compiler_params=_params(("parallel", "parallel")),
        name="compress",
    )(kv_chunks, kv_chunks, w1k, w2k, pk, w1v, w2v, pv)


ONES_ROWS = 16


def _scores(k, q_rows, s_ref, keep=None, keep_rows=None, shift=None):
    s = _dot(k, q_rows)
    if shift is not None:
        s = s + shift
    if keep is None:
        s_ref[...] = s
        return jnp.max(s, axis=0, keepdims=True)
    n = s.shape[0] if keep_rows is None else keep_rows
    top = s[0:n]
    for mask in keep:
        top = jnp.where(mask, top, MASK_NEG)
    s_ref[0:n, :] = top
    col_max = jnp.max(top, axis=0, keepdims=True)
    if n < s.shape[0]:
        s_ref[n:, :] = s[n:]
        col_max = jnp.maximum(col_max, jnp.max(s[n:], axis=0, keepdims=True))
    return col_max


def _accumulate(s_ref, col_max, v_aug_t, state):
    m_old, acc_old = state
    m_new = jnp.maximum(m_old, col_max)
    alpha = jnp.exp2(m_old - m_new)
    e = jnp.exp2(s_ref[...] - m_new).astype(BF16)
    return m_new, alpha * acc_old + _dot(v_aug_t, e)


def _nsa_kernel(q_ref, kc_ref, vc_ref, ks_ref, vs_ref, kw_ref, vw_ref, gn_ref, gm_ref,
                qal_ref, cpos_ref, wsel_ref, o_ref, *scratch, tq, tiles_per_step, **static):
    def tile(j, carry):
        rows = pl.ds(pl.multiple_of(j * tq, tq), tq)
        _nsa_tile(pl.program_id(2) * tiles_per_step + j, q_ref.at[rows, :], kc_ref, vc_ref, ks_ref, vs_ref,
                  kw_ref, vw_ref, gn_ref.at[rows, :], gm_ref.at[rows, :], qal_ref, cpos_ref, wsel_ref,
                  o_ref.at[rows, :], *scratch, tq=tq, **static)
        return carry

    lax.fori_loop(0, tiles_per_step, tile, 0)


def _nsa_tile(qt, q_ref, kc_ref, vc_ref, ks_ref, vs_ref, kw_ref, vw_ref, gn_ref, gm_ref,
              qal_ref, cpos_ref, wsel_ref,
              o_ref, q_t, ksaug, kwaug, kcaug, kstage, vstage, s_win, s_a, s_b, win_max, o_cmp_scr, flags,
              *, tq, nk, nkw, seq, sel_chunk):
    b = pl.program_id(0)
    g = pl.program_id(1)
    r_heads = HEADS_PER_GROUP
    dh = HEAD_DIM
    m_cols = r_heads * tq
    nb = seq // SEL_LEN
    nc = kc_ref.shape[1]
    a_w = 2 * dh
    k_w = a_w + nb

    @pl.when((b == 0) & (g == 0) & (qt == 0))
    def _():
        rows = min(FILL_ROWS, seq)

        def fill(c, carry):
            off = pl.multiple_of(c * rows, rows)
            pos = off + lax.broadcasted_iota(jnp.int32, (rows, k_w), 0)
            lane = lax.broadcasted_iota(jnp.int32, (rows, k_w), 1)
            blk_of = jnp.right_shift(pos, SEL_SHIFT)
            cols = jnp.where((lane >= dh) & (lane < dh + 3), blk_of * SEL_LEN,
                             jnp.where((lane >= dh + 3) & (lane < dh + 6), pos & (SEL_LEN - 1),
                                       jnp.where(lane == a_w + blk_of, 1, 0)))
            cols = cols.astype(F32).astype(BF16)
            ksaug[pl.ds(off, rows), :] = cols
            kwaug[pl.ds(off, rows), :] = cols[:, 0:a_w]
            return carry

        lax.fori_loop(0, seq // rows, fill, 0)
        kcaug[...] = cpos_ref[...]
        vstage[...] = jnp.ones(vstage.shape, BF16)

    @pl.when(qt == 0)
    def _():
        ksaug[:, 0:dh] = ks_ref[0]
        kwaug[:, 0:dh] = kw_ref[0]
        kcaug[:, 0:dh] = kc_ref[0]
        q_t[dh:a_w, :] = qal_ref[0]

    v_rows = dh + ONES_ROWS
    ri = lax.broadcasted_iota(jnp.int32, (v_rows, 2 * dh), 0)
    ci = lax.broadcasted_iota(jnp.int32, (v_rows, 2 * dh), 1)
    pick = (((ri < dh) & (ri == ci)) | ((ri >= dh) & (ci == dh))).astype(F32).astype(BF16)
    transposed = lambda v: _dot_nt(pick[0:dh, 0:dh], v).astype(BF16)
    staged_aug_t = lambda v: _dot_nt(pick, v).astype(BF16)
    ones_rows = jnp.ones((ONES_ROWS, nkw), BF16)
    window_aug_t = lambda v: jnp.concatenate([transposed(v), ones_rows], axis=0)
    init = (jnp.full((1, m_cols), MASK_NEG, F32), jnp.zeros((v_rows, m_cols), F32))

    t0 = qt * tq
    col = lax.broadcasted_iota(jnp.int32, (1, m_cols), 1)
    trow = t0 + (col & (tq - 1))

    scale = np.float32(HEAD_DIM ** -0.5 * LOG2_E)
    qf = jnp.transpose(q_ref[...].astype(F32) * scale)
    for r in range(r_heads):
        q_t[0:dh, r * tq:(r + 1) * tq] = qf[r * dh:(r + 1) * dh, :].astype(BF16)

    n_wt = (WINDOW + tq) // nkw
    win_s = [s_win.at[i * nkw:(i + 1) * nkw, :] for i in range(n_wt)]
    win_off = [pl.multiple_of(jnp.maximum(t0 + tq - (i + 1) * nkw, 0), nkw) for i in range(n_wt)]

    def window_scores():
        for i in range(n_wt):
            start = t0 + tq - (i + 1) * nkw
            kpos_w = start + lax.broadcasted_iota(jnp.int32, (nkw, 1), 0)
            keep = []
            if i * nkw < tq - 1:
                keep.append(kpos_w <= trow)
            if (i + 1) * nkw > WINDOW:
                keep.append(kpos_w > trow - WINDOW)
            shift = jnp.where(start >= 0, 0.0, MASK_NEG) if (i + 1) * nkw > tq else None
            win_max[i:i + 1, :] = _scores(kwaug[pl.ds(win_off[i], nkw), :], q_t[0:a_w, :], win_s[i],
                                          keep=keep or None, shift=shift)

    def compressed_and_select(rows_c, rows_b):
        free = max(rows_c - cmp_chunk - tq // CMP_STRIDE, 0)
        s = _dot(kcaug[0:rows_c, :], q_t[0:a_w, :])
        cend = (free + lax.broadcasted_iota(jnp.int32, (rows_c - free, 1), 0)) * CMP_STRIDE + (CMP_LEN - 1)
        tail = jnp.where(cend <= trow, s[free:], MASK_NEG)
        s = jnp.concatenate([s[0:free], tail], axis=0) if free else tail
        m = jnp.max(s, axis=0, keepdims=True)
        e = jnp.exp2(s - m)
        has_key = (trow >= CMP_LEN - 1).astype(F32)
        p = e * (has_key / jnp.sum(e, axis=0, keepdims=True))
        o_cmp_scr[...] = _dot(transposed(vc_ref[0, 0:rows_c, :]), p.astype(BF16))
        imp = p[:, 0:tq]
        for r in range(1, r_heads):
            imp = imp + p[:, r * tq:(r + 1) * tq]

        hi = imp.astype(BF16)
        lo = (imp - hi.astype(F32)).astype(BF16)
        wsel = wsel_ref[0:rows_b, 0:rows_c]
        imp_t = _dot(wsel, hi) + _dot(wsel, lo)

        window_scores()

        blk = lax.broadcasted_iota(jnp.int32, (rows_b, tq), 0).astype(F32)
        tq_l = t0 + lax.broadcasted_iota(jnp.int32, (rows_b, tq), 1)
        cur = jnp.right_shift(tq_l, SEL_SHIFT).astype(F32)
        valid = blk <= cur
        sel = jnp.where(blk == 0.0, 1.0, jnp.where(blk == cur, 1.0, jnp.where(blk == cur - 1.0, 1.0, 0.0)))
        score = jnp.where(valid, jnp.where(sel > 0.0, -1.0, imp_t), -1.0)
        for _ in range(min(SEL_TOPK, nb) - N_FORCED):
            mx = jnp.max(score, axis=0, keepdims=True)
            idx = jnp.min(jnp.where(score == mx, blk, float(nb)), axis=0, keepdims=True)
            hit = blk == idx
            sel = jnp.where(hit, 1.0, sel)
            score = jnp.where(hit, -2.0, score)
        selv = jnp.where(valid, sel, 0.0) > 0.0
        selneg_t = jnp.where(selv, 0.0, MASK_NEG).astype(BF16)
        unseen = jnp.full((nb - rows_b, tq), MASK_NEG, BF16)
        for r in range(r_heads):
            q_t[a_w:a_w + rows_b, r * tq:(r + 1) * tq] = selneg_t
            if rows_b < nb:
                q_t[a_w + rows_b:, r * tq:(r + 1) * tq] = unseen

        used = jnp.max(jnp.where(selv, 1.0, 0.0), axis=1, keepdims=True)
        bit_id = lax.broadcasted_iota(jnp.int32, (rows_b, 1), 0) & (WORD_BITS - 1)
        weighted = used * jnp.left_shift(1, bit_id).astype(F32)
        for i in range(nb // WORD_BITS):
            if (i + 1) * WORD_BITS <= rows_b:
                flags[i] = jnp.sum(weighted[i * WORD_BITS:(i + 1) * WORD_BITS, :]).astype(jnp.int32)
            else:
                flags[i] = 0

    n_var = nb // sel_chunk
    cmp_chunk = sel_chunk * (SEL_LEN // CMP_STRIDE)
    seen_c = (t0 + tq - CMP_LEN) // CMP_STRIDE + 1
    seen_b = (t0 + tq) // SEL_LEN
    variant = jnp.maximum((seen_c + cmp_chunk - 1) // cmp_chunk, (seen_b + sel_chunk - 1) // sel_chunk)
    for v in range(1, n_var + 1):
        pl.when(variant == v)(functools.partial(compressed_and_select, min(v * cmp_chunk, nc), v * sel_chunk))
    o_cmp = o_cmp_scr[...]

    n_own = tq // SEL_LEN
    off_q = pl.multiple_of(t0, tq)
    kstage[0:tq, :] = ksaug[pl.ds(off_q, tq), :]
    vstage[0:tq, 0:dh] = vs_ref[0, pl.ds(off_q, tq), :]

    def stage_block(j, cnt):
        src = pl.multiple_of(j * SEL_LEN, SEL_LEN)
        dst = pl.multiple_of(cnt * SEL_LEN, SEL_LEN)
        kstage[pl.ds(dst, SEL_LEN), :] = ksaug[pl.ds(src, SEL_LEN), :]
        vstage[pl.ds(dst, SEL_LEN), 0:dh] = vs_ref[0, pl.ds(src, SEL_LEN), :]

    def gather(j, cnt):
        bit = jnp.right_shift(flags[jnp.right_shift(j, WORD_SHIFT)], j & (WORD_BITS - 1)) & 1
        pl.when(bit == 1)(functools.partial(stage_block, j, cnt))
        return cnt + bit

    n_past = jnp.right_shift(t0, SEL_SHIFT)
    near_lo = jnp.maximum(n_past - NEAR_BLOCKS, 0)
    first = jnp.where(near_lo > 0, flags[0] & 1, 0)
    pl.when(first == 1)(functools.partial(stage_block, 0, n_own))

    def far_word(w, cnt):
        lo = jnp.maximum(w * WORD_BITS, 1)
        hi = jnp.minimum((w + 1) * WORD_BITS, near_lo)
        rest = jnp.where(w == 0, flags[w] & -2, flags[w])
        return lax.cond(rest != 0, lambda c: lax.fori_loop(lo, hi, gather, c), lambda c: c, cnt)

    n_blocks = lax.fori_loop(0, jnp.right_shift(near_lo + WORD_BITS - 1, WORD_SHIFT), far_word, n_own + first)
    n_blocks = lax.fori_loop(near_lo, n_past, gather, n_blocks)
    per_tile = nk // SEL_LEN
    n_tiles = (n_blocks + per_tile - 1) // per_tile
    pad_block = jnp.where(lax.broadcasted_iota(jnp.int32, (SEL_LEN, k_w), 1) == PAD_COL, 1.0, 0.0).astype(BF16)

    def pad(j, carry):
        dst = pl.multiple_of(j * SEL_LEN, SEL_LEN)
        kstage[pl.ds(dst, SEL_LEN), :] = pad_block
        return carry

    lax.fori_loop(n_blocks, n_tiles * per_tile, pad, 0)

    own_pos = t0 + lax.broadcasted_iota(jnp.int32, (tq, 1), 0)
    s_first = _dot(kstage[0:nk, :], q_t[...])
    own = jnp.where(own_pos <= trow, s_first[0:tq], MASK_NEG)
    s_a[0:tq, :] = own
    s_a[tq:, :] = s_first[tq:]
    step = (nk - tq) // max(n_wt - 1, 1) // SUBLANES * SUBLANES
    bounds = [0, tq] + [tq + step * i for i in range(1, n_wt - 1)] + [nk]
    m_w = functools.reduce(jnp.maximum, [win_max[i:i + 1, :] for i in range(n_wt)])
    parts, weights = [], []
    for i in range(n_wt):
        rows = own if i == 0 else s_first[bounds[i]:bounds[i + 1]]
        parts.append(jnp.max(rows, axis=0, keepdims=True))
        weights.append(jnp.exp2(win_s[i][...] - (m_w + parts[i] * 0.0)).astype(BF16))
    max_a = functools.reduce(jnp.maximum, parts)
    v_win = jnp.concatenate([window_aug_t(vw_ref[0, pl.ds(win_off[i], nkw), :]) for i in range(n_wt)], axis=1)
    acc_w = _dot(v_win, jnp.concatenate(weights, axis=0))
    o_win = acc_w[0:dh] / acc_w[dh:dh + 1]

    def tile_scores(i, s_ref):
        return _scores(kstage[pl.ds(pl.multiple_of(i * nk, nk), nk), :], q_t[...], s_ref)

    def tile_accumulate(i, s_ref, col_max, state):
        return _accumulate(s_ref, col_max, staged_aug_t(vstage[pl.ds(pl.multiple_of(i * nk, nk), nk), :]), state)

    def scores_and_accumulate(i_next, s_next_ref, i_cur, s_cur_ref, max_cur, state):
        m_old, acc_old = state
        m_new = jnp.maximum(m_old, max_cur)
        alpha = jnp.exp2(m_old - m_new)
        s_next = _dot(kstage[pl.ds(pl.multiple_of(i_next * nk, nk), nk), :], q_t[...])
        s_next_ref[...] = s_next
        rows = nk // OVERLAP_CHUNKS
        partial, weights = [], []
        for c in range(OVERLAP_CHUNKS):
            rs = slice(c * rows, (c + 1) * rows)
            partial.append(jnp.max(s_next[rs], axis=0, keepdims=True))
            m_c = m_new + partial[c] * 0.0
            weights.append(jnp.exp2(s_cur_ref[rs, :] - m_c).astype(BF16))
        v_t = staged_aug_t(vstage[pl.ds(pl.multiple_of(i_cur * nk, nk), nk), :])
        acc = alpha * acc_old + _dot(v_t, jnp.concatenate(weights, axis=0))
        return (m_new, acc), functools.reduce(jnp.maximum, partial)

    def slc_pair(j, carry):
        state, max_a = carry
        i = 2 * j
        state, max_b = scores_and_accumulate(i + 1, s_b, i, s_a, max_a, state)
        state, max_a = scores_and_accumulate(i + 2, s_a, i + 1, s_b, max_b, state)
        return state, max_a

    n_pairs = (n_tiles - 1) // 2
    state, max_a = lax.fori_loop(0, n_pairs, slc_pair, (init, max_a))
    i_a = 2 * n_pairs

    def two_left(state):
        state, max_b = scores_and_accumulate(i_a + 1, s_b, i_a, s_a, max_a, state)
        return tile_accumulate(i_a + 1, s_b, max_b, state)

    def one_left(state):
        return tile_accumulate(i_a, s_a, max_a, state)

    _, acc = lax.cond(n_tiles - i_a == 2, two_left, one_left, state)
    o_slc = acc[0:dh] / acc[dh:dh + 1]

    gates = jnp.transpose(_sigmoid(gn_ref[...]))
    heads = []
    for r in range(r_heads):
        cs = slice(r * tq, (r + 1) * tq)
        heads.append(gates[3 * r:3 * r + 1, :] * o_cmp[:, cs] + gates[3 * r + 1:3 * r + 2, :] * o_slc[:, cs]
                     + gates[3 * r + 2:3 * r + 3, :] * o_win[:, cs])
    o = jnp.transpose(jnp.concatenate(heads, axis=0))
    o_ref[...] = (_sigmoid(gm_ref[...].astype(F32)) * o).astype(o_ref.dtype)


def _alibi_tables(seq, nc, tq):
    import ml_dtypes
    bf = ml_dtypes.bfloat16
    h = np.arange(1, N_HEADS + 1, dtype=np.float32)
    slopes = (np.exp2(-ALIBI_MAX_BIAS * h / N_HEADS) * LOG2_E).astype(np.float32)
    s1 = slopes.astype(bf).astype(np.float32)
    s2 = (slopes - s1).astype(bf).astype(np.float32)
    s3 = (slopes - s1 - s2).astype(bf).astype(np.float32)
    dh = HEAD_DIM
    qal = np.zeros((N_HEADS, dh), np.float32)
    for rep in range(3):
        qal[:, 3 * rep + 0] = s1
        qal[:, 3 * rep + 1] = s2
        qal[:, 3 * rep + 2] = s3
    qal[:, PAD_COL - dh] = MASK_NEG
    qal = qal.reshape(N_KV_GROUPS, HEADS_PER_GROUP, dh).transpose(0, 2, 1)
    qal_p = np.repeat(qal, tq, axis=2)

    nb = seq // SEL_LEN
    c = np.arange(nc)
    cpos = np.zeros((nc, 2 * dh), np.float32)
    cpos[:, dh:dh + 3] = ((c // 16) * 16 * CMP_STRIDE)[:, None]
    cpos[:, dh + 3:dh + 6] = ((c % 16) * CMP_STRIDE)[:, None]
    cpos[:, dh + 6:dh + 9] = CMP_LEN - 1

    r_sel = SEL_LEN // CMP_STRIDE
    r_cmp = CMP_LEN // CMP_STRIDE
    wsel = np.zeros((nb, nc), np.float32)
    for j in range(nb):
        for mm in range(r_sel):
            for nn in range(r_cmp):
                ci = r_sel * j + mm - nn
                if 0 <= ci < nc - 1:
                    wsel[j, ci] += 1.0
    as_bf = lambda a: jnp.asarray(a.astype(bf))
    return as_bf(qal_p), as_bf(cpos), as_bf(wsel)


def _nsa(qm, kvm, kcm, vcm, gnm, fm, batch, seq, tq, nk, nkw, gm_col0):
    t = qm.shape[0]
    g = N_KV_GROUPS
    dh = HEAD_DIM
    nqt = seq // tq
    nc = kcm.shape[1]
    nb = seq // SEL_LEN
    gw = HEADS_PER_GROUP * dh
    m_cols = HEADS_PER_GROUP * tq
    qal, cpos, wsel = _alibi_tables(seq, nc, tq)
    n_wt = (WINDOW + tq) // nkw
    sel_chunk = max(WORD_BITS, nb // 4)
    per_tile = nk // SEL_LEN
    stage_rows = -(-nb // per_tile) * per_tile * SEL_LEN
    tps = NSA_TILES_PER_STEP if nqt % NSA_TILES_PER_STEP == 0 else 1
    n_steps = nqt // tps
    rq = tps * tq
    row = lambda b, gg, i: b * n_steps + i
    slab = lambda base: pl.BlockSpec((1, seq, dh), lambda b, gg, i: (base + gg, b, 0))
    cmp_slab = pl.BlockSpec((1, nc, dh), lambda b, gg, i: (b * g + gg, 0, 0))
    const = lambda a: pl.BlockSpec(a.shape, lambda b, gg, i: (0,) * a.ndim)
    return pl.pallas_call(
        functools.partial(_nsa_kernel, tq=tq, tiles_per_step=tps, nk=nk, nkw=nkw, seq=seq, sel_chunk=sel_chunk),
        grid=(batch, g, n_steps),
        in_specs=[
            pl.BlockSpec((rq, gw), lambda b, gg, i: (row(b, gg, i), gg)),
            cmp_slab, cmp_slab,
            slab(0), slab(g), slab(2 * g), slab(3 * g),
            pl.BlockSpec((rq, LANES), lambda b, gg, i: (row(b, gg, i), gg)),
            pl.BlockSpec((rq, gw), lambda b, gg, i: (row(b, gg, i), gm_col0 + gg)),
            pl.BlockSpec((1, dh, m_cols), lambda b, gg, i: (gg, 0, 0)),
            const(cpos), const(wsel),
        ],
        out_specs=pl.BlockSpec((rq, gw), lambda b, gg, i: (row(b, gg, i), gg)),
        out_shape=jax.ShapeDtypeStruct((t, g * gw), BF16),
        scratch_shapes=[
            pltpu.VMEM((2 * dh + nb, m_cols), BF16),
            pltpu.VMEM((seq, 2 * dh + nb), BF16),
            pltpu.VMEM((seq, 2 * dh), BF16),
            pltpu.VMEM((nc, 2 * dh), BF16),
            pltpu.VMEM((stage_rows, 2 * dh + nb), BF16),
            pltpu.VMEM((stage_rows, 2 * dh), BF16),
            pltpu.VMEM((n_wt * nkw, m_cols), F32),
            pltpu.VMEM((nk, m_cols), F32),
            pltpu.VMEM((nk, m_cols), F32),
            pltpu.VMEM((SUBLANES, m_cols), F32),
            pltpu.VMEM((dh, m_cols), F32),
            pltpu.SMEM((nb // WORD_BITS,), jnp.int32),
        ],
        compiler_params=_params(("arbitrary", "arbitrary", "arbitrary")),
        name="nsa",
    )(qm, kcm, vcm, kvm, kvm, kvm, kvm, gnm, fm, qal, cpos, wsel)


def _ffn_kernel(x_ref, yr_ref, ya_ref, wo_ref, gmix_ref, gpre_ref, wg_ref, wu_ref, wd_ref, gpost_ref,
                p_ref, wpg_ref, bpg_ref, wpp_ref, o_ref, x1_ref, h_ref, acc_ref):
    j = pl.program_id(1)

    @pl.when(j == 0)
    def _():
        y = (yr_ref[...].astype(F32) + ya_ref[...].astype(F32)).astype(BF16)
        x1 = x_ref[...] + _rmsnorm(_dot(y, wo_ref[...]), gmix_ref[...])
        x1_ref[...] = x1
        h_ref[...] = _rmsnorm(x1, gpre_ref[...]).astype(BF16)
        acc_ref[...] = jnp.zeros(acc_ref.shape, F32)

    h = h_ref[...]
    gate = _dot(h, wg_ref[...])
    up = _dot(h, wu_ref[...])
    act = (gate * _sigmoid(gate) * up).astype(BF16)
    acc_ref[...] += _dot(act, wd_ref[...])

    @pl.when(j == pl.num_programs(1) - 1)
    def _():
        x2 = x1_ref[...] + _rmsnorm(acc_ref[...], gpost_ref[...])
        gate_p = _sigmoid(_dot(x2.astype(BF16), wpg_ref[...]) + bpg_ref[...])
        o_ref[...] = x2 + gate_p * _dot(p_ref[...].astype(BF16), wpp_ref[...])


def _ffn(x, yr, ya, wo, gmix, gpre, wgu, wd, gpost, p, wpg, bpg, wpp, tm, tf):
    t, d = x.shape
    dff = wd.shape[0]
    nf = dff // tf
    dp = p.shape[1]
    rows = pl.BlockSpec((tm, d), lambda i, j: (i, 0))
    vec = pl.BlockSpec((1, d), lambda i, j: (0, 0))
    square = pl.BlockSpec((d, d), lambda i, j: (0, 0))
    return pl.pallas_call(
        _ffn_kernel,
        grid=(t // tm, nf),
        in_specs=[
            rows, rows, rows, square, vec, vec,
            pl.BlockSpec((d, tf), lambda i, j: (0, j)),
            pl.BlockSpec((d, tf), lambda i, j: (0, nf + j)),
            pl.BlockSpec((tf, d), lambda i, j: (j, 0)),
            vec,
            pl.BlockSpec((tm, dp), lambda i, j: (i, 0)),
            square,
            vec,
            pl.BlockSpec((dp, d), lambda i, j: (0, 0)),
        ],
        out_specs=rows,
        out_shape=jax.ShapeDtypeStruct((t, d), F32),
        scratch_shapes=[pltpu.VMEM((tm, d), F32), pltpu.VMEM((tm, d), BF16), pltpu.VMEM((tm, d), F32)],
        compiler_params=_params(("parallel", "arbitrary")),
        name="out_ffn_ple",
    )(x, yr, ya, wo, gmix, gpre, wgu, wgu, wd, gpost, p, wpg, bpg, wpp)


def _block_diag_chunks(w):
    n, bs, _ = w.shape
    per = LANES // bs
    w = w.reshape(n // per, per, bs, bs)
    eye = jnp.eye(per, dtype=w.dtype)
    return jnp.einsum('cpij,pq->cpiqj', w, eye).reshape(n // per, LANES, LANES)


def _layer(x, p, norm_mix_pre, norm_mix_post, w_in, conv_w, conv_b, lru_wa, lru_ba, lru_wx, lru_bx,
           lru_lambda, cmp_pos_k, cmp_pos_v, cmp_k_w1, cmp_k_w2, cmp_v_w1, cmp_v_w2, w_out,
           norm_ffn_pre, norm_ffn_post, ffn_w_gate_up, ffn_w_down, ple_w_proj, ple_w_gate, ple_b_gate,
           batch, seq):
    t, d = x.shape
    d_attn = N_HEADS * HEAD_DIM
    d_kv = N_KV_GROUPS * HEAD_DIM
    row2 = lambda v: v.reshape(1, -1)

    o_q = 2 * d
    o_kv = o_q + d_attn
    o_gn = o_kv + 6 * d_kv
    o_gm = o_gn + 3 * N_HEADS
    w_f = jnp.concatenate([w_in[:, 0:o_q], w_in[:, o_gm:o_gm + 2 * d]], axis=1).astype(BF16)
    w_q = w_in[:, o_q:o_kv].astype(BF16)
    w_kv = w_in[:, o_kv:o_gn].astype(BF16)
    per_g = 3 * HEADS_PER_GROUP
    w_gn = w_in[:, o_gn:o_gm].reshape(d, N_KV_GROUPS, per_g)
    w_gn = jnp.pad(w_gn, ((0, 0), (0, 0), (0, LANES - per_g))).reshape(d, N_KV_GROUPS * LANES).astype(BF16)

    g_pre = row2(norm_mix_pre)
    tm = min(512, t)
    fm = _norm_matmul(x, g_pre, w_f, BF16, tm, w_f.shape[1])
    qm, gnm, kv_chunks, kvm = _norm_qkv(x, g_pre, jnp.concatenate([w_q, w_gn, w_kv], axis=1), d_attn,
                                        N_KV_GROUPS * LANES, 2 * N_KV_GROUPS, tm)

    yr = _rnn(fm, conv_w, row2(conv_b), _block_diag_chunks(lru_wa).astype(BF16),
              _block_diag_chunks(lru_wx).astype(BF16), row2(lru_ba), row2(lru_bx), row2(lru_lambda),
              batch, seq, min(512, seq))

    nchunk = seq // CMP_STRIDE
    kcm, vcm = _compress(kv_chunks, cmp_k_w1.astype(BF16), cmp_k_w2.astype(BF16), cmp_pos_k.reshape(1, -1),
                         cmp_v_w1.astype(BF16), cmp_v_w2.astype(BF16), cmp_pos_v.reshape(1, -1), batch, nchunk)

    tq = min(256, seq)
    ya = _nsa(qm, kvm, kcm, vcm, gnm, fm, batch, seq, tq, min(NSA_KEY_TILE, seq), min(256, seq),
              gm_col0=(3 * d) // (HEADS_PER_GROUP * HEAD_DIM))

    dff = ffn_w_down.shape[0]
    tf = dff
    return _ffn(x, yr, ya, w_out.astype(BF16), row2(norm_mix_post), row2(norm_ffn_pre),
                ffn_w_gate_up.astype(BF16), ffn_w_down.astype(BF16), row2(norm_ffn_post),
                p, ple_w_gate.astype(BF16), row2(ple_b_gate), ple_w_proj.astype(BF16), tm, tf)


def kernel(x, p, norm_mix_pre, norm_mix_post, w_in, conv_w, conv_b, lru_wa, lru_ba, lru_wx, lru_bx, lru_lambda, cmp_pos_k, cmp_pos_v, cmp_k_w1, cmp_k_w2, cmp_v_w1, cmp_v_w2, w_out, norm_ffn_pre, norm_ffn_post, ffn_w_gate_up, ffn_w_down, ple_w_proj, ple_w_gate, ple_b_gate):
    batch, seq, d = x.shape
    depth = w_in.shape[0]
    xf = x.reshape(batch * seq, d)
    for i in range(depth):
        xf = _layer(xf, p[i].reshape(batch * seq, -1), norm_mix_pre[i], norm_mix_post[i], w_in[i], conv_w[i],
                    conv_b[i], lru_wa[i], lru_ba[i], lru_wx[i], lru_bx[i], lru_lambda[i], cmp_pos_k[i],
                    cmp_pos_v[i], cmp_k_w1[i], cmp_k_w2[i], cmp_v_w1[i], cmp_v_w2[i], w_out[i],
                    norm_ffn_pre[i], norm_ffn_post[i], ffn_w_gate_up[i], ffn_w_down[i], ple_w_proj[i],
                    ple_w_gate[i], ple_b_gate[i], batch, seq)
    return xf.reshape(batch, seq, d)
```

```python
import functools

import numpy as np
import jax
import jax.numpy as jnp
from jax import lax
from jax.experimental import pallas as pl
from jax.experimental.pallas import tpu as pltpu

CONV_WIDTH = 4
LRU_C = 8.0
N_HEADS = 16
HEAD_DIM = 64
N_KV_GROUPS = 4
HEADS_PER_GROUP = N_HEADS // N_KV_GROUPS
CMP_LEN = 32
CMP_STRIDE = 16
SEL_LEN = 64
SEL_TOPK = 16
WINDOW = 512
ALIBI_MAX_BIAS = 8.0
NORM_EPS = 1e-6

LANES = 128
SUBLANES = 8
VMEM_LIMIT_BYTES = 56 * 1024 * 1024

MASK_NEG = -1e30
LOG2_E = 1.4426950408889634
N_ALIBI_COLS = 9
PAD_COL = HEAD_DIM + N_ALIBI_COLS
WORD_BITS = 16
WORD_SHIFT = 4
SEL_SHIFT = 6
N_FORCED = 3
NEAR_BLOCKS = 20
FILL_ROWS = 512
NSA_KEY_TILE = 11 * SEL_LEN
SCAN_UNROLL = 4
NSA_TILES_PER_STEP = 4
OVERLAP_CHUNKS = 4
BF16 = jnp.bfloat16
F32 = jnp.float32


def _dot(a, b):
    return jnp.dot(a, b, preferred_element_type=F32)


def _dot_nt(a, b):
    return lax.dot_general(a, b, (((1,), (1,)), ((), ())), preferred_element_type=F32)


def _sigmoid(x):
    return 0.5 * jnp.tanh(0.5 * x) + 0.5


def _gelu_tanh(x):
    c = np.float32(np.sqrt(2.0 / np.pi))
    half = 0.5 * x
    return half + half * jnp.tanh(x * (c + (c * 0.044715) * (x * x)))


def _rmsnorm(x, g):
    ms = jnp.mean(x * x, axis=-1, keepdims=True)
    return x * lax.rsqrt(ms + NORM_EPS) * g


def _params(sem):
    return pltpu.CompilerParams(dimension_semantics=sem, vmem_limit_bytes=VMEM_LIMIT_BYTES)


def _norm_matmul_kernel(x_ref, g_ref, w_ref, o_ref, h_ref):
    @pl.when(pl.program_id(1) == 0)
    def _():
        h_ref[...] = _rmsnorm(x_ref[...], g_ref[...]).astype(BF16)

    o_ref[...] = _dot(h_ref[...], w_ref[...]).astype(o_ref.dtype)


def _norm_matmul(x, g, w, out_dtype, tm, tn):
    t, k = x.shape
    n = w.shape[1]
    return pl.pallas_call(
        _norm_matmul_kernel,
        grid=(t // tm, n // tn),
        in_specs=[
            pl.BlockSpec((tm, k), lambda i, j: (i, 0)),
            pl.BlockSpec((1, k), lambda i, j: (0, 0)),
            pl.BlockSpec((k, tn), lambda i, j: (0, j)),
        ],
        out_specs=pl.BlockSpec((tm, tn), lambda i, j: (i, j)),
        out_shape=jax.ShapeDtypeStruct((t, n), out_dtype),
        scratch_shapes=[pltpu.VMEM((tm, k), BF16)],
        compiler_params=_params(("parallel", "arbitrary")),
        name="norm_matmul",
    )(x, g, w)


def _norm_qkv_kernel(x_ref, g_ref, w_ref, q_ref, gn_ref, cv_ref, kv_ref, slab_scr):
    h = _rmsnorm(x_ref[...], g_ref[...]).astype(BF16)
    z = _dot(h, w_ref[...])
    nq = q_ref.shape[1]
    ngn = gn_ref.shape[1]
    q_ref[...] = z[:, 0:nq].astype(q_ref.dtype)
    gn_ref[...] = z[:, nq:nq + ngn]
    n_cv = cv_ref.shape[0]
    rows = cv_ref.shape[1]
    for c in range(n_cv):
        lo = nq + ngn + c * HEAD_DIM
        slab_scr[...] = z[:, lo:lo + HEAD_DIM]
        for l in range(CMP_STRIDE):
            cv_ref[c, :, l * HEAD_DIM:(l + 1) * HEAD_DIM] = (
                slab_scr[pl.ds(l, rows, stride=CMP_STRIDE), :].astype(cv_ref.dtype))
    for c in range(kv_ref.shape[0]):
        lo = nq + ngn + (n_cv + c) * HEAD_DIM
        kv_ref[c] = z[:, lo:lo + HEAD_DIM].astype(kv_ref.dtype)


def _norm_qkv(x, g, w, nq, ngn, n_cv, tm):
    t, k = x.shape
    n = w.shape[1]
    n_kv = (n - nq - ngn) // HEAD_DIM - n_cv
    return pl.pallas_call(
        _norm_qkv_kernel,
        grid=(t // tm,),
        in_specs=[
            pl.BlockSpec((tm, k), lambda i: (i, 0)),
            pl.BlockSpec((1, k), lambda i: (0, 0)),
            pl.BlockSpec((k, n), lambda i: (0, 0)),
        ],
        out_specs=[
            pl.BlockSpec((tm, nq), lambda i: (i, 0)),
            pl.BlockSpec((tm, ngn), lambda i: (i, 0)),
            pl.BlockSpec((n_cv, tm // CMP_STRIDE, CMP_STRIDE * HEAD_DIM), lambda i: (0, i, 0)),
            pl.BlockSpec((n_kv, tm, HEAD_DIM), lambda i: (0, i, 0)),
        ],
        out_shape=[
            jax.ShapeDtypeStruct((t, nq), BF16),
            jax.ShapeDtypeStruct((t, ngn), F32),
            jax.ShapeDtypeStruct((n_cv, t // CMP_STRIDE, CMP_STRIDE * HEAD_DIM), BF16),
            jax.ShapeDtypeStruct((n_kv, t, HEAD_DIM), BF16),
        ],
        scratch_shapes=[pltpu.VMEM((tm, HEAD_DIM), F32)],
        compiler_params=_params(("parallel",)),
        name="norm_qkv",
    )(x, g, w)


def _rnn_kernel(xr_ref, gr_ref, gm_ref, cw_ref, cb_ref, wa_ref, wx_ref, ba_ref, bx_ref, lam_ref,
                o_ref, xbuf, a_scr, b_scr, h_scr, *, ts):
    s_idx = pl.program_id(1)
    d = xr_ref.shape[1]
    halo = SUBLANES

    @pl.when(s_idx == 0)
    def _():
        xbuf[0:halo, :] = jnp.zeros((halo, d), F32)
        h_scr[...] = jnp.zeros(h_scr.shape, F32)

    xbuf[halo:halo + ts, :] = xr_ref[...].astype(F32)
    xc = cb_ref[...] + xbuf[halo:halo + ts, :] * cw_ref[CONV_WIDTH - 1:CONV_WIDTH, :]
    for k in range(1, CONV_WIDTH):
        xc = xc + xbuf[halo - k:halo - k + ts, :] * cw_ref[CONV_WIDTH - 1 - k:CONV_WIDTH - k, :]
    xbuf[0:halo, :] = xbuf[ts:ts + halo, :]

    xcb = xc.astype(BF16)
    n_chunks = d // LANES
    neg_sp = -LRU_C * (jnp.maximum(-lam_ref[...], 0.0) + jnp.log(1.0 + jnp.exp(-jnp.abs(lam_ref[...]))))
    for c in range(n_chunks):
        sl = slice(c * LANES, (c + 1) * LANES)
        xk = xcb[:, sl]
        r = _sigmoid(_dot(xk, wa_ref[c]) + ba_ref[:, sl])
        i = _sigmoid(_dot(xk, wx_ref[c]) + bx_ref[:, sl])
        log_a = r * neg_sp[:, sl]
        a = jnp.exp(log_a)
        a_scr[:, sl] = a
        u = 1.0 - a * a
        root = jnp.where(u > 0.0, u * lax.rsqrt(u), 0.0)
        b_scr[:, sl] = root * (i * xc[:, sl])

    row_id = lax.broadcasted_iota(jnp.int32, (SUBLANES, d), 0)

    def group(gi, h):
        base = pl.multiple_of(gi * SUBLANES, SUBLANES)
        a = a_scr[pl.ds(base, SUBLANES), :]
        b = b_scr[pl.ds(base, SUBLANES), :]
        s = 1
        while s < SUBLANES:
            reach = row_id >= s
            b = jnp.where(reach, a * pltpu.roll(b, s, 0) + b, b)
            a = jnp.where(reach, a * pltpu.roll(a, s, 0), a)
            s *= 2
        hs = a * h + b
        a_scr[pl.ds(base, SUBLANES), :] = hs
        return jnp.broadcast_to(hs[SUBLANES - 1:SUBLANES, :], (SUBLANES, d))

    h_scr[...] = lax.fori_loop(0, ts // SUBLANES, group, h_scr[...], unroll=SCAN_UNROLL)
    o_ref[...] = (_sigmoid(gm_ref[...].astype(F32)) * a_scr[...]
                  * _gelu_tanh(gr_ref[...].astype(F32))).astype(o_ref.dtype)


def _rnn(f, conv_w, conv_b, wa, wx, ba, bx, lam, batch, seq, ts):
    t = f.shape[0]
    d = conv_w.shape[1]
    ns = seq // ts
    nd = d // d
    del nd
    row = lambda b, s: b * ns + s
    vec = lambda r: pl.BlockSpec((r, d), lambda b, s: (0, 0))
    return pl.pallas_call(
        functools.partial(_rnn_kernel, ts=ts),
        grid=(batch, ns),
        in_specs=[
            pl.BlockSpec((ts, d), lambda b, s: (row(b, s), 0)),
            pl.BlockSpec((ts, d), lambda b, s: (row(b, s), 1)),
            pl.BlockSpec((ts, d), lambda b, s: (row(b, s), 2)),
            vec(CONV_WIDTH), vec(1),
            pl.BlockSpec(wa.shape, lambda b, s: (0, 0, 0)),
            pl.BlockSpec(wx.shape, lambda b, s: (0, 0, 0)),
            vec(1), vec(1), vec(1),
        ],
        out_specs=pl.BlockSpec((ts, d), lambda b, s: (row(b, s), 0)),
        out_shape=jax.ShapeDtypeStruct((t, d), BF16),
        scratch_shapes=[
            pltpu.VMEM((ts + SUBLANES, d), F32),
            pltpu.VMEM((ts, d), F32),
            pltpu.VMEM((ts, d), F32),
            pltpu.VMEM((SUBLANES, d), F32),
        ],
        compiler_params=_params(("parallel", "arbitrary")),
        name="rnn_mixer",
    )(f, f, f, conv_w, conv_b, wa, wx, ba, bx, lam)


def _compress_kernel(ak_ref, av_ref, w1k_ref, w2k_ref, pk_ref, w1v_ref, w2v_ref, pv_ref, ok_ref, ov_ref):
    def one(a_ref, w1_ref, w2_ref, p_ref, o_ref):
        a = a_ref[0]
        half = a.shape[1]
        nchunk = a.shape[0]
        lo = _dot(a, w1_ref[0:half, :])
        hi = _dot(a, w1_ref[half:2 * half, :])
        pb = _dot(jnp.broadcast_to(p_ref[...], (SUBLANES, 2 * half)).astype(BF16), w1_ref[...])[0:1, :]
        h = lo + pltpu.roll(hi, nchunk - 1, 0) + pb
        o_ref[0] = _dot(_gelu_tanh(h).astype(BF16), w2_ref[...]).astype(o_ref.dtype)

    one(ak_ref, w1k_ref, w2k_ref, pk_ref, ok_ref)
    one(av_ref, w1v_ref, w2v_ref, pv_ref, ov_ref)


def _compress(kv_chunks, w1k, w2k, pk, w1v, w2v, pv, batch, nchunk):
    g = N_KV_GROUPS
    width = kv_chunks.shape[2]
    full = lambda a: pl.BlockSpec(a.shape, lambda b, gg: (0,) * a.ndim)
    out = jax.ShapeDtypeStruct((batch * g, nchunk, HEAD_DIM), BF16)
    return pl.pallas_call(
        _compress_kernel,
        grid=(batch, g),
        in_specs=[
            pl.BlockSpec((1, nchunk, width), lambda b, gg: (gg, b, 0)),
            pl.BlockSpec((1, nchunk, width), lambda b, gg: (g + gg, b, 0)),
            full(w1k), full(w2k), full(pk), full(w1v), full(w2v), full(pv),
        ],
        out_specs=[
            pl.BlockSpec((1, nchunk, HEAD_DIM), lambda b, gg: (b * g + gg, 0, 0)),
            pl.BlockSpec((1, nchunk, HEAD_DIM), lambda b, gg: (b * g + gg, 0, 0)),
        ],
        out_shape=[out, out],
        compiler_params=_params(("parallel", "parallel")),
        name="compress",
    )(kv_chunks, kv_chunks, w1k, w2k, pk, w1v, w2v, pv)


ONES_ROWS = 16


def _scores(k, q_rows, s_ref, keep=None, keep_rows=None, shift=None):
    s = _dot(k, q_rows)
    if shift is not None:
        s = s + shift
    if keep is None:
        s_ref[...] = s
        return jnp.max(s, axis=0, keepdims=True)
    n = s.shape[0] if keep_rows is None else keep_rows
    top = s[0:n]
    for mask in keep:
        top = jnp.where(mask, top, MASK_NEG)
    s_ref[0:n, :] = top
    col_max = jnp.max(top, axis=0, keepdims=True)
    if n < s.shape[0]:
        s_ref[n:, :] = s[n:]
        col_max = jnp.maximum(col_max, jnp.max(s[n:], axis=0, keepdims=True))
    return col_max


def _accumulate(s_ref, col_max, v_aug_t, state):
    m_old, acc_old = state
    m_new = jnp.maximum(m_old, col_max)
    alpha = jnp.exp2(m_old - m_new)
    e = jnp.exp2(s_ref[...] - m_new).astype(BF16)
    return m_new, alpha * acc_old + _dot(v_aug_t, e)


def _nsa_kernel(q_ref, kc_ref, vc_ref, ks_ref, vs_ref, kw_ref, vw_ref, gn_ref, gm_ref,
                qal_ref, cpos_ref, wsel_ref, o_ref, *scratch, tq, tiles_per_step, **static):
    def tile(j, carry):
        rows = pl.ds(pl.multiple_of(j * tq, tq), tq)
        _nsa_tile(pl.program_id(2) * tiles_per_step + j, q_ref.at[rows, :], kc_ref, vc_ref, ks_ref, vs_ref,
                  kw_ref, vw_ref, gn_ref.at[rows, :], gm_ref.at[rows, :], qal_ref, cpos_ref, wsel_ref,
                  o_ref.at[rows, :], *scratch, tq=tq, **static)
        return carry

    lax.fori_loop(0, tiles_per_step, tile, 0)


def _nsa_tile(qt, q_ref, kc_ref, vc_ref, ks_ref, vs_ref, kw_ref, vw_ref, gn_ref, gm_ref,
              qal_ref, cpos_ref, wsel_ref,
              o_ref, q_t, ksaug, kwaug, kcaug, kstage, vstage, s_win, s_a, s_b, win_max, o_cmp_scr, flags,
              *, tq, nk, nkw, seq, sel_chunk):
    b = pl.program_id(0)
    g = pl.program_id(1)
    r_heads = HEADS_PER_GROUP
    dh = HEAD_DIM
    m_cols = r_heads * tq
    nb = seq // SEL_LEN
    nc = kc_ref.shape[1]
    a_w = 2 * dh
    k_w = a_w + nb

    @pl.when((b == 0) & (g == 0) & (qt == 0))
    def _():
        rows = min(FILL_ROWS, seq)

        def fill(c, carry):
            off = pl.multiple_of(c * rows, rows)
            pos = off + lax.broadcasted_iota(jnp.int32, (rows, k_w), 0)
            lane = lax.broadcasted_iota(jnp.int32, (rows, k_w), 1)
            blk_of = jnp.right_shift(pos, SEL_SHIFT)
            cols = jnp.where((lane >= dh) & (lane < dh + 3), blk_of * SEL_LEN,
                             jnp.where((lane >= dh + 3) & (lane < dh + 6), pos & (SEL_LEN - 1),
                                       jnp.where(lane == a_w + blk_of, 1, 0)))
            cols = cols.astype(F32).astype(BF16)
            ksaug[pl.ds(off, rows), :] = cols
            kwaug[pl.ds(off, rows), :] = cols[:, 0:a_w]
            return carry

        lax.fori_loop(0, seq // rows, fill, 0)
        kcaug[...] = cpos_ref[...]
        vstage[...] = jnp.ones(vstage.shape, BF16)

    @pl.when(qt == 0)
    def _():
        ksaug[:, 0:dh] = ks_ref[0]
        kwaug[:, 0:dh] = kw_ref[0]
        kcaug[:, 0:dh] = kc_ref[0]
        q_t[dh:a_w, :] = qal_ref[0]

    v_rows = dh + ONES_ROWS
    ri = lax.broadcasted_iota(jnp.int32, (v_rows, 2 * dh), 0)
    ci = lax.broadcasted_iota(jnp.int32, (v_rows, 2 * dh), 1)
    pick = (((ri < dh) & (ri == ci)) | ((ri >= dh) & (ci == dh))).astype(F32).astype(BF16)
    transposed = lambda v: _dot_nt(pick[0:dh, 0:dh], v).astype(BF16)
    staged_aug_t = lambda v: _dot_nt(pick, v).astype(BF16)
    ones_rows = jnp.ones((ONES_ROWS, nkw), BF16)
    window_aug_t = lambda v: jnp.concatenate([transposed(v), ones_rows], axis=0)
    init = (jnp.full((1, m_cols), MASK_NEG, F32), jnp.zeros((v_rows, m_cols), F32))

    t0 = qt * tq
    col = lax.broadcasted_iota(jnp.int32, (1, m_cols), 1)
    trow = t0 + (col & (tq - 1))

    scale = np.float32(HEAD_DIM ** -0.5 * LOG2_E)
    qf = jnp.transpose(q_ref[...].astype(F32) * scale)
    for r in range(r_heads):
        q_t[0:dh, r * tq:(r + 1) * tq] = qf[r * dh:(r + 1) * dh, :].astype(BF16)

    n_wt = 3
    win_off = [pl.multiple_of(jnp.maximum(t0 - i * nkw, 0), nkw) for i in range(n_wt)]
    newer = (t0 + lax.broadcasted_iota(jnp.int32, (nkw, 1), 0)) <= trow

    def window_tile(i):
        s = _dot(kwaug[pl.ds(win_off[i], nkw), :], q_t[0:a_w, :])
        return s if i == 0 else s + jnp.where(t0 - i * nkw >= 0, 0.0, MASK_NEG)

    def window_scores():
        merged = jnp.where(newer, window_tile(0), window_tile(2))
        s_win[0:nkw, :] = merged
        win_max[0:1, :] = jnp.max(merged, axis=0, keepdims=True)
        mid = window_tile(1)
        s_win[nkw:2 * nkw, :] = mid
        win_max[1:2, :] = jnp.max(mid, axis=0, keepdims=True)

    def compressed_and_select(rows_c, rows_b):
        free = max(rows_c - cmp_chunk - tq // CMP_STRIDE, 0)
        s = _dot(kcaug[0:rows_c, :], q_t[0:a_w, :])
        cend = (free + lax.broadcasted_iota(jnp.int32, (rows_c - free, 1), 0)) * CMP_STRIDE + (CMP_LEN - 1)
        tail = jnp.where(cend <= trow, s[free:], MASK_NEG)
        s = jnp.concatenate([s[0:free], tail], axis=0) if free else tail
        m = jnp.max(s, axis=0, keepdims=True)
        e = jnp.exp2(s - m)
        has_key = (trow >= CMP_LEN - 1).astype(F32)
        p = e * (has_key / jnp.sum(e, axis=0, keepdims=True))
        o_cmp_scr[...] = _dot(transposed(vc_ref[0, 0:rows_c, :]), p.astype(BF16))
        imp = p[:, 0:tq]
        for r in range(1, r_heads):
            imp = imp + p[:, r * tq:(r + 1) * tq]

        hi = imp.astype(BF16)
        lo = (imp - hi.astype(F32)).astype(BF16)
        wsel = wsel_ref[0:rows_b, 0:rows_c]
        imp_t = _dot(wsel, hi) + _dot(wsel, lo)

        window_scores()

        blk = lax.broadcasted_iota(jnp.int32, (rows_b, tq), 0).astype(F32)
        tq_l = t0 + lax.broadcasted_iota(jnp.int32, (rows_b, tq), 1)
        cur = jnp.right_shift(tq_l, SEL_SHIFT).astype(F32)
        valid = blk <= cur
        sel = jnp.where(blk == 0.0, 1.0, jnp.where(blk == cur, 1.0, jnp.where(blk == cur - 1.0, 1.0, 0.0)))
        score = jnp.where(valid, jnp.where(sel > 0.0, -1.0, imp_t), -1.0)
        for _ in range(min(SEL_TOPK, nb) - N_FORCED):
            mx = jnp.max(score, axis=0, keepdims=True)
            idx = jnp.min(jnp.where(score == mx, blk, float(nb)), axis=0, keepdims=True)
            hit = blk == idx
            sel = jnp.where(hit, 1.0, sel)
            score = jnp.where(hit, -2.0, score)
        selv = jnp.where(valid, sel, 0.0) > 0.0
        selneg_t = jnp.where(selv, 0.0, MASK_NEG).astype(BF16)
        unseen = jnp.full((nb - rows_b, tq), MASK_NEG, BF16)
        for r in range(r_heads):
            q_t[a_w:a_w + rows_b, r * tq:(r + 1) * tq] = selneg_t
            if rows_b < nb:
                q_t[a_w + rows_b:, r * tq:(r + 1) * tq] = unseen

        used = jnp.max(jnp.where(selv, 1.0, 0.0), axis=1, keepdims=True)
        bit_id = lax.broadcasted_iota(jnp.int32, (rows_b, 1), 0) & (WORD_BITS - 1)
        weighted = used * jnp.left_shift(1, bit_id).astype(F32)
        for i in range(nb // WORD_BITS):
            if (i + 1) * WORD_BITS <= rows_b:
                flags[i] = jnp.sum(weighted[i * WORD_BITS:(i + 1) * WORD_BITS, :]).astype(jnp.int32)
            else:
                flags[i] = 0

    n_var = nb // sel_chunk
    cmp_chunk = sel_chunk * (SEL_LEN // CMP_STRIDE)
    seen_c = (t0 + tq - CMP_LEN) // CMP_STRIDE + 1
    seen_b = (t0 + tq) // SEL_LEN
    variant = jnp.maximum((seen_c + cmp_chunk - 1) // cmp_chunk, (seen_b + sel_chunk - 1) // sel_chunk)
    for v in range(1, n_var + 1):
        pl.when(variant == v)(functools.partial(compressed_and_select, min(v * cmp_chunk, nc), v * sel_chunk))
    o_cmp = o_cmp_scr[...]

    n_own = tq // SEL_LEN
    off_q = pl.multiple_of(t0, tq)
    kstage[0:tq, :] = ksaug[pl.ds(off_q, tq), :]
    vstage[0:tq, 0:dh] = vs_ref[0, pl.ds(off_q, tq), :]

    def stage_block(j, cnt):
        src = pl.multiple_of(j * SEL_LEN, SEL_LEN)
        dst = pl.multiple_of(cnt * SEL_LEN, SEL_LEN)
        kstage[pl.ds(dst, SEL_LEN), :] = ksaug[pl.ds(src, SEL_LEN), :]
        vstage[pl.ds(dst, SEL_LEN), 0:dh] = vs_ref[0, pl.ds(src, SEL_LEN), :]

    def gather(j, cnt):
        bit = jnp.right_shift(flags[jnp.right_shift(j, WORD_SHIFT)], j & (WORD_BITS - 1)) & 1
        pl.when(bit == 1)(functools.partial(stage_block, j, cnt))
        return cnt + bit

    n_past = jnp.right_shift(t0, SEL_SHIFT)
    near_lo = jnp.maximum(n_past - NEAR_BLOCKS, 0)
    first = jnp.where(near_lo > 0, flags[0] & 1, 0)
    pl.when(first == 1)(functools.partial(stage_block, 0, n_own))

    def far_word(w, cnt):
        lo = jnp.maximum(w * WORD_BITS, 1)
        hi = jnp.minimum((w + 1) * WORD_BITS, near_lo)
        rest = jnp.where(w == 0, flags[w] & -2, flags[w])
        return lax.cond(rest != 0, lambda c: lax.fori_loop(lo, hi, gather, c), lambda c: c, cnt)

    n_blocks = lax.fori_loop(0, jnp.right_shift(near_lo + WORD_BITS - 1, WORD_SHIFT), far_word, n_own + first)
    n_blocks = lax.fori_loop(near_lo, n_past, gather, n_blocks)
    per_tile = nk // SEL_LEN
    n_tiles = (n_blocks + per_tile - 1) // per_tile
    pad_block = jnp.where(lax.broadcasted_iota(jnp.int32, (SEL_LEN, k_w), 1) == PAD_COL, 1.0, 0.0).astype(BF16)

    def pad(j, carry):
        dst = pl.multiple_of(j * SEL_LEN, SEL_LEN)
        kstage[pl.ds(dst, SEL_LEN), :] = pad_block
        return carry

    lax.fori_loop(n_blocks, n_tiles * per_tile, pad, 0)

    own_pos = t0 + lax.broadcasted_iota(jnp.int32, (tq, 1), 0)
    s_first = _dot(kstage[0:nk, :], q_t[...])
    own = jnp.where(own_pos <= trow, s_first[0:tq], MASK_NEG)
    s_a[0:tq, :] = own
    s_a[tq:, :] = s_first[tq:]
    max_a = jnp.maximum(jnp.max(own, axis=0, keepdims=True), jnp.max(s_first[tq:], axis=0, keepdims=True))

    m_w = jnp.maximum(win_max[0:1, :], win_max[1:2, :])
    e_merged = jnp.exp2(s_win[0:nkw, :] - m_w)
    e_mid = jnp.exp2(s_win[nkw:2 * nkw, :] - m_w)
    weights = jnp.concatenate([jnp.where(newer, e_merged, 0.0).astype(BF16), e_mid.astype(BF16),
                               jnp.where(newer, 0.0, e_merged).astype(BF16)], axis=0)
    v_win = jnp.concatenate([window_aug_t(vw_ref[0, pl.ds(win_off[i], nkw), :]) for i in range(n_wt)], axis=1)
    acc_w = _dot(v_win, weights)
    o_win = acc_w[0:dh] / acc_w[dh:dh + 1]

    def tile_scores(i, s_ref):
        return _scores(kstage[pl.ds(pl.multiple_of(i * nk, nk), nk), :], q_t[...], s_ref)

    def tile_accumulate(i, s_ref, col_max, state):
        return _accumulate(s_ref, col_max, staged_aug_t(vstage[pl.ds(pl.multiple_of(i * nk, nk), nk), :]), state)

    def scores_and_accumulate(i_next, s_next_ref, i_cur, s_cur_ref, max_cur, state):
        m_old, acc_old = state
        m_new = jnp.maximum(m_old, max_cur)
        alpha = jnp.exp2(m_old - m_new)
        s_next = _dot(kstage[pl.ds(pl.multiple_of(i_next * nk, nk), nk), :], q_t[...])
        s_next_ref[...] = s_next
        rows = nk // OVERLAP_CHUNKS
        partial, weights = [], []
        for c in range(OVERLAP_CHUNKS):
            rs = slice(c * rows, (c + 1) * rows)
            partial.append(jnp.max(s_next[rs], axis=0, keepdims=True))
            m_c = m_new + partial[c] * 0.0
            weights.append(jnp.exp2(s_cur_ref[rs, :] - m_c).astype(BF16))
        v_t = staged_aug_t(vstage[pl.ds(pl.multiple_of(i_cur * nk, nk), nk), :])
        acc = alpha * acc_old + _dot(v_t, jnp.concatenate(weights, axis=0))
        return (m_new, acc), functools.reduce(jnp.maximum, partial)

    def slc_pair(j, carry):
        state, max_a = carry
        i = 2 * j
        state, max_b = scores_and_accumulate(i + 1, s_b, i, s_a, max_a, state)
        state, max_a = scores_and_accumulate(i + 2, s_a, i + 1, s_b, max_b, state)
        return state, max_a

    n_pairs = (n_tiles - 1) // 2
    state, max_a = lax.fori_loop(0, n_pairs, slc_pair, (init, max_a))
    i_a = 2 * n_pairs

    def two_left(state):
        state, max_b = scores_and_accumulate(i_a + 1, s_b, i_a, s_a, max_a, state)
        return tile_accumulate(i_a + 1, s_b, max_b, state)

    def one_left(state):
        return tile_accumulate(i_a, s_a, max_a, state)

    _, acc = lax.cond(n_tiles - i_a == 2, two_left, one_left, state)
    o_slc = acc[0:dh] / acc[dh:dh + 1]

    gates = jnp.transpose(_sigmoid(gn_ref[...]))
    heads = []
    for r in range(r_heads):
        cs = slice(r * tq, (r + 1) * tq)
        heads.append(gates[3 * r:3 * r + 1, :] * o_cmp[:, cs] + gates[3 * r + 1:3 * r + 2, :] * o_slc[:, cs]
                     + gates[3 * r + 2:3 * r + 3, :] * o_win[:, cs])
    o = jnp.transpose(jnp.concatenate(heads, axis=0))
    o_ref[...] = (_sigmoid(gm_ref[...].astype(F32)) * o).astype(o_ref.dtype)


def _alibi_tables(seq, nc, tq):
    import ml_dtypes
    bf = ml_dtypes.bfloat16
    h = np.arange(1, N_HEADS + 1, dtype=np.float32)
    slopes = (np.exp2(-ALIBI_MAX_BIAS * h / N_HEADS) * LOG2_E).astype(np.float32)
    s1 = slopes.astype(bf).astype(np.float32)
    s2 = (slopes - s1).astype(bf).astype(np.float32)
    s3 = (slopes - s1 - s2).astype(bf).astype(np.float32)
    dh = HEAD_DIM
    qal = np.zeros((N_HEADS, dh), np.float32)
    for rep in range(3):
        qal[:, 3 * rep + 0] = s1
        qal[:, 3 * rep + 1] = s2
        qal[:, 3 * rep + 2] = s3
    qal[:, PAD_COL - dh] = MASK_NEG
    qal = qal.reshape(N_KV_GROUPS, HEADS_PER_GROUP, dh).transpose(0, 2, 1)
    qal_p = np.repeat(qal, tq, axis=2)

    nb = seq // SEL_LEN
    c = np.arange(nc)
    cpos = np.zeros((nc, 2 * dh), np.float32)
    cpos[:, dh:dh + 3] = ((c // 16) * 16 * CMP_STRIDE)[:, None]
    cpos[:, dh + 3:dh + 6] = ((c % 16) * CMP_STRIDE)[:, None]
    cpos[:, dh + 6:dh + 9] = CMP_LEN - 1

    r_sel = SEL_LEN // CMP_STRIDE
    r_cmp = CMP_LEN // CMP_STRIDE
    wsel = np.zeros((nb, nc), np.float32)
    for j in range(nb):
        for mm in range(r_sel):
            for nn in range(r_cmp):
                ci = r_sel * j + mm - nn
                if 0 <= ci < nc - 1:
                    wsel[j, ci] += 1.0
    as_bf = lambda a: jnp.asarray(a.astype(bf))
    return as_bf(qal_p), as_bf(cpos), as_bf(wsel)


def _nsa(qm, kvm, kcm, vcm, gnm, fm, batch, seq, tq, nk, nkw, gm_col0):
    t = qm.shape[0]
    g = N_KV_GROUPS
    dh = HEAD_DIM
    nqt = seq // tq
    nc = kcm.shape[1]
    nb = seq // SEL_LEN
    gw = HEADS_PER_GROUP * dh
    m_cols = HEADS_PER_GROUP * tq
    qal, cpos, wsel = _alibi_tables(seq, nc, tq)
    assert nkw == tq and WINDOW == 2 * tq, "the window branch merges the new and old key tiles of a query tile"
    n_wt = 2
    sel_chunk = max(WORD_BITS, nb // 4)
    per_tile = nk // SEL_LEN
    stage_rows = -(-nb // per_tile) * per_tile * SEL_LEN
    tps = NSA_TILES_PER_STEP if nqt % NSA_TILES_PER_STEP == 0 else 1
    n_steps = nqt // tps
    rq = tps * tq
    row = lambda b, gg, i: b * n_steps + i
    slab = lambda base: pl.BlockSpec((1, seq, dh), lambda b, gg, i: (base + gg, b, 0))
    cmp_slab = pl.BlockSpec((1, nc, dh), lambda b, gg, i: (b * g + gg, 0, 0))
    const = lambda a: pl.BlockSpec(a.shape, lambda b, gg, i: (0,) * a.ndim)
    return pl.pallas_call(
        functools.partial(_nsa_kernel, tq=tq, tiles_per_step=tps, nk=nk, nkw=nkw, seq=seq, sel_chunk=sel_chunk),
        grid=(batch, g, n_steps),
        in_specs=[
            pl.BlockSpec((rq, gw), lambda b, gg, i: (row(b, gg, i), gg)),
            cmp_slab, cmp_slab,
            slab(0), slab(g), slab(2 * g), slab(3 * g),
            pl.BlockSpec((rq, LANES), lambda b, gg, i: (row(b, gg, i), gg)),
            pl.BlockSpec((rq, gw), lambda b, gg, i: (row(b, gg, i), gm_col0 + gg)),
            pl.BlockSpec((1, dh, m_cols), lambda b, gg, i: (gg, 0, 0)),
            const(cpos), const(wsel),
        ],
        out_specs=pl.BlockSpec((rq, gw), lambda b, gg, i: (row(b, gg, i), gg)),
        out_shape=jax.ShapeDtypeStruct((t, g * gw), BF16),
        scratch_shapes=[
            pltpu.VMEM((2 * dh + nb, m_cols), BF16),
            pltpu.VMEM((seq, 2 * dh + nb), BF16),
            pltpu.VMEM((seq, 2 * dh), BF16),
            pltpu.VMEM((nc, 2 * dh), BF16),
            pltpu.VMEM((stage_rows, 2 * dh + nb), BF16),
            pltpu.VMEM((stage_rows, 2 * dh), BF16),
            pltpu.VMEM((n_wt * nkw, m_cols), F32),
            pltpu.VMEM((nk, m_cols), F32),
            pltpu.VMEM((nk, m_cols), F32),
            pltpu.VMEM((SUBLANES, m_cols), F32),
            pltpu.VMEM((dh, m_cols), F32),
            pltpu.SMEM((nb // WORD_BITS,), jnp.int32),
        ],
        compiler_params=_params(("arbitrary", "arbitrary", "arbitrary")),
        name="nsa",
    )(qm, kcm, vcm, kvm, kvm, kvm, kvm, gnm, fm, qal, cpos, wsel)


def _ffn_kernel(x_ref, yr_ref, ya_ref, wo_ref, gmix_ref, gpre_ref, wg_ref, wu_ref, wd_ref, gpost_ref,
                p_ref, wpg_ref, bpg_ref, wpp_ref, o_ref, x1_ref, h_ref, acc_ref):
    j = pl.program_id(1)

    @pl.when(j == 0)
    def _():
        y = (yr_ref[...].astype(F32) + ya_ref[...].astype(F32)).astype(BF16)
        x1 = x_ref[...] + _rmsnorm(_dot(y, wo_ref[...]), gmix_ref[...])
        x1_ref[...] = x1
        h_ref[...] = _rmsnorm(x1, gpre_ref[...]).astype(BF16)
        acc_ref[...] = jnp.zeros(acc_ref.shape, F32)

    h = h_ref[...]
    gate = _dot(h, wg_ref[...])
    up = _dot(h, wu_ref[...])
    act = (gate * _sigmoid(gate) * up).astype(BF16)
    acc_ref[...] += _dot(act, wd_ref[...])

    @pl.when(j == pl.num_programs(1) - 1)
    def _():
        x2 = x1_ref[...] + _rmsnorm(acc_ref[...], gpost_ref[...])
        gate_p = _sigmoid(_dot(x2.astype(BF16), wpg_ref[...]) + bpg_ref[...])
        o_ref[...] = x2 + gate_p * _dot(p_ref[...].astype(BF16), wpp_ref[...])


def _ffn(x, yr, ya, wo, gmix, gpre, wgu, wd, gpost, p, wpg, bpg, wpp, tm, tf):
    t, d = x.shape
    dff = wd.shape[0]
    nf = dff // tf
    dp = p.shape[1]
    rows = pl.BlockSpec((tm, d), lambda i, j: (i, 0))
    vec = pl.BlockSpec((1, d), lambda i, j: (0, 0))
    square = pl.BlockSpec((d, d), lambda i, j: (0, 0))
    return pl.pallas_call(
        _ffn_kernel,
        grid=(t // tm, nf),
        in_specs=[
            rows, rows, rows, square, vec, vec,
            pl.BlockSpec((d, tf), lambda i, j: (0, j)),
            pl.BlockSpec((d, tf), lambda i, j: (0, nf + j)),
            pl.BlockSpec((tf, d), lambda i, j: (j, 0)),
            vec,
            pl.BlockSpec((tm, dp), lambda i, j: (i, 0)),
            square,
            vec,
            pl.BlockSpec((dp, d), lambda i, j: (0, 0)),
        ],
        out_specs=rows,
        out_shape=jax.ShapeDtypeStruct((t, d), F32),
        scratch_shapes=[pltpu.VMEM((tm, d), F32), pltpu.VMEM((tm, d), BF16), pltpu.VMEM((tm, d), F32)],
        compiler_params=_params(("parallel", "arbitrary")),
        name="out_ffn_ple",
    )(x, yr, ya, wo, gmix, gpre, wgu, wgu, wd, gpost, p, wpg, bpg, wpp)


def _block_diag_chunks(w):
    n, bs, _ = w.shape
    per = LANES // bs
    w = w.reshape(n // per, per, bs, bs)
    eye = jnp.eye(per, dtype=w.dtype)
    return jnp.einsum('cpij,pq->cpiqj', w, eye).reshape(n // per, LANES, LANES)


def _layer(x, p, norm_mix_pre, norm_mix_post, w_in, conv_w, conv_b, lru_wa, lru_ba, lru_wx, lru_bx,
           lru_lambda, cmp_pos_k, cmp_pos_v, cmp_k_w1, cmp_k_w2, cmp_v_w1, cmp_v_w2, w_out,
           norm_ffn_pre, norm_ffn_post, ffn_w_gate_up, ffn_w_down, ple_w_proj, ple_w_gate, ple_b_gate,
           batch, seq):
    t, d = x.shape
    d_attn = N_HEADS * HEAD_DIM
    d_kv = N_KV_GROUPS * HEAD_DIM
    row2 = lambda v: v.reshape(1, -1)

    o_q = 2 * d
    o_kv = o_q + d_attn
    o_gn = o_kv + 6 * d_kv
    o_gm = o_gn + 3 * N_HEADS
    w_f = jnp.concatenate([w_in[:, 0:o_q], w_in[:, o_gm:o_gm + 2 * d]], axis=1).astype(BF16)
    w_q = w_in[:, o_q:o_kv].astype(BF16)
    w_kv = w_in[:, o_kv:o_gn].astype(BF16)
    per_g = 3 * HEADS_PER_GROUP
    w_gn = w_in[:, o_gn:o_gm].reshape(d, N_KV_GROUPS, per_g)
    w_gn = jnp.pad(w_gn, ((0, 0), (0, 0), (0, LANES - per_g))).reshape(d, N_KV_GROUPS * LANES).astype(BF16)

    g_pre = row2(norm_mix_pre)
    tm = min(512, t)
    fm = _norm_matmul(x, g_pre, w_f, BF16, tm, w_f.shape[1])
    qm, gnm, kv_chunks, kvm = _norm_qkv(x, g_pre, jnp.concatenate([w_q, w_gn, w_kv], axis=1), d_attn,
                                        N_KV_GROUPS * LANES, 2 * N_KV_GROUPS, tm)

    yr = _rnn(fm, conv_w, row2(conv_b), _block_diag_chunks(lru_wa).astype(BF16),
              _block_diag_chunks(lru_wx).astype(BF16), row2(lru_ba), row2(lru_bx), row2(lru_lambda),
              batch, seq, min(512, seq))

    nchunk = seq // CMP_STRIDE
    kcm, vcm = _compress(kv_chunks, cmp_k_w1.astype(BF16), cmp_k_w2.astype(BF16), cmp_pos_k.reshape(1, -1),
                         cmp_v_w1.astype(BF16), cmp_v_w2.astype(BF16), cmp_pos_v.reshape(1, -1), batch, nchunk)

    tq = min(256, seq)
    ya = _nsa(qm, kvm, kcm, vcm, gnm, fm, batch, seq, tq, min(NSA_KEY_TILE, seq), min(256, seq),
              gm_col0=(3 * d) // (HEADS_PER_GROUP * HEAD_DIM))

    dff = ffn_w_down.shape[0]
    tf = dff
    return _ffn(x, yr, ya, w_out.astype(BF16), row2(norm_mix_post), row2(norm_ffn_pre),
                ffn_w_gate_up.astype(BF16), ffn_w_down.astype(BF16), row2(norm_ffn_post),
                p, ple_w_gate.astype(BF16), row2(ple_b_gate), ple_w_proj.astype(BF16), tm, tf)


def kernel(x, p, norm_mix_pre, norm_mix_post, w_in, conv_w, conv_b, lru_wa, lru_ba, lru_wx, lru_bx, lru_lambda, cmp_pos_k, cmp_pos_v, cmp_k_w1, cmp_k_w2, cmp_v_w1, cmp_v_w2, w_out, norm_ffn_pre, norm_ffn_post, ffn_w_gate_up, ffn_w_down, ple_w_proj, ple_w_gate, ple_b_gate):
    batch, seq, d = x.shape
    depth = w_in.shape[0]
    xf = x.reshape(batch * seq, d)
    for i in range(depth):
        xf = _layer(xf, p[i].reshape(batch * seq, -1), norm_mix_pre[i], norm_mix_post[i], w_in[i], conv_w[i],
                    conv_b[i], lru_wa[i], lru_ba[i], lru_wx[i], lru_bx[i], lru_lambda[i], cmp_pos_k[i],
                    cmp_pos_v[i], cmp_k_w1[i], cmp_k_w2[i], cmp_v_w1[i], cmp_v_w2[i], w_out[i],
                    norm_ffn_pre[i], norm_ffn_post[i], ffn_w_gate_up[i], ffn_w_down[i], ple_w_proj[i],
                    ple_w_gate[i], ple_b_gate[i], batch, seq)
    return xf.reshape(batch, seq, d)
```

```python
import functools

import numpy as np
import jax
import jax.numpy as jnp
from jax import lax
from jax.experimental import pallas as pl
from jax.experimental.pallas import tpu as pltpu

CONV_WIDTH = 4
LRU_C = 8.0
N_HEADS = 16
HEAD_DIM = 64
N_KV_GROUPS = 4
HEADS_PER_GROUP = N_HEADS // N_KV_GROUPS
CMP_LEN = 32
CMP_STRIDE = 16
SEL_LEN = 64
SEL_TOPK = 16
WINDOW = 512
ALIBI_MAX_BIAS = 8.0
NORM_EPS = 1e-6

LANES = 128
SUBLANES = 8
VMEM_LIMIT_BYTES = 56 * 1024 * 1024

MASK_NEG = -1e30
LOG2_E = 1.4426950408889634
N_ALIBI_COLS = 9
PAD_COL = HEAD_DIM + N_ALIBI_COLS
WORD_BITS = 16
WORD_SHIFT = 4
SEL_SHIFT = 6
N_FORCED = 3
NEAR_BLOCKS = 20
FILL_ROWS = 512
NSA_KEY_TILE = 11 * SEL_LEN
SCAN_UNROLL = 4
NSA_TILES_PER_STEP = 4
OVERLAP_CHUNKS = 4
BF16 = jnp.bfloat16
F32 = jnp.float32


def _dot(a, b):
    return jnp.dot(a, b, preferred_element_type=F32)


def _dot_nt(a, b):
    return lax.dot_general(a, b, (((1,), (1,)), ((), ())), preferred_element_type=F32)


def _sigmoid(x):
    return 0.5 * jnp.tanh(0.5 * x) + 0.5


def _gelu_tanh(x):
    c = np.float32(np.sqrt(2.0 / np.pi))
    half = 0.5 * x
    return half + half * jnp.tanh(x * (c + (c * 0.044715) * (x * x)))


def _rmsnorm(x, g):
    ms = jnp.mean(x * x, axis=-1, keepdims=True)
    return x * lax.rsqrt(ms + NORM_EPS) * g


def _params(sem):
    return pltpu.CompilerParams(dimension_semantics=sem, vmem_limit_bytes=VMEM_LIMIT_BYTES)


def _norm_matmul_kernel(x_ref, g_ref, w_ref, o_ref):
    o_ref[...] = _dot(_rmsnorm(x_ref[...], g_ref[...]).astype(BF16), w_ref[...]).astype(o_ref.dtype)


def _norm_matmul(x, g, w, out_dtype, tm):
    t, k = x.shape
    n = w.shape[1]
    return pl.pallas_call(
        _norm_matmul_kernel,
        grid=(t // tm,),
        in_specs=[
            pl.BlockSpec((tm, k), lambda i: (i, 0)),
            pl.BlockSpec((1, k), lambda i: (0, 0)),
            pl.BlockSpec((k, n), lambda i: (0, 0)),
        ],
        out_specs=pl.BlockSpec((tm, n), lambda i: (i, 0)),
        out_shape=jax.ShapeDtypeStruct((t, n), out_dtype),
        compiler_params=_params(("parallel",)),
        name="norm_matmul",
    )(x, g, w)


def _norm_qkv_kernel(x_ref, g_ref, w_ref, q_ref, gn_ref, cv_ref, kv_ref, slab_scr):
    h = _rmsnorm(x_ref[...], g_ref[...]).astype(BF16)
    z = _dot(h, w_ref[...])
    nq = q_ref.shape[1]
    ngn = gn_ref.shape[1]
    q_ref[...] = z[:, 0:nq].astype(q_ref.dtype)
    gn_ref[...] = z[:, nq:nq + ngn]
    n_cv = cv_ref.shape[0]
    rows = cv_ref.shape[1]
    for c in range(n_cv):
        lo = nq + ngn + c * HEAD_DIM
        slab_scr[...] = z[:, lo:lo + HEAD_DIM]
        for l in range(CMP_STRIDE):
            cv_ref[c, :, l * HEAD_DIM:(l + 1) * HEAD_DIM] = (
                slab_scr[pl.ds(l, rows, stride=CMP_STRIDE), :].astype(cv_ref.dtype))
    for c in range(kv_ref.shape[0]):
        lo = nq + ngn + (n_cv + c) * HEAD_DIM
        kv_ref[c] = z[:, lo:lo + HEAD_DIM].astype(kv_ref.dtype)


def _norm_qkv(x, g, w, nq, ngn, n_cv, tm):
    t, k = x.shape
    n = w.shape[1]
    n_kv = (n - nq - ngn) // HEAD_DIM - n_cv
    return pl.pallas_call(
        _norm_qkv_kernel,
        grid=(t // tm,),
        in_specs=[
            pl.BlockSpec((tm, k), lambda i: (i, 0)),
            pl.BlockSpec((1, k), lambda i: (0, 0)),
            pl.BlockSpec((k, n), lambda i: (0, 0)),
        ],
        out_specs=[
            pl.BlockSpec((tm, nq), lambda i: (i, 0)),
            pl.BlockSpec((tm, ngn), lambda i: (i, 0)),
            pl.BlockSpec((n_cv, tm // CMP_STRIDE, CMP_STRIDE * HEAD_DIM), lambda i: (0, i, 0)),
            pl.BlockSpec((n_kv, tm, HEAD_DIM), lambda i: (0, i, 0)),
        ],
        out_shape=[
            jax.ShapeDtypeStruct((t, nq), BF16),
            jax.ShapeDtypeStruct((t, ngn), F32),
            jax.ShapeDtypeStruct((n_cv, t // CMP_STRIDE, CMP_STRIDE * HEAD_DIM), BF16),
            jax.ShapeDtypeStruct((n_kv, t, HEAD_DIM), BF16),
        ],
        scratch_shapes=[pltpu.VMEM((tm, HEAD_DIM), F32)],
        compiler_params=_params(("parallel",)),
        name="norm_qkv",
    )(x, g, w)


def _rnn_kernel(xr_ref, gr_ref, gm_ref, cw_ref, cb_ref, wa_ref, wx_ref, ba_ref, bx_ref, lam_ref,
                o_ref, xbuf, a_scr, b_scr, h_scr, *, ts):
    s_idx = pl.program_id(1)
    d = xr_ref.shape[1]
    halo = SUBLANES

    @pl.when(s_idx == 0)
    def _():
        xbuf[0:halo, :] = jnp.zeros((halo, d), F32)
        h_scr[...] = jnp.zeros(h_scr.shape, F32)

    xbuf[halo:halo + ts, :] = xr_ref[...].astype(F32)
    xc = cb_ref[...] + xbuf[halo:halo + ts, :] * cw_ref[CONV_WIDTH - 1:CONV_WIDTH, :]
    for k in range(1, CONV_WIDTH):
        xc = xc + xbuf[halo - k:halo - k + ts, :] * cw_ref[CONV_WIDTH - 1 - k:CONV_WIDTH - k, :]
    xbuf[0:halo, :] = xbuf[ts:ts + halo, :]

    xcb = xc.astype(BF16)
    n_chunks = d // LANES
    neg_sp = -LRU_C * (jnp.maximum(-lam_ref[...], 0.0) + jnp.log(1.0 + jnp.exp(-jnp.abs(lam_ref[...]))))
    for c in range(n_chunks):
        sl = slice(c * LANES, (c + 1) * LANES)
        xk = xcb[:, sl]
        r = _sigmoid(_dot(xk, wa_ref[c]) + ba_ref[:, sl])
        i = _sigmoid(_dot(xk, wx_ref[c]) + bx_ref[:, sl])
        log_a = r * neg_sp[:, sl]
        a = jnp.exp(log_a)
        a_scr[:, sl] = a
        u = 1.0 - a * a
        root = jnp.where(u > 0.0, u * lax.rsqrt(u), 0.0)
        b_scr[:, sl] = root * (i * xc[:, sl])

    row_id = lax.broadcasted_iota(jnp.int32, (SUBLANES, d), 0)

    def group(gi, h):
        base = pl.multiple_of(gi * SUBLANES, SUBLANES)
        a = a_scr[pl.ds(base, SUBLANES), :]
        b = b_scr[pl.ds(base, SUBLANES), :]
        s = 1
        while s < SUBLANES:
            reach = row_id >= s
            b = jnp.where(reach, a * pltpu.roll(b, s, 0) + b, b)
            a = jnp.where(reach, a * pltpu.roll(a, s, 0), a)
            s *= 2
        hs = a * h + b
        a_scr[pl.ds(base, SUBLANES), :] = hs
        return jnp.broadcast_to(hs[SUBLANES - 1:SUBLANES, :], (SUBLANES, d))

    h_scr[...] = lax.fori_loop(0, ts // SUBLANES, group, h_scr[...], unroll=SCAN_UNROLL)
    o_ref[...] = (_sigmoid(gm_ref[...].astype(F32)) * a_scr[...]
                  * _gelu_tanh(gr_ref[...].astype(F32))).astype(o_ref.dtype)


def _rnn(f, conv_w, conv_b, wa, wx, ba, bx, lam, batch, seq, ts):
    t = f.shape[0]
    d = conv_w.shape[1]
    ns = seq // ts
    row = lambda b, s: b * ns + s
    vec = lambda r: pl.BlockSpec((r, d), lambda b, s: (0, 0))
    return pl.pallas_call(
        functools.partial(_rnn_kernel, ts=ts),
        grid=(batch, ns),
        in_specs=[
            pl.BlockSpec((ts, d), lambda b, s: (row(b, s), 0)),
            pl.BlockSpec((ts, d), lambda b, s: (row(b, s), 1)),
            pl.BlockSpec((ts, d), lambda b, s: (row(b, s), 2)),
            vec(CONV_WIDTH), vec(1),
            pl.BlockSpec(wa.shape, lambda b, s: (0, 0, 0)),
            pl.BlockSpec(wx.shape, lambda b, s: (0, 0, 0)),
            vec(1), vec(1), vec(1),
        ],
        out_specs=pl.BlockSpec((ts, d), lambda b, s: (row(b, s), 0)),
        out_shape=jax.ShapeDtypeStruct((t, d), BF16),
        scratch_shapes=[
            pltpu.VMEM((ts + SUBLANES, d), F32),
            pltpu.VMEM((ts, d), F32),
            pltpu.VMEM((ts, d), F32),
            pltpu.VMEM((SUBLANES, d), F32),
        ],
        compiler_params=_params(("parallel", "arbitrary")),
        name="rnn_mixer",
    )(f, f, f, conv_w, conv_b, wa, wx, ba, bx, lam)


def _compress_kernel(ak_ref, av_ref, w1k_ref, w2k_ref, pk_ref, w1v_ref, w2v_ref, pv_ref, ok_ref, ov_ref):
    def one(a_ref, w1_ref, w2_ref, p_ref, o_ref):
        a = a_ref[0]
        half = a.shape[1]
        nchunk = a.shape[0]
        lo = _dot(a, w1_ref[0:half, :])
        hi = _dot(a, w1_ref[half:2 * half, :])
        pb = _dot(jnp.broadcast_to(p_ref[...], (SUBLANES, 2 * half)).astype(BF16), w1_ref[...])[0:1, :]
        h = lo + pltpu.roll(hi, nchunk - 1, 0) + pb
        o_ref[0] = _dot(_gelu_tanh(h).astype(BF16), w2_ref[...]).astype(o_ref.dtype)

    one(ak_ref, w1k_ref, w2k_ref, pk_ref, ok_ref)
    one(av_ref, w1v_ref, w2v_ref, pv_ref, ov_ref)


def _compress(kv_chunks, w1k, w2k, pk, w1v, w2v, pv, batch, nchunk):
    g = N_KV_GROUPS
    width = kv_chunks.shape[2]
    full = lambda a: pl.BlockSpec(a.shape, lambda b, gg: (0,) * a.ndim)
    out = jax.ShapeDtypeStruct((batch * g, nchunk, HEAD_DIM), BF16)
    return pl.pallas_call(
        _compress_kernel,
        grid=(batch, g),
        in_specs=[
            pl.BlockSpec((1, nchunk, width), lambda b, gg: (gg, b, 0)),
            pl.BlockSpec((1, nchunk, width), lambda b, gg: (g + gg, b, 0)),
            full(w1k), full(w2k), full(pk), full(w1v), full(w2v), full(pv),
        ],
        out_specs=[
            pl.BlockSpec((1, nchunk, HEAD_DIM), lambda b, gg: (b * g + gg, 0, 0)),
            pl.BlockSpec((1, nchunk, HEAD_DIM), lambda b, gg: (b * g + gg, 0, 0)),
        ],
        out_shape=[out, out],
        compiler_params=_params(("parallel", "parallel")),
        name="compress",
    )(kv_chunks, kv_chunks, w1k, w2k, pk, w1v, w2v, pv)


ONES_ROWS = 16


def _accumulate(s_ref, col_max, v_aug_t, state):
    m_old, acc_old = state
    m_new = jnp.maximum(m_old, col_max)
    alpha = jnp.exp2(m_old - m_new)
    e = jnp.exp2(s_ref[...] - m_new).astype(BF16)
    return m_new, alpha * acc_old + _dot(v_aug_t, e)


def _nsa_kernel(q_ref, kc_ref, vc_ref, ks_ref, vs_ref, kw_ref, vw_ref, gn_ref, gm_ref,
                qal_ref, cpos_ref, wsel_ref, o_ref, *scratch, tq, tiles_per_step, **static):
    def tile(j, carry):
        rows = pl.ds(pl.multiple_of(j * tq, tq), tq)
        _nsa_tile(pl.program_id(2) * tiles_per_step + j, q_ref.at[rows, :], kc_ref, vc_ref, ks_ref, vs_ref,
                  kw_ref, vw_ref, gn_ref.at[rows, :], gm_ref.at[rows, :], qal_ref, cpos_ref, wsel_ref,
                  o_ref.at[rows, :], *scratch, tq=tq, **static)
        return carry

    lax.fori_loop(0, tiles_per_step, tile, 0)


def _nsa_tile(qt, q_ref, kc_ref, vc_ref, ks_ref, vs_ref, kw_ref, vw_ref, gn_ref, gm_ref,
              qal_ref, cpos_ref, wsel_ref,
              o_ref, q_t, ksaug, kwaug, kcaug, kstage, vstage, s_win, s_a, s_b, win_max, o_cmp_scr, flags,
              *, tq, nk, nkw, seq, sel_chunk):
    b = pl.program_id(0)
    g = pl.program_id(1)
    r_heads = HEADS_PER_GROUP
    dh = HEAD_DIM
    m_cols = r_heads * tq
    nb = seq // SEL_LEN
    nc = kc_ref.shape[1]
    a_w = 2 * dh
    k_w = a_w + nb

    @pl.when((b == 0) & (g == 0) & (qt == 0))
    def _():
        rows = min(FILL_ROWS, seq)

        def fill(c, carry):
            off = pl.multiple_of(c * rows, rows)
            pos = off + lax.broadcasted_iota(jnp.int32, (rows, k_w), 0)
            lane = lax.broadcasted_iota(jnp.int32, (rows, k_w), 1)
            blk_of = jnp.right_shift(pos, SEL_SHIFT)
            cols = jnp.where((lane >= dh) & (lane < dh + 3), blk_of * SEL_LEN,
                             jnp.where((lane >= dh + 3) & (lane < dh + 6), pos & (SEL_LEN - 1),
                                       jnp.where(lane == a_w + blk_of, 1, 0)))
            cols = cols.astype(F32).astype(BF16)
            ksaug[pl.ds(off, rows), :] = cols
            kwaug[pl.ds(off, rows), :] = cols[:, 0:a_w]
            return carry

        lax.fori_loop(0, seq // rows, fill, 0)
        kcaug[...] = cpos_ref[...]
        vstage[...] = jnp.ones(vstage.shape, BF16)

    @pl.when(qt == 0)
    def _():
        ksaug[:, 0:dh] = ks_ref[0]
        kwaug[:, 0:dh] = kw_ref[0]
        kcaug[:, 0:dh] = kc_ref[0]
        q_t[dh:a_w, :] = qal_ref[0]

    v_rows = dh + ONES_ROWS
    ri = lax.broadcasted_iota(jnp.int32, (v_rows, 2 * dh), 0)
    ci = lax.broadcasted_iota(jnp.int32, (v_rows, 2 * dh), 1)
    pick = (((ri < dh) & (ri == ci)) | ((ri >= dh) & (ci == dh))).astype(F32).astype(BF16)
    transposed = lambda v: _dot_nt(pick[0:dh, 0:dh], v).astype(BF16)
    staged_aug_t = lambda v: _dot_nt(pick, v).astype(BF16)
    ones_rows = jnp.ones((ONES_ROWS, nkw), BF16)
    window_aug_t = lambda v: jnp.concatenate([transposed(v), ones_rows], axis=0)
    init = (jnp.full((1, m_cols), MASK_NEG, F32), jnp.zeros((v_rows, m_cols), F32))

    t0 = qt * tq
    col = lax.broadcasted_iota(jnp.int32, (1, m_cols), 1)
    trow = t0 + (col & (tq - 1))

    scale = np.float32(HEAD_DIM ** -0.5 * LOG2_E)
    qf = jnp.transpose(q_ref[...].astype(F32) * scale)
    for r in range(r_heads):
        q_t[0:dh, r * tq:(r + 1) * tq] = qf[r * dh:(r + 1) * dh, :].astype(BF16)

    n_wt = 3
    win_off = [pl.multiple_of(jnp.maximum(t0 - i * nkw, 0), nkw) for i in range(n_wt)]
    newer = (t0 + lax.broadcasted_iota(jnp.int32, (nkw, 1), 0)) <= trow

    def window_tile(i):
        s = _dot(kwaug[pl.ds(win_off[i], nkw), :], q_t[0:a_w, :])
        return s if i == 0 else s + jnp.where(t0 - i * nkw >= 0, 0.0, MASK_NEG)

    def window_scores():
        merged = jnp.where(newer, window_tile(0), window_tile(2))
        s_win[0:nkw, :] = merged
        win_max[0:1, :] = jnp.max(merged, axis=0, keepdims=True)
        mid = window_tile(1)
        s_win[nkw:2 * nkw, :] = mid
        win_max[1:2, :] = jnp.max(mid, axis=0, keepdims=True)

    def compressed_and_select(rows_c, rows_b):
        free = max(rows_c - cmp_chunk - tq // CMP_STRIDE, 0)
        s = _dot(kcaug[0:rows_c, :], q_t[0:a_w, :])
        cend = (free + lax.broadcasted_iota(jnp.int32, (rows_c - free, 1), 0)) * CMP_STRIDE + (CMP_LEN - 1)
        tail = jnp.where(cend <= trow, s[free:], MASK_NEG)
        s = jnp.concatenate([s[0:free], tail], axis=0) if free else tail
        m = jnp.max(s, axis=0, keepdims=True)
        e = jnp.exp2(s - m)
        has_key = (trow >= CMP_LEN - 1).astype(F32)
        p = e * (has_key / jnp.sum(e, axis=0, keepdims=True))
        o_cmp_scr[...] = _dot(transposed(vc_ref[0, 0:rows_c, :]), p.astype(BF16))
        imp = p[:, 0:tq]
        for r in range(1, r_heads):
            imp = imp + p[:, r * tq:(r + 1) * tq]

        hi = imp.astype(BF16)
        lo = (imp - hi.astype(F32)).astype(BF16)
        wsel = wsel_ref[0:rows_b, 0:rows_c]
        imp_t = _dot(wsel, hi) + _dot(wsel, lo)

        window_scores()

        blk = lax.broadcasted_iota(jnp.int32, (rows_b, tq), 0).astype(F32)
        tq_l = t0 + lax.broadcasted_iota(jnp.int32, (rows_b, tq), 1)
        cur = jnp.right_shift(tq_l, SEL_SHIFT).astype(F32)
        valid = blk <= cur
        sel = jnp.where(blk == 0.0, 1.0, jnp.where(blk == cur, 1.0, jnp.where(blk == cur - 1.0, 1.0, 0.0)))
        score = jnp.where(valid, jnp.where(sel > 0.0, -1.0, imp_t), -1.0)
        for _ in range(min(SEL_TOPK, nb) - N_FORCED):
            mx = jnp.max(score, axis=0, keepdims=True)
            idx = jnp.min(jnp.where(score == mx, blk, float(nb)), axis=0, keepdims=True)
            hit = blk == idx
            sel = jnp.where(hit, 1.0, sel)
            score = jnp.where(hit, -2.0, score)
        selv = jnp.where(valid, sel, 0.0) > 0.0
        selneg_t = jnp.where(selv, 0.0, MASK_NEG).astype(BF16)
        unseen = jnp.full((nb - rows_b, tq), MASK_NEG, BF16)
        for r in range(r_heads):
            q_t[a_w:a_w + rows_b, r * tq:(r + 1) * tq] = selneg_t
            if rows_b < nb:
                q_t[a_w + rows_b:, r * tq:(r + 1) * tq] = unseen

        used = jnp.max(jnp.where(selv, 1.0, 0.0), axis=1, keepdims=True)
        bit_id = lax.broadcasted_iota(jnp.int32, (rows_b, 1), 0) & (WORD_BITS - 1)
        weighted = used * jnp.left_shift(1, bit_id).astype(F32)
        for i in range(nb // WORD_BITS):
            if (i + 1) * WORD_BITS <= rows_b:
                flags[i] = jnp.sum(weighted[i * WORD_BITS:(i + 1) * WORD_BITS, :]).astype(jnp.int32)
            else:
                flags[i] = 0

    n_var = nb // sel_chunk
    cmp_chunk = sel_chunk * (SEL_LEN // CMP_STRIDE)
    seen_c = (t0 + tq - CMP_LEN) // CMP_STRIDE + 1
    seen_b = (t0 + tq) // SEL_LEN
    variant = jnp.maximum((seen_c + cmp_chunk - 1) // cmp_chunk, (seen_b + sel_chunk - 1) // sel_chunk)
    for v in range(1, n_var + 1):
        pl.when(variant == v)(functools.partial(compressed_and_select, min(v * cmp_chunk, nc), v * sel_chunk))
    o_cmp = o_cmp_scr[...]

    n_own = tq // SEL_LEN
    off_q = pl.multiple_of(t0, tq)
    kstage[0:tq, :] = ksaug[pl.ds(off_q, tq), :]
    vstage[0:tq, 0:dh] = vs_ref[0, pl.ds(off_q, tq), :]

    def stage_block(j, cnt):
        src = pl.multiple_of(j * SEL_LEN, SEL_LEN)
        dst = pl.multiple_of(cnt * SEL_LEN, SEL_LEN)
        kstage[pl.ds(dst, SEL_LEN), :] = ksaug[pl.ds(src, SEL_LEN), :]
        vstage[pl.ds(dst, SEL_LEN), 0:dh] = vs_ref[0, pl.ds(src, SEL_LEN), :]

    def gather(j, cnt):
        bit = jnp.right_shift(flags[jnp.right_shift(j, WORD_SHIFT)], j & (WORD_BITS - 1)) & 1
        pl.when(bit == 1)(functools.partial(stage_block, j, cnt))
        return cnt + bit

    n_past = jnp.right_shift(t0, SEL_SHIFT)
    near_lo = jnp.maximum(n_past - NEAR_BLOCKS, 0)
    first = jnp.where(near_lo > 0, flags[0] & 1, 0)
    pl.when(first == 1)(functools.partial(stage_block, 0, n_own))

    def far_word(w, cnt):
        lo = jnp.maximum(w * WORD_BITS, 1)
        hi = jnp.minimum((w + 1) * WORD_BITS, near_lo)
        rest = jnp.where(w == 0, flags[w] & -2, flags[w])
        return lax.cond(rest != 0, lambda c: lax.fori_loop(lo, hi, gather, c), lambda c: c, cnt)

    n_blocks = lax.fori_loop(0, jnp.right_shift(near_lo + WORD_BITS - 1, WORD_SHIFT), far_word, n_own + first)
    n_blocks = lax.fori_loop(near_lo, n_past, gather, n_blocks)
    per_tile = nk // SEL_LEN
    n_tiles = (n_blocks + per_tile - 1) // per_tile
    pad_block = jnp.where(lax.broadcasted_iota(jnp.int32, (SEL_LEN, k_w), 1) == PAD_COL, 1.0, 0.0).astype(BF16)

    def pad(j, carry):
        dst = pl.multiple_of(j * SEL_LEN, SEL_LEN)
        kstage[pl.ds(dst, SEL_LEN), :] = pad_block
        return carry

    lax.fori_loop(n_blocks, n_tiles * per_tile, pad, 0)

    own_pos = t0 + lax.broadcasted_iota(jnp.int32, (tq, 1), 0)
    s_first = _dot(kstage[0:nk, :], q_t[...])
    own = jnp.where(own_pos <= trow, s_first[0:tq], MASK_NEG)
    s_a[0:tq, :] = own
    s_a[tq:, :] = s_first[tq:]
    max_a = jnp.maximum(jnp.max(own, axis=0, keepdims=True), jnp.max(s_first[tq:], axis=0, keepdims=True))

    m_w = jnp.maximum(win_max[0:1, :], win_max[1:2, :])
    e_merged = jnp.exp2(s_win[0:nkw, :] - m_w)
    e_mid = jnp.exp2(s_win[nkw:2 * nkw, :] - m_w)
    weights = jnp.concatenate([jnp.where(newer, e_merged, 0.0).astype(BF16), e_mid.astype(BF16),
                               jnp.where(newer, 0.0, e_merged).astype(BF16)], axis=0)
    v_win = jnp.concatenate([window_aug_t(vw_ref[0, pl.ds(win_off[i], nkw), :]) for i in range(n_wt)], axis=1)
    acc_w = _dot(v_win, weights)
    o_win = acc_w[0:dh] / acc_w[dh:dh + 1]

    def tile_accumulate(i, s_ref, col_max, state):
        return _accumulate(s_ref, col_max, staged_aug_t(vstage[pl.ds(pl.multiple_of(i * nk, nk), nk), :]), state)

    def scores_and_accumulate(i_next, s_next_ref, i_cur, s_cur_ref, max_cur, state):
        m_old, acc_old = state
        m_new = jnp.maximum(m_old, max_cur)
        alpha = jnp.exp2(m_old - m_new)
        s_next = _dot(kstage[pl.ds(pl.multiple_of(i_next * nk, nk), nk), :], q_t[...])
        s_next_ref[...] = s_next
        rows = nk // OVERLAP_CHUNKS
        partial, weights = [], []
        for c in range(OVERLAP_CHUNKS):
            rs = slice(c * rows, (c + 1) * rows)
            partial.append(jnp.max(s_next[rs], axis=0, keepdims=True))
            m_c = m_new + partial[c] * 0.0
            weights.append(jnp.exp2(s_cur_ref[rs, :] - m_c).astype(BF16))
        v_t = staged_aug_t(vstage[pl.ds(pl.multiple_of(i_cur * nk, nk), nk), :])
        acc = alpha * acc_old + _dot(v_t, jnp.concatenate(weights, axis=0))
        return (m_new, acc), functools.reduce(jnp.maximum, partial)

    def slc_pair(j, carry):
        state, max_a = carry
        i = 2 * j
        state, max_b = scores_and_accumulate(i + 1, s_b, i, s_a, max_a, state)
        state, max_a = scores_and_accumulate(i + 2, s_a, i + 1, s_b, max_b, state)
        return state, max_a

    n_pairs = (n_tiles - 1) // 2
    state, max_a = lax.fori_loop(0, n_pairs, slc_pair, (init, max_a))
    i_a = 2 * n_pairs

    def two_left(state):
        state, max_b = scores_and_accumulate(i_a + 1, s_b, i_a, s_a, max_a, state)
        return tile_accumulate(i_a + 1, s_b, max_b, state)

    def one_left(state):
        return tile_accumulate(i_a, s_a, max_a, state)

    _, acc = lax.cond(n_tiles - i_a == 2, two_left, one_left, state)
    o_slc = acc[0:dh] / acc[dh:dh + 1]

    gates = jnp.transpose(_sigmoid(gn_ref[...]))
    heads = []
    for r in range(r_heads):
        cs = slice(r * tq, (r + 1) * tq)
        heads.append(gates[3 * r:3 * r + 1, :] * o_cmp[:, cs] + gates[3 * r + 1:3 * r + 2, :] * o_slc[:, cs]
                     + gates[3 * r + 2:3 * r + 3, :] * o_win[:, cs])
    o = jnp.transpose(jnp.concatenate(heads, axis=0))
    o_ref[...] = (_sigmoid(gm_ref[...].astype(F32)) * o).astype(o_ref.dtype)


def _alibi_tables(seq, nc, tq):
    import ml_dtypes
    bf = ml_dtypes.bfloat16
    h = np.arange(1, N_HEADS + 1, dtype=np.float32)
    slopes = (np.exp2(-ALIBI_MAX_BIAS * h / N_HEADS) * LOG2_E).astype(np.float32)
    s1 = slopes.astype(bf).astype(np.float32)
    s2 = (slopes - s1).astype(bf).astype(np.float32)
    s3 = (slopes - s1 - s2).astype(bf).astype(np.float32)
    dh = HEAD_DIM
    qal = np.zeros((N_HEADS, dh), np.float32)
    for rep in range(3):
        qal[:, 3 * rep + 0] = s1
        qal[:, 3 * rep + 1] = s2
        qal[:, 3 * rep + 2] = s3
    qal[:, PAD_COL - dh] = MASK_NEG
    qal = qal.reshape(N_KV_GROUPS, HEADS_PER_GROUP, dh).transpose(0, 2, 1)
    qal_p = np.repeat(qal, tq, axis=2)

    nb = seq // SEL_LEN
    c = np.arange(nc)
    cpos = np.zeros((nc, 2 * dh), np.float32)
    cpos[:, dh:dh + 3] = ((c // 16) * 16 * CMP_STRIDE)[:, None]
    cpos[:, dh + 3:dh + 6] = ((c % 16) * CMP_STRIDE)[:, None]
    cpos[:, dh + 6:dh + 9] = CMP_LEN - 1

    r_sel = SEL_LEN // CMP_STRIDE
    r_cmp = CMP_LEN // CMP_STRIDE
    wsel = np.zeros((nb, nc), np.float32)
    for j in range(nb):
        for mm in range(r_sel):
            for nn in range(r_cmp):
                ci = r_sel * j + mm - nn
                if 0 <= ci < nc - 1:
                    wsel[j, ci] += 1.0
    as_bf = lambda a: jnp.asarray(a.astype(bf))
    return as_bf(qal_p), as_bf(cpos), as_bf(wsel)


def _nsa(qm, kvm, kcm, vcm, gnm, fm, batch, seq, tq, nk, nkw, gm_col0):
    t = qm.shape[0]
    g = N_KV_GROUPS
    dh = HEAD_DIM
    nqt = seq // tq
    nc = kcm.shape[1]
    nb = seq // SEL_LEN
    gw = HEADS_PER_GROUP * dh
    m_cols = HEADS_PER_GROUP * tq
    qal, cpos, wsel = _alibi_tables(seq, nc, tq)
    assert nkw == tq and WINDOW == 2 * tq, "the window branch merges the new and old key tiles of a query tile"
    n_wt = 2
    sel_chunk = max(WORD_BITS, nb // 4)
    per_tile = nk // SEL_LEN
    stage_rows = -(-nb // per_tile) * per_tile * SEL_LEN
    tps = NSA_TILES_PER_STEP if nqt % NSA_TILES_PER_STEP == 0 else 1
    n_steps = nqt // tps
    rq = tps * tq
    row = lambda b, gg, i: b * n_steps + i
    slab = lambda base: pl.BlockSpec((1, seq, dh), lambda b, gg, i: (base + gg, b, 0))
    cmp_slab = pl.BlockSpec((1, nc, dh), lambda b, gg, i: (b * g + gg, 0, 0))
    const = lambda a: pl.BlockSpec(a.shape, lambda b, gg, i: (0,) * a.ndim)
    return pl.pallas_call(
        functools.partial(_nsa_kernel, tq=tq, tiles_per_step=tps, nk=nk, nkw=nkw, seq=seq, sel_chunk=sel_chunk),
        grid=(batch, g, n_steps),
        in_specs=[
            pl.BlockSpec((rq, gw), lambda b, gg, i: (row(b, gg, i), gg)),
            cmp_slab, cmp_slab,
            slab(0), slab(g), slab(2 * g), slab(3 * g),
            pl.BlockSpec((rq, LANES), lambda b, gg, i: (row(b, gg, i), gg)),
            pl.BlockSpec((rq, gw), lambda b, gg, i: (row(b, gg, i), gm_col0 + gg)),
            pl.BlockSpec((1, dh, m_cols), lambda b, gg, i: (gg, 0, 0)),
            const(cpos), const(wsel),
        ],
        out_specs=pl.BlockSpec((rq, gw), lambda b, gg, i: (row(b, gg, i), gg)),
        out_shape=jax.ShapeDtypeStruct((t, g * gw), BF16),
        scratch_shapes=[
            pltpu.VMEM((2 * dh + nb, m_cols), BF16),
            pltpu.VMEM((seq, 2 * dh + nb), BF16),
            pltpu.VMEM((seq, 2 * dh), BF16),
            pltpu.VMEM((nc, 2 * dh), BF16),
            pltpu.VMEM((stage_rows, 2 * dh + nb), BF16),
            pltpu.VMEM((stage_rows, 2 * dh), BF16),
            pltpu.VMEM((n_wt * nkw, m_cols), F32),
            pltpu.VMEM((nk, m_cols), F32),
            pltpu.VMEM((nk, m_cols), F32),
            pltpu.VMEM((SUBLANES, m_cols), F32),
            pltpu.VMEM((dh, m_cols), F32),
            pltpu.SMEM((nb // WORD_BITS,), jnp.int32),
        ],
        compiler_params=_params(("arbitrary", "arbitrary", "arbitrary")),
        name="nsa",
    )(qm, kcm, vcm, kvm, kvm, kvm, kvm, gnm, fm, qal, cpos, wsel)


def _ffn_kernel(x_ref, yr_ref, ya_ref, wo_ref, gmix_ref, gpre_ref, wg_ref, wu_ref, wd_ref, gpost_ref,
                p_ref, wpg_ref, bpg_ref, wpp_ref, o_ref, x1_ref, h_ref, acc_ref):
    j = pl.program_id(1)

    @pl.when(j == 0)
    def _():
        y = (yr_ref[...].astype(F32) + ya_ref[...].astype(F32)).astype(BF16)
        x1 = x_ref[...] + _rmsnorm(_dot(y, wo_ref[...]), gmix_ref[...])
        x1_ref[...] = x1
        h_ref[...] = _rmsnorm(x1, gpre_ref[...]).astype(BF16)
        acc_ref[...] = jnp.zeros(acc_ref.shape, F32)

    h = h_ref[...]
    gate = _dot(h, wg_ref[...])
    up = _dot(h, wu_ref[...])
    act = (gate * _sigmoid(gate) * up).astype(BF16)
    acc_ref[...] += _dot(act, wd_ref[...])

    @pl.when(j == pl.num_programs(1) - 1)
    def _():
        x2 = x1_ref[...] + _rmsnorm(acc_ref[...], gpost_ref[...])
        gate_p = _sigmoid(_dot(x2.astype(BF16), wpg_ref[...]) + bpg_ref[...])
        o_ref[...] = x2 + gate_p * _dot(p_ref[...].astype(BF16), wpp_ref[...])


def _ffn(x, yr, ya, wo, gmix, gpre, wgu, wd, gpost, p, wpg, bpg, wpp, tm, tf):
    t, d = x.shape
    dff = wd.shape[0]
    nf = dff // tf
    dp = p.shape[1]
    rows = pl.BlockSpec((tm, d), lambda i, j: (i, 0))
    vec = pl.BlockSpec((1, d), lambda i, j: (0, 0))
    square = pl.BlockSpec((d, d), lambda i, j: (0, 0))
    return pl.pallas_call(
        _ffn_kernel,
        grid=(t // tm, nf),
        in_specs=[
            rows, rows, rows, square, vec, vec,
            pl.BlockSpec((d, tf), lambda i, j: (0, j)),
            pl.BlockSpec((d, tf), lambda i, j: (0, nf + j)),
            pl.BlockSpec((tf, d), lambda i, j: (j, 0)),
            vec,
            pl.BlockSpec((tm, dp), lambda i, j: (i, 0)),
            square,
            vec,
            pl.BlockSpec((dp, d), lambda i, j: (0, 0)),
        ],
        out_specs=rows,
        out_shape=jax.ShapeDtypeStruct((t, d), F32),
        scratch_shapes=[pltpu.VMEM((tm, d), F32), pltpu.VMEM((tm, d), BF16), pltpu.VMEM((tm, d), F32)],
        compiler_params=_params(("parallel", "arbitrary")),
        name="out_ffn_ple",
    )(x, yr, ya, wo, gmix, gpre, wgu, wgu, wd, gpost, p, wpg, bpg, wpp)


def _block_diag_chunks(w):
    n, bs, _ = w.shape
    per = LANES // bs
    w = w.reshape(n // per, per, bs, bs)
    eye = jnp.eye(per, dtype=w.dtype)
    return jnp.einsum('cpij,pq->cpiqj', w, eye).reshape(n // per, LANES, LANES)


def _layer(x, p, norm_mix_pre, norm_mix_post, w_in, conv_w, conv_b, lru_wa, lru_ba, lru_wx, lru_bx,
           lru_lambda, cmp_pos_k, cmp_pos_v, cmp_k_w1, cmp_k_w2, cmp_v_w1, cmp_v_w2, w_out,
           norm_ffn_pre, norm_ffn_post, ffn_w_gate_up, ffn_w_down, ple_w_proj, ple_w_gate, ple_b_gate,
           batch, seq):
    t, d = x.shape
    d_attn = N_HEADS * HEAD_DIM
    d_kv = N_KV_GROUPS * HEAD_DIM
    row2 = lambda v: v.reshape(1, -1)

    o_q = 2 * d
    o_kv = o_q + d_attn
    o_gn = o_kv + 6 * d_kv
    o_gm = o_gn + 3 * N_HEADS
    w_f = jnp.concatenate([w_in[:, 0:o_q], w_in[:, o_gm:o_gm + 2 * d]], axis=1).astype(BF16)
    w_q = w_in[:, o_q:o_kv].astype(BF16)
    w_kv = w_in[:, o_kv:o_gn].astype(BF16)
    per_g = 3 * HEADS_PER_GROUP
    w_gn = w_in[:, o_gn:o_gm].reshape(d, N_KV_GROUPS, per_g)
    w_gn = jnp.pad(w_gn, ((0, 0), (0, 0), (0, LANES - per_g))).reshape(d, N_KV_GROUPS * LANES).astype(BF16)

    g_pre = row2(norm_mix_pre)
    tm = min(512, t)
    fm = _norm_matmul(x, g_pre, w_f, BF16, tm)
    qm, gnm, kv_chunks, kvm = _norm_qkv(x, g_pre, jnp.concatenate([w_q, w_gn, w_kv], axis=1), d_attn,
                                        N_KV_GROUPS * LANES, 2 * N_KV_GROUPS, tm)

    yr = _rnn(fm, conv_w, row2(conv_b), _block_diag_chunks(lru_wa).astype(BF16),
              _block_diag_chunks(lru_wx).astype(BF16), row2(lru_ba), row2(lru_bx), row2(lru_lambda),
              batch, seq, min(512, seq))

    nchunk = seq // CMP_STRIDE
    kcm, vcm = _compress(kv_chunks, cmp_k_w1.astype(BF16), cmp_k_w2.astype(BF16), cmp_pos_k.reshape(1, -1),
                         cmp_v_w1.astype(BF16), cmp_v_w2.astype(BF16), cmp_pos_v.reshape(1, -1), batch, nchunk)

    tq = min(256, seq)
    ya = _nsa(qm, kvm, kcm, vcm, gnm, fm, batch, seq, tq, min(NSA_KEY_TILE, seq), min(256, seq),
              gm_col0=(3 * d) // (HEADS_PER_GROUP * HEAD_DIM))

    dff = ffn_w_down.shape[0]
    tf = dff
    return _ffn(x, yr, ya, w_out.astype(BF16), row2(norm_mix_post), row2(norm_ffn_pre),
                ffn_w_gate_up.astype(BF16), ffn_w_down.astype(BF16), row2(norm_ffn_post),
                p, ple_w_gate.astype(BF16), row2(ple_b_gate), ple_w_proj.astype(BF16), tm, tf)


def kernel(x, p, norm_mix_pre, norm_mix_post, w_in, conv_w, conv_b, lru_wa, lru_ba, lru_wx, lru_bx, lru_lambda, cmp_pos_k, cmp_pos_v, cmp_k_w1, cmp_k_w2, cmp_v_w1, cmp_v_w2, w_out, norm_ffn_pre, norm_ffn_post, ffn_w_gate_up, ffn_w_down, ple_w_proj, ple_w_gate, ple_b_gate):
    batch, seq, d = x.shape
    depth = w_in.shape[0]
    xf = x.reshape(batch * seq, d)
    for i in range(depth):
        xf = _layer(xf, p[i].reshape(batch * seq, -1), norm_mix_pre[i], norm_mix_post[i], w_in[i], conv_w[i],
                    conv_b[i], lru_wa[i], lru_ba[i], lru_wx[i], lru_bx[i], lru_lambda[i], cmp_pos_k[i],
                    cmp_pos_v[i], cmp_k_w1[i], cmp_k_w2[i], cmp_v_w1[i], cmp_v_w2[i], w_out[i],
                    norm_ffn_pre[i], norm_ffn_post[i], ffn_w_gate_up[i], ffn_w_down[i], ple_w_proj[i],
                    ple_w_gate[i], ple_b_gate[i], batch, seq)
    return xf.reshape(batch, seq, d)
```

```python
import functools

import numpy as np
import jax
import jax.numpy as jnp
from jax import lax
from jax.experimental import pallas as pl
from jax.experimental.pallas import tpu as pltpu

CONV_WIDTH = 4
LRU_C = 8.0
N_HEADS = 16
HEAD_DIM = 64
N_KV_GROUPS = 4
HEADS_PER_GROUP = N_HEADS // N_KV_GROUPS
CMP_LEN = 32
CMP_STRIDE = 16
SEL_LEN = 64
SEL_TOPK = 16
WINDOW = 512
ALIBI_MAX_BIAS = 8.0
NORM_EPS = 1e-6

LANES = 128
SUBLANES = 8
VMEM_LIMIT_BYTES = 56 * 1024 * 1024

MASK_NEG = -1e30
LOG2_E = 1.4426950408889634
N_ALIBI_COLS = 9
PAD_COL = HEAD_DIM + N_ALIBI_COLS
WORD_BITS = 16
WORD_SHIFT = 4
SEL_SHIFT = 6
N_FORCED = 3
NEAR_BLOCKS = 20
FILL_ROWS = 512
NSA_KEY_TILE = 11 * SEL_LEN
SCAN_UNROLL = 4
NSA_TILES_PER_STEP = 4
OVERLAP_CHUNKS = 4
BF16 = jnp.bfloat16
F32 = jnp.float32


def _dot(a, b):
    return jnp.dot(a, b, preferred_element_type=F32)


def _dot_nt(a, b):
    return lax.dot_general(a, b, (((1,), (1,)), ((), ())), preferred_element_type=F32)


def _sigmoid(x):
    return 0.5 * jnp.tanh(0.5 * x) + 0.5


def _gelu_tanh(x):
    c = np.float32(np.sqrt(2.0 / np.pi))
    half = 0.5 * x
    return half + half * jnp.tanh(x * (c + (c * 0.044715) * (x * x)))


def _rmsnorm(x, g):
    ms = jnp.mean(x * x, axis=-1, keepdims=True)
    return x * lax.rsqrt(ms + NORM_EPS) * g


def _params(sem):
    return pltpu.CompilerParams(dimension_semantics=sem, vmem_limit_bytes=VMEM_LIMIT_BYTES)


def _norm_matmul_kernel(x_ref, g_ref, w_ref, o_ref):
    o_ref[...] = _dot(_rmsnorm(x_ref[...], g_ref[...]).astype(BF16), w_ref[...]).astype(o_ref.dtype)


def _norm_matmul(x, g, w, out_dtype, tm):
    t, k = x.shape
    n = w.shape[1]
    return pl.pallas_call(
        _norm_matmul_kernel,
        grid=(t // tm,),
        in_specs=[
            pl.BlockSpec((tm, k), lambda i: (i, 0)),
            pl.BlockSpec((1, k), lambda i: (0, 0)),
            pl.BlockSpec((k, n), lambda i: (0, 0)),
        ],
        out_specs=pl.BlockSpec((tm, n), lambda i: (i, 0)),
        out_shape=jax.ShapeDtypeStruct((t, n), out_dtype),
        compiler_params=_params(("parallel",)),
        name="norm_matmul",
    )(x, g, w)


def _norm_qkv_kernel(x_ref, g_ref, w_ref, q_ref, gn_ref, cv_ref, kv_ref, slab_scr):
    h = _rmsnorm(x_ref[...], g_ref[...]).astype(BF16)
    z = _dot(h, w_ref[...])
    nq = q_ref.shape[1]
    ngn = gn_ref.shape[1]
    q_ref[...] = z[:, 0:nq].astype(q_ref.dtype)
    gn_ref[...] = z[:, nq:nq + ngn]
    n_cv = cv_ref.shape[0]
    rows = cv_ref.shape[1]
    for c in range(n_cv):
        lo = nq + ngn + c * HEAD_DIM
        slab_scr[...] = z[:, lo:lo + HEAD_DIM]
        for l in range(CMP_STRIDE):
            cv_ref[c, :, l * HEAD_DIM:(l + 1) * HEAD_DIM] = (
                slab_scr[pl.ds(l, rows, stride=CMP_STRIDE), :].astype(cv_ref.dtype))
    for c in range(kv_ref.shape[0]):
        lo = nq + ngn + (n_cv + c) * HEAD_DIM
        kv_ref[c] = z[:, lo:lo + HEAD_DIM].astype(kv_ref.dtype)


def _norm_qkv(x, g, w, nq, ngn, n_cv, tm):
    t, k = x.shape
    n = w.shape[1]
    n_kv = (n - nq - ngn) // HEAD_DIM - n_cv
    return pl.pallas_call(
        _norm_qkv_kernel,
        grid=(t // tm,),
        in_specs=[
            pl.BlockSpec((tm, k), lambda i: (i, 0)),
            pl.BlockSpec((1, k), lambda i: (0, 0)),
            pl.BlockSpec((k, n), lambda i: (0, 0)),
        ],
        out_specs=[
            pl.BlockSpec((tm, nq), lambda i: (i, 0)),
            pl.BlockSpec((tm, ngn), lambda i: (i, 0)),
            pl.BlockSpec((n_cv, tm // CMP_STRIDE, CMP_STRIDE * HEAD_DIM), lambda i: (0, i, 0)),
            pl.BlockSpec((n_kv, tm, HEAD_DIM), lambda i: (0, i, 0)),
        ],
        out_shape=[
            jax.ShapeDtypeStruct((t, nq), BF16),
            jax.ShapeDtypeStruct((t, ngn), F32),
            jax.ShapeDtypeStruct((n_cv, t // CMP_STRIDE, CMP_STRIDE * HEAD_DIM), BF16),
            jax.ShapeDtypeStruct((n_kv, t, HEAD_DIM), BF16),
        ],
        scratch_shapes=[pltpu.VMEM((tm, HEAD_DIM), F32)],
        compiler_params=_params(("parallel",)),
        name="norm_qkv",
    )(x, g, w)


def _rnn_kernel(xr_ref, gr_ref, gm_ref, cw_ref, cb_ref, wa_ref, wx_ref, ba_ref, bx_ref, lam_ref,
                o_ref, xbuf, a_scr, b_scr, h_scr, *, ts):
    s_idx = pl.program_id(1)
    d = xr_ref.shape[1]
    halo = SUBLANES

    @pl.when(s_idx == 0)
    def _():
        xbuf[0:halo, :] = jnp.zeros((halo, d), F32)
        h_scr[...] = jnp.zeros(h_scr.shape, F32)

    xbuf[halo:halo + ts, :] = xr_ref[...].astype(F32)
    xc = cb_ref[...] + xbuf[halo:halo + ts, :] * cw_ref[CONV_WIDTH - 1:CONV_WIDTH, :]
    for k in range(1, CONV_WIDTH):
        xc = xc + xbuf[halo - k:halo - k + ts, :] * cw_ref[CONV_WIDTH - 1 - k:CONV_WIDTH - k, :]
    xbuf[0:halo, :] = xbuf[ts:ts + halo, :]

    xcb = xc.astype(BF16)
    n_chunks = d // LANES
    neg_sp = -LRU_C * (jnp.maximum(-lam_ref[...], 0.0) + jnp.log(1.0 + jnp.exp(-jnp.abs(lam_ref[...]))))
    for c in range(n_chunks):
        sl = slice(c * LANES, (c + 1) * LANES)
        xk = xcb[:, sl]
        r = _sigmoid(_dot(xk, wa_ref[c]) + ba_ref[:, sl])
        i = _sigmoid(_dot(xk, wx_ref[c]) + bx_ref[:, sl])
        log_a = r * neg_sp[:, sl]
        a = jnp.exp(log_a)
        a_scr[:, sl] = a
        u = 1.0 - a * a
        root = jnp.where(u > 0.0, u * lax.rsqrt(u), 0.0)
        b_scr[:, sl] = root * (i * xc[:, sl])

    row_id = lax.broadcasted_iota(jnp.int32, (SUBLANES, d), 0)

    def group(gi, h):
        base = pl.multiple_of(gi * SUBLANES, SUBLANES)
        a = a_scr[pl.ds(base, SUBLANES), :]
        b = b_scr[pl.ds(base, SUBLANES), :]
        s = 1
        while s < SUBLANES:
            reach = row_id >= s
            b = jnp.where(reach, a * pltpu.roll(b, s, 0) + b, b)
            a = jnp.where(reach, a * pltpu.roll(a, s, 0), a)
            s *= 2
        hs = a * h + b
        a_scr[pl.ds(base, SUBLANES), :] = hs
        return jnp.broadcast_to(hs[SUBLANES - 1:SUBLANES, :], (SUBLANES, d))

    h_scr[...] = lax.fori_loop(0, ts // SUBLANES, group, h_scr[...], unroll=SCAN_UNROLL)
    o_ref[...] = (_sigmoid(gm_ref[...].astype(F32)) * a_scr[...]
                  * _gelu_tanh(gr_ref[...].astype(F32))).astype(o_ref.dtype)


def _rnn(f, conv_w, conv_b, wa, wx, ba, bx, lam, batch, seq, ts):
    t = f.shape[0]
    d = conv_w.shape[1]
    ns = seq // ts
    row = lambda b, s: b * ns + s
    vec = lambda r: pl.BlockSpec((r, d), lambda b, s: (0, 0))
    return pl.pallas_call(
        functools.partial(_rnn_kernel, ts=ts),
        grid=(batch, ns),
        in_specs=[
            pl.BlockSpec((ts, d), lambda b, s: (row(b, s), 0)),
            pl.BlockSpec((ts, d), lambda b, s: (row(b, s), 1)),
            pl.BlockSpec((ts, d), lambda b, s: (row(b, s), 2)),
            vec(CONV_WIDTH), vec(1),
            pl.BlockSpec(wa.shape, lambda b, s: (0, 0, 0)),
            pl.BlockSpec(wx.shape, lambda b, s: (0, 0, 0)),
            vec(1), vec(1), vec(1),
        ],
        out_specs=pl.BlockSpec((ts, d), lambda b, s: (row(b, s), 0)),
        out_shape=jax.ShapeDtypeStruct((t, d), BF16),
        scratch_shapes=[
            pltpu.VMEM((ts + SUBLANES, d), F32),
            pltpu.VMEM((ts, d), F32),
            pltpu.VMEM((ts, d), F32),
            pltpu.VMEM((SUBLANES, d), F32),
        ],
        compiler_params=_params(("parallel", "arbitrary")),
        name="rnn_mixer",
    )(f, f, f, conv_w, conv_b, wa, wx, ba, bx, lam)


def _compress_kernel(ak_ref, av_ref, w1k_ref, w2k_ref, pk_ref, w1v_ref, w2v_ref, pv_ref, ok_ref, ov_ref):
    def one(a_ref, w1_ref, w2_ref, p_ref, o_ref):
        a = a_ref[0]
        half = a.shape[1]
        nchunk = a.shape[0]
        lo = _dot(a, w1_ref[0:half, :])
        hi = _dot(a, w1_ref[half:2 * half, :])
        pb = _dot(jnp.broadcast_to(p_ref[...], (SUBLANES, 2 * half)).astype(BF16), w1_ref[...])[0:1, :]
        h = lo + pltpu.roll(hi, nchunk - 1, 0) + pb
        o_ref[0] = _dot(_gelu_tanh(h).astype(BF16), w2_ref[...]).astype(o_ref.dtype)

    one(ak_ref, w1k_ref, w2k_ref, pk_ref, ok_ref)
    one(av_ref, w1v_ref, w2v_ref, pv_ref, ov_ref)


def _compress(kv_chunks, w1k, w2k, pk, w1v, w2v, pv, batch, nchunk):
    g = N_KV_GROUPS
    width = kv_chunks.shape[2]
    full = lambda a: pl.BlockSpec(a.shape, lambda b, gg: (0,) * a.ndim)
    out = jax.ShapeDtypeStruct((batch * g, nchunk, HEAD_DIM), BF16)
    return pl.pallas_call(
        _compress_kernel,
        grid=(batch, g),
        in_specs=[
            pl.BlockSpec((1, nchunk, width), lambda b, gg: (gg, b, 0)),
            pl.BlockSpec((1, nchunk, width), lambda b, gg: (g + gg, b, 0)),
            full(w1k), full(w2k), full(pk), full(w1v), full(w2v), full(pv),
        ],
        out_specs=[
            pl.BlockSpec((1, nchunk, HEAD_DIM), lambda b, gg: (b * g + gg, 0, 0)),
            pl.BlockSpec((1, nchunk, HEAD_DIM), lambda b, gg: (b * g + gg, 0, 0)),
        ],
        out_shape=[out, out],
        compiler_params=_params(("parallel", "parallel")),
        name="compress",
    )(kv_chunks, kv_chunks, w1k, w2k, pk, w1v, w2v, pv)


ONES_ROWS = 16


def _accumulate(s_ref, col_max, v_aug_t, state):
    m_old, acc_old = state
    m_new = jnp.maximum(m_old, col_max)
    alpha = jnp.exp2(m_old - m_new)
    e = jnp.exp2(s_ref[...] - m_new).astype(BF16)
    return m_new, alpha * acc_old + _dot(v_aug_t, e)


def _nsa_kernel(q_ref, kc_ref, vc_ref, ks_ref, vs_ref, kw_ref, vw_ref, gn_ref, gm_ref,
                qal_ref, cpos_ref, wsel_ref, o_ref, *scratch, tq, tiles_per_step, **static):
    def tile(j, carry):
        rows = pl.ds(pl.multiple_of(j * tq, tq), tq)
        _nsa_tile(pl.program_id(2) * tiles_per_step + j, q_ref.at[rows, :], kc_ref, vc_ref, ks_ref, vs_ref,
                  kw_ref, vw_ref, gn_ref.at[rows, :], gm_ref.at[rows, :], qal_ref, cpos_ref, wsel_ref,
                  o_ref.at[rows, :], *scratch, tq=tq, **static)
        return carry

    lax.fori_loop(0, tiles_per_step, tile, 0)


def _nsa_tile(qt, q_ref, kc_ref, vc_ref, ks_ref, vs_ref, kw_ref, vw_ref, gn_ref, gm_ref,
              qal_ref, cpos_ref, wsel_ref,
              o_ref, q_t, ksaug, kwaug, kcaug, kstage, vstage, s_win, s_a, s_b, win_max, o_cmp_scr, flags,
              *, tq, nk, nkw, seq, sel_chunk):
    b = pl.program_id(0)
    g = pl.program_id(1)
    r_heads = HEADS_PER_GROUP
    dh = HEAD_DIM
    m_cols = r_heads * tq
    nb = seq // SEL_LEN
    nc = kc_ref.shape[1]
    a_w = 2 * dh
    k_w = a_w + nb

    @pl.when((b == 0) & (g == 0) & (qt == 0))
    def _():
        rows = min(FILL_ROWS, seq)

        def fill(c, carry):
            off = pl.multiple_of(c * rows, rows)
            pos = off + lax.broadcasted_iota(jnp.int32, (rows, k_w), 0)
            lane = lax.broadcasted_iota(jnp.int32, (rows, k_w), 1)
            blk_of = jnp.right_shift(pos, SEL_SHIFT)
            cols = jnp.where((lane >= dh) & (lane < dh + 3), blk_of * SEL_LEN,
                             jnp.where((lane >= dh + 3) & (lane < dh + 6), pos & (SEL_LEN - 1),
                                       jnp.where(lane == a_w + blk_of, 1, 0)))
            cols = cols.astype(F32).astype(BF16)
            ksaug[pl.ds(off, rows), :] = cols
            kwaug[pl.ds(off, rows), :] = cols[:, 0:a_w]
            return carry

        lax.fori_loop(0, seq // rows, fill, 0)
        kcaug[...] = cpos_ref[...]
        vstage[...] = jnp.ones(vstage.shape, BF16)

    @pl.when(qt == 0)
    def _():
        ksaug[:, 0:dh] = ks_ref[0]
        kwaug[:, 0:dh] = kw_ref[0]
        kcaug[:, 0:dh] = kc_ref[0]
        q_t[dh:a_w, :] = qal_ref[0]

    v_rows = dh + ONES_ROWS
    ri = lax.broadcasted_iota(jnp.int32, (v_rows, 2 * dh), 0)
    ci = lax.broadcasted_iota(jnp.int32, (v_rows, 2 * dh), 1)
    pick = (((ri < dh) & (ri == ci)) | ((ri >= dh) & (ci == dh))).astype(F32).astype(BF16)
    transposed = lambda v: _dot_nt(pick[0:dh, 0:dh], v).astype(BF16)
    staged_aug_t = lambda v: _dot_nt(pick, v).astype(BF16)
    ones_rows = jnp.ones((ONES_ROWS, nkw), BF16)
    window_aug_t = lambda v: jnp.concatenate([transposed(v), ones_rows], axis=0)
    init = (jnp.full((1, m_cols), MASK_NEG, F32), jnp.zeros((v_rows, m_cols), F32))

    t0 = qt * tq
    col = lax.broadcasted_iota(jnp.int32, (1, m_cols), 1)
    trow = t0 + (col & (tq - 1))

    scale = np.float32(HEAD_DIM ** -0.5 * LOG2_E)
    qf = jnp.transpose(q_ref[...].astype(F32) * scale)
    for r in range(r_heads):
        q_t[0:dh, r * tq:(r + 1) * tq] = qf[r * dh:(r + 1) * dh, :].astype(BF16)

    n_wt = 3
    win_off = [pl.multiple_of(jnp.maximum(t0 - i * nkw, 0), nkw) for i in range(n_wt)]
    newer = (t0 + lax.broadcasted_iota(jnp.int32, (nkw, 1), 0)) <= trow

    def window_tile(i):
        s = _dot(kwaug[pl.ds(win_off[i], nkw), :], q_t[0:a_w, :])
        return s if i == 0 else s + jnp.where(t0 - i * nkw >= 0, 0.0, MASK_NEG)

    def window_scores():
        merged = jnp.where(newer, window_tile(0), window_tile(2))
        s_win[0:nkw, :] = merged
        win_max[0:1, :] = jnp.max(merged, axis=0, keepdims=True)
        mid = window_tile(1)
        s_win[nkw:2 * nkw, :] = mid
        win_max[1:2, :] = jnp.max(mid, axis=0, keepdims=True)

    def compressed_and_select(rows_c, rows_b):
        free = max(rows_c - cmp_chunk - tq // CMP_STRIDE, 0)
        s = _dot(kcaug[0:rows_c, :], q_t[0:a_w, :])
        cend = (free + lax.broadcasted_iota(jnp.int32, (rows_c - free, 1), 0)) * CMP_STRIDE + (CMP_LEN - 1)
        tail = jnp.where(cend <= trow, s[free:], MASK_NEG)
        s = jnp.concatenate([s[0:free], tail], axis=0) if free else tail
        m = jnp.max(s, axis=0, keepdims=True)
        e = jnp.exp2(s - m)
        has_key = (trow >= CMP_LEN - 1).astype(F32)
        p = e * (has_key / jnp.sum(e, axis=0, keepdims=True))
        o_cmp_scr[...] = _dot(transposed(vc_ref[0, 0:rows_c, :]), p.astype(BF16))
        imp = p[:, 0:tq]
        for r in range(1, r_heads):
            imp = imp + p[:, r * tq:(r + 1) * tq]

        hi = imp.astype(BF16)
        lo = (imp - hi.astype(F32)).astype(BF16)
        wsel = wsel_ref[0:rows_b, 0:rows_c]
        imp_t = _dot(wsel, hi) + _dot(wsel, lo)

        window_scores()

        blk = lax.broadcasted_iota(jnp.int32, (rows_b, tq), 0).astype(F32)
        tq_l = t0 + lax.broadcasted_iota(jnp.int32, (rows_b, tq), 1)
        cur = jnp.right_shift(tq_l, SEL_SHIFT).astype(F32)
        valid = blk <= cur
        sel = jnp.where(blk == 0.0, 1.0, jnp.where(blk == cur, 1.0, jnp.where(blk == cur - 1.0, 1.0, 0.0)))
        score = jnp.where(valid, jnp.where(sel > 0.0, -1.0, imp_t), -1.0)
        for _ in range(min(SEL_TOPK, nb) - N_FORCED):
            mx = jnp.max(score, axis=0, keepdims=True)
            idx = jnp.min(jnp.where(score == mx, blk, float(nb)), axis=0, keepdims=True)
            hit = blk == idx
            sel = jnp.where(hit, 1.0, sel)
            score = jnp.where(hit, -2.0, score)
        selv = jnp.where(valid, sel, 0.0) > 0.0
        selneg_t = jnp.where(selv, 0.0, MASK_NEG).astype(BF16)
        unseen = jnp.full((nb - rows_b, tq), MASK_NEG, BF16)
        for r in range(r_heads):
            q_t[a_w:a_w + rows_b, r * tq:(r + 1) * tq] = selneg_t
            if rows_b < nb:
                q_t[a_w + rows_b:, r * tq:(r + 1) * tq] = unseen

        used = jnp.max(jnp.where(selv, 1.0, 0.0), axis=1, keepdims=True)
        bit_id = lax.broadcasted_iota(jnp.int32, (rows_b, 1), 0) & (WORD_BITS - 1)
        weighted = used * jnp.left_shift(1, bit_id).astype(F32)
        for i in range(nb // WORD_BITS):
            if (i + 1) * WORD_BITS <= rows_b:
                flags[i] = jnp.sum(weighted[i * WORD_BITS:(i + 1) * WORD_BITS, :]).astype(jnp.int32)
            else:
                flags[i] = 0

    n_var = nb // sel_chunk
    cmp_chunk = sel_chunk * (SEL_LEN // CMP_STRIDE)
    seen_c = (t0 + tq - CMP_LEN) // CMP_STRIDE + 1
    seen_b = (t0 + tq) // SEL_LEN
    variant = jnp.maximum((seen_c + cmp_chunk - 1) // cmp_chunk, (seen_b + sel_chunk - 1) // sel_chunk)
    for v in range(1, n_var + 1):
        pl.when(variant == v)(functools.partial(compressed_and_select, min(v * cmp_chunk, nc), v * sel_chunk))
    o_cmp = o_cmp_scr[...]

    n_own = tq // SEL_LEN
    off_q = pl.multiple_of(t0, tq)
    kstage[0:tq, :] = ksaug[pl.ds(off_q, tq), :]
    vstage[0:tq, 0:dh] = vs_ref[0, pl.ds(off_q, tq), :]

    def stage_block(j, cnt):
        src = pl.multiple_of(j * SEL_LEN, SEL_LEN)
        dst = pl.multiple_of(cnt * SEL_LEN, SEL_LEN)
        kstage[pl.ds(dst, SEL_LEN), :] = ksaug[pl.ds(src, SEL_LEN), :]
        vstage[pl.ds(dst, SEL_LEN), 0:dh] = vs_ref[0, pl.ds(src, SEL_LEN), :]

    def gather(j, cnt):
        bit = jnp.right_shift(flags[jnp.right_shift(j, WORD_SHIFT)], j & (WORD_BITS - 1)) & 1
        pl.when(bit == 1)(functools.partial(stage_block, j, cnt))
        return cnt + bit

    n_past = jnp.right_shift(t0, SEL_SHIFT)
    near_lo = jnp.maximum(n_past - NEAR_BLOCKS, 0)
    first = jnp.where(near_lo > 0, flags[0] & 1, 0)
    pl.when(first == 1)(functools.partial(stage_block, 0, n_own))

    def far_word(w, cnt):
        lo = jnp.maximum(w * WORD_BITS, 1)
        hi = jnp.minimum((w + 1) * WORD_BITS, near_lo)
        rest = jnp.where(w == 0, flags[w] & -2, flags[w])
        return lax.cond(rest != 0, lambda c: lax.fori_loop(lo, hi, gather, c), lambda c: c, cnt)

    n_blocks = lax.fori_loop(0, jnp.right_shift(near_lo + WORD_BITS - 1, WORD_SHIFT), far_word, n_own + first)
    n_blocks = lax.fori_loop(near_lo, n_past, gather, n_blocks)
    per_tile = nk // SEL_LEN
    n_tiles = (n_blocks + per_tile - 1) // per_tile
    pad_block = jnp.where(lax.broadcasted_iota(jnp.int32, (SEL_LEN, k_w), 1) == PAD_COL, 1.0, 0.0).astype(BF16)

    def pad(j, carry):
        dst = pl.multiple_of(j * SEL_LEN, SEL_LEN)
        kstage[pl.ds(dst, SEL_LEN), :] = pad_block
        return carry

    lax.fori_loop(n_blocks, n_tiles * per_tile, pad, 0)

    own_pos = t0 + lax.broadcasted_iota(jnp.int32, (tq, 1), 0)
    s_first = _dot(kstage[0:nk, :], q_t[...])
    own = jnp.where(own_pos <= trow, s_first[0:tq], MASK_NEG)
    s_a[0:tq, :] = own
    s_a[tq:, :] = s_first[tq:]
    max_a = jnp.maximum(jnp.max(own, axis=0, keepdims=True), jnp.max(s_first[tq:], axis=0, keepdims=True))

    m_w = jnp.maximum(win_max[0:1, :], win_max[1:2, :])
    e_merged = jnp.exp2(s_win[0:nkw, :] - m_w)
    e_mid = jnp.exp2(s_win[nkw:2 * nkw, :] - m_w)
    weights = jnp.concatenate([jnp.where(newer, e_merged, 0.0).astype(BF16), e_mid.astype(BF16),
                               jnp.where(newer, 0.0, e_merged).astype(BF16)], axis=0)
    v_win = jnp.concatenate([window_aug_t(vw_ref[0, pl.ds(win_off[i], nkw), :]) for i in range(n_wt)], axis=1)
    acc_w = _dot(v_win, weights)
    o_win = acc_w[0:dh] / acc_w[dh:dh + 1]

    def tile_accumulate(i, s_ref, col_max, state):
        return _accumulate(s_ref, col_max, staged_aug_t(vstage[pl.ds(pl.multiple_of(i * nk, nk), nk), :]), state)

    def scores_and_accumulate(i_next, s_next_ref, i_cur, s_cur_ref, max_cur, state):
        m_old, acc_old = state
        m_new = jnp.maximum(m_old, max_cur)
        alpha = jnp.exp2(m_old - m_new)
        s_next = _dot(kstage[pl.ds(pl.multiple_of(i_next * nk, nk), nk), :], q_t[...])
        s_next_ref[...] = s_next
        rows = nk // OVERLAP_CHUNKS
        partial, weights = [], []
        for c in range(OVERLAP_CHUNKS):
            rs = slice(c * rows, (c + 1) * rows)
            partial.append(jnp.max(s_next[rs], axis=0, keepdims=True))
            m_c = m_new + partial[c] * 0.0
            weights.append(jnp.exp2(s_cur_ref[rs, :] - m_c).astype(BF16))
        v_t = staged_aug_t(vstage[pl.ds(pl.multiple_of(i_cur * nk, nk), nk), :])
        acc = alpha * acc_old + _dot(v_t, jnp.concatenate(weights, axis=0))
        return (m_new, acc), functools.reduce(jnp.maximum, partial)

    def slc_pair(j, carry):
        state, max_a = carry
        i = 2 * j
        state, max_b = scores_and_accumulate(i + 1, s_b, i, s_a, max_a, state)
        state, max_a = scores_and_accumulate(i + 2, s_a, i + 1, s_b, max_b, state)
        return state, max_a

    n_pairs = (n_tiles - 1) // 2
    state, max_a = lax.fori_loop(0, n_pairs, slc_pair, (init, max_a))
    i_a = 2 * n_pairs

    def two_left(state):
        state, max_b = scores_and_accumulate(i_a + 1, s_b, i_a, s_a, max_a, state)
        return tile_accumulate(i_a + 1, s_b, max_b, state)

    def one_left(state):
        return tile_accumulate(i_a, s_a, max_a, state)

    _, acc = lax.cond(n_tiles - i_a == 2, two_left, one_left, state)
    o_slc = acc[0:dh] / acc[dh:dh + 1]

    gates = jnp.transpose(_sigmoid(gn_ref[...]))
    heads = []
    for r in range(r_heads):
        cs = slice(r * tq, (r + 1) * tq)
        heads.append(gates[3 * r:3 * r + 1, :] * o_cmp[:, cs] + gates[3 * r + 1:3 * r + 2, :] * o_slc[:, cs]
                     + gates[3 * r + 2:3 * r + 3, :] * o_win[:, cs])
    o = jnp.transpose(jnp.concatenate(heads, axis=0))
    o_ref[...] = (_sigmoid(gm_ref[...].astype(F32)) * o).astype(o_ref.dtype)


def _alibi_tables(seq, nc, tq):
    import ml_dtypes
    bf = ml_dtypes.bfloat16
    h = np.arange(1, N_HEADS + 1, dtype=np.float32)
    slopes = (np.exp2(-ALIBI_MAX_BIAS * h / N_HEADS) * LOG2_E).astype(np.float32)
    s1 = slopes.astype(bf).astype(np.float32)
    s2 = (slopes - s1).astype(bf).astype(np.float32)
    s3 = (slopes - s1 - s2).astype(bf).astype(np.float32)
    dh = HEAD_DIM
    qal = np.zeros((N_HEADS, dh), np.float32)
    for rep in range(3):
        qal[:, 3 * rep + 0] = s1
        qal[:, 3 * rep + 1] = s2
        qal[:, 3 * rep + 2] = s3
    qal[:, PAD_COL - dh] = MASK_NEG
    qal = qal.reshape(N_KV_GROUPS, HEADS_PER_GROUP, dh).transpose(0, 2, 1)
    qal_p = np.repeat(qal, tq, axis=2)

    nb = seq // SEL_LEN
    c = np.arange(nc)
    cpos = np.zeros((nc, 2 * dh), np.float32)
    cpos[:, dh:dh + 3] = ((c // 16) * 16 * CMP_STRIDE)[:, None]
    cpos[:, dh + 3:dh + 6] = ((c % 16) * CMP_STRIDE)[:, None]
    cpos[:, dh + 6:dh + 9] = CMP_LEN - 1

    r_sel = SEL_LEN // CMP_STRIDE
    r_cmp = CMP_LEN // CMP_STRIDE
    wsel = np.zeros((nb, nc), np.float32)
    for j in range(nb):
        for mm in range(r_sel):
            for nn in range(r_cmp):
                ci = r_sel * j + mm - nn
                if 0 <= ci < nc - 1:
                    wsel[j, ci] += 1.0
    as_bf = lambda a: jnp.asarray(a.astype(bf))
    return as_bf(qal_p), as_bf(cpos), as_bf(wsel)


def _nsa(qm, kvm, kcm, vcm, gnm, fm, batch, seq, tq, nk, nkw, gm_col0):
    t = qm.shape[0]
    g = N_KV_GROUPS
    dh = HEAD_DIM
    nqt = seq // tq
    nc = kcm.shape[1]
    nb = seq // SEL_LEN
    gw = HEADS_PER_GROUP * dh
    m_cols = HEADS_PER_GROUP * tq
    qal, cpos, wsel = _alibi_tables(seq, nc, tq)
    assert nkw == tq and WINDOW == 2 * tq, "the window branch merges the new and old key tiles of a query tile"
    n_wt = 2
    sel_chunk = max(WORD_BITS, nb // 8)
    per_tile = nk // SEL_LEN
    stage_rows = -(-nb // per_tile) * per_tile * SEL_LEN
    tps = NSA_TILES_PER_STEP if nqt % NSA_TILES_PER_STEP == 0 else 1
    n_steps = nqt // tps
    rq = tps * tq
    row = lambda b, gg, i: b * n_steps + i
    slab = lambda base: pl.BlockSpec((1, seq, dh), lambda b, gg, i: (base + gg, b, 0))
    cmp_slab = pl.BlockSpec((1, nc, dh), lambda b, gg, i: (b * g + gg, 0, 0))
    const = lambda a: pl.BlockSpec(a.shape, lambda b, gg, i: (0,) * a.ndim)
    return pl.pallas_call(
        functools.partial(_nsa_kernel, tq=tq, tiles_per_step=tps, nk=nk, nkw=nkw, seq=seq, sel_chunk=sel_chunk),
        grid=(batch, g, n_steps),
        in_specs=[
            pl.BlockSpec((rq, gw), lambda b, gg, i: (row(b, gg, i), gg)),
            cmp_slab, cmp_slab,
            slab(0), slab(g), slab(2 * g), slab(3 * g),
            pl.BlockSpec((rq, LANES), lambda b, gg, i: (row(b, gg, i), gg)),
            pl.BlockSpec((rq, gw), lambda b, gg, i: (row(b, gg, i), gm_col0 + gg)),
            pl.BlockSpec((1, dh, m_cols), lambda b, gg, i: (gg, 0, 0)),
            const(cpos), const(wsel),
        ],
        out_specs=pl.BlockSpec((rq, gw), lambda b, gg, i: (row(b, gg, i), gg)),
        out_shape=jax.ShapeDtypeStruct((t, g * gw), BF16),
        scratch_shapes=[
            pltpu.VMEM((2 * dh + nb, m_cols), BF16),
            pltpu.VMEM((seq, 2 * dh + nb), BF16),
            pltpu.VMEM((seq, 2 * dh), BF16),
            pltpu.VMEM((nc, 2 * dh), BF16),
            pltpu.VMEM((stage_rows, 2 * dh + nb), BF16),
            pltpu.VMEM((stage_rows, 2 * dh), BF16),
            pltpu.VMEM((n_wt * nkw, m_cols), F32),
            pltpu.VMEM((nk, m_cols), F32),
            pltpu.VMEM((nk, m_cols), F32),
            pltpu.VMEM((SUBLANES, m_cols), F32),
            pltpu.VMEM((dh, m_cols), F32),
            pltpu.SMEM((nb // WORD_BITS,), jnp.int32),
        ],
        compiler_params=_params(("arbitrary", "arbitrary", "arbitrary")),
        name="nsa",
    )(qm, kcm, vcm, kvm, kvm, kvm, kvm, gnm, fm, qal, cpos, wsel)


def _ffn_kernel(x_ref, yr_ref, ya_ref, wo_ref, gmix_ref, gpre_ref, wg_ref, wu_ref, wd_ref, gpost_ref,
                p_ref, wpg_ref, bpg_ref, wpp_ref, o_ref, x1_ref, h_ref, acc_ref):
    j = pl.program_id(1)

    @pl.when(j == 0)
    def _():
        y = (yr_ref[...].astype(F32) + ya_ref[...].astype(F32)).astype(BF16)
        x1 = x_ref[...] + _rmsnorm(_dot(y, wo_ref[...]), gmix_ref[...])
        x1_ref[...] = x1
        h_ref[...] = _rmsnorm(x1, gpre_ref[...]).astype(BF16)
        acc_ref[...] = jnp.zeros(acc_ref.shape, F32)

    h = h_ref[...]
    gate = _dot(h, wg_ref[...])
    up = _dot(h, wu_ref[...])
    act = (gate * _sigmoid(gate) * up).astype(BF16)
    acc_ref[...] += _dot(act, wd_ref[...])

    @pl.when(j == pl.num_programs(1) - 1)
    def _():
        x2 = x1_ref[...] + _rmsnorm(acc_ref[...], gpost_ref[...])
        gate_p = _sigmoid(_dot(x2.astype(BF16), wpg_ref[...]) + bpg_ref[...])
        o_ref[...] = x2 + gate_p * _dot(p_ref[...].astype(BF16), wpp_ref[...])


def _ffn(x, yr, ya, wo, gmix, gpre, wgu, wd, gpost, p, wpg, bpg, wpp, tm, tf):
    t, d = x.shape
    dff = wd.shape[0]
    nf = dff // tf
    dp = p.shape[1]
    rows = pl.BlockSpec((tm, d), lambda i, j: (i, 0))
    vec = pl.BlockSpec((1, d), lambda i, j: (0, 0))
    square = pl.BlockSpec((d, d), lambda i, j: (0, 0))
    return pl.pallas_call(
        _ffn_kernel,
        grid=(t // tm, nf),
        in_specs=[
            rows, rows, rows, square, vec, vec,
            pl.BlockSpec((d, tf), lambda i, j: (0, j)),
            pl.BlockSpec((d, tf), lambda i, j: (0, nf + j)),
            pl.BlockSpec((tf, d), lambda i, j: (j, 0)),
            vec,
            pl.BlockSpec((tm, dp), lambda i, j: (i, 0)),
            square,
            vec,
            pl.BlockSpec((dp, d), lambda i, j: (0, 0)),
        ],
        out_specs=rows,
        out_shape=jax.ShapeDtypeStruct((t, d), F32),
        scratch_shapes=[pltpu.VMEM((tm, d), F32), pltpu.VMEM((tm, d), BF16), pltpu.VMEM((tm, d), F32)],
        compiler_params=_params(("parallel", "arbitrary")),
        name="out_ffn_ple",
    )(x, yr, ya, wo, gmix, gpre, wgu, wgu, wd, gpost, p, wpg, bpg, wpp)


def _block_diag_chunks(w):
    n, bs, _ = w.shape
    per = LANES // bs
    w = w.reshape(n // per, per, bs, bs)
    eye = jnp.eye(per, dtype=w.dtype)
    return jnp.einsum('cpij,pq->cpiqj', w, eye).reshape(n // per, LANES, LANES)


def _layer(x, p, norm_mix_pre, norm_mix_post, w_in, conv_w, conv_b, lru_wa, lru_ba, lru_wx, lru_bx,
           lru_lambda, cmp_pos_k, cmp_pos_v, cmp_k_w1, cmp_k_w2, cmp_v_w1, cmp_v_w2, w_out,
           norm_ffn_pre, norm_ffn_post, ffn_w_gate_up, ffn_w_down, ple_w_proj, ple_w_gate, ple_b_gate,
           batch, seq):
    t, d = x.shape
    d_attn = N_HEADS * HEAD_DIM
    d_kv = N_KV_GROUPS * HEAD_DIM
    row2 = lambda v: v.reshape(1, -1)

    o_q = 2 * d
    o_kv = o_q + d_attn
    o_gn = o_kv + 6 * d_kv
    o_gm = o_gn + 3 * N_HEADS
    w_f = jnp.concatenate([w_in[:, 0:o_q], w_in[:, o_gm:o_gm + 2 * d]], axis=1).astype(BF16)
    w_q = w_in[:, o_q:o_kv].astype(BF16)
    w_kv = w_in[:, o_kv:o_gn].astype(BF16)
    per_g = 3 * HEADS_PER_GROUP
    w_gn = w_in[:, o_gn:o_gm].reshape(d, N_KV_GROUPS, per_g)
    w_gn = jnp.pad(w_gn, ((0, 0), (0, 0), (0, LANES - per_g))).reshape(d, N_KV_GROUPS * LANES).astype(BF16)

    g_pre = row2(norm_mix_pre)
    tm = min(512, t)
    fm = _norm_matmul(x, g_pre, w_f, BF16, tm)
    qm, gnm, kv_chunks, kvm = _norm_qkv(x, g_pre, jnp.concatenate([w_q, w_gn, w_kv], axis=1), d_attn,
                                        N_KV_GROUPS * LANES, 2 * N_KV_GROUPS, tm)

    yr = _rnn(fm, conv_w, row2(conv_b), _block_diag_chunks(lru_wa).astype(BF16),
              _block_diag_chunks(lru_wx).astype(BF16), row2(lru_ba), row2(lru_bx), row2(lru_lambda),
              batch, seq, min(512, seq))

    nchunk = seq // CMP_STRIDE
    kcm, vcm = _compress(kv_chunks, cmp_k_w1.astype(BF16), cmp_k_w2.astype(BF16), cmp_pos_k.reshape(1, -1),
                         cmp_v_w1.astype(BF16), cmp_v_w2.astype(BF16), cmp_pos_v.reshape(1, -1), batch, nchunk)

    tq = min(256, seq)
    ya = _nsa(qm, kvm, kcm, vcm, gnm, fm, batch, seq, tq, min(NSA_KEY_TILE, seq), min(256, seq),
              gm_col0=(3 * d) // (HEADS_PER_GROUP * HEAD_DIM))

    dff = ffn_w_down.shape[0]
    tf = dff
    return _ffn(x, yr, ya, w_out.astype(BF16), row2(norm_mix_post), row2(norm_ffn_pre),
                ffn_w_gate_up.astype(BF16), ffn_w_down.astype(BF16), row2(norm_ffn_post),
                p, ple_w_gate.astype(BF16), row2(ple_b_gate), ple_w_proj.astype(BF16), tm, tf)


def kernel(x, p, norm_mix_pre, norm_mix_post, w_in, conv_w, conv_b, lru_wa, lru_ba, lru_wx, lru_bx, lru_lambda, cmp_pos_k, cmp_pos_v, cmp_k_w1, cmp_k_w2, cmp_v_w1, cmp_v_w2, w_out, norm_ffn_pre, norm_ffn_post, ffn_w_gate_up, ffn_w_down, ple_w_proj, ple_w_gate, ple_b_gate):
    batch, seq, d = x.shape
    depth = w_in.shape[0]
    xf = x.reshape(batch * seq, d)
    for i in range(depth):
        xf = _layer(xf, p[i].reshape(batch * seq, -1), norm_mix_pre[i], norm_mix_post[i], w_in[i], conv_w[i],
                    conv_b[i], lru_wa[i], lru_ba[i], lru_wx[i], lru_bx[i], lru_lambda[i], cmp_pos_k[i],
                    cmp_pos_v[i], cmp_k_w1[i], cmp_k_w2[i], cmp_v_w1[i], cmp_v_w2[i], w_out[i],
                    norm_ffn_pre[i], norm_ffn_post[i], ffn_w_gate_up[i], ffn_w_down[i], ple_w_proj[i],
                    ple_w_gate[i], ple_b_gate[i], batch, seq)
    return xf.reshape(batch, seq, d)
```

```python
import functools

import numpy as np
import jax
import jax.numpy as jnp
from jax import lax
from jax.experimental import pallas as pl
from jax.experimental.pallas import tpu as pltpu

CONV_WIDTH = 4
LRU_C = 8.0
N_HEADS = 16
HEAD_DIM = 64
N_KV_GROUPS = 4
HEADS_PER_GROUP = N_HEADS // N_KV_GROUPS
CMP_LEN = 32
CMP_STRIDE = 16
SEL_LEN = 64
SEL_TOPK = 16
WINDOW = 512
ALIBI_MAX_BIAS = 8.0
NORM_EPS = 1e-6

LANES = 128
SUBLANES = 8
VMEM_LIMIT_BYTES = 56 * 1024 * 1024

MASK_NEG = -1e30
LOG2_E = 1.4426950408889634
N_ALIBI_COLS = 9
PAD_COL = HEAD_DIM + N_ALIBI_COLS
WORD_BITS = 16
WORD_SHIFT = 4
SEL_SHIFT = 6
N_FORCED = 3
NEAR_BLOCKS = 20
FILL_ROWS = 512
NSA_KEY_TILE = 11 * SEL_LEN
SCAN_UNROLL = 4
NSA_TILES_PER_STEP = 4
OVERLAP_CHUNKS = 4
BF16 = jnp.bfloat16
F32 = jnp.float32


def _dot(a, b):
    return jnp.dot(a, b, preferred_element_type=F32)


def _dot_nt(a, b):
    return lax.dot_general(a, b, (((1,), (1,)), ((), ())), preferred_element_type=F32)


def _sigmoid(x):
    return 0.5 * jnp.tanh(0.5 * x) + 0.5


def _gelu_tanh(x):
    c = np.float32(np.sqrt(2.0 / np.pi))
    half = 0.5 * x
    return half + half * jnp.tanh(x * (c + (c * 0.044715) * (x * x)))


def _rmsnorm(x, g):
    ms = jnp.mean(x * x, axis=-1, keepdims=True)
    return x * lax.rsqrt(ms + NORM_EPS) * g


def _params(sem):
    return pltpu.CompilerParams(dimension_semantics=sem, vmem_limit_bytes=VMEM_LIMIT_BYTES)


def _norm_qkv_kernel(x_ref, g_ref, w_ref, q_ref, gn_ref, cv_ref, kv_ref, slab_scr):
    h = _rmsnorm(x_ref[...], g_ref[...]).astype(BF16)
    z = _dot(h, w_ref[...])
    nq = q_ref.shape[1]
    ngn = gn_ref.shape[1]
    q_ref[...] = z[:, 0:nq].astype(q_ref.dtype)
    gn_ref[...] = z[:, nq:nq + ngn]
    n_cv = cv_ref.shape[0]
    rows = cv_ref.shape[1]
    for c in range(n_cv):
        lo = nq + ngn + c * HEAD_DIM
        slab_scr[...] = z[:, lo:lo + HEAD_DIM]
        for l in range(CMP_STRIDE):
            cv_ref[c, :, l * HEAD_DIM:(l + 1) * HEAD_DIM] = (
                slab_scr[pl.ds(l, rows, stride=CMP_STRIDE), :].astype(cv_ref.dtype))
    for c in range(kv_ref.shape[0]):
        lo = nq + ngn + (n_cv + c) * HEAD_DIM
        kv_ref[c] = z[:, lo:lo + HEAD_DIM].astype(kv_ref.dtype)


def _norm_qkv(x, g, w, nq, ngn, n_cv, tm):
    t, k = x.shape
    n = w.shape[1]
    n_kv = (n - nq - ngn) // HEAD_DIM - n_cv
    return pl.pallas_call(
        _norm_qkv_kernel,
        grid=(t // tm,),
        in_specs=[
            pl.BlockSpec((tm, k), lambda i: (i, 0)),
            pl.BlockSpec((1, k), lambda i: (0, 0)),
            pl.BlockSpec((k, n), lambda i: (0, 0)),
        ],
        out_specs=[
            pl.BlockSpec((tm, nq), lambda i: (i, 0)),
            pl.BlockSpec((tm, ngn), lambda i: (i, 0)),
            pl.BlockSpec((n_cv, tm // CMP_STRIDE, CMP_STRIDE * HEAD_DIM), lambda i: (0, i, 0)),
            pl.BlockSpec((n_kv, tm, HEAD_DIM), lambda i: (0, i, 0)),
        ],
        out_shape=[
            jax.ShapeDtypeStruct((t, nq), BF16),
            jax.ShapeDtypeStruct((t, ngn), F32),
            jax.ShapeDtypeStruct((n_cv, t // CMP_STRIDE, CMP_STRIDE * HEAD_DIM), BF16),
            jax.ShapeDtypeStruct((n_kv, t, HEAD_DIM), BF16),
        ],
        scratch_shapes=[pltpu.VMEM((tm, HEAD_DIM), F32)],
        compiler_params=_params(("parallel",)),
        name="norm_qkv",
    )(x, g, w)


def _rnn_kernel(x_ref, g_ref, wf_ref, cw_ref, cb_ref, wa_ref, wx_ref, ba_ref, bx_ref, lam_ref,
                o_ref, gma_ref, xbuf, a_scr, b_scr, h_scr, *, ts):
    s_idx = pl.program_id(1)
    d = o_ref.shape[1]
    halo = SUBLANES

    @pl.when(s_idx == 0)
    def _():
        xbuf[0:halo, :] = jnp.zeros((halo, d), F32)
        h_scr[...] = jnp.zeros(h_scr.shape, F32)

    hn = _rmsnorm(x_ref[...], g_ref[...]).astype(BF16)
    xr = _dot(hn, wf_ref[:, 0:d])
    gr = _dot(hn, wf_ref[:, d:2 * d])
    gm = _dot(hn, wf_ref[:, 2 * d:3 * d])
    gma_ref[...] = _dot(hn, wf_ref[:, 3 * d:4 * d]).astype(gma_ref.dtype)

    xbuf[halo:halo + ts, :] = xr
    xc = cb_ref[...] + xbuf[halo:halo + ts, :] * cw_ref[CONV_WIDTH - 1:CONV_WIDTH, :]
    for k in range(1, CONV_WIDTH):
        xc = xc + xbuf[halo - k:halo - k + ts, :] * cw_ref[CONV_WIDTH - 1 - k:CONV_WIDTH - k, :]
    xbuf[0:halo, :] = xbuf[ts:ts + halo, :]

    xcb = xc.astype(BF16)
    n_chunks = d // LANES
    neg_sp = -LRU_C * (jnp.maximum(-lam_ref[...], 0.0) + jnp.log(1.0 + jnp.exp(-jnp.abs(lam_ref[...]))))
    for c in range(n_chunks):
        sl = slice(c * LANES, (c + 1) * LANES)
        xk = xcb[:, sl]
        r = _sigmoid(_dot(xk, wa_ref[c]) + ba_ref[:, sl])
        i = _sigmoid(_dot(xk, wx_ref[c]) + bx_ref[:, sl])
        log_a = r * neg_sp[:, sl]
        a = jnp.exp(log_a)
        a_scr[:, sl] = a
        u = 1.0 - a * a
        root = jnp.where(u > 0.0, u * lax.rsqrt(u), 0.0)
        b_scr[:, sl] = root * (i * xc[:, sl])

    row_id = lax.broadcasted_iota(jnp.int32, (SUBLANES, d), 0)

    def group(gi, h):
        base = pl.multiple_of(gi * SUBLANES, SUBLANES)
        a = a_scr[pl.ds(base, SUBLANES), :]
        b = b_scr[pl.ds(base, SUBLANES), :]
        s = 1
        while s < SUBLANES:
            reach = row_id >= s
            b = jnp.where(reach, a * pltpu.roll(b, s, 0) + b, b)
            a = jnp.where(reach, a * pltpu.roll(a, s, 0), a)
            s *= 2
        hs = a * h + b
        a_scr[pl.ds(base, SUBLANES), :] = hs
        return jnp.broadcast_to(hs[SUBLANES - 1:SUBLANES, :], (SUBLANES, d))

    h_scr[...] = lax.fori_loop(0, ts // SUBLANES, group, h_scr[...], unroll=SCAN_UNROLL)
    o_ref[...] = (_sigmoid(gm) * a_scr[...] * _gelu_tanh(gr)).astype(o_ref.dtype)


def _rnn(x, g, w_f, conv_w, conv_b, wa, wx, ba, bx, lam, batch, seq, ts):
    t = x.shape[0]
    d = conv_w.shape[1]
    ns = seq // ts
    rows = pl.BlockSpec((ts, d), lambda b, s: (b * ns + s, 0))
    vec = lambda r: pl.BlockSpec((r, d), lambda b, s: (0, 0))
    out = jax.ShapeDtypeStruct((t, d), BF16)
    return pl.pallas_call(
        functools.partial(_rnn_kernel, ts=ts),
        grid=(batch, ns),
        in_specs=[
            rows, vec(1),
            pl.BlockSpec(w_f.shape, lambda b, s: (0, 0)),
            vec(CONV_WIDTH), vec(1),
            pl.BlockSpec(wa.shape, lambda b, s: (0, 0, 0)),
            pl.BlockSpec(wx.shape, lambda b, s: (0, 0, 0)),
            vec(1), vec(1), vec(1),
        ],
        out_specs=[rows, rows],
        out_shape=[out, out],
        scratch_shapes=[
            pltpu.VMEM((ts + SUBLANES, d), F32),
            pltpu.VMEM((ts, d), F32),
            pltpu.VMEM((ts, d), F32),
            pltpu.VMEM((SUBLANES, d), F32),
        ],
        compiler_params=_params(("parallel", "arbitrary")),
        name="proj_rnn_mixer",
    )(x, g, w_f, conv_w, conv_b, wa, wx, ba, bx, lam)


def _compress_kernel(ak_ref, av_ref, w1k_ref, w2k_ref, pk_ref, w1v_ref, w2v_ref, pv_ref, ok_ref, ov_ref):
    def one(a_ref, w1_ref, w2_ref, p_ref, o_ref):
        a = a_ref[0]
        half = a.shape[1]
        nchunk = a.shape[0]
        lo = _dot(a, w1_ref[0:half, :])
        hi = _dot(a, w1_ref[half:2 * half, :])
        pb = _dot(jnp.broadcast_to(p_ref[...], (SUBLANES, 2 * half)).astype(BF16), w1_ref[...])[0:1, :]
        h = lo + pltpu.roll(hi, nchunk - 1, 0) + pb
        o_ref[0] = _dot(_gelu_tanh(h).astype(BF16), w2_ref[...]).astype(o_ref.dtype)

    one(ak_ref, w1k_ref, w2k_ref, pk_ref, ok_ref)
    one(av_ref, w1v_ref, w2v_ref, pv_ref, ov_ref)


def _compress(kv_chunks, w1k, w2k, pk, w1v, w2v, pv, batch, nchunk):
    g = N_KV_GROUPS
    width = kv_chunks.shape[2]
    full = lambda a: pl.BlockSpec(a.shape, lambda b, gg: (0,) * a.ndim)
    out = jax.ShapeDtypeStruct((batch * g, nchunk, HEAD_DIM), BF16)
    return pl.pallas_call(
        _compress_kernel,
        grid=(batch, g),
        in_specs=[
            pl.BlockSpec((1, nchunk, width), lambda b, gg: (gg, b, 0)),
            pl.BlockSpec((1, nchunk, width), lambda b, gg: (g + gg, b, 0)),
            full(w1k), full(w2k), full(pk), full(w1v), full(w2v), full(pv),
        ],
        out_specs=[
            pl.BlockSpec((1, nchunk, HEAD_DIM), lambda b, gg: (b * g + gg, 0, 0)),
            pl.BlockSpec((1, nchunk, HEAD_DIM), lambda b, gg: (b * g + gg, 0, 0)),
        ],
        out_shape=[out, out],
        compiler_params=_params(("parallel", "parallel")),
        name="compress",
    )(kv_chunks, kv_chunks, w1k, w2k, pk, w1v, w2v, pv)


ONES_ROWS = 16


def _accumulate(s_ref, col_max, v_aug_t, state):
    m_old, acc_old = state
    m_new = jnp.maximum(m_old, col_max)
    alpha = jnp.exp2(m_old - m_new)
    e = jnp.exp2(s_ref[...] - m_new).astype(BF16)
    return m_new, alpha * acc_old + _dot(v_aug_t, e)


def _nsa_kernel(q_ref, kc_ref, vc_ref, ks_ref, vs_ref, kw_ref, vw_ref, gn_ref, gm_ref,
                qal_ref, cpos_ref, wsel_ref, o_ref, *scratch, tq, tiles_per_step, **static):
    def tile(j, carry):
        rows = pl.ds(pl.multiple_of(j * tq, tq), tq)
        _nsa_tile(pl.program_id(2) * tiles_per_step + j, q_ref.at[rows, :], kc_ref, vc_ref, ks_ref, vs_ref,
                  kw_ref, vw_ref, gn_ref.at[rows, :], gm_ref.at[rows, :], qal_ref, cpos_ref, wsel_ref,
                  o_ref.at[rows, :], *scratch, tq=tq, **static)
        return carry

    lax.fori_loop(0, tiles_per_step, tile, 0)


def _nsa_tile(qt, q_ref, kc_ref, vc_ref, ks_ref, vs_ref, kw_ref, vw_ref, gn_ref, gm_ref,
              qal_ref, cpos_ref, wsel_ref,
              o_ref, q_t, ksaug, kwaug, kcaug, kstage, vstage, s_win, s_a, s_b, win_max, o_cmp_scr, flags,
              *, tq, nk, nkw, seq, sel_chunk):
    b = pl.program_id(0)
    g = pl.program_id(1)
    r_heads = HEADS_PER_GROUP
    dh = HEAD_DIM
    m_cols = r_heads * tq
    nb = seq // SEL_LEN
    nc = kc_ref.shape[1]
    a_w = 2 * dh
    k_w = a_w + nb

    @pl.when((b == 0) & (g == 0) & (qt == 0))
    def _():
        rows = min(FILL_ROWS, seq)

        def fill(c, carry):
            off = pl.multiple_of(c * rows, rows)
            pos = off + lax.broadcasted_iota(jnp.int32, (rows, k_w), 0)
            lane = lax.broadcasted_iota(jnp.int32, (rows, k_w), 1)
            blk_of = jnp.right_shift(pos, SEL_SHIFT)
            cols = jnp.where((lane >= dh) & (lane < dh + 3), blk_of * SEL_LEN,
                             jnp.where((lane >= dh + 3) & (lane < dh + 6), pos & (SEL_LEN - 1),
                                       jnp.where(lane == a_w + blk_of, 1, 0)))
            cols = cols.astype(F32).astype(BF16)
            ksaug[pl.ds(off, rows), :] = cols
            kwaug[pl.ds(off, rows), :] = cols[:, 0:a_w]
            return carry

        lax.fori_loop(0, seq // rows, fill, 0)
        kcaug[...] = cpos_ref[...]
        vstage[...] = jnp.ones(vstage.shape, BF16)

    @pl.when(qt == 0)
    def _():
        ksaug[:, 0:dh] = ks_ref[0]
        kwaug[:, 0:dh] = kw_ref[0]
        kcaug[:, 0:dh] = kc_ref[0]
        q_t[dh:a_w, :] = qal_ref[0]

    v_rows = dh + ONES_ROWS
    ri = lax.broadcasted_iota(jnp.int32, (v_rows, 2 * dh), 0)
    ci = lax.broadcasted_iota(jnp.int32, (v_rows, 2 * dh), 1)
    pick = (((ri < dh) & (ri == ci)) | ((ri >= dh) & (ci == dh))).astype(F32).astype(BF16)
    transposed = lambda v: _dot_nt(pick[0:dh, 0:dh], v).astype(BF16)
    staged_aug_t = lambda v: _dot_nt(pick, v).astype(BF16)
    ones_rows = jnp.ones((ONES_ROWS, nkw), BF16)
    window_aug_t = lambda v: jnp.concatenate([transposed(v), ones_rows], axis=0)
    init = (jnp.full((1, m_cols), MASK_NEG, F32), jnp.zeros((v_rows, m_cols), F32))

    t0 = qt * tq
    col = lax.broadcasted_iota(jnp.int32, (1, m_cols), 1)
    trow = t0 + (col & (tq - 1))

    scale = np.float32(HEAD_DIM ** -0.5 * LOG2_E)
    qf = jnp.transpose(q_ref[...].astype(F32) * scale)
    for r in range(r_heads):
        q_t[0:dh, r * tq:(r + 1) * tq] = qf[r * dh:(r + 1) * dh, :].astype(BF16)

    n_wt = 3
    win_off = [pl.multiple_of(jnp.maximum(t0 - i * nkw, 0), nkw) for i in range(n_wt)]
    newer = (t0 + lax.broadcasted_iota(jnp.int32, (nkw, 1), 0)) <= trow

    def window_tile(i):
        s = _dot(kwaug[pl.ds(win_off[i], nkw), :], q_t[0:a_w, :])
        return s if i == 0 else s + jnp.where(t0 - i * nkw >= 0, 0.0, MASK_NEG)

    def window_scores():
        merged = jnp.where(newer, window_tile(0), window_tile(2))
        s_win[0:nkw, :] = merged
        win_max[0:1, :] = jnp.max(merged, axis=0, keepdims=True)
        mid = window_tile(1)
        s_win[nkw:2 * nkw, :] = mid
        win_max[1:2, :] = jnp.max(mid, axis=0, keepdims=True)

    def compressed_and_select(rows_c, rows_b):
        free = max(rows_c - cmp_chunk - tq // CMP_STRIDE, 0)
        s = _dot(kcaug[0:rows_c, :], q_t[0:a_w, :])
        cend = (free + lax.broadcasted_iota(jnp.int32, (rows_c - free, 1), 0)) * CMP_STRIDE + (CMP_LEN - 1)
        tail = jnp.where(cend <= trow, s[free:], MASK_NEG)
        s = jnp.concatenate([s[0:free], tail], axis=0) if free else tail
        m = jnp.max(s, axis=0, keepdims=True)
        e = jnp.exp2(s - m)
        has_key = (trow >= CMP_LEN - 1).astype(F32)
        p = e * (has_key / jnp.sum(e, axis=0, keepdims=True))
        o_cmp_scr[...] = _dot(transposed(vc_ref[0, 0:rows_c, :]), p.astype(BF16))
        imp = p[:, 0:tq]
        for r in range(1, r_heads):
            imp = imp + p[:, r * tq:(r + 1) * tq]

        hi = imp.astype(BF16)
        lo = (imp - hi.astype(F32)).astype(BF16)
        wsel = wsel_ref[0:rows_b, 0:rows_c]
        imp_t = _dot(wsel, hi) + _dot(wsel, lo)

        window_scores()

        blk = lax.broadcasted_iota(jnp.int32, (rows_b, tq), 0).astype(F32)
        tq_l = t0 + lax.broadcasted_iota(jnp.int32, (rows_b, tq), 1)
        cur = jnp.right_shift(tq_l, SEL_SHIFT).astype(F32)
        valid = blk <= cur
        sel = jnp.where(blk == 0.0, 1.0, jnp.where(blk == cur, 1.0, jnp.where(blk == cur - 1.0, 1.0, 0.0)))
        score = jnp.where(valid, jnp.where(sel > 0.0, -1.0, imp_t), -1.0)
        for _ in range(min(SEL_TOPK, nb) - N_FORCED):
            mx = jnp.max(score, axis=0, keepdims=True)
            idx = jnp.min(jnp.where(score == mx, blk, float(nb)), axis=0, keepdims=True)
            hit = blk == idx
            sel = jnp.where(hit, 1.0, sel)
            score = jnp.where(hit, -2.0, score)
        selv = jnp.where(valid, sel, 0.0) > 0.0
        selneg_t = jnp.where(selv, 0.0, MASK_NEG).astype(BF16)
        unseen = jnp.full((nb - rows_b, tq), MASK_NEG, BF16)
        for r in range(r_heads):
            q_t[a_w:a_w + rows_b, r * tq:(r + 1) * tq] = selneg_t
            if rows_b < nb:
                q_t[a_w + rows_b:, r * tq:(r + 1) * tq] = unseen

        used = jnp.max(jnp.where(selv, 1.0, 0.0), axis=1, keepdims=True)
        bit_id = lax.broadcasted_iota(jnp.int32, (rows_b, 1), 0) & (WORD_BITS - 1)
        weighted = used * jnp.left_shift(1, bit_id).astype(F32)
        for i in range(nb // WORD_BITS):
            if (i + 1) * WORD_BITS <= rows_b:
                flags[i] = jnp.sum(weighted[i * WORD_BITS:(i + 1) * WORD_BITS, :]).astype(jnp.int32)
            else:
                flags[i] = 0

    n_var = nb // sel_chunk
    cmp_chunk = sel_chunk * (SEL_LEN // CMP_STRIDE)
    seen_c = (t0 + tq - CMP_LEN) // CMP_STRIDE + 1
    seen_b = (t0 + tq) // SEL_LEN
    variant = jnp.maximum((seen_c + cmp_chunk - 1) // cmp_chunk, (seen_b + sel_chunk - 1) // sel_chunk)
    for v in range(1, n_var + 1):
        pl.when(variant == v)(functools.partial(compressed_and_select, min(v * cmp_chunk, nc), v * sel_chunk))
    o_cmp = o_cmp_scr[...]

    n_own = tq // SEL_LEN
    off_q = pl.multiple_of(t0, tq)
    kstage[0:tq, :] = ksaug[pl.ds(off_q, tq), :]
    vstage[0:tq, 0:dh] = vs_ref[0, pl.ds(off_q, tq), :]

    def stage_block(j, cnt):
        src = pl.multiple_of(j * SEL_LEN, SEL_LEN)
        dst = pl.multiple_of(cnt * SEL_LEN, SEL_LEN)
        kstage[pl.ds(dst, SEL_LEN), :] = ksaug[pl.ds(src, SEL_LEN), :]
        vstage[pl.ds(dst, SEL_LEN), 0:dh] = vs_ref[0, pl.ds(src, SEL_LEN), :]

    def gather(j, cnt):
        bit = jnp.right_shift(flags[jnp.right_shift(j, WORD_SHIFT)], j & (WORD_BITS - 1)) & 1
        pl.when(bit == 1)(functools.partial(stage_block, j, cnt))
        return cnt + bit

    n_past = jnp.right_shift(t0, SEL_SHIFT)
    near_lo = jnp.maximum(n_past - NEAR_BLOCKS, 0)
    first = jnp.where(near_lo > 0, flags[0] & 1, 0)
    pl.when(first == 1)(functools.partial(stage_block, 0, n_own))

    def far_word(w, cnt):
        lo = jnp.maximum(w * WORD_BITS, 1)
        hi = jnp.minimum((w + 1) * WORD_BITS, near_lo)
        rest = jnp.where(w == 0, flags[w] & -2, flags[w])
        return lax.cond(rest != 0, lambda c: lax.fori_loop(lo, hi, gather, c), lambda c: c, cnt)

    n_blocks = lax.fori_loop(0, jnp.right_shift(near_lo + WORD_BITS - 1, WORD_SHIFT), far_word, n_own + first)
    n_blocks = lax.fori_loop(near_lo, n_past, gather, n_blocks)
    per_tile = nk // SEL_LEN
    n_tiles = (n_blocks + per_tile - 1) // per_tile
    pad_block = jnp.where(lax.broadcasted_iota(jnp.int32, (SEL_LEN, k_w), 1) == PAD_COL, 1.0, 0.0).astype(BF16)

    def pad(j, carry):
        dst = pl.multiple_of(j * SEL_LEN, SEL_LEN)
        kstage[pl.ds(dst, SEL_LEN), :] = pad_block
        return carry

    lax.fori_loop(n_blocks, n_tiles * per_tile, pad, 0)

    own_pos = t0 + lax.broadcasted_iota(jnp.int32, (tq, 1), 0)
    s_first = _dot(kstage[0:nk, :], q_t[...])
    own = jnp.where(own_pos <= trow, s_first[0:tq], MASK_NEG)
    s_a[0:tq, :] = own
    s_a[tq:, :] = s_first[tq:]
    max_a = jnp.maximum(jnp.max(own, axis=0, keepdims=True), jnp.max(s_first[tq:], axis=0, keepdims=True))

    m_w = jnp.maximum(win_max[0:1, :], win_max[1:2, :])
    e_merged = jnp.exp2(s_win[0:nkw, :] - m_w)
    e_mid = jnp.exp2(s_win[nkw:2 * nkw, :] - m_w)
    weights = jnp.concatenate([jnp.where(newer, e_merged, 0.0).astype(BF16), e_mid.astype(BF16),
                               jnp.where(newer, 0.0, e_merged).astype(BF16)], axis=0)
    v_win = jnp.concatenate([window_aug_t(vw_ref[0, pl.ds(win_off[i], nkw), :]) for i in range(n_wt)], axis=1)
    acc_w = _dot(v_win, weights)
    o_win = acc_w[0:dh] / acc_w[dh:dh + 1]

    def tile_accumulate(i, s_ref, col_max, state):
        return _accumulate(s_ref, col_max, staged_aug_t(vstage[pl.ds(pl.multiple_of(i * nk, nk), nk), :]), state)

    def scores_and_accumulate(i_next, s_next_ref, i_cur, s_cur_ref, max_cur, state):
        m_old, acc_old = state
        m_new = jnp.maximum(m_old, max_cur)
        alpha = jnp.exp2(m_old - m_new)
        s_next = _dot(kstage[pl.ds(pl.multiple_of(i_next * nk, nk), nk), :], q_t[...])
        s_next_ref[...] = s_next
        rows = nk // OVERLAP_CHUNKS
        partial, weights = [], []
        for c in range(OVERLAP_CHUNKS):
            rs = slice(c * rows, (c + 1) * rows)
            partial.append(jnp.max(s_next[rs], axis=0, keepdims=True))
            m_c = m_new + partial[c] * 0.0
            weights.append(jnp.exp2(s_cur_ref[rs, :] - m_c).astype(BF16))
        v_t = staged_aug_t(vstage[pl.ds(pl.multiple_of(i_cur * nk, nk), nk), :])
        acc = alpha * acc_old + _dot(v_t, jnp.concatenate(weights, axis=0))
        return (m_new, acc), functools.reduce(jnp.maximum, partial)

    def slc_pair(j, carry):
        state, max_a = carry
        i = 2 * j
        state, max_b = scores_and_accumulate(i + 1, s_b, i, s_a, max_a, state)
        state, max_a = scores_and_accumulate(i + 2, s_a, i + 1, s_b, max_b, state)
        return state, max_a

    n_pairs = (n_tiles - 1) // 2
    state, max_a = lax.fori_loop(0, n_pairs, slc_pair, (init, max_a))
    i_a = 2 * n_pairs

    def two_left(state):
        state, max_b = scores_and_accumulate(i_a + 1, s_b, i_a, s_a, max_a, state)
        return tile_accumulate(i_a + 1, s_b, max_b, state)

    def one_left(state):
        return tile_accumulate(i_a, s_a, max_a, state)

    _, acc = lax.cond(n_tiles - i_a == 2, two_left, one_left, state)
    o_slc = acc[0:dh] / acc[dh:dh + 1]

    gates = jnp.transpose(_sigmoid(gn_ref[...]))
    heads = []
    for r in range(r_heads):
        cs = slice(r * tq, (r + 1) * tq)
        heads.append(gates[3 * r:3 * r + 1, :] * o_cmp[:, cs] + gates[3 * r + 1:3 * r + 2, :] * o_slc[:, cs]
                     + gates[3 * r + 2:3 * r + 3, :] * o_win[:, cs])
    o = jnp.transpose(jnp.concatenate(heads, axis=0))
    o_ref[...] = (_sigmoid(gm_ref[...].astype(F32)) * o).astype(o_ref.dtype)


def _alibi_tables(seq, nc, tq):
    import ml_dtypes
    bf = ml_dtypes.bfloat16
    h = np.arange(1, N_HEADS + 1, dtype=np.float32)
    slopes = (np.exp2(-ALIBI_MAX_BIAS * h / N_HEADS) * LOG2_E).astype(np.float32)
    s1 = slopes.astype(bf).astype(np.float32)
    s2 = (slopes - s1).astype(bf).astype(np.float32)
    s3 = (slopes - s1 - s2).astype(bf).astype(np.float32)
    dh = HEAD_DIM
    qal = np.zeros((N_HEADS, dh), np.float32)
    for rep in range(3):
        qal[:, 3 * rep + 0] = s1
        qal[:, 3 * rep + 1] = s2
        qal[:, 3 * rep + 2] = s3
    qal[:, PAD_COL - dh] = MASK_NEG
    qal = qal.reshape(N_KV_GROUPS, HEADS_PER_GROUP, dh).transpose(0, 2, 1)
    qal_p = np.repeat(qal, tq, axis=2)

    nb = seq // SEL_LEN
    c = np.arange(nc)
    cpos = np.zeros((nc, 2 * dh), np.float32)
    cpos[:, dh:dh + 3] = ((c // 16) * 16 * CMP_STRIDE)[:, None]
    cpos[:, dh + 3:dh + 6] = ((c % 16) * CMP_STRIDE)[:, None]
    cpos[:, dh + 6:dh + 9] = CMP_LEN - 1

    r_sel = SEL_LEN // CMP_STRIDE
    r_cmp = CMP_LEN // CMP_STRIDE
    wsel = np.zeros((nb, nc), np.float32)
    for j in range(nb):
        for mm in range(r_sel):
            for nn in range(r_cmp):
                ci = r_sel * j + mm - nn
                if 0 <= ci < nc - 1:
                    wsel[j, ci] += 1.0
    as_bf = lambda a: jnp.asarray(a.astype(bf))
    return as_bf(qal_p), as_bf(cpos), as_bf(wsel)


def _nsa(qm, kvm, kcm, vcm, gnm, gma, batch, seq, tq, nk, nkw):
    t = qm.shape[0]
    g = N_KV_GROUPS
    dh = HEAD_DIM
    nqt = seq // tq
    nc = kcm.shape[1]
    nb = seq // SEL_LEN
    gw = HEADS_PER_GROUP * dh
    m_cols = HEADS_PER_GROUP * tq
    qal, cpos, wsel = _alibi_tables(seq, nc, tq)
    assert nkw == tq and WINDOW == 2 * tq, "the window branch merges the new and old key tiles of a query tile"
    n_wt = 2
    sel_chunk = max(WORD_BITS, nb // 8)
    per_tile = nk // SEL_LEN
    stage_rows = -(-nb // per_tile) * per_tile * SEL_LEN
    tps = NSA_TILES_PER_STEP if nqt % NSA_TILES_PER_STEP == 0 else 1
    n_steps = nqt // tps
    rq = tps * tq
    row = lambda b, gg, i: b * n_steps + i
    slab = lambda base: pl.BlockSpec((1, seq, dh), lambda b, gg, i: (base + gg, b, 0))
    cmp_slab = pl.BlockSpec((1, nc, dh), lambda b, gg, i: (b * g + gg, 0, 0))
    const = lambda a: pl.BlockSpec(a.shape, lambda b, gg, i: (0,) * a.ndim)
    return pl.pallas_call(
        functools.partial(_nsa_kernel, tq=tq, tiles_per_step=tps, nk=nk, nkw=nkw, seq=seq, sel_chunk=sel_chunk),
        grid=(batch, g, n_steps),
        in_specs=[
            pl.BlockSpec((rq, gw), lambda b, gg, i: (row(b, gg, i), gg)),
            cmp_slab, cmp_slab,
            slab(0), slab(g), slab(2 * g), slab(3 * g),
            pl.BlockSpec((rq, LANES), lambda b, gg, i: (row(b, gg, i), gg)),
            pl.BlockSpec((rq, gw), lambda b, gg, i: (row(b, gg, i), gg)),
            pl.BlockSpec((1, dh, m_cols), lambda b, gg, i: (gg, 0, 0)),
            const(cpos), const(wsel),
        ],
        out_specs=pl.BlockSpec((rq, gw), lambda b, gg, i: (row(b, gg, i), gg)),
        out_shape=jax.ShapeDtypeStruct((t, g * gw), BF16),
        scratch_shapes=[
            pltpu.VMEM((2 * dh + nb, m_cols), BF16),
            pltpu.VMEM((seq, 2 * dh + nb), BF16),
            pltpu.VMEM((seq, 2 * dh), BF16),
            pltpu.VMEM((nc, 2 * dh), BF16),
            pltpu.VMEM((stage_rows, 2 * dh + nb), BF16),
            pltpu.VMEM((stage_rows, 2 * dh), BF16),
            pltpu.VMEM((n_wt * nkw, m_cols), F32),
            pltpu.VMEM((nk, m_cols), F32),
            pltpu.VMEM((nk, m_cols), F32),
            pltpu.VMEM((SUBLANES, m_cols), F32),
            pltpu.VMEM((dh, m_cols), F32),
            pltpu.SMEM((nb // WORD_BITS,), jnp.int32),
        ],
        compiler_params=_params(("arbitrary", "arbitrary", "arbitrary")),
        name="nsa",
    )(qm, kcm, vcm, kvm, kvm, kvm, kvm, gnm, gma, qal, cpos, wsel)


def _ffn_kernel(x_ref, yr_ref, ya_ref, wo_ref, gmix_ref, gpre_ref, wg_ref, wu_ref, wd_ref, gpost_ref,
                p_ref, wpg_ref, bpg_ref, wpp_ref, o_ref, x1_ref, h_ref, acc_ref):
    j = pl.program_id(1)

    @pl.when(j == 0)
    def _():
        y = (yr_ref[...].astype(F32) + ya_ref[...].astype(F32)).astype(BF16)
        x1 = x_ref[...] + _rmsnorm(_dot(y, wo_ref[...]), gmix_ref[...])
        x1_ref[...] = x1
        h_ref[...] = _rmsnorm(x1, gpre_ref[...]).astype(BF16)
        acc_ref[...] = jnp.zeros(acc_ref.shape, F32)

    h = h_ref[...]
    gate = _dot(h, wg_ref[...])
    up = _dot(h, wu_ref[...])
    act = (gate * _sigmoid(gate) * up).astype(BF16)
    acc_ref[...] += _dot(act, wd_ref[...])

    @pl.when(j == pl.num_programs(1) - 1)
    def _():
        x2 = x1_ref[...] + _rmsnorm(acc_ref[...], gpost_ref[...])
        gate_p = _sigmoid(_dot(x2.astype(BF16), wpg_ref[...]) + bpg_ref[...])
        o_ref[...] = x2 + gate_p * _dot(p_ref[...].astype(BF16), wpp_ref[...])


def _ffn(x, yr, ya, wo, gmix, gpre, wgu, wd, gpost, p, wpg, bpg, wpp, tm, tf):
    t, d = x.shape
    dff = wd.shape[0]
    nf = dff // tf
    dp = p.shape[1]
    rows = pl.BlockSpec((tm, d), lambda i, j: (i, 0))
    vec = pl.BlockSpec((1, d), lambda i, j: (0, 0))
    square = pl.BlockSpec((d, d), lambda i, j: (0, 0))
    return pl.pallas_call(
        _ffn_kernel,
        grid=(t // tm, nf),
        in_specs=[
            rows, rows, rows, square, vec, vec,
            pl.BlockSpec((d, tf), lambda i, j: (0, j)),
            pl.BlockSpec((d, tf), lambda i, j: (0, nf + j)),
            pl.BlockSpec((tf, d), lambda i, j: (j, 0)),
            vec,
            pl.BlockSpec((tm, dp), lambda i, j: (i, 0)),
            square,
            vec,
            pl.BlockSpec((dp, d), lambda i, j: (0, 0)),
        ],
        out_specs=rows,
        out_shape=jax.ShapeDtypeStruct((t, d), F32),
        scratch_shapes=[pltpu.VMEM((tm, d), F32), pltpu.VMEM((tm, d), BF16), pltpu.VMEM((tm, d), F32)],
        compiler_params=_params(("parallel", "arbitrary")),
        name="out_ffn_ple",
    )(x, yr, ya, wo, gmix, gpre, wgu, wgu, wd, gpost, p, wpg, bpg, wpp)


def _block_diag_chunks(w):
    n, bs, _ = w.shape
    per = LANES // bs
    w = w.reshape(n // per, per, bs, bs)
    eye = jnp.eye(per, dtype=w.dtype)
    return jnp.einsum('cpij,pq->cpiqj', w, eye).reshape(n // per, LANES, LANES)


def _layer(x, p, norm_mix_pre, norm_mix_post, w_in, conv_w, conv_b, lru_wa, lru_ba, lru_wx, lru_bx,
           lru_lambda, cmp_pos_k, cmp_pos_v, cmp_k_w1, cmp_k_w2, cmp_v_w1, cmp_v_w2, w_out,
           norm_ffn_pre, norm_ffn_post, ffn_w_gate_up, ffn_w_down, ple_w_proj, ple_w_gate, ple_b_gate,
           batch, seq):
    t, d = x.shape
    d_attn = N_HEADS * HEAD_DIM
    d_kv = N_KV_GROUPS * HEAD_DIM
    row2 = lambda v: v.reshape(1, -1)

    o_q = 2 * d
    o_kv = o_q + d_attn
    o_gn = o_kv + 6 * d_kv
    o_gm = o_gn + 3 * N_HEADS
    w_f = jnp.concatenate([w_in[:, 0:o_q], w_in[:, o_gm:o_gm + 2 * d]], axis=1).astype(BF16)
    w_q = w_in[:, o_q:o_kv].astype(BF16)
    w_kv = w_in[:, o_kv:o_gn].astype(BF16)
    per_g = 3 * HEADS_PER_GROUP
    w_gn = w_in[:, o_gn:o_gm].reshape(d, N_KV_GROUPS, per_g)
    w_gn = jnp.pad(w_gn, ((0, 0), (0, 0), (0, LANES - per_g))).reshape(d, N_KV_GROUPS * LANES).astype(BF16)

    g_pre = row2(norm_mix_pre)
    tm = min(512, t)
    qm, gnm, kv_chunks, kvm = _norm_qkv(x, g_pre, jnp.concatenate([w_q, w_gn, w_kv], axis=1), d_attn,
                                        N_KV_GROUPS * LANES, 2 * N_KV_GROUPS, tm)

    yr, gma = _rnn(x, g_pre, w_f, conv_w, row2(conv_b), _block_diag_chunks(lru_wa).astype(BF16),
                   _block_diag_chunks(lru_wx).astype(BF16), row2(lru_ba), row2(lru_bx), row2(lru_lambda),
                   batch, seq, min(512, seq))

    nchunk = seq // CMP_STRIDE
    kcm, vcm = _compress(kv_chunks, cmp_k_w1.astype(BF16), cmp_k_w2.astype(BF16), cmp_pos_k.reshape(1, -1),
                         cmp_v_w1.astype(BF16), cmp_v_w2.astype(BF16), cmp_pos_v.reshape(1, -1), batch, nchunk)

    tq = min(256, seq)
    ya = _nsa(qm, kvm, kcm, vcm, gnm, gma, batch, seq, tq, min(NSA_KEY_TILE, seq), min(256, seq))

    dff = ffn_w_down.shape[0]
    tf = dff
    return _ffn(x, yr, ya, w_out.astype(BF16), row2(norm_mix_post), row2(norm_ffn_pre),
                ffn_w_gate_up.astype(BF16), ffn_w_down.astype(BF16), row2(norm_ffn_post),
                p, ple_w_gate.astype(BF16), row2(ple_b_gate), ple_w_proj.astype(BF16), tm, tf)


def kernel(x, p, norm_mix_pre, norm_mix_post, w_in, conv_w, conv_b, lru_wa, lru_ba, lru_wx, lru_bx, lru_lambda, cmp_pos_k, cmp_pos_v, cmp_k_w1, cmp_k_w2, cmp_v_w1, cmp_v_w2, w_out, norm_ffn_pre, norm_ffn_post, ffn_w_gate_up, ffn_w_down, ple_w_proj, ple_w_gate, ple_b_gate):
    batch, seq, d = x.shape
    depth = w_in.shape[0]
    xf = x.reshape(batch * seq, d)
    for i in range(depth):
        xf = _layer(xf, p[i].reshape(batch * seq, -1), norm_mix_pre[i], norm_mix_post[i], w_in[i], conv_w[i],
                    conv_b[i], lru_wa[i], lru_ba[i], lru_wx[i], lru_bx[i], lru_lambda[i], cmp_pos_k[i],
                    cmp_pos_v[i], cmp_k_w1[i], cmp_k_w2[i], cmp_v_w1[i], cmp_v_w2[i], w_out[i],
                    norm_ffn_pre[i], norm_ffn_post[i], ffn_w_gate_up[i], ffn_w_down[i], ple_w_proj[i],
                    ple_w_gate[i], ple_b_gate[i], batch, seq)
    return xf.reshape(batch, seq, d)
```

```python
import functools

import numpy as np
import jax
import jax.numpy as jnp
from jax import lax
from jax.experimental import pallas as pl
from jax.experimental.pallas import tpu as pltpu

CONV_WIDTH = 4
LRU_C = 8.0
N_HEADS = 16
HEAD_DIM = 64
N_KV_GROUPS = 4
HEADS_PER_GROUP = N_HEADS // N_KV_GROUPS
CMP_LEN = 32
CMP_STRIDE = 16
SEL_LEN = 64
SEL_TOPK = 16
WINDOW = 512
ALIBI_MAX_BIAS = 8.0
NORM_EPS = 1e-6

LANES = 128
SUBLANES = 8
VMEM_LIMIT_BYTES = 56 * 1024 * 1024

MASK_NEG = -1e30
LOG2_E = 1.4426950408889634
N_ALIBI_COLS = 9
PAD_COL = HEAD_DIM + N_ALIBI_COLS
WORD_BITS = 16
WORD_SHIFT = 4
SEL_SHIFT = 6
N_FORCED = 3
NEAR_BLOCKS = 20
FILL_ROWS = 512
NSA_KEY_TILE = 11 * SEL_LEN
SCAN_UNROLL = 64
NSA_TILES_PER_STEP = 4
OVERLAP_CHUNKS = 4
BF16 = jnp.bfloat16
F32 = jnp.float32


def _dot(a, b):
    return jnp.dot(a, b, preferred_element_type=F32)


def _dot_nt(a, b):
    return lax.dot_general(a, b, (((1,), (1,)), ((), ())), preferred_element_type=F32)


def _sigmoid(x):
    return 0.5 * jnp.tanh(0.5 * x) + 0.5


def _gelu_tanh(x):
    c = np.float32(np.sqrt(2.0 / np.pi))
    half = 0.5 * x
    return half + half * jnp.tanh(x * (c + (c * 0.044715) * (x * x)))


def _rmsnorm(x, g):
    ms = jnp.mean(x * x, axis=-1, keepdims=True)
    return x * lax.rsqrt(ms + NORM_EPS) * g


def _params(sem):
    return pltpu.CompilerParams(dimension_semantics=sem, vmem_limit_bytes=VMEM_LIMIT_BYTES)


def _norm_qkv_kernel(x_ref, g_ref, w_ref, q_ref, gn_ref, cv_ref, kv_ref, slab_scr):
    h = _rmsnorm(x_ref[...], g_ref[...]).astype(BF16)
    z = _dot(h, w_ref[...])
    nq = q_ref.shape[1]
    ngn = gn_ref.shape[1]
    q_ref[...] = z[:, 0:nq].astype(q_ref.dtype)
    gn_ref[...] = z[:, nq:nq + ngn]
    n_cv = cv_ref.shape[0]
    rows = cv_ref.shape[1]
    for c in range(n_cv):
        lo = nq + ngn + c * HEAD_DIM
        slab_scr[...] = z[:, lo:lo + HEAD_DIM]
        for l in range(CMP_STRIDE):
            cv_ref[c, :, l * HEAD_DIM:(l + 1) * HEAD_DIM] = (
                slab_scr[pl.ds(l, rows, stride=CMP_STRIDE), :].astype(cv_ref.dtype))
    for c in range(kv_ref.shape[0]):
        lo = nq + ngn + (n_cv + c) * HEAD_DIM
        kv_ref[c] = z[:, lo:lo + HEAD_DIM].astype(kv_ref.dtype)


def _norm_qkv(x, g, w, nq, ngn, n_cv, tm):
    t, k = x.shape
    n = w.shape[1]
    n_kv = (n - nq - ngn) // HEAD_DIM - n_cv
    return pl.pallas_call(
        _norm_qkv_kernel,
        grid=(t // tm,),
        in_specs=[
            pl.BlockSpec((tm, k), lambda i: (i, 0)),
            pl.BlockSpec((1, k), lambda i: (0, 0)),
            pl.BlockSpec((k, n), lambda i: (0, 0)),
        ],
        out_specs=[
            pl.BlockSpec((tm, nq), lambda i: (i, 0)),
            pl.BlockSpec((tm, ngn), lambda i: (i, 0)),
            pl.BlockSpec((n_cv, tm // CMP_STRIDE, CMP_STRIDE * HEAD_DIM), lambda i: (0, i, 0)),
            pl.BlockSpec((n_kv, tm, HEAD_DIM), lambda i: (0, i, 0)),
        ],
        out_shape=[
            jax.ShapeDtypeStruct((t, nq), BF16),
            jax.ShapeDtypeStruct((t, ngn), F32),
            jax.ShapeDtypeStruct((n_cv, t // CMP_STRIDE, CMP_STRIDE * HEAD_DIM), BF16),
            jax.ShapeDtypeStruct((n_kv, t, HEAD_DIM), BF16),
        ],
        scratch_shapes=[pltpu.VMEM((tm, HEAD_DIM), F32)],
        compiler_params=_params(("parallel",)),
        name="norm_qkv",
    )(x, g, w)


def _rnn_kernel(x_ref, g_ref, wf_ref, cw_ref, cb_ref, wa_ref, wx_ref, ba_ref, bx_ref, lam_ref,
                o_ref, gma_ref, xbuf, a_scr, b_scr, h_scr, *, ts):
    s_idx = pl.program_id(1)
    d = o_ref.shape[1]
    halo = SUBLANES

    @pl.when(s_idx == 0)
    def _():
        xbuf[0:halo, :] = jnp.zeros((halo, d), F32)
        h_scr[...] = jnp.zeros(h_scr.shape, F32)

    hn = _rmsnorm(x_ref[...], g_ref[...]).astype(BF16)
    xr = _dot(hn, wf_ref[:, 0:d])
    gr = _dot(hn, wf_ref[:, d:2 * d])
    gm = _dot(hn, wf_ref[:, 2 * d:3 * d])
    gma_ref[...] = _dot(hn, wf_ref[:, 3 * d:4 * d]).astype(gma_ref.dtype)

    xbuf[halo:halo + ts, :] = xr
    xc = cb_ref[...] + xbuf[halo:halo + ts, :] * cw_ref[CONV_WIDTH - 1:CONV_WIDTH, :]
    for k in range(1, CONV_WIDTH):
        xc = xc + xbuf[halo - k:halo - k + ts, :] * cw_ref[CONV_WIDTH - 1 - k:CONV_WIDTH - k, :]
    xbuf[0:halo, :] = xbuf[ts:ts + halo, :]

    xcb = xc.astype(BF16)
    n_chunks = d // LANES
    neg_sp = -LRU_C * (jnp.maximum(-lam_ref[...], 0.0) + jnp.log(1.0 + jnp.exp(-jnp.abs(lam_ref[...]))))
    for c in range(n_chunks):
        sl = slice(c * LANES, (c + 1) * LANES)
        xk = xcb[:, sl]
        r = _sigmoid(_dot(xk, wa_ref[c]) + ba_ref[:, sl])
        i = _sigmoid(_dot(xk, wx_ref[c]) + bx_ref[:, sl])
        log_a = r * neg_sp[:, sl]
        a = jnp.exp(log_a)
        a_scr[:, sl] = a
        u = 1.0 - a * a
        root = jnp.where(u > 0.0, u * lax.rsqrt(u), 0.0)
        b_scr[:, sl] = root * (i * xc[:, sl])

    row_id = lax.broadcasted_iota(jnp.int32, (SUBLANES, d), 0)

    def group(gi, h):
        base = pl.multiple_of(gi * SUBLANES, SUBLANES)
        a = a_scr[pl.ds(base, SUBLANES), :]
        b = b_scr[pl.ds(base, SUBLANES), :]
        s = 1
        while s < SUBLANES:
            reach = row_id >= s
            b = jnp.where(reach, a * pltpu.roll(b, s, 0) + b, b)
            a = jnp.where(reach, a * pltpu.roll(a, s, 0), a)
            s *= 2
        hs = a * h + b
        a_scr[pl.ds(base, SUBLANES), :] = hs
        return jnp.broadcast_to(hs[SUBLANES - 1:SUBLANES, :], (SUBLANES, d))

    h_scr[...] = lax.fori_loop(0, ts // SUBLANES, group, h_scr[...], unroll=SCAN_UNROLL)
    o_ref[...] = (_sigmoid(gm) * a_scr[...] * _gelu_tanh(gr)).astype(o_ref.dtype)


def _rnn(x, g, w_f, conv_w, conv_b, wa, wx, ba, bx, lam, batch, seq, ts):
    t = x.shape[0]
    d = conv_w.shape[1]
    ns = seq // ts
    rows = pl.BlockSpec((ts, d), lambda b, s: (b * ns + s, 0))
    vec = lambda r: pl.BlockSpec((r, d), lambda b, s: (0, 0))
    out = jax.ShapeDtypeStruct((t, d), BF16)
    return pl.pallas_call(
        functools.partial(_rnn_kernel, ts=ts),
        grid=(batch, ns),
        in_specs=[
            rows, vec(1),
            pl.BlockSpec(w_f.shape, lambda b, s: (0, 0)),
            vec(CONV_WIDTH), vec(1),
            pl.BlockSpec(wa.shape, lambda b, s: (0, 0, 0)),
            pl.BlockSpec(wx.shape, lambda b, s: (0, 0, 0)),
            vec(1), vec(1), vec(1),
        ],
        out_specs=[rows, rows],
        out_shape=[out, out],
        scratch_shapes=[
            pltpu.VMEM((ts + SUBLANES, d), F32),
            pltpu.VMEM((ts, d), F32),
            pltpu.VMEM((ts, d), F32),
            pltpu.VMEM((SUBLANES, d), F32),
        ],
        compiler_params=_params(("parallel", "arbitrary")),
        name="proj_rnn_mixer",
    )(x, g, w_f, conv_w, conv_b, wa, wx, ba, bx, lam)


def _compress_kernel(ak_ref, av_ref, w1k_ref, w2k_ref, pk_ref, w1v_ref, w2v_ref, pv_ref, ok_ref, ov_ref):
    def one(a_ref, w1_ref, w2_ref, p_ref, o_ref):
        a = a_ref[0]
        half = a.shape[1]
        nchunk = a.shape[0]
        lo = _dot(a, w1_ref[0:half, :])
        hi = _dot(a, w1_ref[half:2 * half, :])
        pb = _dot(jnp.broadcast_to(p_ref[...], (SUBLANES, 2 * half)).astype(BF16), w1_ref[...])[0:1, :]
        h = lo + pltpu.roll(hi, nchunk - 1, 0) + pb
        o_ref[0] = _dot(_gelu_tanh(h).astype(BF16), w2_ref[...]).astype(o_ref.dtype)

    one(ak_ref, w1k_ref, w2k_ref, pk_ref, ok_ref)
    one(av_ref, w1v_ref, w2v_ref, pv_ref, ov_ref)


def _compress(kv_chunks, w1k, w2k, pk, w1v, w2v, pv, batch, nchunk):
    g = N_KV_GROUPS
    width = kv_chunks.shape[2]
    full = lambda a: pl.BlockSpec(a.shape, lambda b, gg: (0,) * a.ndim)
    out = jax.ShapeDtypeStruct((batch * g, nchunk, HEAD_DIM), BF16)
    return pl.pallas_call(
        _compress_kernel,
        grid=(batch, g),
        in_specs=[
            pl.BlockSpec((1, nchunk, width), lambda b, gg: (gg, b, 0)),
            pl.BlockSpec((1, nchunk, width), lambda b, gg: (g + gg, b, 0)),
            full(w1k), full(w2k), full(pk), full(w1v), full(w2v), full(pv),
        ],
        out_specs=[
            pl.BlockSpec((1, nchunk, HEAD_DIM), lambda b, gg: (b * g + gg, 0, 0)),
            pl.BlockSpec((1, nchunk, HEAD_DIM), lambda b, gg: (b * g + gg, 0, 0)),
        ],
        out_shape=[out, out],
        compiler_params=_params(("parallel", "parallel")),
        name="compress",
    )(kv_chunks, kv_chunks, w1k, w2k, pk, w1v, w2v, pv)


ONES_ROWS = 16


def _accumulate(s_ref, col_max, v_aug_t, state):
    m_old, acc_old = state
    m_new = jnp.maximum(m_old, col_max)
    alpha = jnp.exp2(m_old - m_new)
    e = jnp.exp2(s_ref[...] - m_new).astype(BF16)
    return m_new, alpha * acc_old + _dot(v_aug_t, e)


def _nsa_kernel(q_ref, kc_ref, vc_ref, ks_ref, vs_ref, kw_ref, vw_ref, gn_ref, gm_ref,
                qal_ref, cpos_ref, wsel_ref, o_ref, *scratch, tq, tiles_per_step, **static):
    def tile(j, carry):
        rows = pl.ds(pl.multiple_of(j * tq, tq), tq)
        _nsa_tile(pl.program_id(2) * tiles_per_step + j, q_ref.at[rows, :], kc_ref, vc_ref, ks_ref, vs_ref,
                  kw_ref, vw_ref, gn_ref.at[rows, :], gm_ref.at[rows, :], qal_ref, cpos_ref, wsel_ref,
                  o_ref.at[rows, :], *scratch, tq=tq, **static)
        return carry

    lax.fori_loop(0, tiles_per_step, tile, 0)


def _nsa_tile(qt, q_ref, kc_ref, vc_ref, ks_ref, vs_ref, kw_ref, vw_ref, gn_ref, gm_ref,
              qal_ref, cpos_ref, wsel_ref,
              o_ref, q_t, ksaug, kwaug, kcaug, kstage, vstage, s_win, s_a, s_b, win_max, o_cmp_scr, flags,
              *, tq, nk, nkw, seq, sel_chunk):
    b = pl.program_id(0)
    g = pl.program_id(1)
    r_heads = HEADS_PER_GROUP
    dh = HEAD_DIM
    m_cols = r_heads * tq
    nb = seq // SEL_LEN
    nc = kc_ref.shape[1]
    a_w = 2 * dh
    k_w = a_w + nb

    @pl.when((b == 0) & (g == 0) & (qt == 0))
    def _():
        rows = min(FILL_ROWS, seq)

        def fill(c, carry):
            off = pl.multiple_of(c * rows, rows)
            pos = off + lax.broadcasted_iota(jnp.int32, (rows, k_w), 0)
            lane = lax.broadcasted_iota(jnp.int32, (rows, k_w), 1)
            blk_of = jnp.right_shift(pos, SEL_SHIFT)
            cols = jnp.where((lane >= dh) & (lane < dh + 3), blk_of * SEL_LEN,
                             jnp.where((lane >= dh + 3) & (lane < dh + 6), pos & (SEL_LEN - 1),
                                       jnp.where(lane == a_w + blk_of, 1, 0)))
            cols = cols.astype(F32).astype(BF16)
            ksaug[pl.ds(off, rows), :] = cols
            kwaug[pl.ds(off, rows), :] = cols[:, 0:a_w]
            return carry

        lax.fori_loop(0, seq // rows, fill, 0)
        kcaug[...] = cpos_ref[...]
        vstage[...] = jnp.ones(vstage.shape, BF16)

    @pl.when(qt == 0)
    def _():
        ksaug[:, 0:dh] = ks_ref[0]
        kwaug[:, 0:dh] = kw_ref[0]
        kcaug[:, 0:dh] = kc_ref[0]
        q_t[dh:a_w, :] = qal_ref[0]

    v_rows = dh + ONES_ROWS
    ri = lax.broadcasted_iota(jnp.int32, (v_rows, 2 * dh), 0)
    ci = lax.broadcasted_iota(jnp.int32, (v_rows, 2 * dh), 1)
    pick = (((ri < dh) & (ri == ci)) | ((ri >= dh) & (ci == dh))).astype(F32).astype(BF16)
    transposed = lambda v: _dot_nt(pick[0:dh, 0:dh], v).astype(BF16)
    staged_aug_t = lambda v: _dot_nt(pick, v).astype(BF16)
    ones_rows = jnp.ones((ONES_ROWS, nkw), BF16)
    window_aug_t = lambda v: jnp.concatenate([transposed(v), ones_rows], axis=0)
    init = (jnp.full((1, m_cols), MASK_NEG, F32), jnp.zeros((v_rows, m_cols), F32))

    t0 = qt * tq
    col = lax.broadcasted_iota(jnp.int32, (1, m_cols), 1)
    trow = t0 + (col & (tq - 1))

    scale = np.float32(HEAD_DIM ** -0.5 * LOG2_E)
    qf = jnp.transpose(q_ref[...].astype(F32) * scale)
    for r in range(r_heads):
        q_t[0:dh, r * tq:(r + 1) * tq] = qf[r * dh:(r + 1) * dh, :].astype(BF16)

    n_wt = 3
    win_off = [pl.multiple_of(jnp.maximum(t0 - i * nkw, 0), nkw) for i in range(n_wt)]
    newer = (t0 + lax.broadcasted_iota(jnp.int32, (nkw, 1), 0)) <= trow

    def window_tile(i):
        s = _dot(kwaug[pl.ds(win_off[i], nkw), :], q_t[0:a_w, :])
        return s if i == 0 else s + jnp.where(t0 - i * nkw >= 0, 0.0, MASK_NEG)

    def window_scores():
        merged = jnp.where(newer, window_tile(0), window_tile(2))
        s_win[0:nkw, :] = merged
        win_max[0:1, :] = jnp.max(merged, axis=0, keepdims=True)
        mid = window_tile(1)
        s_win[nkw:2 * nkw, :] = mid
        win_max[1:2, :] = jnp.max(mid, axis=0, keepdims=True)

    def compressed_and_select(rows_c, rows_b):
        free = max(rows_c - cmp_chunk - tq // CMP_STRIDE, 0)
        s = _dot(kcaug[0:rows_c, :], q_t[0:a_w, :])
        cend = (free + lax.broadcasted_iota(jnp.int32, (rows_c - free, 1), 0)) * CMP_STRIDE + (CMP_LEN - 1)
        tail = jnp.where(cend <= trow, s[free:], MASK_NEG)
        s = jnp.concatenate([s[0:free], tail], axis=0) if free else tail
        m = jnp.max(s, axis=0, keepdims=True)
        e = jnp.exp2(s - m)
        has_key = (trow >= CMP_LEN - 1).astype(F32)
        p = e * (has_key / jnp.sum(e, axis=0, keepdims=True))
        o_cmp_scr[...] = _dot(transposed(vc_ref[0, 0:rows_c, :]), p.astype(BF16))
        imp = p[:, 0:tq]
        for r in range(1, r_heads):
            imp = imp + p[:, r * tq:(r + 1) * tq]

        hi = imp.astype(BF16)
        lo = (imp - hi.astype(F32)).astype(BF16)
        wsel = wsel_ref[0:rows_b, 0:rows_c]
        imp_t = _dot(wsel, hi) + _dot(wsel, lo)

        window_scores()

        blk = lax.broadcasted_iota(jnp.int32, (rows_b, tq), 0).astype(F32)
        tq_l = t0 + lax.broadcasted_iota(jnp.int32, (rows_b, tq), 1)
        cur = jnp.right_shift(tq_l, SEL_SHIFT).astype(F32)
        valid = blk <= cur
        sel = jnp.where(blk == 0.0, 1.0, jnp.where(blk == cur, 1.0, jnp.where(blk == cur - 1.0, 1.0, 0.0)))
        score = jnp.where(valid, jnp.where(sel > 0.0, -1.0, imp_t), -1.0)
        for _ in range(min(SEL_TOPK, nb) - N_FORCED):
            mx = jnp.max(score, axis=0, keepdims=True)
            idx = jnp.min(jnp.where(score == mx, blk, float(nb)), axis=0, keepdims=True)
            hit = blk == idx
            sel = jnp.where(hit, 1.0, sel)
            score = jnp.where(hit, -2.0, score)
        selv = jnp.where(valid, sel, 0.0) > 0.0
        selneg_t = jnp.where(selv, 0.0, MASK_NEG).astype(BF16)
        unseen = jnp.full((nb - rows_b, tq), MASK_NEG, BF16)
        for r in range(r_heads):
            q_t[a_w:a_w + rows_b, r * tq:(r + 1) * tq] = selneg_t
            if rows_b < nb:
                q_t[a_w + rows_b:, r * tq:(r + 1) * tq] = unseen

        used = jnp.max(jnp.where(selv, 1.0, 0.0), axis=1, keepdims=True)
        bit_id = lax.broadcasted_iota(jnp.int32, (rows_b, 1), 0) & (WORD_BITS - 1)
        weighted = used * jnp.left_shift(1, bit_id).astype(F32)
        for i in range(nb // WORD_BITS):
            if (i + 1) * WORD_BITS <= rows_b:
                flags[i] = jnp.sum(weighted[i * WORD_BITS:(i + 1) * WORD_BITS, :]).astype(jnp.int32)
            else:
                flags[i] = 0

    n_var = nb // sel_chunk
    cmp_chunk = sel_chunk * (SEL_LEN // CMP_STRIDE)
    seen_c = (t0 + tq - CMP_LEN) // CMP_STRIDE + 1
    seen_b = (t0 + tq) // SEL_LEN
    variant = jnp.maximum((seen_c + cmp_chunk - 1) // cmp_chunk, (seen_b + sel_chunk - 1) // sel_chunk)
    for v in range(1, n_var + 1):
        pl.when(variant == v)(functools.partial(compressed_and_select, min(v * cmp_chunk, nc), v * sel_chunk))
    o_cmp = o_cmp_scr[...]

    n_own = tq // SEL_LEN
    off_q = pl.multiple_of(t0, tq)
    kstage[0:tq, :] = ksaug[pl.ds(off_q, tq), :]
    vstage[0:tq, 0:dh] = vs_ref[0, pl.ds(off_q, tq), :]

    def stage_block(j, cnt):
        src = pl.multiple_of(j * SEL_LEN, SEL_LEN)
        dst = pl.multiple_of(cnt * SEL_LEN, SEL_LEN)
        kstage[pl.ds(dst, SEL_LEN), :] = ksaug[pl.ds(src, SEL_LEN), :]
        vstage[pl.ds(dst, SEL_LEN), 0:dh] = vs_ref[0, pl.ds(src, SEL_LEN), :]

    def gather(j, cnt):
        bit = jnp.right_shift(flags[jnp.right_shift(j, WORD_SHIFT)], j & (WORD_BITS - 1)) & 1
        pl.when(bit == 1)(functools.partial(stage_block, j, cnt))
        return cnt + bit

    n_past = jnp.right_shift(t0, SEL_SHIFT)
    near_lo = jnp.maximum(n_past - NEAR_BLOCKS, 0)
    first = jnp.where(near_lo > 0, flags[0] & 1, 0)
    pl.when(first == 1)(functools.partial(stage_block, 0, n_own))

    def far_word(w, cnt):
        lo = jnp.maximum(w * WORD_BITS, 1)
        hi = jnp.minimum((w + 1) * WORD_BITS, near_lo)
        rest = jnp.where(w == 0, flags[w] & -2, flags[w])
        return lax.cond(rest != 0, lambda c: lax.fori_loop(lo, hi, gather, c), lambda c: c, cnt)

    n_blocks = lax.fori_loop(0, jnp.right_shift(near_lo + WORD_BITS - 1, WORD_SHIFT), far_word, n_own + first)
    n_blocks = lax.fori_loop(near_lo, n_past, gather, n_blocks)
    per_tile = nk // SEL_LEN
    n_tiles = (n_blocks + per_tile - 1) // per_tile
    pad_block = jnp.where(lax.broadcasted_iota(jnp.int32, (SEL_LEN, k_w), 1) == PAD_COL, 1.0, 0.0).astype(BF16)

    def pad(j, carry):
        dst = pl.multiple_of(j * SEL_LEN, SEL_LEN)
        kstage[pl.ds(dst, SEL_LEN), :] = pad_block
        return carry

    lax.fori_loop(n_blocks, n_tiles * per_tile, pad, 0)

    own_pos = t0 + lax.broadcasted_iota(jnp.int32, (tq, 1), 0)
    s_first = _dot(kstage[0:nk, :], q_t[...])
    own = jnp.where(own_pos <= trow, s_first[0:tq], MASK_NEG)
    s_a[0:tq, :] = own
    s_a[tq:, :] = s_first[tq:]
    max_a = jnp.maximum(jnp.max(own, axis=0, keepdims=True), jnp.max(s_first[tq:], axis=0, keepdims=True))

    m_w = jnp.maximum(win_max[0:1, :], win_max[1:2, :])
    e_merged = jnp.exp2(s_win[0:nkw, :] - m_w)
    e_mid = jnp.exp2(s_win[nkw:2 * nkw, :] - m_w)
    weights = jnp.concatenate([jnp.where(newer, e_merged, 0.0).astype(BF16), e_mid.astype(BF16),
                               jnp.where(newer, 0.0, e_merged).astype(BF16)], axis=0)
    v_win = jnp.concatenate([window_aug_t(vw_ref[0, pl.ds(win_off[i], nkw), :]) for i in range(n_wt)], axis=1)
    acc_w = _dot(v_win, weights)
    o_win = acc_w[0:dh] / acc_w[dh:dh + 1]

    def tile_accumulate(i, s_ref, col_max, state):
        return _accumulate(s_ref, col_max, staged_aug_t(vstage[pl.ds(pl.multiple_of(i * nk, nk), nk), :]), state)

    def scores_and_accumulate(i_next, s_next_ref, i_cur, s_cur_ref, max_cur, state):
        m_old, acc_old = state
        m_new = jnp.maximum(m_old, max_cur)
        alpha = jnp.exp2(m_old - m_new)
        s_next = _dot(kstage[pl.ds(pl.multiple_of(i_next * nk, nk), nk), :], q_t[...])
        s_next_ref[...] = s_next
        rows = nk // OVERLAP_CHUNKS
        partial, weights = [], []
        for c in range(OVERLAP_CHUNKS):
            rs = slice(c * rows, (c + 1) * rows)
            partial.append(jnp.max(s_next[rs], axis=0, keepdims=True))
            m_c = m_new + partial[c] * 0.0
            weights.append(jnp.exp2(s_cur_ref[rs, :] - m_c).astype(BF16))
        v_t = staged_aug_t(vstage[pl.ds(pl.multiple_of(i_cur * nk, nk), nk), :])
        acc = alpha * acc_old + _dot(v_t, jnp.concatenate(weights, axis=0))
        return (m_new, acc), functools.reduce(jnp.maximum, partial)

    def slc_pair(j, carry):
        state, max_a = carry
        i = 2 * j
        state, max_b = scores_and_accumulate(i + 1, s_b, i, s_a, max_a, state)
        state, max_a = scores_and_accumulate(i + 2, s_a, i + 1, s_b, max_b, state)
        return state, max_a

    n_pairs = (n_tiles - 1) // 2
    state, max_a = lax.fori_loop(0, n_pairs, slc_pair, (init, max_a))
    i_a = 2 * n_pairs

    def two_left(state):
        state, max_b = scores_and_accumulate(i_a + 1, s_b, i_a, s_a, max_a, state)
        return tile_accumulate(i_a + 1, s_b, max_b, state)

    def one_left(state):
        return tile_accumulate(i_a, s_a, max_a, state)

    _, acc = lax.cond(n_tiles - i_a == 2, two_left, one_left, state)
    o_slc = acc[0:dh] / acc[dh:dh + 1]

    gates = jnp.transpose(_sigmoid(gn_ref[...]))
    heads = []
    for r in range(r_heads):
        cs = slice(r * tq, (r + 1) * tq)
        heads.append(gates[3 * r:3 * r + 1, :] * o_cmp[:, cs] + gates[3 * r + 1:3 * r + 2, :] * o_slc[:, cs]
                     + gates[3 * r + 2:3 * r + 3, :] * o_win[:, cs])
    o = jnp.transpose(jnp.concatenate(heads, axis=0))
    o_ref[...] = (_sigmoid(gm_ref[...].astype(F32)) * o).astype(o_ref.dtype)


def _alibi_tables(seq, nc, tq):
    import ml_dtypes
    bf = ml_dtypes.bfloat16
    h = np.arange(1, N_HEADS + 1, dtype=np.float32)
    slopes = (np.exp2(-ALIBI_MAX_BIAS * h / N_HEADS) * LOG2_E).astype(np.float32)
    s1 = slopes.astype(bf).astype(np.float32)
    s2 = (slopes - s1).astype(bf).astype(np.float32)
    s3 = (slopes - s1 - s2).astype(bf).astype(np.float32)
    dh = HEAD_DIM
    qal = np.zeros((N_HEADS, dh), np.float32)
    for rep in range(3):
        qal[:, 3 * rep + 0] = s1
        qal[:, 3 * rep + 1] = s2
        qal[:, 3 * rep + 2] = s3
    qal[:, PAD_COL - dh] = MASK_NEG
    qal = qal.reshape(N_KV_GROUPS, HEADS_PER_GROUP, dh).transpose(0, 2, 1)
    qal_p = np.repeat(qal, tq, axis=2)

    nb = seq // SEL_LEN
    c = np.arange(nc)
    cpos = np.zeros((nc, 2 * dh), np.float32)
    cpos[:, dh:dh + 3] = ((c // 16) * 16 * CMP_STRIDE)[:, None]
    cpos[:, dh + 3:dh + 6] = ((c % 16) * CMP_STRIDE)[:, None]
    cpos[:, dh + 6:dh + 9] = CMP_LEN - 1

    r_sel = SEL_LEN // CMP_STRIDE
    r_cmp = CMP_LEN // CMP_STRIDE
    wsel = np.zeros((nb, nc), np.float32)
    for j in range(nb):
        for mm in range(r_sel):
            for nn in range(r_cmp):
                ci = r_sel * j + mm - nn
                if 0 <= ci < nc - 1:
                    wsel[j, ci] += 1.0
    as_bf = lambda a: jnp.asarray(a.astype(bf))
    return as_bf(qal_p), as_bf(cpos), as_bf(wsel)


def _nsa(qm, kvm, kcm, vcm, gnm, gma, batch, seq, tq, nk, nkw):
    t = qm.shape[0]
    g = N_KV_GROUPS
    dh = HEAD_DIM
    nqt = seq // tq
    nc = kcm.shape[1]
    nb = seq // SEL_LEN
    gw = HEADS_PER_GROUP * dh
    m_cols = HEADS_PER_GROUP * tq
    qal, cpos, wsel = _alibi_tables(seq, nc, tq)
    assert nkw == tq and WINDOW == 2 * tq, "the window branch merges the new and old key tiles of a query tile"
    n_wt = 2
    sel_chunk = max(WORD_BITS, nb // 8)
    per_tile = nk // SEL_LEN
    stage_rows = -(-nb // per_tile) * per_tile * SEL_LEN
    tps = NSA_TILES_PER_STEP if nqt % NSA_TILES_PER_STEP == 0 else 1
    n_steps = nqt // tps
    rq = tps * tq
    row = lambda b, gg, i: b * n_steps + i
    slab = lambda base: pl.BlockSpec((1, seq, dh), lambda b, gg, i: (base + gg, b, 0))
    cmp_slab = pl.BlockSpec((1, nc, dh), lambda b, gg, i: (b * g + gg, 0, 0))
    const = lambda a: pl.BlockSpec(a.shape, lambda b, gg, i: (0,) * a.ndim)
    return pl.pallas_call(
        functools.partial(_nsa_kernel, tq=tq, tiles_per_step=tps, nk=nk, nkw=nkw, seq=seq, sel_chunk=sel_chunk),
        grid=(batch, g, n_steps),
        in_specs=[
            pl.BlockSpec((rq, gw), lambda b, gg, i: (row(b, gg, i), gg)),
            cmp_slab, cmp_slab,
            slab(0), slab(g), slab(2 * g), slab(3 * g),
            pl.BlockSpec((rq, LANES), lambda b, gg, i: (row(b, gg, i), gg)),
            pl.BlockSpec((rq, gw), lambda b, gg, i: (row(b, gg, i), gg)),
            pl.BlockSpec((1, dh, m_cols), lambda b, gg, i: (gg, 0, 0)),
            const(cpos), const(wsel),
        ],
        out_specs=pl.BlockSpec((rq, gw), lambda b, gg, i: (row(b, gg, i), gg)),
        out_shape=jax.ShapeDtypeStruct((t, g * gw), BF16),
        scratch_shapes=[
            pltpu.VMEM((2 * dh + nb, m_cols), BF16),
            pltpu.VMEM((seq, 2 * dh + nb), BF16),
            pltpu.VMEM((seq, 2 * dh), BF16),
            pltpu.VMEM((nc, 2 * dh), BF16),
            pltpu.VMEM((stage_rows, 2 * dh + nb), BF16),
            pltpu.VMEM((stage_rows, 2 * dh), BF16),
            pltpu.VMEM((n_wt * nkw, m_cols), F32),
            pltpu.VMEM((nk, m_cols), F32),
            pltpu.VMEM((nk, m_cols), F32),
            pltpu.VMEM((SUBLANES, m_cols), F32),
            pltpu.VMEM((dh, m_cols), F32),
            pltpu.SMEM((nb // WORD_BITS,), jnp.int32),
        ],
        compiler_params=_params(("arbitrary", "arbitrary", "arbitrary")),
        name="nsa",
    )(qm, kcm, vcm, kvm, kvm, kvm, kvm, gnm, gma, qal, cpos, wsel)


def _ffn_kernel(x_ref, yr_ref, ya_ref, wo_ref, gmix_ref, gpre_ref, wg_ref, wu_ref, wd_ref, gpost_ref,
                p_ref, wpg_ref, bpg_ref, wpp_ref, o_ref, x1_ref, h_ref, acc_ref):
    j = pl.program_id(1)

    @pl.when(j == 0)
    def _():
        y = (yr_ref[...].astype(F32) + ya_ref[...].astype(F32)).astype(BF16)
        x1 = x_ref[...] + _rmsnorm(_dot(y, wo_ref[...]), gmix_ref[...])
        x1_ref[...] = x1
        h_ref[...] = _rmsnorm(x1, gpre_ref[...]).astype(BF16)
        acc_ref[...] = jnp.zeros(acc_ref.shape, F32)

    h = h_ref[...]
    gate = _dot(h, wg_ref[...])
    up = _dot(h, wu_ref[...])
    act = (gate * _sigmoid(gate) * up).astype(BF16)
    acc_ref[...] += _dot(act, wd_ref[...])

    @pl.when(j == pl.num_programs(1) - 1)
    def _():
        x2 = x1_ref[...] + _rmsnorm(acc_ref[...], gpost_ref[...])
        gate_p = _sigmoid(_dot(x2.astype(BF16), wpg_ref[...]) + bpg_ref[...])
        o_ref[...] = x2 + gate_p * _dot(p_ref[...].astype(BF16), wpp_ref[...])


def _ffn(x, yr, ya, wo, gmix, gpre, wgu, wd, gpost, p, wpg, bpg, wpp, tm, tf):
    t, d = x.shape
    dff = wd.shape[0]
    nf = dff // tf
    dp = p.shape[1]
    rows = pl.BlockSpec((tm, d), lambda i, j: (i, 0))
    vec = pl.BlockSpec((1, d), lambda i, j: (0, 0))
    square = pl.BlockSpec((d, d), lambda i, j: (0, 0))
    return pl.pallas_call(
        _ffn_kernel,
        grid=(t // tm, nf),
        in_specs=[
            rows, rows, rows, square, vec, vec,
            pl.BlockSpec((d, tf), lambda i, j: (0, j)),
            pl.BlockSpec((d, tf), lambda i, j: (0, nf + j)),
            pl.BlockSpec((tf, d), lambda i, j: (j, 0)),
            vec,
            pl.BlockSpec((tm, dp), lambda i, j: (i, 0)),
            square,
            vec,
            pl.BlockSpec((dp, d), lambda i, j: (0, 0)),
        ],
        out_specs=rows,
        out_shape=jax.ShapeDtypeStruct((t, d), F32),
        scratch_shapes=[pltpu.VMEM((tm, d), F32), pltpu.VMEM((tm, d), BF16), pltpu.VMEM((tm, d), F32)],
        compiler_params=_params(("parallel", "arbitrary")),
        name="out_ffn_ple",
    )(x, yr, ya, wo, gmix, gpre, wgu, wgu, wd, gpost, p, wpg, bpg, wpp)


def _block_diag_chunks(w):
    n, bs, _ = w.shape
    per = LANES // bs
    w = w.reshape(n // per, per, bs, bs)
    eye = jnp.eye(per, dtype=w.dtype)
    return jnp.einsum('cpij,pq->cpiqj', w, eye).reshape(n // per, LANES, LANES)


def _layer(x, p, norm_mix_pre, norm_mix_post, w_in, conv_w, conv_b, lru_wa, lru_ba, lru_wx, lru_bx,
           lru_lambda, cmp_pos_k, cmp_pos_v, cmp_k_w1, cmp_k_w2, cmp_v_w1, cmp_v_w2, w_out,
           norm_ffn_pre, norm_ffn_post, ffn_w_gate_up, ffn_w_down, ple_w_proj, ple_w_gate, ple_b_gate,
           batch, seq):
    t, d = x.shape
    d_attn = N_HEADS * HEAD_DIM
    d_kv = N_KV_GROUPS * HEAD_DIM
    row2 = lambda v: v.reshape(1, -1)

    o_q = 2 * d
    o_kv = o_q + d_attn
    o_gn = o_kv + 6 * d_kv
    o_gm = o_gn + 3 * N_HEADS
    w_f = jnp.concatenate([w_in[:, 0:o_q], w_in[:, o_gm:o_gm + 2 * d]], axis=1).astype(BF16)
    w_q = w_in[:, o_q:o_kv].astype(BF16)
    w_kv = w_in[:, o_kv:o_gn].astype(BF16)
    per_g = 3 * HEADS_PER_GROUP
    w_gn = w_in[:, o_gn:o_gm].reshape(d, N_KV_GROUPS, per_g)
    w_gn = jnp.pad(w_gn, ((0, 0), (0, 0), (0, LANES - per_g))).reshape(d, N_KV_GROUPS * LANES).astype(BF16)

    g_pre = row2(norm_mix_pre)
    tm = min(512, t)
    qm, gnm, kv_chunks, kvm = _norm_qkv(x, g_pre, jnp.concatenate([w_q, w_gn, w_kv], axis=1), d_attn,
                                        N_KV_GROUPS * LANES, 2 * N_KV_GROUPS, tm)

    yr, gma = _rnn(x, g_pre, w_f, conv_w, row2(conv_b), _block_diag_chunks(lru_wa).astype(BF16),
                   _block_diag_chunks(lru_wx).astype(BF16), row2(lru_ba), row2(lru_bx), row2(lru_lambda),
                   batch, seq, min(512, seq))

    nchunk = seq // CMP_STRIDE
    kcm, vcm = _compress(kv_chunks, cmp_k_w1.astype(BF16), cmp_k_w2.astype(BF16), cmp_pos_k.reshape(1, -1),
                         cmp_v_w1.astype(BF16), cmp_v_w2.astype(BF16), cmp_pos_v.reshape(1, -1), batch, nchunk)

    tq = min(256, seq)
    ya = _nsa(qm, kvm, kcm, vcm, gnm, gma, batch, seq, tq, min(NSA_KEY_TILE, seq), min(256, seq))

    dff = ffn_w_down.shape[0]
    tf = dff
    return _ffn(x, yr, ya, w_out.astype(BF16), row2(norm_mix_post), row2(norm_ffn_pre),
                ffn_w_gate_up.astype(BF16), ffn_w_down.astype(BF16), row2(norm_ffn_post),
                p, ple_w_gate.astype(BF16), row2(ple_b_gate), ple_w_proj.astype(BF16), tm, tf)


def kernel(x, p, norm_mix_pre, norm_mix_post, w_in, conv_w, conv_b, lru_wa, lru_ba, lru_wx, lru_bx, lru_lambda, cmp_pos_k, cmp_pos_v, cmp_k_w1, cmp_k_w2, cmp_v_w1, cmp_v_w2, w_out, norm_ffn_pre, norm_ffn_post, ffn_w_gate_up, ffn_w_down, ple_w_proj, ple_w_gate, ple_b_gate):
    batch, seq, d = x.shape
    depth = w_in.shape[0]
    xf = x.reshape(batch * seq, d)
    for i in range(depth):
        xf = _layer(xf, p[i].reshape(batch * seq, -1), norm_mix_pre[i], norm_mix_post[i], w_in[i], conv_w[i],
                    conv_b[i], lru_wa[i], lru_ba[i], lru_wx[i], lru_bx[i], lru_lambda[i], cmp_pos_k[i],
                    cmp_pos_v[i], cmp_k_w1[i], cmp_k_w2[i], cmp_v_w1[i], cmp_v_w2[i], w_out[i],
                    norm_ffn_pre[i], norm_ffn_post[i], ffn_w_gate_up[i], ffn_w_down[i], ple_w_proj[i],
                    ple_w_gate[i], ple_b_gate[i], batch, seq)
    return xf.reshape(batch, seq, d)
```

```python
import functools

import numpy as np
import jax
import jax.numpy as jnp
from jax import lax
from jax.experimental import pallas as pl
from jax.experimental.pallas import tpu as pltpu

CONV_WIDTH = 4
LRU_C = 8.0
N_HEADS = 16
HEAD_DIM = 64
N_KV_GROUPS = 4
HEADS_PER_GROUP = N_HEADS // N_KV_GROUPS
CMP_LEN = 32
CMP_STRIDE = 16
SEL_LEN = 64
SEL_TOPK = 16
WINDOW = 512
ALIBI_MAX_BIAS = 8.0
NORM_EPS = 1e-6

LANES = 128
SUBLANES = 8
VMEM_LIMIT_BYTES = 56 * 1024 * 1024

MASK_NEG = -1e30
LOG2_E = 1.4426950408889634
N_ALIBI_COLS = 9
PAD_COL = HEAD_DIM + N_ALIBI_COLS
WORD_BITS = 16
WORD_SHIFT = 4
SEL_SHIFT = 6
N_FORCED = 3
NEAR_BLOCKS = 20
FILL_ROWS = 512
NSA_KEY_TILE = 11 * SEL_LEN
SCAN_UNROLL = 64
NSA_TILES_PER_STEP = 4
OVERLAP_CHUNKS = 4
BF16 = jnp.bfloat16
F32 = jnp.float32


def _dot(a, b):
    return jnp.dot(a, b, preferred_element_type=F32)


def _dot_nt(a, b):
    return lax.dot_general(a, b, (((1,), (1,)), ((), ())), preferred_element_type=F32)


def _sigmoid(x):
    return 0.5 * jnp.tanh(0.5 * x) + 0.5


def _gelu_tanh(x):
    c = np.float32(np.sqrt(2.0 / np.pi))
    half = 0.5 * x
    return half + half * jnp.tanh(x * (c + (c * 0.044715) * (x * x)))


def _rmsnorm(x, g):
    ms = jnp.mean(x * x, axis=-1, keepdims=True)
    return x * lax.rsqrt(ms + NORM_EPS) * g


def _params(sem):
    return pltpu.CompilerParams(dimension_semantics=sem, vmem_limit_bytes=VMEM_LIMIT_BYTES)


def _store_qkv(z, q_ref, gn_ref, cv_ref, kv_ref, slab_scr):
    nq = q_ref.shape[1]
    ngn = gn_ref.shape[1]
    q_ref[...] = z[:, 0:nq].astype(q_ref.dtype)
    gn_ref[...] = z[:, nq:nq + ngn]
    n_cv = cv_ref.shape[0]
    rows = cv_ref.shape[1]
    for c in range(n_cv):
        lo = nq + ngn + c * HEAD_DIM
        slab_scr[...] = z[:, lo:lo + HEAD_DIM]
        for l in range(CMP_STRIDE):
            cv_ref[c, :, l * HEAD_DIM:(l + 1) * HEAD_DIM] = (
                slab_scr[pl.ds(l, rows, stride=CMP_STRIDE), :].astype(cv_ref.dtype))
    for c in range(kv_ref.shape[0]):
        lo = nq + ngn + (n_cv + c) * HEAD_DIM
        kv_ref[c] = z[:, lo:lo + HEAD_DIM].astype(kv_ref.dtype)


def _rnn_kernel(x_ref, g_ref, wf_ref, wq_ref, cw_ref, cb_ref, wa_ref, wx_ref, ba_ref, bx_ref, lam_ref,
                o_ref, gma_ref, q_ref, gn_ref, cv_ref, kv_ref, xbuf, a_scr, b_scr, h_scr, slab_scr, *, ts):
    s_idx = pl.program_id(1)
    d = o_ref.shape[1]
    halo = SUBLANES

    @pl.when(s_idx == 0)
    def _():
        xbuf[0:halo, :] = jnp.zeros((halo, d), F32)
        h_scr[...] = jnp.zeros(h_scr.shape, F32)

    hn = _rmsnorm(x_ref[...], g_ref[...]).astype(BF16)
    xr = _dot(hn, wf_ref[:, 0:d])
    _store_qkv(_dot(hn, wq_ref[...]), q_ref, gn_ref, cv_ref, kv_ref, slab_scr)
    gr = _dot(hn, wf_ref[:, d:2 * d])
    gm = _dot(hn, wf_ref[:, 2 * d:3 * d])
    gma_ref[...] = _dot(hn, wf_ref[:, 3 * d:4 * d]).astype(gma_ref.dtype)

    xbuf[halo:halo + ts, :] = xr
    xc = cb_ref[...] + xbuf[halo:halo + ts, :] * cw_ref[CONV_WIDTH - 1:CONV_WIDTH, :]
    for k in range(1, CONV_WIDTH):
        xc = xc + xbuf[halo - k:halo - k + ts, :] * cw_ref[CONV_WIDTH - 1 - k:CONV_WIDTH - k, :]
    xbuf[0:halo, :] = xbuf[ts:ts + halo, :]

    xcb = xc.astype(BF16)
    n_chunks = d // LANES
    neg_sp = -LRU_C * (jnp.maximum(-lam_ref[...], 0.0) + jnp.log(1.0 + jnp.exp(-jnp.abs(lam_ref[...]))))
    for c in range(n_chunks):
        sl = slice(c * LANES, (c + 1) * LANES)
        xk = xcb[:, sl]
        r = _sigmoid(_dot(xk, wa_ref[c]) + ba_ref[:, sl])
        i = _sigmoid(_dot(xk, wx_ref[c]) + bx_ref[:, sl])
        log_a = r * neg_sp[:, sl]
        a = jnp.exp(log_a)
        a_scr[:, sl] = a
        u = 1.0 - a * a
        root = jnp.where(u > 0.0, u * lax.rsqrt(u), 0.0)
        b_scr[:, sl] = root * (i * xc[:, sl])

    row_id = lax.broadcasted_iota(jnp.int32, (SUBLANES, d), 0)

    def group(gi, h):
        base = pl.multiple_of(gi * SUBLANES, SUBLANES)
        a = a_scr[pl.ds(base, SUBLANES), :]
        b = b_scr[pl.ds(base, SUBLANES), :]
        s = 1
        while s < SUBLANES:
            reach = row_id >= s
            b = jnp.where(reach, a * pltpu.roll(b, s, 0) + b, b)
            a = jnp.where(reach, a * pltpu.roll(a, s, 0), a)
            s *= 2
        hs = a * h + b
        a_scr[pl.ds(base, SUBLANES), :] = hs
        return jnp.broadcast_to(hs[SUBLANES - 1:SUBLANES, :], (SUBLANES, d))

    h_scr[...] = lax.fori_loop(0, ts // SUBLANES, group, h_scr[...], unroll=SCAN_UNROLL)
    o_ref[...] = (_sigmoid(gm) * a_scr[...] * _gelu_tanh(gr)).astype(o_ref.dtype)


def _rnn(x, g, w_f, w_q, nq, ngn, n_cv, conv_w, conv_b, wa, wx, ba, bx, lam, batch, seq, ts):
    t = x.shape[0]
    d = conv_w.shape[1]
    ns = seq // ts
    n_kv = (w_q.shape[1] - nq - ngn) // HEAD_DIM - n_cv
    tile = lambda b, s: b * ns + s
    rows = lambda width: pl.BlockSpec((ts, width), lambda b, s: (tile(b, s), 0))
    vec = lambda r: pl.BlockSpec((r, d), lambda b, s: (0, 0))
    resident = lambda w: pl.BlockSpec(w.shape, lambda b, s: (0, 0), pipeline_mode=pl.Buffered(1))
    return pl.pallas_call(
        functools.partial(_rnn_kernel, ts=ts),
        grid=(batch, ns),
        in_specs=[
            rows(d), vec(1), resident(w_f), resident(w_q),
            vec(CONV_WIDTH), vec(1),
            pl.BlockSpec(wa.shape, lambda b, s: (0, 0, 0)),
            pl.BlockSpec(wx.shape, lambda b, s: (0, 0, 0)),
            vec(1), vec(1), vec(1),
        ],
        out_specs=[
            rows(d), rows(d), rows(nq), rows(ngn),
            pl.BlockSpec((n_cv, ts // CMP_STRIDE, CMP_STRIDE * HEAD_DIM), lambda b, s: (0, tile(b, s), 0)),
            pl.BlockSpec((n_kv, ts, HEAD_DIM), lambda b, s: (0, tile(b, s), 0)),
        ],
        out_shape=[
            jax.ShapeDtypeStruct((t, d), BF16),
            jax.ShapeDtypeStruct((t, d), BF16),
            jax.ShapeDtypeStruct((t, nq), BF16),
            jax.ShapeDtypeStruct((t, ngn), F32),
            jax.ShapeDtypeStruct((n_cv, t // CMP_STRIDE, CMP_STRIDE * HEAD_DIM), BF16),
            jax.ShapeDtypeStruct((n_kv, t, HEAD_DIM), BF16),
        ],
        scratch_shapes=[
            pltpu.VMEM((ts + SUBLANES, d), F32),
            pltpu.VMEM((ts, d), F32),
            pltpu.VMEM((ts, d), F32),
            pltpu.VMEM((SUBLANES, d), F32),
            pltpu.VMEM((ts, HEAD_DIM), F32),
        ],
        compiler_params=_params(("parallel", "arbitrary")),
        name="proj_rnn_mixer",
    )(x, g, w_f, w_q, conv_w, conv_b, wa, wx, ba, bx, lam)


def _compress_kernel(ak_ref, av_ref, w1k_ref, w2k_ref, pk_ref, w1v_ref, w2v_ref, pv_ref, ok_ref, ov_ref):
    def one(a_ref, w1_ref, w2_ref, p_ref, o_ref):
        a = a_ref[0]
        half = a.shape[1]
        nchunk = a.shape[0]
        lo = _dot(a, w1_ref[0:half, :])
        hi = _dot(a, w1_ref[half:2 * half, :])
        pb = _dot(jnp.broadcast_to(p_ref[...], (SUBLANES, 2 * half)).astype(BF16), w1_ref[...])[0:1, :]
        h = lo + pltpu.roll(hi, nchunk - 1, 0) + pb
        o_ref[0] = _dot(_gelu_tanh(h).astype(BF16), w2_ref[...]).astype(o_ref.dtype)

    one(ak_ref, w1k_ref, w2k_ref, pk_ref, ok_ref)
    one(av_ref, w1v_ref, w2v_ref, pv_ref, ov_ref)


def _compress(kv_chunks, w1k, w2k, pk, w1v, w2v, pv, batch, nchunk):
    g = N_KV_GROUPS
    width = kv_chunks.shape[2]
    full = lambda a: pl.BlockSpec(a.shape, lambda b, gg: (0,) * a.ndim)
    out = jax.ShapeDtypeStruct((batch * g, nchunk, HEAD_DIM), BF16)
    return pl.pallas_call(
        _compress_kernel,
        grid=(batch, g),
        in_specs=[
            pl.BlockSpec((1, nchunk, width), lambda b, gg: (gg, b, 0)),
            pl.BlockSpec((1, nchunk, width), lambda b, gg: (g + gg, b, 0)),
            full(w1k), full(w2k), full(pk), full(w1v), full(w2v), full(pv),
        ],
        out_specs=[
            pl.BlockSpec((1, nchunk, HEAD_DIM), lambda b, gg: (b * g + gg, 0, 0)),
            pl.BlockSpec((1, nchunk, HEAD_DIM), lambda b, gg: (b * g + gg, 0, 0)),
        ],
        out_shape=[out, out],
        compiler_params=_params(("parallel", "parallel")),
        name="compress",
    )(kv_chunks, kv_chunks, w1k, w2k, pk, w1v, w2v, pv)


ONES_ROWS = 16


def _accumulate(s_ref, col_max, v_aug_t, state):
    m_old, acc_old = state
    m_new = jnp.maximum(m_old, col_max)
    alpha = jnp.exp2(m_old - m_new)
    e = jnp.exp2(s_ref[...] - m_new).astype(BF16)
    return m_new, alpha * acc_old + _dot(v_aug_t, e)


def _nsa_kernel(q_ref, kc_ref, vc_ref, ks_ref, vs_ref, kw_ref, vw_ref, gn_ref, gm_ref,
                qal_ref, cpos_ref, wsel_ref, o_ref, *scratch, tq, tiles_per_step, **static):
    def tile(j, carry):
        rows = pl.ds(pl.multiple_of(j * tq, tq), tq)
        _nsa_tile(pl.program_id(2) * tiles_per_step + j, q_ref.at[rows, :], kc_ref, vc_ref, ks_ref, vs_ref,
                  kw_ref, vw_ref, gn_ref.at[rows, :], gm_ref.at[rows, :], qal_ref, cpos_ref, wsel_ref,
                  o_ref.at[rows, :], *scratch, tq=tq, **static)
        return carry

    lax.fori_loop(0, tiles_per_step, tile, 0)


def _nsa_tile(qt, q_ref, kc_ref, vc_ref, ks_ref, vs_ref, kw_ref, vw_ref, gn_ref, gm_ref,
              qal_ref, cpos_ref, wsel_ref,
              o_ref, q_t, ksaug, kwaug, kcaug, kstage, vstage, s_win, s_a, s_b, win_max, o_cmp_scr, flags,
              *, tq, nk, nkw, seq, sel_chunk):
    b = pl.program_id(0)
    g = pl.program_id(1)
    r_heads = HEADS_PER_GROUP
    dh = HEAD_DIM
    m_cols = r_heads * tq
    nb = seq // SEL_LEN
    nc = kc_ref.shape[1]
    a_w = 2 * dh
    k_w = a_w + nb

    @pl.when((b == 0) & (g == 0) & (qt == 0))
    def _():
        rows = min(FILL_ROWS, seq)

        def fill(c, carry):
            off = pl.multiple_of(c * rows, rows)
            pos = off + lax.broadcasted_iota(jnp.int32, (rows, k_w), 0)
            lane = lax.broadcasted_iota(jnp.int32, (rows, k_w), 1)
            blk_of = jnp.right_shift(pos, SEL_SHIFT)
            cols = jnp.where((lane >= dh) & (lane < dh + 3), blk_of * SEL_LEN,
                             jnp.where((lane >= dh + 3) & (lane < dh + 6), pos & (SEL_LEN - 1),
                                       jnp.where(lane == a_w + blk_of, 1, 0)))
            cols = cols.astype(F32).astype(BF16)
            ksaug[pl.ds(off, rows), :] = cols
            kwaug[pl.ds(off, rows), :] = cols[:, 0:a_w]
            return carry

        lax.fori_loop(0, seq // rows, fill, 0)
        kcaug[...] = cpos_ref[...]
        vstage[...] = jnp.ones(vstage.shape, BF16)

    @pl.when(qt == 0)
    def _():
        ksaug[:, 0:dh] = ks_ref[0]
        kwaug[:, 0:dh] = kw_ref[0]
        kcaug[:, 0:dh] = kc_ref[0]
        q_t[dh:a_w, :] = qal_ref[0]

    v_rows = dh + ONES_ROWS
    ri = lax.broadcasted_iota(jnp.int32, (v_rows, 2 * dh), 0)
    ci = lax.broadcasted_iota(jnp.int32, (v_rows, 2 * dh), 1)
    pick = (((ri < dh) & (ri == ci)) | ((ri >= dh) & (ci == dh))).astype(F32).astype(BF16)
    transposed = lambda v: _dot_nt(pick[0:dh, 0:dh], v).astype(BF16)
    staged_aug_t = lambda v: _dot_nt(pick, v).astype(BF16)
    ones_rows = jnp.ones((ONES_ROWS, nkw), BF16)
    window_aug_t = lambda v: jnp.concatenate([transposed(v), ones_rows], axis=0)
    init = (jnp.full((1, m_cols), MASK_NEG, F32), jnp.zeros((v_rows, m_cols), F32))

    t0 = qt * tq
    col = lax.broadcasted_iota(jnp.int32, (1, m_cols), 1)
    trow = t0 + (col & (tq - 1))

    scale = np.float32(HEAD_DIM ** -0.5 * LOG2_E)
    qf = jnp.transpose(q_ref[...].astype(F32) * scale)
    for r in range(r_heads):
        q_t[0:dh, r * tq:(r + 1) * tq] = qf[r * dh:(r + 1) * dh, :].astype(BF16)

    n_wt = 3
    win_off = [pl.multiple_of(jnp.maximum(t0 - i * nkw, 0), nkw) for i in range(n_wt)]
    newer = (t0 + lax.broadcasted_iota(jnp.int32, (nkw, 1), 0)) <= trow

    def window_tile(i):
        s = _dot(kwaug[pl.ds(win_off[i], nkw), :], q_t[0:a_w, :])
        return s if i == 0 else s + jnp.where(t0 - i * nkw >= 0, 0.0, MASK_NEG)

    def window_scores():
        merged = jnp.where(newer, window_tile(0), window_tile(2))
        s_win[0:nkw, :] = merged
        win_max[0:1, :] = jnp.max(merged, axis=0, keepdims=True)
        mid = window_tile(1)
        s_win[nkw:2 * nkw, :] = mid
        win_max[1:2, :] = jnp.max(mid, axis=0, keepdims=True)

    def compressed_and_select(rows_c, rows_b):
        free = max(rows_c - cmp_chunk - tq // CMP_STRIDE, 0)
        s = _dot(kcaug[0:rows_c, :], q_t[0:a_w, :])
        cend = (free + lax.broadcasted_iota(jnp.int32, (rows_c - free, 1), 0)) * CMP_STRIDE + (CMP_LEN - 1)
        tail = jnp.where(cend <= trow, s[free:], MASK_NEG)
        s = jnp.concatenate([s[0:free], tail], axis=0) if free else tail
        m = jnp.max(s, axis=0, keepdims=True)
        e = jnp.exp2(s - m)
        has_key = (trow >= CMP_LEN - 1).astype(F32)
        p = e * (has_key / jnp.sum(e, axis=0, keepdims=True))
        o_cmp_scr[...] = _dot(transposed(vc_ref[0, 0:rows_c, :]), p.astype(BF16))
        imp = p[:, 0:tq]
        for r in range(1, r_heads):
            imp = imp + p[:, r * tq:(r + 1) * tq]

        hi = imp.astype(BF16)
        lo = (imp - hi.astype(F32)).astype(BF16)
        wsel = wsel_ref[0:rows_b, 0:rows_c]
        imp_t = _dot(wsel, hi) + _dot(wsel, lo)

        window_scores()

        blk = lax.broadcasted_iota(jnp.int32, (rows_b, tq), 0).astype(F32)
        tq_l = t0 + lax.broadcasted_iota(jnp.int32, (rows_b, tq), 1)
        cur = jnp.right_shift(tq_l, SEL_SHIFT).astype(F32)
        valid = blk <= cur
        sel = jnp.where(blk == 0.0, 1.0, jnp.where(blk == cur, 1.0, jnp.where(blk == cur - 1.0, 1.0, 0.0)))
        score = jnp.where(valid, jnp.where(sel > 0.0, -1.0, imp_t), -1.0)
        for _ in range(min(SEL_TOPK, nb) - N_FORCED):
            mx = jnp.max(score, axis=0, keepdims=True)
            idx = jnp.min(jnp.where(score == mx, blk, float(nb)), axis=0, keepdims=True)
            hit = blk == idx
            sel = jnp.where(hit, 1.0, sel)
            score = jnp.where(hit, -2.0, score)
        selv = jnp.where(valid, sel, 0.0) > 0.0
        selneg_t = jnp.where(selv, 0.0, MASK_NEG).astype(BF16)
        unseen = jnp.full((nb - rows_b, tq), MASK_NEG, BF16)
        for r in range(r_heads):
            q_t[a_w:a_w + rows_b, r * tq:(r + 1) * tq] = selneg_t
            if rows_b < nb:
                q_t[a_w + rows_b:, r * tq:(r + 1) * tq] = unseen

        used = jnp.max(jnp.where(selv, 1.0, 0.0), axis=1, keepdims=True)
        bit_id = lax.broadcasted_iota(jnp.int32, (rows_b, 1), 0) & (WORD_BITS - 1)
        weighted = used * jnp.left_shift(1, bit_id).astype(F32)
        for i in range(nb // WORD_BITS):
            if (i + 1) * WORD_BITS <= rows_b:
                flags[i] = jnp.sum(weighted[i * WORD_BITS:(i + 1) * WORD_BITS, :]).astype(jnp.int32)
            else:
                flags[i] = 0

    n_var = nb // sel_chunk
    cmp_chunk = sel_chunk * (SEL_LEN // CMP_STRIDE)
    seen_c = (t0 + tq - CMP_LEN) // CMP_STRIDE + 1
    seen_b = (t0 + tq) // SEL_LEN
    variant = jnp.maximum((seen_c + cmp_chunk - 1) // cmp_chunk, (seen_b + sel_chunk - 1) // sel_chunk)
    for v in range(1, n_var + 1):
        pl.when(variant == v)(functools.partial(compressed_and_select, min(v * cmp_chunk, nc), v * sel_chunk))
    o_cmp = o_cmp_scr[...]

    n_own = tq // SEL_LEN
    off_q = pl.multiple_of(t0, tq)
    kstage[0:tq, :] = ksaug[pl.ds(off_q, tq), :]
    vstage[0:tq, 0:dh] = vs_ref[0, pl.ds(off_q, tq), :]

    def stage_block(j, cnt):
        src = pl.multiple_of(j * SEL_LEN, SEL_LEN)
        dst = pl.multiple_of(cnt * SEL_LEN, SEL_LEN)
        kstage[pl.ds(dst, SEL_LEN), :] = ksaug[pl.ds(src, SEL_LEN), :]
        vstage[pl.ds(dst, SEL_LEN), 0:dh] = vs_ref[0, pl.ds(src, SEL_LEN), :]

    def gather(j, cnt):
        bit = jnp.right_shift(flags[jnp.right_shift(j, WORD_SHIFT)], j & (WORD_BITS - 1)) & 1
        pl.when(bit == 1)(functools.partial(stage_block, j, cnt))
        return cnt + bit

    n_past = jnp.right_shift(t0, SEL_SHIFT)
    near_lo = jnp.maximum(n_past - NEAR_BLOCKS, 0)
    first = jnp.where(near_lo > 0, flags[0] & 1, 0)
    pl.when(first == 1)(functools.partial(stage_block, 0, n_own))

    def far_word(w, cnt):
        lo = jnp.maximum(w * WORD_BITS, 1)
        hi = jnp.minimum((w + 1) * WORD_BITS, near_lo)
        rest = jnp.where(w == 0, flags[w] & -2, flags[w])
        return lax.cond(rest != 0, lambda c: lax.fori_loop(lo, hi, gather, c), lambda c: c, cnt)

    n_blocks = lax.fori_loop(0, jnp.right_shift(near_lo + WORD_BITS - 1, WORD_SHIFT), far_word, n_own + first)
    n_blocks = lax.fori_loop(near_lo, n_past, gather, n_blocks)
    per_tile = nk // SEL_LEN
    n_tiles = (n_blocks + per_tile - 1) // per_tile
    pad_block = jnp.where(lax.broadcasted_iota(jnp.int32, (SEL_LEN, k_w), 1) == PAD_COL, 1.0, 0.0).astype(BF16)

    def pad(j, carry):
        dst = pl.multiple_of(j * SEL_LEN, SEL_LEN)
        kstage[pl.ds(dst, SEL_LEN), :] = pad_block
        return carry

    lax.fori_loop(n_blocks, n_tiles * per_tile, pad, 0)

    own_pos = t0 + lax.broadcasted_iota(jnp.int32, (tq, 1), 0)
    s_first = _dot(kstage[0:nk, :], q_t[...])
    own = jnp.where(own_pos <= trow, s_first[0:tq], MASK_NEG)
    s_a[0:tq, :] = own
    s_a[tq:, :] = s_first[tq:]
    max_a = jnp.maximum(jnp.max(own, axis=0, keepdims=True), jnp.max(s_first[tq:], axis=0, keepdims=True))

    m_w = jnp.maximum(win_max[0:1, :], win_max[1:2, :])
    e_merged = jnp.exp2(s_win[0:nkw, :] - m_w)
    e_mid = jnp.exp2(s_win[nkw:2 * nkw, :] - m_w)
    weights = jnp.concatenate([jnp.where(newer, e_merged, 0.0).astype(BF16), e_mid.astype(BF16),
                               jnp.where(newer, 0.0, e_merged).astype(BF16)], axis=0)
    v_win = jnp.concatenate([window_aug_t(vw_ref[0, pl.ds(win_off[i], nkw), :]) for i in range(n_wt)], axis=1)
    acc_w = _dot(v_win, weights)
    o_win = acc_w[0:dh] / acc_w[dh:dh + 1]

    def tile_accumulate(i, s_ref, col_max, state):
        return _accumulate(s_ref, col_max, staged_aug_t(vstage[pl.ds(pl.multiple_of(i * nk, nk), nk), :]), state)

    def scores_and_accumulate(i_next, s_next_ref, i_cur, s_cur_ref, max_cur, state):
        m_old, acc_old = state
        m_new = jnp.maximum(m_old, max_cur)
        alpha = jnp.exp2(m_old - m_new)
        s_next = _dot(kstage[pl.ds(pl.multiple_of(i_next * nk, nk), nk), :], q_t[...])
        s_next_ref[...] = s_next
        rows = nk // OVERLAP_CHUNKS
        partial, weights = [], []
        for c in range(OVERLAP_CHUNKS):
            rs = slice(c * rows, (c + 1) * rows)
            partial.append(jnp.max(s_next[rs], axis=0, keepdims=True))
            m_c = m_new + partial[c] * 0.0
            weights.append(jnp.exp2(s_cur_ref[rs, :] - m_c).astype(BF16))
        v_t = staged_aug_t(vstage[pl.ds(pl.multiple_of(i_cur * nk, nk), nk), :])
        acc = alpha * acc_old + _dot(v_t, jnp.concatenate(weights, axis=0))
        return (m_new, acc), functools.reduce(jnp.maximum, partial)

    def slc_pair(j, carry):
        state, max_a = carry
        i = 2 * j
        state, max_b = scores_and_accumulate(i + 1, s_b, i, s_a, max_a, state)
        state, max_a = scores_and_accumulate(i + 2, s_a, i + 1, s_b, max_b, state)
        return state, max_a

    n_pairs = (n_tiles - 1) // 2
    state, max_a = lax.fori_loop(0, n_pairs, slc_pair, (init, max_a))
    i_a = 2 * n_pairs

    def two_left(state):
        state, max_b = scores_and_accumulate(i_a + 1, s_b, i_a, s_a, max_a, state)
        return tile_accumulate(i_a + 1, s_b, max_b, state)

    def one_left(state):
        return tile_accumulate(i_a, s_a, max_a, state)

    _, acc = lax.cond(n_tiles - i_a == 2, two_left, one_left, state)
    o_slc = acc[0:dh] / acc[dh:dh + 1]

    gates = jnp.transpose(_sigmoid(gn_ref[...]))
    heads = []
    for r in range(r_heads):
        cs = slice(r * tq, (r + 1) * tq)
        heads.append(gates[3 * r:3 * r + 1, :] * o_cmp[:, cs] + gates[3 * r + 1:3 * r + 2, :] * o_slc[:, cs]
                     + gates[3 * r + 2:3 * r + 3, :] * o_win[:, cs])
    o = jnp.transpose(jnp.concatenate(heads, axis=0))
    o_ref[...] = (_sigmoid(gm_ref[...].astype(F32)) * o).astype(o_ref.dtype)


def _alibi_tables(seq, nc, tq):
    import ml_dtypes
    bf = ml_dtypes.bfloat16
    h = np.arange(1, N_HEADS + 1, dtype=np.float32)
    slopes = (np.exp2(-ALIBI_MAX_BIAS * h / N_HEADS) * LOG2_E).astype(np.float32)
    s1 = slopes.astype(bf).astype(np.float32)
    s2 = (slopes - s1).astype(bf).astype(np.float32)
    s3 = (slopes - s1 - s2).astype(bf).astype(np.float32)
    dh = HEAD_DIM
    qal = np.zeros((N_HEADS, dh), np.float32)
    for rep in range(3):
        qal[:, 3 * rep + 0] = s1
        qal[:, 3 * rep + 1] = s2
        qal[:, 3 * rep + 2] = s3
    qal[:, PAD_COL - dh] = MASK_NEG
    qal = qal.reshape(N_KV_GROUPS, HEADS_PER_GROUP, dh).transpose(0, 2, 1)
    qal_p = np.repeat(qal, tq, axis=2)

    nb = seq // SEL_LEN
    c = np.arange(nc)
    cpos = np.zeros((nc, 2 * dh), np.float32)
    cpos[:, dh:dh + 3] = ((c // 16) * 16 * CMP_STRIDE)[:, None]
    cpos[:, dh + 3:dh + 6] = ((c % 16) * CMP_STRIDE)[:, None]
    cpos[:, dh + 6:dh + 9] = CMP_LEN - 1

    r_sel = SEL_LEN // CMP_STRIDE
    r_cmp = CMP_LEN // CMP_STRIDE
    wsel = np.zeros((nb, nc), np.float32)
    for j in range(nb):
        for mm in range(r_sel):
            for nn in range(r_cmp):
                ci = r_sel * j + mm - nn
                if 0 <= ci < nc - 1:
                    wsel[j, ci] += 1.0
    as_bf = lambda a: jnp.asarray(a.astype(bf))
    return as_bf(qal_p), as_bf(cpos), as_bf(wsel)


def _nsa(qm, kvm, kcm, vcm, gnm, gma, batch, seq, tq, nk, nkw):
    t = qm.shape[0]
    g = N_KV_GROUPS
    dh = HEAD_DIM
    nqt = seq // tq
    nc = kcm.shape[1]
    nb = seq // SEL_LEN
    gw = HEADS_PER_GROUP * dh
    m_cols = HEADS_PER_GROUP * tq
    qal, cpos, wsel = _alibi_tables(seq, nc, tq)
    assert nkw == tq and WINDOW == 2 * tq, "the window branch merges the new and old key tiles of a query tile"
    n_wt = 2
    sel_chunk = max(WORD_BITS, nb // 8)
    per_tile = nk // SEL_LEN
    stage_rows = -(-nb // per_tile) * per_tile * SEL_LEN
    tps = NSA_TILES_PER_STEP if nqt % NSA_TILES_PER_STEP == 0 else 1
    n_steps = nqt // tps
    rq = tps * tq
    row = lambda b, gg, i: b * n_steps + i
    slab = lambda base: pl.BlockSpec((1, seq, dh), lambda b, gg, i: (base + gg, b, 0))
    cmp_slab = pl.BlockSpec((1, nc, dh), lambda b, gg, i: (b * g + gg, 0, 0))
    const = lambda a: pl.BlockSpec(a.shape, lambda b, gg, i: (0,) * a.ndim)
    return pl.pallas_call(
        functools.partial(_nsa_kernel, tq=tq, tiles_per_step=tps, nk=nk, nkw=nkw, seq=seq, sel_chunk=sel_chunk),
        grid=(batch, g, n_steps),
        in_specs=[
            pl.BlockSpec((rq, gw), lambda b, gg, i: (row(b, gg, i), gg)),
            cmp_slab, cmp_slab,
            slab(0), slab(g), slab(2 * g), slab(3 * g),
            pl.BlockSpec((rq, LANES), lambda b, gg, i: (row(b, gg, i), gg)),
            pl.BlockSpec((rq, gw), lambda b, gg, i: (row(b, gg, i), gg)),
            pl.BlockSpec((1, dh, m_cols), lambda b, gg, i: (gg, 0, 0)),
            const(cpos), const(wsel),
        ],
        out_specs=pl.BlockSpec((rq, gw), lambda b, gg, i: (row(b, gg, i), gg)),
        out_shape=jax.ShapeDtypeStruct((t, g * gw), BF16),
        scratch_shapes=[
            pltpu.VMEM((2 * dh + nb, m_cols), BF16),
            pltpu.VMEM((seq, 2 * dh + nb), BF16),
            pltpu.VMEM((seq, 2 * dh), BF16),
            pltpu.VMEM((nc, 2 * dh), BF16),
            pltpu.VMEM((stage_rows, 2 * dh + nb), BF16),
            pltpu.VMEM((stage_rows, 2 * dh), BF16),
            pltpu.VMEM((n_wt * nkw, m_cols), F32),
            pltpu.VMEM((nk, m_cols), F32),
            pltpu.VMEM((nk, m_cols), F32),
            pltpu.VMEM((SUBLANES, m_cols), F32),
            pltpu.VMEM((dh, m_cols), F32),
            pltpu.SMEM((nb // WORD_BITS,), jnp.int32),
        ],
        compiler_params=_params(("arbitrary", "arbitrary", "arbitrary")),
        name="nsa",
    )(qm, kcm, vcm, kvm, kvm, kvm, kvm, gnm, gma, qal, cpos, wsel)


def _ffn_kernel(x_ref, yr_ref, ya_ref, wo_ref, gmix_ref, gpre_ref, wg_ref, wu_ref, wd_ref, gpost_ref,
                p_ref, wpg_ref, bpg_ref, wpp_ref, o_ref, x1_ref, h_ref, acc_ref):
    j = pl.program_id(1)

    @pl.when(j == 0)
    def _():
        y = (yr_ref[...].astype(F32) + ya_ref[...].astype(F32)).astype(BF16)
        x1 = x_ref[...] + _rmsnorm(_dot(y, wo_ref[...]), gmix_ref[...])
        x1_ref[...] = x1
        h_ref[...] = _rmsnorm(x1, gpre_ref[...]).astype(BF16)
        acc_ref[...] = jnp.zeros(acc_ref.shape, F32)

    h = h_ref[...]
    gate = _dot(h, wg_ref[...])
    up = _dot(h, wu_ref[...])
    act = (gate * _sigmoid(gate) * up).astype(BF16)
    acc_ref[...] += _dot(act, wd_ref[...])

    @pl.when(j == pl.num_programs(1) - 1)
    def _():
        x2 = x1_ref[...] + _rmsnorm(acc_ref[...], gpost_ref[...])
        gate_p = _sigmoid(_dot(x2.astype(BF16), wpg_ref[...]) + bpg_ref[...])
        o_ref[...] = x2 + gate_p * _dot(p_ref[...].astype(BF16), wpp_ref[...])


def _ffn(x, yr, ya, wo, gmix, gpre, wgu, wd, gpost, p, wpg, bpg, wpp, tm, tf):
    t, d = x.shape
    dff = wd.shape[0]
    nf = dff // tf
    dp = p.shape[1]
    rows = pl.BlockSpec((tm, d), lambda i, j: (i, 0))
    vec = pl.BlockSpec((1, d), lambda i, j: (0, 0))
    square = pl.BlockSpec((d, d), lambda i, j: (0, 0))
    return pl.pallas_call(
        _ffn_kernel,
        grid=(t // tm, nf),
        in_specs=[
            rows, rows, rows, square, vec, vec,
            pl.BlockSpec((d, tf), lambda i, j: (0, j)),
            pl.BlockSpec((d, tf), lambda i, j: (0, nf + j)),
            pl.BlockSpec((tf, d), lambda i, j: (j, 0)),
            vec,
            pl.BlockSpec((tm, dp), lambda i, j: (i, 0)),
            square,
            vec,
            pl.BlockSpec((dp, d), lambda i, j: (0, 0)),
        ],
        out_specs=rows,
        out_shape=jax.ShapeDtypeStruct((t, d), F32),
        scratch_shapes=[pltpu.VMEM((tm, d), F32), pltpu.VMEM((tm, d), BF16), pltpu.VMEM((tm, d), F32)],
        compiler_params=_params(("parallel", "arbitrary")),
        name="out_ffn_ple",
    )(x, yr, ya, wo, gmix, gpre, wgu, wgu, wd, gpost, p, wpg, bpg, wpp)


def _block_diag_chunks(w):
    n, bs, _ = w.shape
    per = LANES // bs
    w = w.reshape(n // per, per, bs, bs)
    eye = jnp.eye(per, dtype=w.dtype)
    return jnp.einsum('cpij,pq->cpiqj', w, eye).reshape(n // per, LANES, LANES)


def _layer(x, p, norm_mix_pre, norm_mix_post, w_in, conv_w, conv_b, lru_wa, lru_ba, lru_wx, lru_bx,
           lru_lambda, cmp_pos_k, cmp_pos_v, cmp_k_w1, cmp_k_w2, cmp_v_w1, cmp_v_w2, w_out,
           norm_ffn_pre, norm_ffn_post, ffn_w_gate_up, ffn_w_down, ple_w_proj, ple_w_gate, ple_b_gate,
           batch, seq):
    t, d = x.shape
    d_attn = N_HEADS * HEAD_DIM
    d_kv = N_KV_GROUPS * HEAD_DIM
    row2 = lambda v: v.reshape(1, -1)

    o_q = 2 * d
    o_kv = o_q + d_attn
    o_gn = o_kv + 6 * d_kv
    o_gm = o_gn + 3 * N_HEADS
    w_f = jnp.concatenate([w_in[:, 0:o_q], w_in[:, o_gm:o_gm + 2 * d]], axis=1).astype(BF16)
    w_q = w_in[:, o_q:o_kv].astype(BF16)
    w_kv = w_in[:, o_kv:o_gn].astype(BF16)
    per_g = 3 * HEADS_PER_GROUP
    w_gn = w_in[:, o_gn:o_gm].reshape(d, N_KV_GROUPS, per_g)
    w_gn = jnp.pad(w_gn, ((0, 0), (0, 0), (0, LANES - per_g))).reshape(d, N_KV_GROUPS * LANES).astype(BF16)

    g_pre = row2(norm_mix_pre)
    tm = min(512, t)
    yr, gma, qm, gnm, kv_chunks, kvm = _rnn(
        x, g_pre, w_f, jnp.concatenate([w_q, w_gn, w_kv], axis=1), d_attn, N_KV_GROUPS * LANES, 2 * N_KV_GROUPS,
        conv_w, row2(conv_b), _block_diag_chunks(lru_wa).astype(BF16), _block_diag_chunks(lru_wx).astype(BF16),
        row2(lru_ba), row2(lru_bx), row2(lru_lambda), batch, seq, min(512, seq))

    nchunk = seq // CMP_STRIDE
    kcm, vcm = _compress(kv_chunks, cmp_k_w1.astype(BF16), cmp_k_w2.astype(BF16), cmp_pos_k.reshape(1, -1),
                         cmp_v_w1.astype(BF16), cmp_v_w2.astype(BF16), cmp_pos_v.reshape(1, -1), batch, nchunk)

    tq = min(256, seq)
    ya = _nsa(qm, kvm, kcm, vcm, gnm, gma, batch, seq, tq, min(NSA_KEY_TILE, seq), min(256, seq))

    dff = ffn_w_down.shape[0]
    tf = dff
    return _ffn(x, yr, ya, w_out.astype(BF16), row2(norm_mix_post), row2(norm_ffn_pre),
                ffn_w_gate_up.astype(BF16), ffn_w_down.astype(BF16), row2(norm_ffn_post),
                p, ple_w_gate.astype(BF16), row2(ple_b_gate), ple_w_proj.astype(BF16), tm, tf)


def kernel(x, p, norm_mix_pre, norm_mix_post, w_in, conv_w, conv_b, lru_wa, lru_ba, lru_wx, lru_bx, lru_lambda, cmp_pos_k, cmp_pos_v, cmp_k_w1, cmp_k_w2, cmp_v_w1, cmp_v_w2, w_out, norm_ffn_pre, norm_ffn_post, ffn_w_gate_up, ffn_w_down, ple_w_proj, ple_w_gate, ple_b_gate):
    batch, seq, d = x.shape
    depth = w_in.shape[0]
    xf = x.reshape(batch * seq, d)
    for i in range(depth):
        xf = _layer(xf, p[i].reshape(batch * seq, -1), norm_mix_pre[i], norm_mix_post[i], w_in[i], conv_w[i],
                    conv_b[i], lru_wa[i], lru_ba[i], lru_wx[i], lru_bx[i], lru_lambda[i], cmp_pos_k[i],
                    cmp_pos_v[i], cmp_k_w1[i], cmp_k_w2[i], cmp_v_w1[i], cmp_v_w2[i], w_out[i],
                    norm_ffn_pre[i], norm_ffn_post[i], ffn_w_gate_up[i], ffn_w_down[i], ple_w_proj[i],
                    ple_w_gate[i], ple_b_gate[i], batch, seq)
    return xf.reshape(batch, seq, d)
```

```python
import functools

import numpy as np
import jax
import jax.numpy as jnp
from jax import lax
from jax.experimental import pallas as pl
from jax.experimental.pallas import tpu as pltpu

CONV_WIDTH = 4
LRU_C = 8.0
N_HEADS = 16
HEAD_DIM = 64
N_KV_GROUPS = 4
HEADS_PER_GROUP = N_HEADS // N_KV_GROUPS
CMP_LEN = 32
CMP_STRIDE = 16
SEL_LEN = 64
SEL_TOPK = 16
WINDOW = 512
ALIBI_MAX_BIAS = 8.0
NORM_EPS = 1e-6

LANES = 128
SUBLANES = 8
VMEM_LIMIT_BYTES = 56 * 1024 * 1024

MASK_NEG = -1e30
LOG2_E = 1.4426950408889634
N_ALIBI_COLS = 9
PAD_COL = HEAD_DIM + N_ALIBI_COLS
WORD_BITS = 16
WORD_SHIFT = 4
SEL_SHIFT = 6
N_FORCED = 3
NEAR_BLOCKS = 20
FILL_ROWS = 512
NSA_KEY_TILE = 11 * SEL_LEN
SCAN_UNROLL = 64
NSA_TILES_PER_STEP = 8
OVERLAP_CHUNKS = 4
BF16 = jnp.bfloat16
F32 = jnp.float32


def _dot(a, b):
    return jnp.dot(a, b, preferred_element_type=F32)


def _dot_nt(a, b):
    return lax.dot_general(a, b, (((1,), (1,)), ((), ())), preferred_element_type=F32)


def _sigmoid(x):
    return 0.5 * jnp.tanh(0.5 * x) + 0.5


def _gelu_tanh(x):
    c = np.float32(np.sqrt(2.0 / np.pi))
    half = 0.5 * x
    return half + half * jnp.tanh(x * (c + (c * 0.044715) * (x * x)))


def _rmsnorm(x, g):
    ms = jnp.mean(x * x, axis=-1, keepdims=True)
    return x * lax.rsqrt(ms + NORM_EPS) * g


def _params(sem):
    return pltpu.CompilerParams(dimension_semantics=sem, vmem_limit_bytes=VMEM_LIMIT_BYTES)


def _store_qkv(z, q_ref, gn_ref, cv_ref, kv_ref, slab_scr):
    nq = q_ref.shape[1]
    ngn = gn_ref.shape[1]
    q_ref[...] = z[:, 0:nq].astype(q_ref.dtype)
    gn_ref[...] = z[:, nq:nq + ngn]
    n_cv = cv_ref.shape[0]
    rows = cv_ref.shape[1]
    for c in range(n_cv):
        lo = nq + ngn + c * HEAD_DIM
        slab_scr[...] = z[:, lo:lo + HEAD_DIM]
        for l in range(CMP_STRIDE):
            cv_ref[c, :, l * HEAD_DIM:(l + 1) * HEAD_DIM] = (
                slab_scr[pl.ds(l, rows, stride=CMP_STRIDE), :].astype(cv_ref.dtype))
    for c in range(kv_ref.shape[0]):
        lo = nq + ngn + (n_cv + c) * HEAD_DIM
        kv_ref[c] = z[:, lo:lo + HEAD_DIM].astype(kv_ref.dtype)


def _rnn_kernel(x_ref, g_ref, wf_ref, wq_ref, cw_ref, cb_ref, wa_ref, wx_ref, ba_ref, bx_ref, lam_ref,
                o_ref, gma_ref, q_ref, gn_ref, cv_ref, kv_ref, xbuf, a_scr, b_scr, h_scr, slab_scr, *, ts):
    s_idx = pl.program_id(1)
    d = o_ref.shape[1]
    halo = SUBLANES

    @pl.when(s_idx == 0)
    def _():
        xbuf[0:halo, :] = jnp.zeros((halo, d), F32)
        h_scr[...] = jnp.zeros(h_scr.shape, F32)

    hn = _rmsnorm(x_ref[...], g_ref[...]).astype(BF16)
    xr = _dot(hn, wf_ref[:, 0:d])
    _store_qkv(_dot(hn, wq_ref[...]), q_ref, gn_ref, cv_ref, kv_ref, slab_scr)
    gr = _dot(hn, wf_ref[:, d:2 * d])
    gm = _dot(hn, wf_ref[:, 2 * d:3 * d])
    gma_ref[...] = _dot(hn, wf_ref[:, 3 * d:4 * d]).astype(gma_ref.dtype)

    xbuf[halo:halo + ts, :] = xr
    xc = cb_ref[...] + xbuf[halo:halo + ts, :] * cw_ref[CONV_WIDTH - 1:CONV_WIDTH, :]
    for k in range(1, CONV_WIDTH):
        xc = xc + xbuf[halo - k:halo - k + ts, :] * cw_ref[CONV_WIDTH - 1 - k:CONV_WIDTH - k, :]
    xbuf[0:halo, :] = xbuf[ts:ts + halo, :]

    xcb = xc.astype(BF16)
    n_chunks = d // LANES
    neg_sp = -LRU_C * (jnp.maximum(-lam_ref[...], 0.0) + jnp.log(1.0 + jnp.exp(-jnp.abs(lam_ref[...]))))
    for c in range(n_chunks):
        sl = slice(c * LANES, (c + 1) * LANES)
        xk = xcb[:, sl]
        r = _sigmoid(_dot(xk, wa_ref[c]) + ba_ref[:, sl])
        i = _sigmoid(_dot(xk, wx_ref[c]) + bx_ref[:, sl])
        log_a = r * neg_sp[:, sl]
        a = jnp.exp(log_a)
        a_scr[:, sl] = a
        u = 1.0 - a * a
        root = jnp.where(u > 0.0, u * lax.rsqrt(u), 0.0)
        b_scr[:, sl] = root * (i * xc[:, sl])

    row_id = lax.broadcasted_iota(jnp.int32, (SUBLANES, d), 0)

    def group(gi, h):
        base = pl.multiple_of(gi * SUBLANES, SUBLANES)
        a = a_scr[pl.ds(base, SUBLANES), :]
        b = b_scr[pl.ds(base, SUBLANES), :]
        s = 1
        while s < SUBLANES:
            reach = row_id >= s
            b = jnp.where(reach, a * pltpu.roll(b, s, 0) + b, b)
            a = jnp.where(reach, a * pltpu.roll(a, s, 0), a)
            s *= 2
        hs = a * h + b
        a_scr[pl.ds(base, SUBLANES), :] = hs
        return jnp.broadcast_to(hs[SUBLANES - 1:SUBLANES, :], (SUBLANES, d))

    h_scr[...] = lax.fori_loop(0, ts // SUBLANES, group, h_scr[...], unroll=SCAN_UNROLL)
    o_ref[...] = (_sigmoid(gm) * a_scr[...] * _gelu_tanh(gr)).astype(o_ref.dtype)


def _rnn(x, g, w_f, w_q, nq, ngn, n_cv, conv_w, conv_b, wa, wx, ba, bx, lam, batch, seq, ts):
    t = x.shape[0]
    d = conv_w.shape[1]
    ns = seq // ts
    n_kv = (w_q.shape[1] - nq - ngn) // HEAD_DIM - n_cv
    tile = lambda b, s: b * ns + s
    rows = lambda width: pl.BlockSpec((ts, width), lambda b, s: (tile(b, s), 0))
    vec = lambda r: pl.BlockSpec((r, d), lambda b, s: (0, 0))
    resident = lambda w: pl.BlockSpec(w.shape, lambda b, s: (0, 0), pipeline_mode=pl.Buffered(1))
    return pl.pallas_call(
        functools.partial(_rnn_kernel, ts=ts),
        grid=(batch, ns),
        in_specs=[
            rows(d), vec(1), resident(w_f), resident(w_q),
            vec(CONV_WIDTH), vec(1),
            pl.BlockSpec(wa.shape, lambda b, s: (0, 0, 0)),
            pl.BlockSpec(wx.shape, lambda b, s: (0, 0, 0)),
            vec(1), vec(1), vec(1),
        ],
        out_specs=[
            rows(d), rows(d), rows(nq), rows(ngn),
            pl.BlockSpec((n_cv, ts // CMP_STRIDE, CMP_STRIDE * HEAD_DIM), lambda b, s: (0, tile(b, s), 0)),
            pl.BlockSpec((n_kv, ts, HEAD_DIM), lambda b, s: (0, tile(b, s), 0)),
        ],
        out_shape=[
            jax.ShapeDtypeStruct((t, d), BF16),
            jax.ShapeDtypeStruct((t, d), BF16),
            jax.ShapeDtypeStruct((t, nq), BF16),
            jax.ShapeDtypeStruct((t, ngn), F32),
            jax.ShapeDtypeStruct((n_cv, t // CMP_STRIDE, CMP_STRIDE * HEAD_DIM), BF16),
            jax.ShapeDtypeStruct((n_kv, t, HEAD_DIM), BF16),
        ],
        scratch_shapes=[
            pltpu.VMEM((ts + SUBLANES, d), F32),
            pltpu.VMEM((ts, d), F32),
            pltpu.VMEM((ts, d), F32),
            pltpu.VMEM((SUBLANES, d), F32),
            pltpu.VMEM((ts, HEAD_DIM), F32),
        ],
        compiler_params=_params(("parallel", "arbitrary")),
        name="proj_rnn_mixer",
    )(x, g, w_f, w_q, conv_w, conv_b, wa, wx, ba, bx, lam)


def _compress_kernel(ak_ref, av_ref, w1k_ref, w2k_ref, pk_ref, w1v_ref, w2v_ref, pv_ref, ok_ref, ov_ref):
    def one(a_ref, w1_ref, w2_ref, p_ref, o_ref):
        a = a_ref[0]
        half = a.shape[1]
        nchunk = a.shape[0]
        lo = _dot(a, w1_ref[0:half, :])
        hi = _dot(a, w1_ref[half:2 * half, :])
        pb = _dot(jnp.broadcast_to(p_ref[...], (SUBLANES, 2 * half)).astype(BF16), w1_ref[...])[0:1, :]
        h = lo + pltpu.roll(hi, nchunk - 1, 0) + pb
        o_ref[0] = _dot(_gelu_tanh(h).astype(BF16), w2_ref[...]).astype(o_ref.dtype)

    one(ak_ref, w1k_ref, w2k_ref, pk_ref, ok_ref)
    one(av_ref, w1v_ref, w2v_ref, pv_ref, ov_ref)


def _compress(kv_chunks, w1k, w2k, pk, w1v, w2v, pv, batch, nchunk):
    g = N_KV_GROUPS
    width = kv_chunks.shape[2]
    full = lambda a: pl.BlockSpec(a.shape, lambda b, gg: (0,) * a.ndim)
    out = jax.ShapeDtypeStruct((batch * g, nchunk, HEAD_DIM), BF16)
    return pl.pallas_call(
        _compress_kernel,
        grid=(batch, g),
        in_specs=[
            pl.BlockSpec((1, nchunk, width), lambda b, gg: (gg, b, 0)),
            pl.BlockSpec((1, nchunk, width), lambda b, gg: (g + gg, b, 0)),
            full(w1k), full(w2k), full(pk), full(w1v), full(w2v), full(pv),
        ],
        out_specs=[
            pl.BlockSpec((1, nchunk, HEAD_DIM), lambda b, gg: (b * g + gg, 0, 0)),
            pl.BlockSpec((1, nchunk, HEAD_DIM), lambda b, gg: (b * g + gg, 0, 0)),
        ],
        out_shape=[out, out],
        compiler_params=_params(("parallel", "parallel")),
        name="compress",
    )(kv_chunks, kv_chunks, w1k, w2k, pk, w1v, w2v, pv)


ONES_ROWS = 16


def _accumulate(s_ref, col_max, v_aug_t, state):
    m_old, acc_old = state
    m_new = jnp.maximum(m_old, col_max)
    alpha = jnp.exp2(m_old - m_new)
    e = jnp.exp2(s_ref[...] - m_new).astype(BF16)
    return m_new, alpha * acc_old + _dot(v_aug_t, e)


def _nsa_kernel(q_ref, kc_ref, vc_ref, ks_ref, vs_ref, kw_ref, vw_ref, gn_ref, gm_ref,
                qal_ref, cpos_ref, wsel_ref, o_ref, *scratch, tq, tiles_per_step, **static):
    def tile(j, carry):
        rows = pl.ds(pl.multiple_of(j * tq, tq), tq)
        _nsa_tile(pl.program_id(2) * tiles_per_step + j, q_ref.at[rows, :], kc_ref, vc_ref, ks_ref, vs_ref,
                  kw_ref, vw_ref, gn_ref.at[rows, :], gm_ref.at[rows, :], qal_ref, cpos_ref, wsel_ref,
                  o_ref.at[rows, :], *scratch, tq=tq, **static)
        return carry

    lax.fori_loop(0, tiles_per_step, tile, 0)


def _nsa_tile(qt, q_ref, kc_ref, vc_ref, ks_ref, vs_ref, kw_ref, vw_ref, gn_ref, gm_ref,
              qal_ref, cpos_ref, wsel_ref,
              o_ref, q_t, ksaug, kwaug, kcaug, kstage, vstage, s_win, s_a, s_b, win_max, o_cmp_scr, flags,
              *, tq, nk, nkw, seq, sel_chunk):
    b = pl.program_id(0)
    g = pl.program_id(1)
    r_heads = HEADS_PER_GROUP
    dh = HEAD_DIM
    m_cols = r_heads * tq
    nb = seq // SEL_LEN
    nc = kc_ref.shape[1]
    a_w = 2 * dh
    k_w = a_w + nb

    @pl.when((b == 0) & (g == 0) & (qt == 0))
    def _():
        rows = min(FILL_ROWS, seq)

        def fill(c, carry):
            off = pl.multiple_of(c * rows, rows)
            pos = off + lax.broadcasted_iota(jnp.int32, (rows, k_w), 0)
            lane = lax.broadcasted_iota(jnp.int32, (rows, k_w), 1)
            blk_of = jnp.right_shift(pos, SEL_SHIFT)
            cols = jnp.where((lane >= dh) & (lane < dh + 3), blk_of * SEL_LEN,
                             jnp.where((lane >= dh + 3) & (lane < dh + 6), pos & (SEL_LEN - 1),
                                       jnp.where(lane == a_w + blk_of, 1, 0)))
            cols = cols.astype(F32).astype(BF16)
            ksaug[pl.ds(off, rows), :] = cols
            kwaug[pl.ds(off, rows), :] = cols[:, 0:a_w]
            return carry

        lax.fori_loop(0, seq // rows, fill, 0)
        kcaug[...] = cpos_ref[...]
        vstage[...] = jnp.ones(vstage.shape, BF16)

    @pl.when(qt == 0)
    def _():
        ksaug[:, 0:dh] = ks_ref[0]
        kwaug[:, 0:dh] = kw_ref[0]
        kcaug[:, 0:dh] = kc_ref[0]
        q_t[dh:a_w, :] = qal_ref[0]

    v_rows = dh + ONES_ROWS
    ri = lax.broadcasted_iota(jnp.int32, (v_rows, 2 * dh), 0)
    ci = lax.broadcasted_iota(jnp.int32, (v_rows, 2 * dh), 1)
    pick = (((ri < dh) & (ri == ci)) | ((ri >= dh) & (ci == dh))).astype(F32).astype(BF16)
    transposed = lambda v: _dot_nt(pick[0:dh, 0:dh], v).astype(BF16)
    staged_aug_t = lambda v: _dot_nt(pick, v).astype(BF16)
    ones_rows = jnp.ones((ONES_ROWS, nkw), BF16)
    window_aug_t = lambda v: jnp.concatenate([transposed(v), ones_rows], axis=0)
    init = (jnp.full((1, m_cols), MASK_NEG, F32), jnp.zeros((v_rows, m_cols), F32))

    t0 = qt * tq
    col = lax.broadcasted_iota(jnp.int32, (1, m_cols), 1)
    trow = t0 + (col & (tq - 1))

    scale = np.float32(HEAD_DIM ** -0.5 * LOG2_E)
    qf = jnp.transpose(q_ref[...].astype(F32) * scale)
    for r in range(r_heads):
        q_t[0:dh, r * tq:(r + 1) * tq] = qf[r * dh:(r + 1) * dh, :].astype(BF16)

    n_wt = 3
    win_off = [pl.multiple_of(jnp.maximum(t0 - i * nkw, 0), nkw) for i in range(n_wt)]
    newer = (t0 + lax.broadcasted_iota(jnp.int32, (nkw, 1), 0)) <= trow

    def window_tile(i):
        s = _dot(kwaug[pl.ds(win_off[i], nkw), :], q_t[0:a_w, :])
        return s if i == 0 else s + jnp.where(t0 - i * nkw >= 0, 0.0, MASK_NEG)

    def window_scores():
        merged = jnp.where(newer, window_tile(0), window_tile(2))
        s_win[0:nkw, :] = merged
        win_max[0:1, :] = jnp.max(merged, axis=0, keepdims=True)
        mid = window_tile(1)
        s_win[nkw:2 * nkw, :] = mid
        win_max[1:2, :] = jnp.max(mid, axis=0, keepdims=True)

    def compressed_and_select(rows_c, rows_b):
        free = max(rows_c - cmp_chunk - tq // CMP_STRIDE, 0)
        s = _dot(kcaug[0:rows_c, :], q_t[0:a_w, :])
        cend = (free + lax.broadcasted_iota(jnp.int32, (rows_c - free, 1), 0)) * CMP_STRIDE + (CMP_LEN - 1)
        tail = jnp.where(cend <= trow, s[free:], MASK_NEG)
        s = jnp.concatenate([s[0:free], tail], axis=0) if free else tail
        m = jnp.max(s, axis=0, keepdims=True)
        e = jnp.exp2(s - m)
        has_key = (trow >= CMP_LEN - 1).astype(F32)
        p = e * (has_key / jnp.sum(e, axis=0, keepdims=True))
        o_cmp_scr[...] = _dot(transposed(vc_ref[0, 0:rows_c, :]), p.astype(BF16))
        imp = p[:, 0:tq]
        for r in range(1, r_heads):
            imp = imp + p[:, r * tq:(r + 1) * tq]

        hi = imp.astype(BF16)
        lo = (imp - hi.astype(F32)).astype(BF16)
        wsel = wsel_ref[0:rows_b, 0:rows_c]
        imp_t = _dot(wsel, hi) + _dot(wsel, lo)

        window_scores()

        blk = lax.broadcasted_iota(jnp.int32, (rows_b, tq), 0).astype(F32)
        tq_l = t0 + lax.broadcasted_iota(jnp.int32, (rows_b, tq), 1)
        cur = jnp.right_shift(tq_l, SEL_SHIFT).astype(F32)
        valid = blk <= cur
        sel = jnp.where(blk == 0.0, 1.0, jnp.where(blk == cur, 1.0, jnp.where(blk == cur - 1.0, 1.0, 0.0)))
        score = jnp.where(valid, jnp.where(sel > 0.0, -1.0, imp_t), -1.0)
        for _ in range(min(SEL_TOPK, nb) - N_FORCED):
            mx = jnp.max(score, axis=0, keepdims=True)
            idx = jnp.min(jnp.where(score == mx, blk, float(nb)), axis=0, keepdims=True)
            hit = blk == idx
            sel = jnp.where(hit, 1.0, sel)
            score = jnp.where(hit, -2.0, score)
        selv = jnp.where(valid, sel, 0.0) > 0.0
        selneg_t = jnp.where(selv, 0.0, MASK_NEG).astype(BF16)
        unseen = jnp.full((nb - rows_b, tq), MASK_NEG, BF16)
        for r in range(r_heads):
            q_t[a_w:a_w + rows_b, r * tq:(r + 1) * tq] = selneg_t
            if rows_b < nb:
                q_t[a_w + rows_b:, r * tq:(r + 1) * tq] = unseen

        used = jnp.max(jnp.where(selv, 1.0, 0.0), axis=1, keepdims=True)
        bit_id = lax.broadcasted_iota(jnp.int32, (rows_b, 1), 0) & (WORD_BITS - 1)
        weighted = used * jnp.left_shift(1, bit_id).astype(F32)
        for i in range(nb // WORD_BITS):
            if (i + 1) * WORD_BITS <= rows_b:
                flags[i] = jnp.sum(weighted[i * WORD_BITS:(i + 1) * WORD_BITS, :]).astype(jnp.int32)
            else:
                flags[i] = 0

    n_var = nb // sel_chunk
    cmp_chunk = sel_chunk * (SEL_LEN // CMP_STRIDE)
    seen_c = (t0 + tq - CMP_LEN) // CMP_STRIDE + 1
    seen_b = (t0 + tq) // SEL_LEN
    variant = jnp.maximum((seen_c + cmp_chunk - 1) // cmp_chunk, (seen_b + sel_chunk - 1) // sel_chunk)
    for v in range(1, n_var + 1):
        pl.when(variant == v)(functools.partial(compressed_and_select, min(v * cmp_chunk, nc), v * sel_chunk))
    o_cmp = o_cmp_scr[...]

    n_own = tq // SEL_LEN
    off_q = pl.multiple_of(t0, tq)
    kstage[0:tq, :] = ksaug[pl.ds(off_q, tq), :]
    vstage[0:tq, 0:dh] = vs_ref[0, pl.ds(off_q, tq), :]

    def stage_block(j, cnt):
        src = pl.multiple_of(j * SEL_LEN, SEL_LEN)
        dst = pl.multiple_of(cnt * SEL_LEN, SEL_LEN)
        kstage[pl.ds(dst, SEL_LEN), :] = ksaug[pl.ds(src, SEL_LEN), :]
        vstage[pl.ds(dst, SEL_LEN), 0:dh] = vs_ref[0, pl.ds(src, SEL_LEN), :]

    def gather(j, cnt):
        bit = jnp.right_shift(flags[jnp.right_shift(j, WORD_SHIFT)], j & (WORD_BITS - 1)) & 1
        pl.when(bit == 1)(functools.partial(stage_block, j, cnt))
        return cnt + bit

    n_past = jnp.right_shift(t0, SEL_SHIFT)
    near_lo = jnp.maximum(n_past - NEAR_BLOCKS, 0)
    first = jnp.where(near_lo > 0, flags[0] & 1, 0)
    pl.when(first == 1)(functools.partial(stage_block, 0, n_own))

    def far_word(w, cnt):
        lo = jnp.maximum(w * WORD_BITS, 1)
        hi = jnp.minimum((w + 1) * WORD_BITS, near_lo)
        rest = jnp.where(w == 0, flags[w] & -2, flags[w])
        return lax.cond(rest != 0, lambda c: lax.fori_loop(lo, hi, gather, c), lambda c: c, cnt)

    n_blocks = lax.fori_loop(0, jnp.right_shift(near_lo + WORD_BITS - 1, WORD_SHIFT), far_word, n_own + first)
    n_blocks = lax.fori_loop(near_lo, n_past, gather, n_blocks)
    per_tile = nk // SEL_LEN
    n_tiles = (n_blocks + per_tile - 1) // per_tile
    pad_block = jnp.where(lax.broadcasted_iota(jnp.int32, (SEL_LEN, k_w), 1) == PAD_COL, 1.0, 0.0).astype(BF16)

    def pad(j, carry):
        dst = pl.multiple_of(j * SEL_LEN, SEL_LEN)
        kstage[pl.ds(dst, SEL_LEN), :] = pad_block
        return carry

    lax.fori_loop(n_blocks, n_tiles * per_tile, pad, 0)

    own_pos = t0 + lax.broadcasted_iota(jnp.int32, (tq, 1), 0)
    s_first = _dot(kstage[0:nk, :], q_t[...])
    own = jnp.where(own_pos <= trow, s_first[0:tq], MASK_NEG)
    s_a[0:tq, :] = own
    s_a[tq:, :] = s_first[tq:]
    max_a = jnp.maximum(jnp.max(own, axis=0, keepdims=True), jnp.max(s_first[tq:], axis=0, keepdims=True))

    m_w = jnp.maximum(win_max[0:1, :], win_max[1:2, :])
    e_merged = jnp.exp2(s_win[0:nkw, :] - m_w)
    e_mid = jnp.exp2(s_win[nkw:2 * nkw, :] - m_w)
    weights = jnp.concatenate([jnp.where(newer, e_merged, 0.0).astype(BF16), e_mid.astype(BF16),
                               jnp.where(newer, 0.0, e_merged).astype(BF16)], axis=0)
    v_win = jnp.concatenate([window_aug_t(vw_ref[0, pl.ds(win_off[i], nkw), :]) for i in range(n_wt)], axis=1)
    acc_w = _dot(v_win, weights)
    o_win = acc_w[0:dh] / acc_w[dh:dh + 1]

    def tile_accumulate(i, s_ref, col_max, state):
        return _accumulate(s_ref, col_max, staged_aug_t(vstage[pl.ds(pl.multiple_of(i * nk, nk), nk), :]), state)

    def scores_and_accumulate(i_next, s_next_ref, i_cur, s_cur_ref, max_cur, state):
        m_old, acc_old = state
        m_new = jnp.maximum(m_old, max_cur)
        alpha = jnp.exp2(m_old - m_new)
        s_next = _dot(kstage[pl.ds(pl.multiple_of(i_next * nk, nk), nk), :], q_t[...])
        s_next_ref[...] = s_next
        rows = nk // OVERLAP_CHUNKS
        partial, weights = [], []
        for c in range(OVERLAP_CHUNKS):
            rs = slice(c * rows, (c + 1) * rows)
            partial.append(jnp.max(s_next[rs], axis=0, keepdims=True))
            m_c = m_new + partial[c] * 0.0
            weights.append(jnp.exp2(s_cur_ref[rs, :] - m_c).astype(BF16))
        v_t = staged_aug_t(vstage[pl.ds(pl.multiple_of(i_cur * nk, nk), nk), :])
        acc = alpha * acc_old + _dot(v_t, jnp.concatenate(weights, axis=0))
        return (m_new, acc), functools.reduce(jnp.maximum, partial)

    def slc_pair(j, carry):
        state, max_a = carry
        i = 2 * j
        state, max_b = scores_and_accumulate(i + 1, s_b, i, s_a, max_a, state)
        state, max_a = scores_and_accumulate(i + 2, s_a, i + 1, s_b, max_b, state)
        return state, max_a

    n_pairs = (n_tiles - 1) // 2
    state, max_a = lax.fori_loop(0, n_pairs, slc_pair, (init, max_a))
    i_a = 2 * n_pairs

    def two_left(state):
        state, max_b = scores_and_accumulate(i_a + 1, s_b, i_a, s_a, max_a, state)
        return tile_accumulate(i_a + 1, s_b, max_b, state)

    def one_left(state):
        return tile_accumulate(i_a, s_a, max_a, state)

    _, acc = lax.cond(n_tiles - i_a == 2, two_left, one_left, state)
    o_slc = acc[0:dh] / acc[dh:dh + 1]

    gates = jnp.transpose(_sigmoid(gn_ref[...]))
    heads = []
    for r in range(r_heads):
        cs = slice(r * tq, (r + 1) * tq)
        heads.append(gates[3 * r:3 * r + 1, :] * o_cmp[:, cs] + gates[3 * r + 1:3 * r + 2, :] * o_slc[:, cs]
                     + gates[3 * r + 2:3 * r + 3, :] * o_win[:, cs])
    o = jnp.transpose(jnp.concatenate(heads, axis=0))
    o_ref[...] = (_sigmoid(gm_ref[...].astype(F32)) * o).astype(o_ref.dtype)


def _alibi_tables(seq, nc, tq):
    import ml_dtypes
    bf = ml_dtypes.bfloat16
    h = np.arange(1, N_HEADS + 1, dtype=np.float32)
    slopes = (np.exp2(-ALIBI_MAX_BIAS * h / N_HEADS) * LOG2_E).astype(np.float32)
    s1 = slopes.astype(bf).astype(np.float32)
    s2 = (slopes - s1).astype(bf).astype(np.float32)
    s3 = (slopes - s1 - s2).astype(bf).astype(np.float32)
    dh = HEAD_DIM
    qal = np.zeros((N_HEADS, dh), np.float32)
    for rep in range(3):
        qal[:, 3 * rep + 0] = s1
        qal[:, 3 * rep + 1] = s2
        qal[:, 3 * rep + 2] = s3
    qal[:, PAD_COL - dh] = MASK_NEG
    qal = qal.reshape(N_KV_GROUPS, HEADS_PER_GROUP, dh).transpose(0, 2, 1)
    qal_p = np.repeat(qal, tq, axis=2)

    nb = seq // SEL_LEN
    c = np.arange(nc)
    cpos = np.zeros((nc, 2 * dh), np.float32)
    cpos[:, dh:dh + 3] = ((c // 16) * 16 * CMP_STRIDE)[:, None]
    cpos[:, dh + 3:dh + 6] = ((c % 16) * CMP_STRIDE)[:, None]
    cpos[:, dh + 6:dh + 9] = CMP_LEN - 1

    r_sel = SEL_LEN // CMP_STRIDE
    r_cmp = CMP_LEN // CMP_STRIDE
    wsel = np.zeros((nb, nc), np.float32)
    for j in range(nb):
        for mm in range(r_sel):
            for nn in range(r_cmp):
                ci = r_sel * j + mm - nn
                if 0 <= ci < nc - 1:
                    wsel[j, ci] += 1.0
    as_bf = lambda a: jnp.asarray(a.astype(bf))
    return as_bf(qal_p), as_bf(cpos), as_bf(wsel)


def _nsa(qm, kvm, kcm, vcm, gnm, gma, batch, seq, tq, nk, nkw):
    t = qm.shape[0]
    g = N_KV_GROUPS
    dh = HEAD_DIM
    nqt = seq // tq
    nc = kcm.shape[1]
    nb = seq // SEL_LEN
    gw = HEADS_PER_GROUP * dh
    m_cols = HEADS_PER_GROUP * tq
    qal, cpos, wsel = _alibi_tables(seq, nc, tq)
    assert nkw == tq and WINDOW == 2 * tq, "the window branch merges the new and old key tiles of a query tile"
    n_wt = 2
    sel_chunk = max(WORD_BITS, nb // 8)
    per_tile = nk // SEL_LEN
    stage_rows = -(-nb // per_tile) * per_tile * SEL_LEN
    tps = NSA_TILES_PER_STEP if nqt % NSA_TILES_PER_STEP == 0 else 1
    n_steps = nqt // tps
    rq = tps * tq
    row = lambda b, gg, i: b * n_steps + i
    slab = lambda base: pl.BlockSpec((1, seq, dh), lambda b, gg, i: (base + gg, b, 0))
    cmp_slab = pl.BlockSpec((1, nc, dh), lambda b, gg, i: (b * g + gg, 0, 0))
    const = lambda a: pl.BlockSpec(a.shape, lambda b, gg, i: (0,) * a.ndim)
    return pl.pallas_call(
        functools.partial(_nsa_kernel, tq=tq, tiles_per_step=tps, nk=nk, nkw=nkw, seq=seq, sel_chunk=sel_chunk),
        grid=(batch, g, n_steps),
        in_specs=[
            pl.BlockSpec((rq, gw), lambda b, gg, i: (row(b, gg, i), gg)),
            cmp_slab, cmp_slab,
            slab(0), slab(g), slab(2 * g), slab(3 * g),
            pl.BlockSpec((rq, LANES), lambda b, gg, i: (row(b, gg, i), gg)),
            pl.BlockSpec((rq, gw), lambda b, gg, i: (row(b, gg, i), gg)),
            pl.BlockSpec((1, dh, m_cols), lambda b, gg, i: (gg, 0, 0)),
            const(cpos), const(wsel),
        ],
        out_specs=pl.BlockSpec((rq, gw), lambda b, gg, i: (row(b, gg, i), gg)),
        out_shape=jax.ShapeDtypeStruct((t, g * gw), BF16),
        scratch_shapes=[
            pltpu.VMEM((2 * dh + nb, m_cols), BF16),
            pltpu.VMEM((seq, 2 * dh + nb), BF16),
            pltpu.VMEM((seq, 2 * dh), BF16),
            pltpu.VMEM((nc, 2 * dh), BF16),
            pltpu.VMEM((stage_rows, 2 * dh + nb), BF16),
            pltpu.VMEM((stage_rows, 2 * dh), BF16),
            pltpu.VMEM((n_wt * nkw, m_cols), F32),
            pltpu.VMEM((nk, m_cols), F32),
            pltpu.VMEM((nk, m_cols), F32),
            pltpu.VMEM((SUBLANES, m_cols), F32),
            pltpu.VMEM((dh, m_cols), F32),
            pltpu.SMEM((nb // WORD_BITS,), jnp.int32),
        ],
        compiler_params=_params(("arbitrary", "arbitrary", "arbitrary")),
        name="nsa",
    )(qm, kcm, vcm, kvm, kvm, kvm, kvm, gnm, gma, qal, cpos, wsel)


def _ffn_kernel(x_ref, yr_ref, ya_ref, wo_ref, gmix_ref, gpre_ref, wg_ref, wu_ref, wd_ref, gpost_ref,
                p_ref, wpg_ref, bpg_ref, wpp_ref, o_ref, x1_ref, h_ref, acc_ref):
    j = pl.program_id(1)

    @pl.when(j == 0)
    def _():
        y = (yr_ref[...].astype(F32) + ya_ref[...].astype(F32)).astype(BF16)
        x1 = x_ref[...] + _rmsnorm(_dot(y, wo_ref[...]), gmix_ref[...])
        x1_ref[...] = x1
        h_ref[...] = _rmsnorm(x1, gpre_ref[...]).astype(BF16)
        acc_ref[...] = jnp.zeros(acc_ref.shape, F32)

    h = h_ref[...]
    gate = _dot(h, wg_ref[...])
    up = _dot(h, wu_ref[...])
    act = (gate * _sigmoid(gate) * up).astype(BF16)
    acc_ref[...] += _dot(act, wd_ref[...])

    @pl.when(j == pl.num_programs(1) - 1)
    def _():
        x2 = x1_ref[...] + _rmsnorm(acc_ref[...], gpost_ref[...])
        gate_p = _sigmoid(_dot(x2.astype(BF16), wpg_ref[...]) + bpg_ref[...])
        o_ref[...] = x2 + gate_p * _dot(p_ref[...].astype(BF16), wpp_ref[...])


def _ffn(x, yr, ya, wo, gmix, gpre, wgu, wd, gpost, p, wpg, bpg, wpp, tm, tf):
    t, d = x.shape
    dff = wd.shape[0]
    nf = dff // tf
    dp = p.shape[1]
    rows = pl.BlockSpec((tm, d), lambda i, j: (i, 0))
    vec = pl.BlockSpec((1, d), lambda i, j: (0, 0))
    square = pl.BlockSpec((d, d), lambda i, j: (0, 0))
    return pl.pallas_call(
        _ffn_kernel,
        grid=(t // tm, nf),
        in_specs=[
            rows, rows, rows, square, vec, vec,
            pl.BlockSpec((d, tf), lambda i, j: (0, j)),
            pl.BlockSpec((d, tf), lambda i, j: (0, nf + j)),
            pl.BlockSpec((tf, d), lambda i, j: (j, 0)),
            vec,
            pl.BlockSpec((tm, dp), lambda i, j: (i, 0)),
            square,
            vec,
            pl.BlockSpec((dp, d), lambda i, j: (0, 0)),
        ],
        out_specs=rows,
        out_shape=jax.ShapeDtypeStruct((t, d), F32),
        scratch_shapes=[pltpu.VMEM((tm, d), F32), pltpu.VMEM((tm, d), BF16), pltpu.VMEM((tm, d), F32)],
        compiler_params=_params(("parallel", "arbitrary")),
        name="out_ffn_ple",
    )(x, yr, ya, wo, gmix, gpre, wgu, wgu, wd, gpost, p, wpg, bpg, wpp)


def _block_diag_chunks(w):
    n, bs, _ = w.shape
    per = LANES // bs
    w = w.reshape(n // per, per, bs, bs)
    eye = jnp.eye(per, dtype=w.dtype)
    return jnp.einsum('cpij,pq->cpiqj', w, eye).reshape(n // per, LANES, LANES)


def _layer(x, p, norm_mix_pre, norm_mix_post, w_in, conv_w, conv_b, lru_wa, lru_ba, lru_wx, lru_bx,
           lru_lambda, cmp_pos_k, cmp_pos_v, cmp_k_w1, cmp_k_w2, cmp_v_w1, cmp_v_w2, w_out,
           norm_ffn_pre, norm_ffn_post, ffn_w_gate_up, ffn_w_down, ple_w_proj, ple_w_gate, ple_b_gate,
           batch, seq):
    t, d = x.shape
    d_attn = N_HEADS * HEAD_DIM
    d_kv = N_KV_GROUPS * HEAD_DIM
    row2 = lambda v: v.reshape(1, -1)

    o_q = 2 * d
    o_kv = o_q + d_attn
    o_gn = o_kv + 6 * d_kv
    o_gm = o_gn + 3 * N_HEADS
    w_f = jnp.concatenate([w_in[:, 0:o_q], w_in[:, o_gm:o_gm + 2 * d]], axis=1).astype(BF16)
    w_q = w_in[:, o_q:o_kv].astype(BF16)
    w_kv = w_in[:, o_kv:o_gn].astype(BF16)
    per_g = 3 * HEADS_PER_GROUP
    w_gn = w_in[:, o_gn:o_gm].reshape(d, N_KV_GROUPS, per_g)
    w_gn = jnp.pad(w_gn, ((0, 0), (0, 0), (0, LANES - per_g))).reshape(d, N_KV_GROUPS * LANES).astype(BF16)

    g_pre = row2(norm_mix_pre)
    tm = min(512, t)
    yr, gma, qm, gnm, kv_chunks, kvm = _rnn(
        x, g_pre, w_f, jnp.concatenate([w_q, w_gn, w_kv], axis=1), d_attn, N_KV_GROUPS * LANES, 2 * N_KV_GROUPS,
        conv_w, row2(conv_b), _block_diag_chunks(lru_wa).astype(BF16), _block_diag_chunks(lru_wx).astype(BF16),
        row2(lru_ba), row2(lru_bx), row2(lru_lambda), batch, seq, min(512, seq))

    nchunk = seq // CMP_STRIDE
    kcm, vcm = _compress(kv_chunks, cmp_k_w1.astype(BF16), cmp_k_w2.astype(BF16), cmp_pos_k.reshape(1, -1),
                         cmp_v_w1.astype(BF16), cmp_v_w2.astype(BF16), cmp_pos_v.reshape(1, -1), batch, nchunk)

    tq = min(256, seq)
    ya = _nsa(qm, kvm, kcm, vcm, gnm, gma, batch, seq, tq, min(NSA_KEY_TILE, seq), min(256, seq))

    dff = ffn_w_down.shape[0]
    tf = dff
    return _ffn(x, yr, ya, w_out.astype(BF16), row2(norm_mix_post), row2(norm_ffn_pre),
                ffn_w_gate_up.astype(BF16), ffn_w_down.astype(BF16), row2(norm_ffn_post),
                p, ple_w_gate.astype(BF16), row2(ple_b_gate), ple_w_proj.astype(BF16), tm, tf)


def kernel(x, p, norm_mix_pre, norm_mix_post, w_in, conv_w, conv_b, lru_wa, lru_ba, lru_wx, lru_bx, lru_lambda, cmp_pos_k, cmp_pos_v, cmp_k_w1, cmp_k_w2, cmp_v_w1, cmp_v_w2, w_out, norm_ffn_pre, norm_ffn_post, ffn_w_gate_up, ffn_w_down, ple_w_proj, ple_w_gate, ple_b_gate):
    batch, seq, d = x.shape
    depth = w_in.shape[0]
    xf = x.reshape(batch * seq, d)
    for i in range(depth):
        xf = _layer(xf, p[i].reshape(batch * seq, -1), norm_mix_pre[i], norm_mix_post[i], w_in[i], conv_w[i],
                    conv_b[i], lru_wa[i], lru_ba[i], lru_wx[i], lru_bx[i], lru_lambda[i], cmp_pos_k[i],
                    cmp_pos_v[i], cmp_k_w1[i], cmp_k_w2[i], cmp_v_w1[i], cmp_v_w2[i], w_out[i],
                    norm_ffn_pre[i], norm_ffn_post[i], ffn_w_gate_up[i], ffn_w_down[i], ple_w_proj[i],
                    ple_w_gate[i], ple_b_gate[i], batch, seq)
    return xf.reshape(batch, seq, d)
```

```python
import functools

import numpy as np
import jax
import jax.numpy as jnp
from jax import lax
from jax.experimental import pallas as pl
from jax.experimental.pallas import tpu as pltpu

CONV_WIDTH = 4
LRU_C = 8.0
N_HEADS = 16
HEAD_DIM = 64
N_KV_GROUPS = 4
HEADS_PER_GROUP = N_HEADS // N_KV_GROUPS
CMP_LEN = 32
CMP_STRIDE = 16
SEL_LEN = 64
SEL_TOPK = 16
WINDOW = 512
ALIBI_MAX_BIAS = 8.0
NORM_EPS = 1e-6

LANES = 128
SUBLANES = 8
VMEM_LIMIT_BYTES = 56 * 1024 * 1024

MASK_NEG = -1e30
LOG2_E = 1.4426950408889634
N_ALIBI_COLS = 9
PAD_COL = HEAD_DIM + N_ALIBI_COLS
WORD_BITS = 16
WORD_SHIFT = 4
SEL_SHIFT = 6
N_FORCED = 3
NEAR_BLOCKS = 20
FILL_ROWS = 512
NSA_KEY_TILE = 11 * SEL_LEN
NSA_TILES_PER_STEP = 8
OVERLAP_CHUNKS = 4
BF16 = jnp.bfloat16
F32 = jnp.float32


def _dot(a, b):
    return jnp.dot(a, b, preferred_element_type=F32)


def _dot_nt(a, b):
    return lax.dot_general(a, b, (((1,), (1,)), ((), ())), preferred_element_type=F32)


def _sigmoid(x):
    return 0.5 * jnp.tanh(0.5 * x) + 0.5


def _gelu_tanh(x):
    c = np.float32(np.sqrt(2.0 / np.pi))
    half = 0.5 * x
    return half + half * jnp.tanh(x * (c + (c * 0.044715) * (x * x)))


def _rmsnorm(x, g):
    ms = jnp.mean(x * x, axis=-1, keepdims=True)
    return x * lax.rsqrt(ms + NORM_EPS) * g


def _params(sem):
    return pltpu.CompilerParams(dimension_semantics=sem, vmem_limit_bytes=VMEM_LIMIT_BYTES)


def _store_qkv(z, q_ref, gn_ref, cv_ref, kv_ref, slab_scr):
    nq = q_ref.shape[1]
    ngn = gn_ref.shape[1]
    q_ref[...] = z[:, 0:nq].astype(q_ref.dtype)
    gn_ref[...] = z[:, nq:nq + ngn]
    n_cv = cv_ref.shape[0]
    rows = cv_ref.shape[1]
    for c in range(n_cv):
        lo = nq + ngn + c * HEAD_DIM
        slab_scr[...] = z[:, lo:lo + HEAD_DIM]
        for l in range(CMP_STRIDE):
            cv_ref[c, :, l * HEAD_DIM:(l + 1) * HEAD_DIM] = (
                slab_scr[pl.ds(l, rows, stride=CMP_STRIDE), :].astype(cv_ref.dtype))
    for c in range(kv_ref.shape[0]):
        lo = nq + ngn + (n_cv + c) * HEAD_DIM
        kv_ref[c] = z[:, lo:lo + HEAD_DIM].astype(kv_ref.dtype)


def _rnn_kernel(x_ref, g_ref, wf_ref, wq_ref, cw_ref, cb_ref, wa_ref, wx_ref, ba_ref, bx_ref, lam_ref,
                o_ref, gma_ref, q_ref, gn_ref, cv_ref, kv_ref, xbuf, a_scr, b_scr, h_scr, slab_scr, *, ts):
    s_idx = pl.program_id(1)
    d = o_ref.shape[1]
    halo = SUBLANES

    @pl.when(s_idx == 0)
    def _():
        xbuf[0:halo, :] = jnp.zeros((halo, d), F32)
        h_scr[...] = jnp.zeros(h_scr.shape, F32)

    hn = _rmsnorm(x_ref[...], g_ref[...]).astype(BF16)
    xr = _dot(hn, wf_ref[:, 0:d])
    _store_qkv(_dot(hn, wq_ref[...]), q_ref, gn_ref, cv_ref, kv_ref, slab_scr)
    gr = _dot(hn, wf_ref[:, d:2 * d])
    gm = _dot(hn, wf_ref[:, 2 * d:3 * d])
    gma_ref[...] = _dot(hn, wf_ref[:, 3 * d:4 * d]).astype(gma_ref.dtype)

    xbuf[halo:halo + ts, :] = xr
    xc = cb_ref[...] + xbuf[halo:halo + ts, :] * cw_ref[CONV_WIDTH - 1:CONV_WIDTH, :]
    for k in range(1, CONV_WIDTH):
        xc = xc + xbuf[halo - k:halo - k + ts, :] * cw_ref[CONV_WIDTH - 1 - k:CONV_WIDTH - k, :]
    xbuf[0:halo, :] = xbuf[ts:ts + halo, :]

    xcb = xc.astype(BF16)
    n_chunks = d // LANES
    neg_sp = -LRU_C * (jnp.maximum(-lam_ref[...], 0.0) + jnp.log(1.0 + jnp.exp(-jnp.abs(lam_ref[...]))))
    for c in range(n_chunks):
        sl = slice(c * LANES, (c + 1) * LANES)
        xk = xcb[:, sl]
        r = _sigmoid(_dot(xk, wa_ref[c]) + ba_ref[:, sl])
        i = _sigmoid(_dot(xk, wx_ref[c]) + bx_ref[:, sl])
        log_a = r * neg_sp[:, sl]
        a = jnp.exp(log_a)
        a_scr[:, sl] = a
        u = 1.0 - a * a
        root = jnp.where(u > 0.0, u * lax.rsqrt(u), 0.0)
        b_scr[:, sl] = root * (i * xc[:, sl])

    row_id = lax.broadcasted_iota(jnp.int32, (SUBLANES, d), 0)

    def group(gi, h):
        base = pl.multiple_of(gi * SUBLANES, SUBLANES)
        a = a_scr[pl.ds(base, SUBLANES), :]
        b = b_scr[pl.ds(base, SUBLANES), :]
        s = 1
        while s < SUBLANES:
            reach = row_id >= s
            b = jnp.where(reach, a * pltpu.roll(b, s, 0) + b, b)
            a = jnp.where(reach, a * pltpu.roll(a, s, 0), a)
            s *= 2
        hs = a * h + b
        a_scr[pl.ds(base, SUBLANES), :] = hs
        return jnp.broadcast_to(hs[SUBLANES - 1:SUBLANES, :], (SUBLANES, d))

    h_scr[...] = lax.fori_loop(0, ts // SUBLANES, group, h_scr[...], unroll=True)
    o_ref[...] = (_sigmoid(gm) * a_scr[...] * _gelu_tanh(gr)).astype(o_ref.dtype)


def _rnn(x, g, w_f, w_q, nq, ngn, n_cv, conv_w, conv_b, wa, wx, ba, bx, lam, batch, seq, ts):
    t = x.shape[0]
    d = conv_w.shape[1]
    ns = seq // ts
    n_kv = (w_q.shape[1] - nq - ngn) // HEAD_DIM - n_cv
    tile = lambda b, s: b * ns + s
    rows = lambda width: pl.BlockSpec((ts, width), lambda b, s: (tile(b, s), 0))
    vec = lambda r: pl.BlockSpec((r, d), lambda b, s: (0, 0))
    resident = lambda w: pl.BlockSpec(w.shape, lambda b, s: (0, 0), pipeline_mode=pl.Buffered(1))
    return pl.pallas_call(
        functools.partial(_rnn_kernel, ts=ts),
        grid=(batch, ns),
        in_specs=[
            rows(d), vec(1), resident(w_f), resident(w_q),
            vec(CONV_WIDTH), vec(1),
            pl.BlockSpec(wa.shape, lambda b, s: (0, 0, 0)),
            pl.BlockSpec(wx.shape, lambda b, s: (0, 0, 0)),
            vec(1), vec(1), vec(1),
        ],
        out_specs=[
            rows(d), rows(d), rows(nq), rows(ngn),
            pl.BlockSpec((n_cv, ts // CMP_STRIDE, CMP_STRIDE * HEAD_DIM), lambda b, s: (0, tile(b, s), 0)),
            pl.BlockSpec((n_kv, ts, HEAD_DIM), lambda b, s: (0, tile(b, s), 0)),
        ],
        out_shape=[
            jax.ShapeDtypeStruct((t, d), BF16),
            jax.ShapeDtypeStruct((t, d), BF16),
            jax.ShapeDtypeStruct((t, nq), BF16),
            jax.ShapeDtypeStruct((t, ngn), F32),
            jax.ShapeDtypeStruct((n_cv, t // CMP_STRIDE, CMP_STRIDE * HEAD_DIM), BF16),
            jax.ShapeDtypeStruct((n_kv, t, HEAD_DIM), BF16),
        ],
        scratch_shapes=[
            pltpu.VMEM((ts + SUBLANES, d), F32),
            pltpu.VMEM((ts, d), F32),
            pltpu.VMEM((ts, d), F32),
            pltpu.VMEM((SUBLANES, d), F32),
            pltpu.VMEM((ts, HEAD_DIM), F32),
        ],
        compiler_params=_params(("parallel", "arbitrary")),
        name="proj_rnn_mixer",
    )(x, g, w_f, w_q, conv_w, conv_b, wa, wx, ba, bx, lam)


def _compress_kernel(ak_ref, av_ref, w1k_ref, w2k_ref, pk_ref, w1v_ref, w2v_ref, pv_ref, ok_ref, ov_ref):
    def one(a_ref, w1_ref, w2_ref, p_ref, o_ref):
        a = a_ref[0]
        half = a.shape[1]
        nchunk = a.shape[0]
        lo = _dot(a, w1_ref[0:half, :])
        hi = _dot(a, w1_ref[half:2 * half, :])
        pb = _dot(jnp.broadcast_to(p_ref[...], (SUBLANES, 2 * half)).astype(BF16), w1_ref[...])[0:1, :]
        h = lo + pltpu.roll(hi, nchunk - 1, 0) + pb
        o_ref[0] = _dot(_gelu_tanh(h).astype(BF16), w2_ref[...]).astype(o_ref.dtype)

    one(ak_ref, w1k_ref, w2k_ref, pk_ref, ok_ref)
    one(av_ref, w1v_ref, w2v_ref, pv_ref, ov_ref)


def _compress(kv_chunks, w1k, w2k, pk, w1v, w2v, pv, batch, nchunk):
    g = N_KV_GROUPS
    width = kv_chunks.shape[2]
    full = lambda a: pl.BlockSpec(a.shape, lambda b, gg: (0,) * a.ndim)
    out = jax.ShapeDtypeStruct((batch * g, nchunk, HEAD_DIM), BF16)
    return pl.pallas_call(
        _compress_kernel,
        grid=(batch, g),
        in_specs=[
            pl.BlockSpec((1, nchunk, width), lambda b, gg: (gg, b, 0)),
            pl.BlockSpec((1, nchunk, width), lambda b, gg: (g + gg, b, 0)),
            full(w1k), full(w2k), full(pk), full(w1v), full(w2v), full(pv),
        ],
        out_specs=[
            pl.BlockSpec((1, nchunk, HEAD_DIM), lambda b, gg: (b * g + gg, 0, 0)),
            pl.BlockSpec((1, nchunk, HEAD_DIM), lambda b, gg: (b * g + gg, 0, 0)),
        ],
        out_shape=[out, out],
        compiler_params=_params(("parallel", "parallel")),
        name="compress",
    )(kv_chunks, kv_chunks, w1k, w2k, pk, w1v, w2v, pv)


ONES_ROWS = 16


def _accumulate(s_ref, col_max, v_aug_t, state):
    m_old, acc_old = state
    m_new = jnp.maximum(m_old, col_max)
    alpha = jnp.exp2(m_old - m_new)
    e = jnp.exp2(s_ref[...] - m_new).astype(BF16)
    return m_new, alpha * acc_old + _dot(v_aug_t, e)


def _nsa_kernel(q_ref, kc_ref, vc_ref, ks_ref, vs_ref, kw_ref, vw_ref, gn_ref, gm_ref,
                qal_ref, cpos_ref, wsel_ref, o_ref, *scratch, tq, tiles_per_step, **static):
    def tile(j, carry):
        rows = pl.ds(pl.multiple_of(j * tq, tq), tq)
        _nsa_tile(pl.program_id(2) * tiles_per_step + j, q_ref.at[rows, :], kc_ref, vc_ref, ks_ref, vs_ref,
                  kw_ref, vw_ref, gn_ref.at[rows, :], gm_ref.at[rows, :], qal_ref, cpos_ref, wsel_ref,
                  o_ref.at[rows, :], *scratch, tq=tq, **static)
        return carry

    lax.fori_loop(0, tiles_per_step, tile, 0)


def _nsa_tile(qt, q_ref, kc_ref, vc_ref, ks_ref, vs_ref, kw_ref, vw_ref, gn_ref, gm_ref,
              qal_ref, cpos_ref, wsel_ref,
              o_ref, q_t, ksaug, kwaug, kcaug, kstage, vstage, s_win, s_a, s_b, win_max, o_cmp_scr, flags,
              *, tq, nk, nkw, seq, sel_chunk):
    b = pl.program_id(0)
    g = pl.program_id(1)
    r_heads = HEADS_PER_GROUP
    dh = HEAD_DIM
    m_cols = r_heads * tq
    nb = seq // SEL_LEN
    nc = kc_ref.shape[1]
    a_w = 2 * dh
    k_w = a_w + nb

    @pl.when((b == 0) & (g == 0) & (qt == 0))
    def _():
        rows = min(FILL_ROWS, seq)

        def fill(c, carry):
            off = pl.multiple_of(c * rows, rows)
            pos = off + lax.broadcasted_iota(jnp.int32, (rows, k_w), 0)
            lane = lax.broadcasted_iota(jnp.int32, (rows, k_w), 1)
            blk_of = jnp.right_shift(pos, SEL_SHIFT)
            cols = jnp.where((lane >= dh) & (lane < dh + 3), blk_of * SEL_LEN,
                             jnp.where((lane >= dh + 3) & (lane < dh + 6), pos & (SEL_LEN - 1),
                                       jnp.where(lane == a_w + blk_of, 1, 0)))
            cols = cols.astype(F32).astype(BF16)
            ksaug[pl.ds(off, rows), :] = cols
            kwaug[pl.ds(off, rows), :] = cols[:, 0:a_w]
            return carry

        lax.fori_loop(0, seq // rows, fill, 0)
        kcaug[...] = cpos_ref[...]
        vstage[...] = jnp.ones(vstage.shape, BF16)

    @pl.when(qt == 0)
    def _():
        ksaug[:, 0:dh] = ks_ref[0]
        kwaug[:, 0:dh] = kw_ref[0]
        kcaug[:, 0:dh] = kc_ref[0]
        q_t[dh:a_w, :] = qal_ref[0]

    v_rows = dh + ONES_ROWS
    ri = lax.broadcasted_iota(jnp.int32, (v_rows, 2 * dh), 0)
    ci = lax.broadcasted_iota(jnp.int32, (v_rows, 2 * dh), 1)
    pick = (((ri < dh) & (ri == ci)) | ((ri >= dh) & (ci == dh))).astype(F32).astype(BF16)
    transposed = lambda v: _dot_nt(pick[0:dh, 0:dh], v).astype(BF16)
    staged_aug_t = lambda v: _dot_nt(pick, v).astype(BF16)
    ones_rows = jnp.ones((ONES_ROWS, nkw), BF16)
    window_aug_t = lambda v: jnp.concatenate([transposed(v), ones_rows], axis=0)
    init = (jnp.full((1, m_cols), MASK_NEG, F32), jnp.zeros((v_rows, m_cols), F32))

    t0 = qt * tq
    col = lax.broadcasted_iota(jnp.int32, (1, m_cols), 1)
    trow = t0 + (col & (tq - 1))

    scale = np.float32(HEAD_DIM ** -0.5 * LOG2_E)
    qf = jnp.transpose(q_ref[...].astype(F32) * scale)
    for r in range(r_heads):
        q_t[0:dh, r * tq:(r + 1) * tq] = qf[r * dh:(r + 1) * dh, :].astype(BF16)

    n_wt = 3
    win_off = [pl.multiple_of(jnp.maximum(t0 - i * nkw, 0), nkw) for i in range(n_wt)]
    newer = (t0 + lax.broadcasted_iota(jnp.int32, (nkw, 1), 0)) <= trow

    def window_tile(i):
        s = _dot(kwaug[pl.ds(win_off[i], nkw), :], q_t[0:a_w, :])
        return s if i == 0 else s + jnp.where(t0 - i * nkw >= 0, 0.0, MASK_NEG)

    def window_scores():
        merged = jnp.where(newer, window_tile(0), window_tile(2))
        s_win[0:nkw, :] = merged
        win_max[0:1, :] = jnp.max(merged, axis=0, keepdims=True)
        mid = window_tile(1)
        s_win[nkw:2 * nkw, :] = mid
        win_max[1:2, :] = jnp.max(mid, axis=0, keepdims=True)

    def compressed_and_select(rows_c, rows_b):
        free = max(rows_c - cmp_chunk - tq // CMP_STRIDE, 0)
        s = _dot(kcaug[0:rows_c, :], q_t[0:a_w, :])
        cend = (free + lax.broadcasted_iota(jnp.int32, (rows_c - free, 1), 0)) * CMP_STRIDE + (CMP_LEN - 1)
        tail = jnp.where(cend <= trow, s[free:], MASK_NEG)
        s = jnp.concatenate([s[0:free], tail], axis=0) if free else tail
        m = jnp.max(s, axis=0, keepdims=True)
        e = jnp.exp2(s - m)
        has_key = (trow >= CMP_LEN - 1).astype(F32)
        p = e * (has_key / jnp.sum(e, axis=0, keepdims=True))
        o_cmp_scr[...] = _dot(transposed(vc_ref[0, 0:rows_c, :]), p.astype(BF16))
        imp = p[:, 0:tq]
        for r in range(1, r_heads):
            imp = imp + p[:, r * tq:(r + 1) * tq]

        hi = imp.astype(BF16)
        lo = (imp - hi.astype(F32)).astype(BF16)
        wsel = wsel_ref[0:rows_b, 0:rows_c]
        imp_t = _dot(wsel, hi) + _dot(wsel, lo)

        window_scores()

        blk = lax.broadcasted_iota(jnp.int32, (rows_b, tq), 0).astype(F32)
        tq_l = t0 + lax.broadcasted_iota(jnp.int32, (rows_b, tq), 1)
        cur = jnp.right_shift(tq_l, SEL_SHIFT).astype(F32)
        valid = blk <= cur
        sel = jnp.where(blk == 0.0, 1.0, jnp.where(blk == cur, 1.0, jnp.where(blk == cur - 1.0, 1.0, 0.0)))
        score = jnp.where(valid, jnp.where(sel > 0.0, -1.0, imp_t), -1.0)
        for _ in range(min(SEL_TOPK, nb) - N_FORCED):
            mx = jnp.max(score, axis=0, keepdims=True)
            idx = jnp.min(jnp.where(score == mx, blk, float(nb)), axis=0, keepdims=True)
            hit = blk == idx
            sel = jnp.where(hit, 1.0, sel)
            score = jnp.where(hit, -2.0, score)
        selv = jnp.where(valid, sel, 0.0) > 0.0
        selneg_t = jnp.where(selv, 0.0, MASK_NEG).astype(BF16)
        unseen = jnp.full((nb - rows_b, tq), MASK_NEG, BF16)
        for r in range(r_heads):
            q_t[a_w:a_w + rows_b, r * tq:(r + 1) * tq] = selneg_t
            if rows_b < nb:
                q_t[a_w + rows_b:, r * tq:(r + 1) * tq] = unseen

        used = jnp.max(jnp.where(selv, 1.0, 0.0), axis=1, keepdims=True)
        bit_id = lax.broadcasted_iota(jnp.int32, (rows_b, 1), 0) & (WORD_BITS - 1)
        weighted = used * jnp.left_shift(1, bit_id).astype(F32)
        for i in range(nb // WORD_BITS):
            if (i + 1) * WORD_BITS <= rows_b:
                flags[i] = jnp.sum(weighted[i * WORD_BITS:(i + 1) * WORD_BITS, :]).astype(jnp.int32)
            else:
                flags[i] = 0

    n_var = nb // sel_chunk
    cmp_chunk = sel_chunk * (SEL_LEN // CMP_STRIDE)
    seen_c = (t0 + tq - CMP_LEN) // CMP_STRIDE + 1
    seen_b = (t0 + tq) // SEL_LEN
    variant = jnp.maximum((seen_c + cmp_chunk - 1) // cmp_chunk, (seen_b + sel_chunk - 1) // sel_chunk)
    for v in range(1, n_var + 1):
        pl.when(variant == v)(functools.partial(compressed_and_select, min(v * cmp_chunk, nc), v * sel_chunk))
    o_cmp = o_cmp_scr[...]

    n_own = tq // SEL_LEN
    off_q = pl.multiple_of(t0, tq)
    kstage[0:tq, :] = ksaug[pl.ds(off_q, tq), :]
    vstage[0:tq, 0:dh] = vs_ref[0, pl.ds(off_q, tq), :]

    def stage_block(j, cnt):
        src = pl.multiple_of(j * SEL_LEN, SEL_LEN)
        dst = pl.multiple_of(cnt * SEL_LEN, SEL_LEN)
        kstage[pl.ds(dst, SEL_LEN), :] = ksaug[pl.ds(src, SEL_LEN), :]
        vstage[pl.ds(dst, SEL_LEN), 0:dh] = vs_ref[0, pl.ds(src, SEL_LEN), :]

    def gather(j, cnt):
        bit = jnp.right_shift(flags[jnp.right_shift(j, WORD_SHIFT)], j & (WORD_BITS - 1)) & 1
        pl.when(bit == 1)(functools.partial(stage_block, j, cnt))
        return cnt + bit

    n_past = jnp.right_shift(t0, SEL_SHIFT)
    near_lo = jnp.maximum(n_past - NEAR_BLOCKS, 0)
    first = jnp.where(near_lo > 0, flags[0] & 1, 0)
    pl.when(first == 1)(functools.partial(stage_block, 0, n_own))

    def far_word(w, cnt):
        lo = jnp.maximum(w * WORD_BITS, 1)
        hi = jnp.minimum((w + 1) * WORD_BITS, near_lo)
        rest = jnp.where(w == 0, flags[w] & -2, flags[w])
        return lax.cond(rest != 0, lambda c: lax.fori_loop(lo, hi, gather, c), lambda c: c, cnt)

    n_blocks = lax.fori_loop(0, jnp.right_shift(near_lo + WORD_BITS - 1, WORD_SHIFT), far_word, n_own + first)
    n_blocks = lax.fori_loop(near_lo, n_past, gather, n_blocks)
    per_tile = nk // SEL_LEN
    n_tiles = (n_blocks + per_tile - 1) // per_tile
    pad_block = jnp.where(lax.broadcasted_iota(jnp.int32, (SEL_LEN, k_w), 1) == PAD_COL, 1.0, 0.0).astype(BF16)

    def pad(j, carry):
        dst = pl.multiple_of(j * SEL_LEN, SEL_LEN)
        kstage[pl.ds(dst, SEL_LEN), :] = pad_block
        return carry

    lax.fori_loop(n_blocks, n_tiles * per_tile, pad, 0)

    own_pos = t0 + lax.broadcasted_iota(jnp.int32, (tq, 1), 0)
    s_first = _dot(kstage[0:nk, :], q_t[...])
    own = jnp.where(own_pos <= trow, s_first[0:tq], MASK_NEG)
    s_a[0:tq, :] = own
    s_a[tq:, :] = s_first[tq:]
    max_a = jnp.maximum(jnp.max(own, axis=0, keepdims=True), jnp.max(s_first[tq:], axis=0, keepdims=True))

    m_w = jnp.maximum(win_max[0:1, :], win_max[1:2, :])
    e_merged = jnp.exp2(s_win[0:nkw, :] - m_w)
    e_mid = jnp.exp2(s_win[nkw:2 * nkw, :] - m_w)
    weights = jnp.concatenate([jnp.where(newer, e_merged, 0.0).astype(BF16), e_mid.astype(BF16),
                               jnp.where(newer, 0.0, e_merged).astype(BF16)], axis=0)
    v_win = jnp.concatenate([window_aug_t(vw_ref[0, pl.ds(win_off[i], nkw), :]) for i in range(n_wt)], axis=1)
    acc_w = _dot(v_win, weights)
    o_win = acc_w[0:dh] / acc_w[dh:dh + 1]

    def tile_accumulate(i, s_ref, col_max, state):
        return _accumulate(s_ref, col_max, staged_aug_t(vstage[pl.ds(pl.multiple_of(i * nk, nk), nk), :]), state)

    def scores_and_accumulate(i_next, s_next_ref, i_cur, s_cur_ref, max_cur, state):
        m_old, acc_old = state
        m_new = jnp.maximum(m_old, max_cur)
        alpha = jnp.exp2(m_old - m_new)
        s_next = _dot(kstage[pl.ds(pl.multiple_of(i_next * nk, nk), nk), :], q_t[...])
        s_next_ref[...] = s_next
        rows = nk // OVERLAP_CHUNKS
        partial, weights = [], []
        for c in range(OVERLAP_CHUNKS):
            rs = slice(c * rows, (c + 1) * rows)
            partial.append(jnp.max(s_next[rs], axis=0, keepdims=True))
            m_c = m_new + partial[c] * 0.0
            weights.append(jnp.exp2(s_cur_ref[rs, :] - m_c).astype(BF16))
        v_t = staged_aug_t(vstage[pl.ds(pl.multiple_of(i_cur * nk, nk), nk), :])
        acc = alpha * acc_old + _dot(v_t, jnp.concatenate(weights, axis=0))
        return (m_new, acc), functools.reduce(jnp.maximum, partial)

    def slc_pair(j, carry):
        state, max_a = carry
        i = 2 * j
        state, max_b = scores_and_accumulate(i + 1, s_b, i, s_a, max_a, state)
        state, max_a = scores_and_accumulate(i + 2, s_a, i + 1, s_b, max_b, state)
        return state, max_a

    n_pairs = (n_tiles - 1) // 2
    state, max_a = lax.fori_loop(0, n_pairs, slc_pair, (init, max_a))
    i_a = 2 * n_pairs

    def two_left(state):
        state, max_b = scores_and_accumulate(i_a + 1, s_b, i_a, s_a, max_a, state)
        return tile_accumulate(i_a + 1, s_b, max_b, state)

    def one_left(state):
        return tile_accumulate(i_a, s_a, max_a, state)

    _, acc = lax.cond(n_tiles - i_a == 2, two_left, one_left, state)
    o_slc = acc[0:dh] / acc[dh:dh + 1]

    gates = jnp.transpose(_sigmoid(gn_ref[...]))
    heads = []
    for r in range(r_heads):
        cs = slice(r * tq, (r + 1) * tq)
        heads.append(gates[3 * r:3 * r + 1, :] * o_cmp[:, cs] + gates[3 * r + 1:3 * r + 2, :] * o_slc[:, cs]
                     + gates[3 * r + 2:3 * r + 3, :] * o_win[:, cs])
    o = jnp.transpose(jnp.concatenate(heads, axis=0))
    o_ref[...] = (_sigmoid(gm_ref[...].astype(F32)) * o).astype(o_ref.dtype)


def _alibi_tables(seq, nc, tq):
    import ml_dtypes
    bf = ml_dtypes.bfloat16
    h = np.arange(1, N_HEADS + 1, dtype=np.float32)
    slopes = (np.exp2(-ALIBI_MAX_BIAS * h / N_HEADS) * LOG2_E).astype(np.float32)
    s1 = slopes.astype(bf).astype(np.float32)
    s2 = (slopes - s1).astype(bf).astype(np.float32)
    s3 = (slopes - s1 - s2).astype(bf).astype(np.float32)
    dh = HEAD_DIM
    qal = np.zeros((N_HEADS, dh), np.float32)
    for rep in range(3):
        qal[:, 3 * rep + 0] = s1
        qal[:, 3 * rep + 1] = s2
        qal[:, 3 * rep + 2] = s3
    qal[:, PAD_COL - dh] = MASK_NEG
    qal = qal.reshape(N_KV_GROUPS, HEADS_PER_GROUP, dh).transpose(0, 2, 1)
    qal_p = np.repeat(qal, tq, axis=2)

    nb = seq // SEL_LEN
    c = np.arange(nc)
    cpos = np.zeros((nc, 2 * dh), np.float32)
    cpos[:, dh:dh + 3] = ((c // 16) * 16 * CMP_STRIDE)[:, None]
    cpos[:, dh + 3:dh + 6] = ((c % 16) * CMP_STRIDE)[:, None]
    cpos[:, dh + 6:dh + 9] = CMP_LEN - 1

    r_sel = SEL_LEN // CMP_STRIDE
    r_cmp = CMP_LEN // CMP_STRIDE
    wsel = np.zeros((nb, nc), np.float32)
    for j in range(nb):
        for mm in range(r_sel):
            for nn in range(r_cmp):
                ci = r_sel * j + mm - nn
                if 0 <= ci < nc - 1:
                    wsel[j, ci] += 1.0
    as_bf = lambda a: jnp.asarray(a.astype(bf))
    return as_bf(qal_p), as_bf(cpos), as_bf(wsel)


def _nsa(qm, kvm, kcm, vcm, gnm, gma, batch, seq, tq, nk, nkw):
    t = qm.shape[0]
    g = N_KV_GROUPS
    dh = HEAD_DIM
    nqt = seq // tq
    nc = kcm.shape[1]
    nb = seq // SEL_LEN
    gw = HEADS_PER_GROUP * dh
    m_cols = HEADS_PER_GROUP * tq
    qal, cpos, wsel = _alibi_tables(seq, nc, tq)
    assert nkw == tq and WINDOW == 2 * tq, "the window branch merges the new and old key tiles of a query tile"
    n_wt = 2
    sel_chunk = max(WORD_BITS, nb // 8)
    per_tile = nk // SEL_LEN
    stage_rows = -(-nb // per_tile) * per_tile * SEL_LEN
    tps = NSA_TILES_PER_STEP if nqt % NSA_TILES_PER_STEP == 0 else 1
    n_steps = nqt // tps
    rq = tps * tq
    row = lambda b, gg, i: b * n_steps + i
    slab = lambda base: pl.BlockSpec((1, seq, dh), lambda b, gg, i: (base + gg, b, 0))
    cmp_slab = pl.BlockSpec((1, nc, dh), lambda b, gg, i: (b * g + gg, 0, 0))
    const = lambda a: pl.BlockSpec(a.shape, lambda b, gg, i: (0,) * a.ndim)
    return pl.pallas_call(
        functools.partial(_nsa_kernel, tq=tq, tiles_per_step=tps, nk=nk, nkw=nkw, seq=seq, sel_chunk=sel_chunk),
        grid=(batch, g, n_steps),
        in_specs=[
            pl.BlockSpec((rq, gw), lambda b, gg, i: (row(b, gg, i), gg)),
            cmp_slab, cmp_slab,
            slab(0), slab(g), slab(2 * g), slab(3 * g),
            pl.BlockSpec((rq, LANES), lambda b, gg, i: (row(b, gg, i), gg)),
            pl.BlockSpec((rq, gw), lambda b, gg, i: (row(b, gg, i), gg)),
            pl.BlockSpec((1, dh, m_cols), lambda b, gg, i: (gg, 0, 0)),
            const(cpos), const(wsel),
        ],
        out_specs=pl.BlockSpec((rq, gw), lambda b, gg, i: (row(b, gg, i), gg)),
        out_shape=jax.ShapeDtypeStruct((t, g * gw), BF16),
        scratch_shapes=[
            pltpu.VMEM((2 * dh + nb, m_cols), BF16),
            pltpu.VMEM((seq, 2 * dh + nb), BF16),
            pltpu.VMEM((seq, 2 * dh), BF16),
            pltpu.VMEM((nc, 2 * dh), BF16),
            pltpu.VMEM((stage_rows, 2 * dh + nb), BF16),
            pltpu.VMEM((stage_rows, 2 * dh), BF16),
            pltpu.VMEM((n_wt * nkw, m_cols), F32),
            pltpu.VMEM((nk, m_cols), F32),
            pltpu.VMEM((nk, m_cols), F32),
            pltpu.VMEM((SUBLANES, m_cols), F32),
            pltpu.VMEM((dh, m_cols), F32),
            pltpu.SMEM((nb // WORD_BITS,), jnp.int32),
        ],
        compiler_params=_params(("arbitrary", "arbitrary", "arbitrary")),
        name="nsa",
    )(qm, kcm, vcm, kvm, kvm, kvm, kvm, gnm, gma, qal, cpos, wsel)


def _ffn_kernel(x_ref, yr_ref, ya_ref, wo_ref, gmix_ref, gpre_ref, wg_ref, wu_ref, wd_ref, gpost_ref,
                p_ref, wpg_ref, bpg_ref, wpp_ref, o_ref, x1_ref, h_ref, acc_ref):
    j = pl.program_id(1)

    @pl.when(j == 0)
    def _():
        y = (yr_ref[...].astype(F32) + ya_ref[...].astype(F32)).astype(BF16)
        x1 = x_ref[...] + _rmsnorm(_dot(y, wo_ref[...]), gmix_ref[...])
        x1_ref[...] = x1
        h_ref[...] = _rmsnorm(x1, gpre_ref[...]).astype(BF16)
        acc_ref[...] = jnp.zeros(acc_ref.shape, F32)

    h = h_ref[...]
    gate = _dot(h, wg_ref[...])
    up = _dot(h, wu_ref[...])
    act = (gate * _sigmoid(gate) * up).astype(BF16)
    acc_ref[...] += _dot(act, wd_ref[...])

    @pl.when(j == pl.num_programs(1) - 1)
    def _():
        x2 = x1_ref[...] + _rmsnorm(acc_ref[...], gpost_ref[...])
        gate_p = _sigmoid(_dot(x2.astype(BF16), wpg_ref[...]) + bpg_ref[...])
        o_ref[...] = x2 + gate_p * _dot(p_ref[...].astype(BF16), wpp_ref[...])


def _ffn(x, yr, ya, wo, gmix, gpre, wgu, wd, gpost, p, wpg, bpg, wpp, tm, tf):
    t, d = x.shape
    dff = wd.shape[0]
    nf = dff // tf
    dp = p.shape[1]
    rows = pl.BlockSpec((tm, d), lambda i, j: (i, 0))
    vec = pl.BlockSpec((1, d), lambda i, j: (0, 0))
    square = pl.BlockSpec((d, d), lambda i, j: (0, 0))
    return pl.pallas_call(
        _ffn_kernel,
        grid=(t // tm, nf),
        in_specs=[
            rows, rows, rows, square, vec, vec,
            pl.BlockSpec((d, tf), lambda i, j: (0, j)),
            pl.BlockSpec((d, tf), lambda i, j: (0, nf + j)),
            pl.BlockSpec((tf, d), lambda i, j: (j, 0)),
            vec,
            pl.BlockSpec((tm, dp), lambda i, j: (i, 0)),
            square,
            vec,
            pl.BlockSpec((dp, d), lambda i, j: (0, 0)),
        ],
        out_specs=rows,
        out_shape=jax.ShapeDtypeStruct((t, d), F32),
        scratch_shapes=[pltpu.VMEM((tm, d), F32), pltpu.VMEM((tm, d), BF16), pltpu.VMEM((tm, d), F32)],
        compiler_params=_params(("parallel", "arbitrary")),
        name="out_ffn_ple",
    )(x, yr, ya, wo, gmix, gpre, wgu, wgu, wd, gpost, p, wpg, bpg, wpp)


def _block_diag_chunks(w):
    n, bs, _ = w.shape
    per = LANES // bs
    w = w.reshape(n // per, per, bs, bs)
    eye = jnp.eye(per, dtype=w.dtype)
    return jnp.einsum('cpij,pq->cpiqj', w, eye).reshape(n // per, LANES, LANES)


def _layer(x, p, norm_mix_pre, norm_mix_post, w_in, conv_w, conv_b, lru_wa, lru_ba, lru_wx, lru_bx,
           lru_lambda, cmp_pos_k, cmp_pos_v, cmp_k_w1, cmp_k_w2, cmp_v_w1, cmp_v_w2, w_out,
           norm_ffn_pre, norm_ffn_post, ffn_w_gate_up, ffn_w_down, ple_w_proj, ple_w_gate, ple_b_gate,
           batch, seq):
    t, d = x.shape
    d_attn = N_HEADS * HEAD_DIM
    d_kv = N_KV_GROUPS * HEAD_DIM
    row2 = lambda v: v.reshape(1, -1)

    o_q = 2 * d
    o_kv = o_q + d_attn
    o_gn = o_kv + 6 * d_kv
    o_gm = o_gn + 3 * N_HEADS
    w_f = jnp.concatenate([w_in[:, 0:o_q], w_in[:, o_gm:o_gm + 2 * d]], axis=1).astype(BF16)
    w_q = w_in[:, o_q:o_kv].astype(BF16)
    w_kv = w_in[:, o_kv:o_gn].astype(BF16)
    per_g = 3 * HEADS_PER_GROUP
    w_gn = w_in[:, o_gn:o_gm].reshape(d, N_KV_GROUPS, per_g)
    w_gn = jnp.pad(w_gn, ((0, 0), (0, 0), (0, LANES - per_g))).reshape(d, N_KV_GROUPS * LANES).astype(BF16)

    g_pre = row2(norm_mix_pre)
    tm = min(512, t)
    yr, gma, qm, gnm, kv_chunks, kvm = _rnn(
        x, g_pre, w_f, jnp.concatenate([w_q, w_gn, w_kv], axis=1), d_attn, N_KV_GROUPS * LANES, 2 * N_KV_GROUPS,
        conv_w, row2(conv_b), _block_diag_chunks(lru_wa).astype(BF16), _block_diag_chunks(lru_wx).astype(BF16),
        row2(lru_ba), row2(lru_bx), row2(lru_lambda), batch, seq, min(512, seq))

    nchunk = seq // CMP_STRIDE
    kcm, vcm = _compress(kv_chunks, cmp_k_w1.astype(BF16), cmp_k_w2.astype(BF16), cmp_pos_k.reshape(1, -1),
                         cmp_v_w1.astype(BF16), cmp_v_w2.astype(BF16), cmp_pos_v.reshape(1, -1), batch, nchunk)

    tq = min(256, seq)
    ya = _nsa(qm, kvm, kcm, vcm, gnm, gma, batch, seq, tq, min(NSA_KEY_TILE, seq), min(256, seq))

    dff = ffn_w_down.shape[0]
    tf = dff
    return _ffn(x, yr, ya, w_out.astype(BF16), row2(norm_mix_post), row2(norm_ffn_pre),
                ffn_w_gate_up.astype(BF16), ffn_w_down.astype(BF16), row2(norm_ffn_post),
                p, ple_w_gate.astype(BF16), row2(ple_b_gate), ple_w_proj.astype(BF16), tm, tf)


def kernel(x, p, norm_mix_pre, norm_mix_post, w_in, conv_w, conv_b, lru_wa, lru_ba, lru_wx, lru_bx, lru_lambda, cmp_pos_k, cmp_pos_v, cmp_k_w1, cmp_k_w2, cmp_v_w1, cmp_v_w2, w_out, norm_ffn_pre, norm_ffn_post, ffn_w_gate_up, ffn_w_down, ple_w_proj, ple_w_gate, ple_b_gate):
    batch, seq, d = x.shape
    depth = w_in.shape[0]
    xf = x.reshape(batch * seq, d)
    for i in range(depth):
        xf = _layer(xf, p[i].reshape(batch * seq, -1), norm_mix_pre[i], norm_mix_post[i], w_in[i], conv_w[i],
                    conv_b[i], lru_wa[i], lru_ba[i], lru_wx[i], lru_bx[i], lru_lambda[i], cmp_pos_k[i],
                    cmp_pos_v[i], cmp_k_w1[i], cmp_k_w2[i], cmp_v_w1[i], cmp_v_w2[i], w_out[i],
                    norm_ffn_pre[i], norm_ffn_post[i], ffn_w_gate_up[i], ffn_w_down[i], ple_w_proj[i],
                    ple_w_gate[i], ple_b_gate[i], batch, seq)
    return xf.reshape(batch, seq, d)
```
